```python
import math
import jax, jax.numpy as jnp
from jax import lax
import numpy as np

D_MODEL = 2048
BATCH = 2
SEQ = 16384
DEPTH = 2

A_GROUPS = 8
A_DIM = 128
A_WIDTH = A_GROUPS * A_DIM
CHUNK = 128
B_GROUPS = 8
B_DIM = 128
B_WIDTH = B_GROUPS * B_DIM
CONV_W = 3
N_HEADS = 16
HEAD_DIM = 128
N_KV = 4
Q_PER_KV = N_HEADS // N_KV
Q_WIDTH = N_HEADS * HEAD_DIM
KV_WIDTH = N_KV * HEAD_DIM
CMP_BLOCK = 32
CMP_STRIDE = 16
CMP_RATIO = CMP_BLOCK // CMP_STRIDE
CMP_HIDDEN = 256
SLC_BLOCK = 64
N_SELECT = 16
WINDOW = 512
Q_BLOCK = 128
D_FF = 7168
N_EXPERTS = 8
TOP_K = 2

EPS = 1e-6
NEG_INF = -1e30
FORCE_SCORE = 1e4

kernel_name = "hybrid_gmlp_conv_nsa_moe_trunk"


def rms_norm(x, g):
    xf = x.astype(jnp.float32)
    y = xf * lax.rsqrt(jnp.mean(xf * xf, axis=-1, keepdims=True) + EPS)
    return (y * g).astype(x.dtype)


def layer_norm(x, g, b):
    xf = x.astype(jnp.float32)
    mu = jnp.mean(xf, axis=-1, keepdims=True)
    var = jnp.mean(jnp.square(xf - mu), axis=-1, keepdims=True)
    return ((xf - mu) * lax.rsqrt(var + EPS) * g + b).astype(x.dtype)


def ada_modulation(c, w, b):
    m = (jax.nn.silu(c) @ w + b)[:, None, :]
    return jnp.split(m, 6, axis=-1)


def modulate(h, shift, scale):
    return h * (1.0 + scale) + shift


def masked_softmax(s, mask, axis=-1):
    s = jnp.where(mask, s, NEG_INF)
    p = jax.nn.softmax(s, axis=axis)
    return jnp.where(mask, p, 0.0)


def alibi_slopes():
    h = jnp.arange(1, N_HEADS + 1, dtype=jnp.float32)
    return jnp.exp2(-8.0 * h / N_HEADS).reshape(N_KV, Q_PER_KV)


def swiglu(h, w_gate, w_up, w_down):
    return (jax.nn.silu(h @ w_gate) * (h @ w_up)) @ w_down


def moe_swiglu(h, w_router, w_gate, w_up, w_down):
    logits = (h @ w_router).astype(jnp.float32)
    top_v, top_i = lax.top_k(logits, TOP_K)
    w = jax.nn.softmax(top_v, axis=-1)
    combine = jnp.einsum("btk,btke->bte", w, jax.nn.one_hot(top_i, N_EXPERTS, dtype=jnp.float32)).astype(h.dtype)
    y = combine[..., 0:1] * swiglu(h, w_gate[0], w_up[0], w_down[0])
    for e in range(1, N_EXPERTS):
        y = y + combine[..., e:e + 1] * swiglu(h, w_gate[e], w_up[e], w_down[e])
    return y


def gating_conv_mixer(h, w_in, v_ln_g, v_ln_b, w_spatial, b_spatial, conv_w, w_out):
    bsz, seq, _ = h.shape
    n_chunk = seq // CHUNK
    u, v, gb, gc, xb = jnp.split(
        h @ w_in, [A_WIDTH, 2 * A_WIDTH, 2 * A_WIDTH + B_WIDTH, 2 * A_WIDTH + 2 * B_WIDTH], axis=-1)
    u = jax.nn.gelu(u)
    v = layer_norm(jax.nn.gelu(v), v_ln_g, v_ln_b)
    v = v.reshape(bsz, n_chunk, CHUNK, A_GROUPS, A_DIM)
    causal = jnp.tril(jnp.ones((CHUNK, CHUNK), dtype=w_spatial.dtype))
    z = jnp.einsum("hts,bnshd->bnthd", w_spatial * causal, v) + b_spatial.T[None, None, :, :, None]
    a_out = (u.reshape(z.shape) * z).reshape(bsz, seq, A_WIDTH)
    zc = gc * xb
    zp = jnp.pad(zc, ((0, 0), (CONV_W - 1, 0), (0, 0)))
    y = conv_w[0] * zp[:, 0:seq]
    for k in range(1, CONV_W):
        y = y + conv_w[k] * zp[:, k:k + seq]
    b_out = gb * y
    return jnp.concatenate([a_out, b_out], axis=-1) @ w_out


def compress_blocks(k, pe, w1, w2):
    bsz, seq = k.shape[:2]
    n_chunk = seq // CMP_STRIDE
    n_cmp = n_chunk - CMP_RATIO + 1
    ch = k.reshape(bsz, n_chunk, CMP_STRIDE, N_KV, HEAD_DIM)
    blocks = jnp.concatenate([ch[:, r:r + n_cmp] for r in range(CMP_RATIO)], axis=2)
    blocks = blocks + pe[None, None, :, None, :]
    hid = jax.nn.gelu(jnp.einsum("bilgd,ldf->bigf", blocks, w1))
    return hid @ w2


def nsa_mixer(h, w_in, cmp_k_pe, cmp_k_w1, cmp_k_w2, cmp_v_pe, cmp_v_w1, cmp_v_w2, w_out):
    bsz, seq, _ = h.shape
    splits = np.cumsum([Q_WIDTH] + [KV_WIDTH] * 6).tolist()
    q, kc, vc, ksl, vsl, kw, vw, gts = jnp.split(h @ w_in, splits, axis=-1)
    q = q.reshape(bsz, seq, N_KV, Q_PER_KV, HEAD_DIM)
    kv = lambda t: t.reshape(bsz, seq, N_KV, HEAD_DIM)
    kc, vc, ksl, vsl, kw, vw = map(kv, (kc, vc, ksl, vsl, kw, vw))
    gates = jax.nn.sigmoid(gts).reshape(bsz, seq, 3, N_KV, Q_PER_KV)

    kcmp = compress_blocks(kc, cmp_k_pe, cmp_k_w1, cmp_k_w2)
    vcmp = compress_blocks(vc, cmp_v_pe, cmp_v_w1, cmp_v_w2)
    n_cmp = kcmp.shape[1]
    cmp_start = jnp.arange(n_cmp) * CMP_STRIDE
    cmp_end = cmp_start + CMP_BLOCK - 1

    n_slc = seq // SLC_BLOCK
    n_sel = min(N_SELECT, n_slc)
    kslc = ksl.reshape(bsz, n_slc, SLC_BLOCK, N_KV, HEAD_DIM).transpose(0, 3, 1, 2, 4)
    vslc = vsl.reshape(bsz, n_slc, SLC_BLOCK, N_KV, HEAD_DIM).transpose(0, 3, 1, 2, 4)
    slc_start = jnp.arange(n_slc) * SLC_BLOCK
    overlap = ((cmp_start[:, None] < slc_start[None] + SLC_BLOCK)
               & (cmp_start[:, None] + CMP_BLOCK > slc_start[None])).astype(jnp.float32)
    bi = jnp.arange(bsz)[:, None, None, None]
    gi = jnp.arange(N_KV)[None, :, None, None]

    kwp = jnp.pad(kw, ((0, 0), (WINDOW, 0), (0, 0), (0, 0)))
    vwp = jnp.pad(vw, ((0, 0), (WINDOW, 0), (0, 0), (0, 0)))

    slopes = alibi_slopes()
    scale = HEAD_DIM ** -0.5

    def query_block(qb):
        q0 = qb * Q_BLOCK
        t = q0 + jnp.arange(Q_BLOCK)
        qh = lax.dynamic_slice_in_dim(q, q0, Q_BLOCK, axis=1).transpose(0, 2, 3, 1, 4) * scale
        g = lax.dynamic_slice_in_dim(gates, q0, Q_BLOCK, axis=1).transpose(2, 0, 3, 4, 1)

        dist = t[:, None] - cmp_end[None, :]
        s = jnp.einsum("bgrtd,bigd->bgrti", qh, kcmp).astype(jnp.float32)
        s = s - slopes[..., None, None] * dist.astype(jnp.float32)
        p_c = masked_softmax(s, dist >= 0)
        o_c = jnp.einsum("bgrti,bigd->bgrtd", p_c.astype(vcmp.dtype), vcmp)

        imp = jnp.einsum("bgrti,ij->bgtj", p_c, overlap)
        cur = t // SLC_BLOCK
        jb = jnp.arange(n_slc)[None, :]
        forced = (jb == 0) | (jb == cur[:, None]) | (jb == cur[:, None] - 1)
        score = jnp.where(forced, FORCE_SCORE, imp)
        score = jnp.where(jb <= cur[:, None], score, -1.0)
        _, idx = lax.top_k(score, n_sel)
        k_g = kslc[bi, gi, idx]
        v_g = vslc[bi, gi, idx]
        pos = idx[..., None] * SLC_BLOCK + jnp.arange(SLC_BLOCK)
        dist = (t[None, None, :, None, None] - pos)[:, :, None]
        s = jnp.einsum("bgrtd,bgtksd->bgrtks", qh, k_g).astype(jnp.float32)
        s = s - slopes[:, :, None, None, None] * dist.astype(jnp.float32)
        p_s = masked_softmax(s, dist >= 0, axis=(-2, -1))
        o_s = jnp.einsum("bgrtks,bgtksd->bgrtd", p_s.astype(v_g.dtype), v_g)

        kwb = lax.dynamic_slice_in_dim(kwp, q0, WINDOW + Q_BLOCK, axis=1)
        vwb = lax.dynamic_slice_in_dim(vwp, q0, WINDOW + Q_BLOCK, axis=1)
        kpos = q0 - WINDOW + jnp.arange(WINDOW + Q_BLOCK)
        dist = t[:, None] - kpos[None, :]
        valid = (dist >= 0) & (dist < WINDOW) & (kpos[None, :] >= 0)
        s = jnp.einsum("bgrtd,bsgd->bgrts", qh, kwb).astype(jnp.float32)
        s = s - slopes[..., None, None] * dist.astype(jnp.float32)
        p_w = masked_softmax(s, valid)
        o_w = jnp.einsum("bgrts,bsgd->bgrtd", p_w.astype(vwb.dtype), vwb)

        o = g[0][..., None] * o_c + g[1][..., None] * o_s + g[2][..., None] * o_w
        return o.transpose(0, 3, 1, 2, 4).reshape(bsz, Q_BLOCK, Q_WIDTH)

    out = lax.map(query_block, jnp.arange(seq // Q_BLOCK))
    out = out.transpose(1, 0, 2, 3).reshape(bsz, seq, Q_WIDTH)
    return out @ w_out


def even_layer(x, c, ada_w, ada_b, mix_pre_g, mix_post_g, ffn_pre_g, ffn_post_g,
               w_in, v_ln_g, v_ln_b, w_spatial, b_spatial, conv_w, w_out,
               ffn_w_gate, ffn_w_up, ffn_w_down):
    sh_m, sc_m, gt_m, sh_f, sc_f, gt_f = ada_modulation(c, ada_w, ada_b)
    h = modulate(rms_norm(x, mix_pre_g), sh_m, sc_m)
    y = gating_conv_mixer(h, w_in, v_ln_g, v_ln_b, w_spatial, b_spatial, conv_w, w_out)
    x = x + gt_m * rms_norm(y, mix_post_g)
    h = modulate(rms_norm(x, ffn_pre_g), sh_f, sc_f)
    x = x + gt_f * rms_norm(swiglu(h, ffn_w_gate, ffn_w_up, ffn_w_down), ffn_post_g)
    return x


def odd_layer(x, c, ada_w, ada_b, mix_pre_g, mix_post_g, ffn_pre_g, ffn_post_g,
              w_in, cmp_k_pe, cmp_k_w1, cmp_k_w2, cmp_v_pe, cmp_v_w1, cmp_v_w2, w_out,
              router_w, exp_w_gate, exp_w_up, exp_w_down):
    sh_m, sc_m, gt_m, sh_f, sc_f, gt_f = ada_modulation(c, ada_w, ada_b)
    h = modulate(rms_norm(x, mix_pre_g), sh_m, sc_m)
    y = nsa_mixer(h, w_in, cmp_k_pe, cmp_k_w1, cmp_k_w2, cmp_v_pe, cmp_v_w1, cmp_v_w2, w_out)
    x = x + gt_m * rms_norm(y, mix_post_g)
    h = modulate(rms_norm(x, ffn_pre_g), sh_f, sc_f)
    x = x + gt_f * rms_norm(moe_swiglu(h, router_w, exp_w_gate, exp_w_up, exp_w_down), ffn_post_g)
    return x


def setup_inputs(seed: int = 0) -> dict:
    key = jax.random.key(seed)
    ks = iter(jax.random.split(key, 40))
    nrm = lambda shape, s: s * jax.random.normal(next(ks), shape, jnp.float32)
    gain = lambda n: 1.0 + nrm((n,), 0.05)
    d = D_MODEL
    return {
        "x": nrm((BATCH, SEQ, d), 1.0),
        "c": nrm((BATCH, d), 1.0),
        "l0_ada_w": nrm((d, 6 * d), 0.5 * d ** -0.5),
        "l0_ada_b": nrm((6 * d,), 0.01),
        "l0_mix_pre_g": gain(d),
        "l0_mix_post_g": gain(d),
        "l0_ffn_pre_g": gain(d),
        "l0_ffn_post_g": gain(d),
        "l0_w_in": nrm((d, 2 * A_WIDTH + 3 * B_WIDTH), d ** -0.5),
        "l0_v_ln_g": gain(A_WIDTH),
        "l0_v_ln_b": nrm((A_WIDTH,), 0.02),
        "l0_w_spatial": nrm((A_GROUPS, CHUNK, CHUNK), 0.5 * CHUNK ** -0.5),
        "l0_b_spatial": 1.0 + nrm((A_GROUPS, CHUNK), 0.05),
        "l0_conv_w": nrm((CONV_W, B_WIDTH), CONV_W ** -0.5),
        "l0_w_out": nrm((A_WIDTH + B_WIDTH, d), (A_WIDTH + B_WIDTH) ** -0.5),
        "l0_ffn_w_gate": nrm((d, D_FF), d ** -0.5),
        "l0_ffn_w_up": nrm((d, D_FF), d ** -0.5),
        "l0_ffn_w_down": nrm((D_FF, d), D_FF ** -0.5),
        "l1_ada_w": nrm((d, 6 * d), 0.5 * d ** -0.5),
        "l1_ada_b": nrm((6 * d,), 0.01),
        "l1_mix_pre_g": gain(d),
        "l1_mix_post_g": gain(d),
        "l1_ffn_pre_g": gain(d),
        "l1_ffn_post_g": gain(d),
        "l1_w_in": nrm((d, Q_WIDTH + 6 * KV_WIDTH + 3 * N_HEADS), d ** -0.5),
        "l1_cmp_k_pe": nrm((CMP_BLOCK, HEAD_DIM), 0.1),
        "l1_cmp_k_w1": nrm((CMP_BLOCK, HEAD_DIM, CMP_HIDDEN), (CMP_BLOCK * HEAD_DIM) ** -0.5),
        "l1_cmp_k_w2": nrm((CMP_HIDDEN, HEAD_DIM), CMP_HIDDEN ** -0.5),
        "l1_cmp_v_pe": nrm((CMP_BLOCK, HEAD_DIM), 0.1),
        "l1_cmp_v_w1": nrm((CMP_BLOCK, HEAD_DIM, CMP_HIDDEN), (CMP_BLOCK * HEAD_DIM) ** -0.5),
        "l1_cmp_v_w2": nrm((CMP_HIDDEN, HEAD_DIM), CMP_HIDDEN ** -0.5),
        "l1_w_out": nrm((Q_WIDTH, d), Q_WIDTH ** -0.5),
        "l1_router_w": nrm((d, N_EXPERTS), d ** -0.5),
        "l1_exp_w_gate": nrm((N_EXPERTS, d, D_FF), d ** -0.5),
        "l1_exp_w_up": nrm((N_EXPERTS, d, D_FF), d ** -0.5),
        "l1_exp_w_down": nrm((N_EXPERTS, D_FF, d), D_FF ** -0.5),
    }


def reference(x, c,
              l0_ada_w, l0_ada_b, l0_mix_pre_g, l0_mix_post_g, l0_ffn_pre_g, l0_ffn_post_g,
              l0_w_in, l0_v_ln_g, l0_v_ln_b, l0_w_spatial, l0_b_spatial, l0_conv_w, l0_w_out,
              l0_ffn_w_gate, l0_ffn_w_up, l0_ffn_w_down,
              l1_ada_w, l1_ada_b, l1_mix_pre_g, l1_mix_post_g, l1_ffn_pre_g, l1_ffn_post_g,
              l1_w_in, l1_cmp_k_pe, l1_cmp_k_w1, l1_cmp_k_w2, l1_cmp_v_pe, l1_cmp_v_w1, l1_cmp_v_w2,
              l1_w_out, l1_router_w, l1_exp_w_gate, l1_exp_w_up, l1_exp_w_down):
    for layer in range(DEPTH):
        if layer % 2 == 0:
            x = even_layer(x, c, l0_ada_w, l0_ada_b, l0_mix_pre_g, l0_mix_post_g, l0_ffn_pre_g, l0_ffn_post_g,
                           l0_w_in, l0_v_ln_g, l0_v_ln_b, l0_w_spatial, l0_b_spatial, l0_conv_w, l0_w_out,
                           l0_ffn_w_gate, l0_ffn_w_up, l0_ffn_w_down)
        else:
            x = odd_layer(x, c, l1_ada_w, l1_ada_b, l1_mix_pre_g, l1_mix_post_g, l1_ffn_pre_g, l1_ffn_post_g,
                          l1_w_in, l1_cmp_k_pe, l1_cmp_k_w1, l1_cmp_k_w2, l1_cmp_v_pe, l1_cmp_v_w1, l1_cmp_v_w2,
                          l1_w_out, l1_router_w, l1_exp_w_gate, l1_exp_w_up, l1_exp_w_down)
    return x
```

```python
import functools
import math

import numpy as np
import jax
import jax.numpy as jnp
from jax import lax
from jax.experimental import pallas as pl
from jax.experimental.pallas import tpu as pltpu

F32 = jnp.float32
BF16 = jnp.bfloat16
HIGHEST = lax.Precision.HIGHEST

EPS = 1e-6
NEG_INF = -1e30
FORCE_SCORE = 1e4

LANES = 128
SUBLANES = 8
VMEM_LIMIT_BYTES = 56 * 1024 * 1024

A_GROUPS = 8
A_WIDTH = 1024
B_WIDTH = 1024
CHUNK = 128
CONV_W = 3
N_HEADS = 16
HEAD_DIM = 128
N_KV = 4
Q_PER_KV = N_HEADS // N_KV
Q_WIDTH = N_HEADS * HEAD_DIM
KV_WIDTH = N_KV * HEAD_DIM
GQ_WIDTH = Q_PER_KV * HEAD_DIM
CMP_BLOCK = 32
CMP_STRIDE = 16
CMP_HIDDEN = 256
SLC_BLOCK = 64
N_SELECT = 16
WINDOW = 512
Q_BLOCK = 128
N_EXPERTS = 8
TOP_K = 2


def _cparams(semantics):
    return pltpu.CompilerParams(dimension_semantics=semantics, vmem_limit_bytes=VMEM_LIMIT_BYTES)


def _sigmoid(x):
    return 1.0 / (1.0 + jnp.exp(-x))


def _gelu_tanh(x):
    return 0.5 * x * (1.0 + jnp.tanh(0.7978845608028654 * (x + 0.044715 * (x * x * x))))


def _rms(x):
    return x * lax.rsqrt(jnp.mean(x * x, axis=-1, keepdims=True) + EPS)


def _tile(n, pref):
    if n <= pref:
        return n
    t = (pref // LANES) * LANES
    while n % t:
        t -= LANES
    assert t > 0, (n, pref)
    return t


def _ada_kernel(c_ref, w_ref, b_ref, o_ref):
    c = c_ref[...]
    s = c * _sigmoid(c)
    o_ref[...] = jnp.dot(s, w_ref[...], preferred_element_type=F32, precision=HIGHEST) + b_ref[...]


def _ada_call(c, w, b):
    bsz, d = c.shape
    n = w.shape[1]
    tn = _tile(n, 1024)
    cp = jnp.pad(c, ((0, SUBLANES - bsz), (0, 0)))
    out = pl.pallas_call(
        _ada_kernel,
        out_shape=jax.ShapeDtypeStruct((SUBLANES, n), F32),
        grid=(n // tn,),
        in_specs=[pl.BlockSpec((SUBLANES, d), lambda j: (0, 0)),
                  pl.BlockSpec((d, tn), lambda j: (0, j)),
                  pl.BlockSpec((1, tn), lambda j: (0, j))],
        out_specs=pl.BlockSpec((SUBLANES, tn), lambda j: (0, j)),
        compiler_params=_cparams(("parallel",)),
        name="ada_modulation",
    )(cp, w, b.reshape(1, n))
    m = out[:bsz]
    return [t.reshape(bsz, 1, d) for t in jnp.split(m, 6, axis=-1)]


def _proj_kernel(*refs, q_tiles, q_scale, with_gates):
    if with_gates:
        x_ref, g_ref, sh_ref, sc_ref, w_ref, wg_ref, o_ref, og_ref, h_ref = refs
    else:
        x_ref, g_ref, sh_ref, sc_ref, w_ref, o_ref, h_ref = refs
    j = pl.program_id(1)

    @pl.when(j == 0)
    def _():
        h = _rms(x_ref[...]) * g_ref[...]
        h = h * (1.0 + sc_ref[0]) + sh_ref[0]
        h_ref[...] = h.astype(BF16)
        if with_gates:
            og_ref[...] = _sigmoid(jnp.dot(h_ref[...], wg_ref[...], preferred_element_type=F32))

    acc = jnp.dot(h_ref[...], w_ref[...], preferred_element_type=F32)
    if q_tiles:
        acc = acc * jnp.where(j < q_tiles, jnp.float32(q_scale), jnp.float32(1.0))
    o_ref[...] = acc.astype(o_ref.dtype)


def _proj_call(x, g, shift, scale, w, wg=None, *, seq, q_cols=0, q_scale=1.0, name):
    n, d = x.shape
    nout = w.shape[1]
    tm = _tile(seq, 1024)
    tn = _tile(nout, 1024)
    assert q_cols % tn == 0
    tpb = seq // tm
    with_gates = wg is not None
    in_specs = [pl.BlockSpec((tm, d), lambda i, j: (i, 0)),
                pl.BlockSpec((1, d), lambda i, j: (0, 0)),
                pl.BlockSpec((1, 1, d), lambda i, j: (i // tpb, 0, 0)),
                pl.BlockSpec((1, 1, d), lambda i, j: (i // tpb, 0, 0)),
                pl.BlockSpec((d, tn), lambda i, j: (0, j))]
    args = [x, g.reshape(1, d), shift, scale, w]
    out_shape = [jax.ShapeDtypeStruct((n, nout), BF16)]
    out_specs = [pl.BlockSpec((tm, tn), lambda i, j: (i, j))]
    if with_gates:
        in_specs.append(pl.BlockSpec((d, LANES), lambda i, j: (0, 0)))
        args.append(wg)
        out_shape.append(jax.ShapeDtypeStruct((n, LANES), F32))
        out_specs.append(pl.BlockSpec((tm, LANES), lambda i, j: (i, 0)))
    out = pl.pallas_call(
        functools.partial(_proj_kernel, q_tiles=q_cols // tn, q_scale=q_scale, with_gates=with_gates),
        out_shape=out_shape,
        grid=(n // tm, nout // tn),
        in_specs=in_specs,
        out_specs=out_specs,
        scratch_shapes=[pltpu.VMEM((tm, d), BF16)],
        compiler_params=_cparams(("parallel", "arbitrary")),
        name=name,
    )(*args)
    return out if with_gates else out[0]


def _mixer0_kernel(p_ref, pgc_ref, pxb_ref, x_ref, vg_ref, vb_ref, wsp_ref, bspt_ref, cw_ref,
                   wout_ref, gpost_ref, gate_ref, o_ref, cat_ref, *, tm, tiles_per_batch):
    i = pl.program_id(0)
    nchunk = tm // CHUNK
    v = _gelu_tanh(p_ref[:, A_WIDTH:2 * A_WIDTH].astype(F32))
    mu = jnp.mean(v, axis=-1, keepdims=True)
    vc = v - mu
    var = jnp.mean(vc * vc, axis=-1, keepdims=True)
    vn = (vc * lax.rsqrt(var + EPS) * vg_ref[...] + vb_ref[...]).astype(BF16)
    row = lax.broadcasted_iota(jnp.int32, (CHUNK, CHUNK), 0)
    col = lax.broadcasted_iota(jnp.int32, (CHUNK, CHUNK), 1)
    causal = col <= row
    for h in range(A_GROUPS):
        cs = slice(h * LANES, (h + 1) * LANES)
        w = jnp.where(causal, wsp_ref[h], 0.0).astype(BF16)
        rhs = jnp.concatenate([vn[c * CHUNK:(c + 1) * CHUNK, cs] for c in range(nchunk)], axis=1)
        z = jnp.dot(w, rhs, preferred_element_type=F32) + bspt_ref[:, h:h + 1]
        for c in range(nchunk):
            rs = slice(c * CHUNK, (c + 1) * CHUNK)
            u = _gelu_tanh(p_ref[rs, cs].astype(F32))
            cat_ref[rs, cs] = (u * z[:, c * CHUNK:(c + 1) * CHUNK]).astype(BF16)
    o_gb, o_gc, o_xb = 2 * A_WIDTH, 2 * A_WIDTH + B_WIDTH, 2 * A_WIDTH + 2 * B_WIDTH
    zc = p_ref[:, o_gc:o_gc + B_WIDTH].astype(F32) * p_ref[:, o_xb:o_xb + B_WIDTH].astype(F32)
    prev = pgc_ref[...].astype(F32) * pxb_ref[...].astype(F32)
    prev = jnp.where(i % tiles_per_batch == 0, 0.0, prev)
    nprev = prev.shape[0]
    p1 = prev[nprev - 1:nprev, :]
    p2 = prev[nprev - 2:nprev - 1, :]
    rowi = lax.broadcasted_iota(jnp.int32, (tm, B_WIDTH), 0)
    r1 = jnp.where(rowi == 0, p1, pltpu.roll(zc, 1, 0))
    r2 = jnp.where(rowi == 0, p2, jnp.where(rowi == 1, p1, pltpu.roll(zc, 2, 0)))
    y = cw_ref[0:1, :] * r2 + cw_ref[1:2, :] * r1 + cw_ref[2:3, :] * zc
    cat_ref[:, A_WIDTH:] = (p_ref[:, o_gb:o_gb + B_WIDTH].astype(F32) * y).astype(BF16)
    yo = jnp.dot(cat_ref[...], wout_ref[...], preferred_element_type=F32)
    o_ref[...] = x_ref[...] + gate_ref[0] * (_rms(yo) * gpost_ref[...])


def _mixer0_call(proj, x, v_ln_g, v_ln_b, w_spatial, b_spatial, conv_w, w_out, g_post, gate, *, seq):
    n, d = x.shape
    tm = _tile(seq, 512)
    tpb = seq // tm
    pr = 16
    rpb = tm // pr
    wcat = A_WIDTH + B_WIDTH
    cwp = jnp.pad(conv_w, ((0, SUBLANES - CONV_W), (0, 0)))
    return pl.pallas_call(
        functools.partial(_mixer0_kernel, tm=tm, tiles_per_batch=tpb),
        out_shape=jax.ShapeDtypeStruct((n, d), F32),
        grid=(n // tm,),
        in_specs=[pl.BlockSpec((tm, proj.shape[1]), lambda i: (i, 0)),
                  pl.BlockSpec((pr, B_WIDTH), lambda i: (jnp.maximum(i * rpb - 1, 0), 3)),
                  pl.BlockSpec((pr, B_WIDTH), lambda i: (jnp.maximum(i * rpb - 1, 0), 4)),
                  pl.BlockSpec((tm, d), lambda i: (i, 0)),
                  pl.BlockSpec((1, A_WIDTH), lambda i: (0, 0)),
                  pl.BlockSpec((1, A_WIDTH), lambda i: (0, 0)),
                  pl.BlockSpec((A_GROUPS, CHUNK, CHUNK), lambda i: (0, 0, 0)),
                  pl.BlockSpec((CHUNK, A_GROUPS), lambda i: (0, 0)),
                  pl.BlockSpec((SUBLANES, B_WIDTH), lambda i: (0, 0)),
                  pl.BlockSpec((wcat, d), lambda i: (0, 0)),
                  pl.BlockSpec((1, d), lambda i: (0, 0)),
                  pl.BlockSpec((1, 1, d), lambda i: (i // tpb, 0, 0))],
        out_specs=pl.BlockSpec((tm, d), lambda i: (i, 0)),
        scratch_shapes=[pltpu.VMEM((tm, wcat), BF16)],
        compiler_params=_cparams(("parallel",)),
        name="l0_mixer",
    )(proj, proj, proj, x, v_ln_g.reshape(1, -1), v_ln_b.reshape(1, -1), w_spatial, b_spatial.T, cwp,
      w_out, g_post.reshape(1, d), gate)


def _ffn_kernel(x_ref, gpre_ref, sh_ref, sc_ref, wg_ref, wu_ref, wd_ref, gpost_ref, gate_ref, o_ref, h_ref):
    j = pl.program_id(1)

    @pl.when(j == 0)
    def _():
        h = _rms(x_ref[...]) * gpre_ref[...]
        h_ref[...] = (h * (1.0 + sc_ref[0]) + sh_ref[0]).astype(BF16)
        o_ref[...] = jnp.zeros_like(o_ref)

    h = h_ref[...]
    g = jnp.dot(h, wg_ref[...], preferred_element_type=F32)
    u = jnp.dot(h, wu_ref[...], preferred_element_type=F32)
    a = (g * _sigmoid(g) * u).astype(BF16)
    o_ref[...] += jnp.dot(a, wd_ref[...], preferred_element_type=F32)

    @pl.when(j == pl.num_programs(1) - 1)
    def _():
        o_ref[...] = x_ref[...] + gate_ref[0] * (_rms(o_ref[...]) * gpost_ref[...])


def _ffn_call(x, g_pre, shift, scale, wg, wu, wd, g_post, gate, *, seq):
    n, d = x.shape
    f = wg.shape[1]
    tm = _tile(seq, 1024)
    tf = _tile(f, 256)
    tpb = seq // tm
    return pl.pallas_call(
        _ffn_kernel,
        out_shape=jax.ShapeDtypeStruct((n, d), F32),
        grid=(n // tm, f // tf),
        in_specs=[pl.BlockSpec((tm, d), lambda i, j: (i, 0)),
                  pl.BlockSpec((1, d), lambda i, j: (0, 0)),
                  pl.BlockSpec((1, 1, d), lambda i, j: (i // tpb, 0, 0)),
                  pl.BlockSpec((1, 1, d), lambda i, j: (i // tpb, 0, 0)),
                  pl.BlockSpec((d, tf), lambda i, j: (0, j)),
                  pl.BlockSpec((d, tf), lambda i, j: (0, j)),
                  pl.BlockSpec((tf, d), lambda i, j: (j, 0)),
                  pl.BlockSpec((1, d), lambda i, j: (0, 0)),
                  pl.BlockSpec((1, 1, d), lambda i, j: (i // tpb, 0, 0))],
        out_specs=pl.BlockSpec((tm, d), lambda i, j: (i, 0)),
        scratch_shapes=[pltpu.VMEM((tm, d), BF16)],
        compiler_params=_cparams(("parallel", "arbitrary")),
        name="l0_ffn",
    )(x, g_pre.reshape(1, d), shift, scale, wg, wu, wd, g_post.reshape(1, d), gate)


def _compress_kernel(x_ref, pe_ref, w1_ref, w2_ref, o_ref, ot_ref):
    half = CMP_STRIDE * HEAD_DIM
    x = x_ref[0]
    nc = x.shape[0]
    a = jnp.dot(x, w1_ref[0:half, :], preferred_element_type=F32)
    b = jnp.dot(x, w1_ref[half:2 * half, :], preferred_element_type=F32)
    bias = jnp.dot(pe_ref[...], w1_ref[...], preferred_element_type=F32)[0:1, :]
    hid = _gelu_tanh(a + pltpu.roll(b, nc - 1, 0) + bias)
    out = jnp.dot(hid.astype(BF16), w2_ref[...], preferred_element_type=F32)
    rowi = lax.broadcasted_iota(jnp.int32, out.shape, 0)
    out = jnp.where(rowi < nc - 1, out, 0.0)
    o_ref[0] = out.astype(BF16)
    ot_ref[0] = out.T.astype(BF16)


def _compress_call(xc, pe, w1, w2):
    bg, nc, kdim = xc.shape
    pe_flat = jnp.pad(pe.reshape(1, -1), ((0, SUBLANES - 1), (0, 0))).astype(BF16)
    w1f = w1.reshape(CMP_BLOCK * HEAD_DIM, CMP_HIDDEN).astype(BF16)
    return pl.pallas_call(
        _compress_kernel,
        out_shape=[jax.ShapeDtypeStruct((bg, nc, HEAD_DIM), BF16),
                   jax.ShapeDtypeStruct((bg, HEAD_DIM, nc), BF16)],
        grid=(bg,),
        in_specs=[pl.BlockSpec((1, nc, kdim), lambda i: (i, 0, 0)),
                  pl.BlockSpec((SUBLANES, CMP_BLOCK * HEAD_DIM), lambda i: (0, 0)),
                  pl.BlockSpec((CMP_BLOCK * HEAD_DIM, CMP_HIDDEN), lambda i: (0, 0)),
                  pl.BlockSpec((CMP_HIDDEN, HEAD_DIM), lambda i: (0, 0))],
        out_specs=[pl.BlockSpec((1, nc, HEAD_DIM), lambda i: (i, 0, 0)),
                   pl.BlockSpec((1, HEAD_DIM, nc), lambda i: (i, 0, 0))],
        compiler_params=_cparams(("parallel",)),
        name="nsa_compress",
    )(xc, pe_flat, w1f, w2.astype(BF16))


def _q_transposed(q_ref):
    parts = []
    for r in range(Q_PER_KV):
        parts.append(q_ref[:, r * HEAD_DIM:(r + 1) * HEAD_DIM].astype(F32).T.astype(BF16))
    return jnp.concatenate(parts, axis=1)


def _attn_cmp_kernel(q_ref, kc_ref, vct_ref, ovt_ref, slope_ref, oc_ref, sel_ref, *, n_sel):
    qb = pl.program_id(2)
    q0 = qb * Q_BLOCK
    qt = _q_transposed(q_ref)
    ncp = kc_ref.shape[1]
    nslc = ovt_ref.shape[0]
    s_all = jnp.dot(kc_ref[0], qt, preferred_element_type=F32)
    isub = lax.broadcasted_iota(jnp.int32, (ncp, Q_BLOCK), 0)
    tl = lax.broadcasted_iota(jnp.int32, (ncp, Q_BLOCK), 1) + q0
    dist = tl - (isub * CMP_STRIDE + (CMP_BLOCK - 1))
    valid = dist >= 0
    distf = dist.astype(F32)
    psum = jnp.zeros((ncp, Q_BLOCK), F32)
    pparts = []
    for r in range(Q_PER_KV):
        ls = slice(r * Q_BLOCK, (r + 1) * Q_BLOCK)
        s = s_all[:, ls] - slope_ref[0, 0:1, ls] * distf
        s = jnp.where(valid, s, NEG_INF)
        m = jnp.max(s, axis=0, keepdims=True)
        e = jnp.exp(s - m)
        p = jnp.where(valid, e / jnp.sum(e, axis=0, keepdims=True), 0.0)
        psum = psum + p
        pparts.append(p.astype(BF16))
    pt = jnp.concatenate(pparts, axis=1)
    oc_ref[0, 0, 0] = jnp.dot(vct_ref[0], pt, preferred_element_type=F32)
    hi = psum.astype(BF16)
    lo = (psum - hi.astype(F32)).astype(BF16)
    ovt = ovt_ref[...]
    imp = jnp.dot(ovt, hi, preferred_element_type=F32) + jnp.dot(ovt, lo, preferred_element_type=F32)
    jb = lax.broadcasted_iota(jnp.int32, (nslc, Q_BLOCK), 0)
    cur = (lax.broadcasted_iota(jnp.int32, (nslc, Q_BLOCK), 1) + q0) // SLC_BLOCK
    forced = (jb == 0) | (jb == cur) | (jb == cur - 1)
    score = jnp.where(forced, FORCE_SCORE, imp)
    score = jnp.where(jb <= cur, score, -1.0)

    def pick_next(_, carry):
        work, sel = carry
        m = jnp.max(work, axis=0, keepdims=True)
        first = jnp.min(jnp.where(work == m, jb, nslc), axis=0, keepdims=True)
        pick = jb == first
        return jnp.where(pick, -3e38, work), jnp.where(pick, 1.0, sel)

    _, sel = lax.fori_loop(0, n_sel, pick_next, (score, jnp.zeros((nslc, Q_BLOCK), F32)))
    sel_ref[0, 0, 0] = jnp.where(jb <= cur, sel, 0.0)


def _attn_cmp_call(proj, kcmp, vcmpt, ovt, slopes, *, bsz, seq):
    nqb = seq // Q_BLOCK
    ncp = kcmp.shape[1]
    nslc = seq // SLC_BLOCK
    n_sel = min(N_SELECT, nslc)
    return pl.pallas_call(
        functools.partial(_attn_cmp_kernel, n_sel=n_sel),
        out_shape=[jax.ShapeDtypeStruct((bsz, N_KV, nqb, HEAD_DIM, GQ_WIDTH), F32),
                   jax.ShapeDtypeStruct((bsz, N_KV, nqb, nslc, Q_BLOCK), F32)],
        grid=(bsz, N_KV, nqb),
        in_specs=[pl.BlockSpec((Q_BLOCK, GQ_WIDTH), lambda b, g, q: (b * nqb + q, g)),
                  pl.BlockSpec((1, ncp, HEAD_DIM), lambda b, g, q: (b * N_KV + g, 0, 0)),
                  pl.BlockSpec((1, HEAD_DIM, ncp), lambda b, g, q: (b * N_KV + g, 0, 0)),
                  pl.BlockSpec((nslc, ncp), lambda b, g, q: (0, 0)),
                  pl.BlockSpec((1, SUBLANES, GQ_WIDTH), lambda b, g, q: (g, 0, 0))],
        out_specs=[pl.BlockSpec((1, 1, 1, HEAD_DIM, GQ_WIDTH), lambda b, g, q: (b, g, q, 0, 0)),
                   pl.BlockSpec((1, 1, 1, nslc, Q_BLOCK), lambda b, g, q: (b, g, q, 0, 0))],
        compiler_params=_cparams(("parallel", "parallel", "arbitrary")),
        name="nsa_compressed_select",
    )(proj, kcmp, vcmpt, ovt, slopes)


def _attn_sel_kernel(q_ref, ks_ref, vst_ref, kw_ref, vwt_ref, sel_ref, oc_ref, gates_ref, slope_ref, o_ref):
    qb = pl.program_id(2)
    q0 = qb * Q_BLOCK
    qt = _q_transposed(q_ref)
    slope = slope_ref[0, 0:1, :]
    sub = lax.broadcasted_iota(jnp.int32, (Q_BLOCK, GQ_WIDTH), 0)
    tq = (lax.broadcasted_iota(jnp.int32, (Q_BLOCK, GQ_WIDTH), 1) & (Q_BLOCK - 1)) + q0

    def step(k, vt, dist, mask, state):
        m_old, l_old, acc = state
        s = jnp.dot(k, qt, preferred_element_type=F32) - slope * dist.astype(F32)
        s = jnp.where(mask, s, NEG_INF)
        m_new = jnp.maximum(m_old, jnp.max(s, axis=0, keepdims=True))
        alpha = jnp.exp(m_old - m_new)
        p = jnp.where(mask, jnp.exp(s - m_new), 0.0)
        l_new = alpha * l_old + jnp.sum(p, axis=0, keepdims=True)
        acc = alpha * acc + jnp.dot(vt, p.astype(BF16), preferred_element_type=F32)
        return m_new, l_new, acc

    init = (jnp.full((1, GQ_WIDTH), NEG_INF, F32), jnp.zeros((1, GQ_WIDTH), F32),
            jnp.zeros((HEAD_DIM, GQ_WIDTH), F32))

    def sel_body(p, state):
        k = ks_ref[0, pl.ds(pl.multiple_of(p * Q_BLOCK, Q_BLOCK), Q_BLOCK), :]
        vt = vst_ref[0, p]
        two = sel_ref[0, 0, 0, p]
        lo = jnp.concatenate([two[0:1, :]] * Q_PER_KV, axis=1)
        hi = jnp.concatenate([two[1:2, :]] * Q_PER_KV, axis=1)
        chosen = jnp.where(sub < SLC_BLOCK, lo, hi) > 0.5
        dist = tq - (p * Q_BLOCK + sub)
        return step(k, vt, dist, chosen & (dist >= 0), state)

    _, l_s, acc_s = lax.fori_loop(0, qb + 1, sel_body, init)

    state = init
    for w in range(WINDOW // Q_BLOCK + 1):
        kt = qb - WINDOW // Q_BLOCK + w
        ktc = jnp.maximum(kt, 0)
        k = kw_ref[0, pl.ds(pl.multiple_of(ktc * Q_BLOCK, Q_BLOCK), Q_BLOCK), :]
        vt = vwt_ref[0, ktc]
        pos = kt * Q_BLOCK + sub
        dist = tq - pos
        state = step(k, vt, dist, (dist >= 0) & (dist < WINDOW) & (pos >= 0), state)
    _, l_w, acc_w = state

    g = gates_ref[0, 0, 0]
    ot = g[0:1, :] * oc_ref[0, 0, 0] + g[1:2, :] * (acc_s / l_s) + g[2:3, :] * (acc_w / l_w)
    for r in range(Q_PER_KV):
        cs = slice(r * HEAD_DIM, (r + 1) * HEAD_DIM)
        o_ref[:, cs] = ot[:, cs].T.astype(o_ref.dtype)


def _attn_sel_call(proj, vst, vwt, sel, oc, gates_t, slopes, *, bsz, seq):
    nqb = seq // Q_BLOCK
    n = bsz * seq
    proj3 = proj.reshape(bsz, seq, proj.shape[1])
    ks_col = (Q_WIDTH + 2 * KV_WIDTH) // HEAD_DIM
    kw_col = (Q_WIDTH + 4 * KV_WIDTH) // HEAD_DIM
    sel5 = sel.reshape(bsz, N_KV, nqb, nqb, 2, Q_BLOCK)
    return pl.pallas_call(
        _attn_sel_kernel,
        out_shape=jax.ShapeDtypeStruct((n, Q_WIDTH), BF16),
        grid=(bsz, N_KV, nqb),
        in_specs=[pl.BlockSpec((Q_BLOCK, GQ_WIDTH), lambda b, g, q: (b * nqb + q, g)),
                  pl.BlockSpec((1, seq, HEAD_DIM), lambda b, g, q: (b, 0, ks_col + g)),
                  pl.BlockSpec((1, nqb, HEAD_DIM, Q_BLOCK), lambda b, g, q: (b * N_KV + g, 0, 0, 0)),
                  pl.BlockSpec((1, seq, HEAD_DIM), lambda b, g, q: (b, 0, kw_col + g)),
                  pl.BlockSpec((1, nqb, HEAD_DIM, Q_BLOCK), lambda b, g, q: (b * N_KV + g, 0, 0, 0)),
                  pl.BlockSpec((1, 1, 1, nqb, 2, Q_BLOCK), lambda b, g, q: (b, g, q, 0, 0, 0)),
                  pl.BlockSpec((1, 1, 1, HEAD_DIM, GQ_WIDTH), lambda b, g, q: (b, g, q, 0, 0)),
                  pl.BlockSpec((1, 1, 1, SUBLANES, GQ_WIDTH), lambda b, g, q: (b, g, q, 0, 0)),
                  pl.BlockSpec((1, SUBLANES, GQ_WIDTH), lambda b, g, q: (g, 0, 0))],
        out_specs=pl.BlockSpec((Q_BLOCK, GQ_WIDTH), lambda b, g, q: (b * nqb + q, g)),
        compiler_params=_cparams(("parallel", "parallel", "arbitrary")),
        name="nsa_selected_window",
    )(proj, proj3, vst, proj3, vwt, sel5, oc, gates_t, slopes)


def _outproj_router_kernel(a_ref, x_ref, w_ref, gpost_ref, gate_ref, gpre_ref, sh_ref, sc_ref, wr_ref,
                           x2_ref, h2_ref, rt_ref):
    y = jnp.dot(a_ref[...], w_ref[...], preferred_element_type=F32)
    x2 = x_ref[...] + gate_ref[0] * (_rms(y) * gpost_ref[...])
    x2_ref[...] = x2
    h = _rms(x2) * gpre_ref[...]
    h = h * (1.0 + sc_ref[0]) + sh_ref[0]
    h2_ref[...] = h.astype(BF16)
    logits = jnp.dot(h, wr_ref[...], preferred_element_type=F32, precision=HIGHEST)
    lane = lax.broadcasted_iota(jnp.int32, logits.shape, 1)
    logits = jnp.where(lane < N_EXPERTS, logits, -jnp.inf)
    v1 = jnp.max(logits, axis=-1, keepdims=True)
    i1 = jnp.min(jnp.where(logits == v1, lane, LANES), axis=-1, keepdims=True)
    rest = jnp.where(lane == i1, -jnp.inf, logits)
    v2 = jnp.max(rest, axis=-1, keepdims=True)
    i2 = jnp.min(jnp.where(rest == v2, lane, LANES), axis=-1, keepdims=True)
    e2 = jnp.exp(v2 - v1)
    w1 = 1.0 / (1.0 + e2)
    w2 = e2 / (1.0 + e2)
    out = jnp.where(lane == 0, w1, jnp.where(lane == 1, w2, 0.0))
    out = jnp.where(lane == 2, i1.astype(F32), jnp.where(lane == 3, i2.astype(F32), out))
    rt_ref[...] = out


def _outproj_router_call(a, x, w_out, g_post, gate, g_pre, shift, scale, w_router, *, seq):
    n, d = x.shape
    tm = _tile(seq, 512)
    tpb = seq // tm
    wr = jnp.pad(w_router, ((0, 0), (0, LANES - w_router.shape[1])))
    row = lambda i: (i, 0)
    fixed = lambda i: (0, 0)
    per_b = lambda i: (i // tpb, 0, 0)
    return pl.pallas_call(
        _outproj_router_kernel,
        out_shape=[jax.ShapeDtypeStruct((n, d), F32),
                   jax.ShapeDtypeStruct((n, d), BF16),
                   jax.ShapeDtypeStruct((n, LANES), F32)],
        grid=(n // tm,),
        in_specs=[pl.BlockSpec((tm, a.shape[1]), row),
                  pl.BlockSpec((tm, d), row),
                  pl.BlockSpec(w_out.shape, fixed),
                  pl.BlockSpec((1, d), fixed),
                  pl.BlockSpec((1, 1, d), per_b),
                  pl.BlockSpec((1, d), fixed),
                  pl.BlockSpec((1, 1, d), per_b),
                  pl.BlockSpec((1, 1, d), per_b),
                  pl.BlockSpec((d, LANES), fixed)],
        out_specs=[pl.BlockSpec((tm, d), row), pl.BlockSpec((tm, d), row), pl.BlockSpec((tm, LANES), row)],
        compiler_params=_cparams(("parallel",)),
        name="l1_outproj_router",
    )(a, x, w_out, g_post.reshape(1, d), gate, g_pre.reshape(1, d), shift, scale, wr)


def _moe_kernel(te_ref, nt_ref, h_ref, rw_ref, wg_ref, wu_ref, wd_ref, o_ref):
    i = pl.program_id(0)
    j = pl.program_id(1)

    @pl.when(j == 0)
    def _():
        o_ref[...] = jnp.zeros_like(o_ref)

    @pl.when(i < nt_ref[0])
    def _():
        h = h_ref[...]
        g = jnp.dot(h, wg_ref[0], preferred_element_type=F32)
        u = jnp.dot(h, wu_ref[0], preferred_element_type=F32)
        a = (g * _sigmoid(g) * u).astype(BF16)
        o_ref[...] += jnp.dot(a, wd_ref[0], preferred_element_type=F32)

    @pl.when(j == pl.num_programs(1) - 1)
    def _():
        o_ref[...] = o_ref[...] * rw_ref[...]


def _moe_call(tile_expert, n_tiles, hs, row_w, wg, wu, wd, *, tm):
    mp, d = hs.shape
    f = wg.shape[2]
    tf = _tile(f, 512)
    nj = f // tf

    def wcol(i, j, nt):
        return jnp.where(i < nt[0], j, nj - 1)

    grid_spec = pltpu.PrefetchScalarGridSpec(
        num_scalar_prefetch=2,
        grid=(mp // tm, f // tf),
        in_specs=[pl.BlockSpec((tm, d), lambda i, j, te, nt: (i, 0)),
                  pl.BlockSpec((tm, 1), lambda i, j, te, nt: (i, 0)),
                  pl.BlockSpec((1, d, tf), lambda i, j, te, nt: (te[i], 0, wcol(i, j, nt))),
                  pl.BlockSpec((1, d, tf), lambda i, j, te, nt: (te[i], 0, wcol(i, j, nt))),
                  pl.BlockSpec((1, tf, d), lambda i, j, te, nt: (te[i], wcol(i, j, nt), 0))],
        out_specs=pl.BlockSpec((tm, d), lambda i, j, te, nt: (i, 0)),
    )
    return pl.pallas_call(
        _moe_kernel,
        out_shape=jax.ShapeDtypeStruct((mp, d), F32),
        grid_spec=grid_spec,
        compiler_params=_cparams(("arbitrary", "arbitrary")),
        name="l1_moe_experts",
    )(tile_expert, n_tiles, hs, row_w, wg, wu, wd)


def _residual_norm_kernel(x_ref, y0_ref, y1_ref, gpost_ref, gate_ref, o_ref):
    y = y0_ref[...] + y1_ref[...]
    o_ref[...] = x_ref[...] + gate_ref[0] * (_rms(y) * gpost_ref[...])


def _residual_norm_call(x, y0, y1, g_post, gate, *, seq):
    n, d = x.shape
    tm = _tile(seq, 512)
    tpb = seq // tm
    row = lambda i: (i, 0)
    return pl.pallas_call(
        _residual_norm_kernel,
        out_shape=jax.ShapeDtypeStruct((n, d), F32),
        grid=(n // tm,),
        in_specs=[pl.BlockSpec((tm, d), row), pl.BlockSpec((tm, d), row), pl.BlockSpec((tm, d), row),
                  pl.BlockSpec((1, d), lambda i: (0, 0)),
                  pl.BlockSpec((1, 1, d), lambda i: (i // tpb, 0, 0))],
        out_specs=pl.BlockSpec((tm, d), row),
        compiler_params=_cparams(("parallel",)),
        name="l1_moe_residual",
    )(x, y0, y1, g_post.reshape(1, d), gate)


def _even_layer(x2d, c, ada_w, ada_b, mix_pre_g, mix_post_g, ffn_pre_g, ffn_post_g,
                w_in, v_ln_g, v_ln_b, w_spatial, b_spatial, conv_w, w_out,
                ffn_w_gate, ffn_w_up, ffn_w_down, *, seq):
    sh_m, sc_m, gt_m, sh_f, sc_f, gt_f = _ada_call(c, ada_w, ada_b)
    proj = _proj_call(x2d, mix_pre_g, sh_m, sc_m, w_in.astype(BF16), seq=seq, name="l0_in_proj")
    x2d = _mixer0_call(proj, x2d, v_ln_g, v_ln_b, w_spatial, b_spatial, conv_w, w_out.astype(BF16),
                       mix_post_g, gt_m, seq=seq)
    return _ffn_call(x2d, ffn_pre_g, sh_f, sc_f, ffn_w_gate.astype(BF16), ffn_w_up.astype(BF16),
                     ffn_w_down.astype(BF16), ffn_post_g, gt_f, seq=seq)


def _moe_routing(rt, *, tm):
    n = rt.shape[0]
    na = n * TOP_K
    mp = na + N_EXPERTS * tm
    w_flat = rt[:, 0:TOP_K].reshape(na)
    e_flat = rt[:, TOP_K:2 * TOP_K].astype(jnp.int32).reshape(na)
    onehot = (e_flat[:, None] == jnp.arange(N_EXPERTS, dtype=jnp.int32)[None, :]).astype(jnp.int32)
    csum = jnp.cumsum(onehot, axis=0)
    counts = csum[-1]
    rank = jnp.sum(onehot * csum, axis=1) - 1
    padded = ((counts + tm - 1) // tm) * tm
    ends = jnp.cumsum(padded)
    starts = ends - padded
    dest = jnp.sum(onehot * starts[None, :], axis=1) + rank
    row_token = jnp.zeros((mp,), jnp.int32).at[dest].set(jnp.arange(na, dtype=jnp.int32) // TOP_K)
    row_w = jnp.zeros((mp,), F32).at[dest].set(w_flat)
    n_tiles = (ends[-1] // tm).astype(jnp.int32)
    tile_ids = jnp.arange(mp // tm, dtype=jnp.int32)
    tile_expert = jnp.sum((tile_ids[:, None] * tm >= ends[None, :]).astype(jnp.int32), axis=1)
    last = jnp.sum(((n_tiles - 1) * tm >= ends).astype(jnp.int32))
    tile_expert = jnp.where(tile_ids < n_tiles, tile_expert, last).astype(jnp.int32)
    return row_token, row_w, dest.reshape(n, TOP_K), tile_expert, n_tiles.reshape(1)


def _alibi_slope_rows():
    h = np.arange(1, N_HEADS + 1, dtype=np.float64)
    s = np.exp2(-8.0 * h / N_HEADS).astype(np.float32).reshape(N_KV, 1, Q_PER_KV, 1)
    rows = np.broadcast_to(s, (N_KV, SUBLANES, Q_PER_KV, Q_BLOCK)).reshape(N_KV, SUBLANES, GQ_WIDTH)
    return jnp.asarray(rows)


def _overlap_t(seq):
    ncp = seq // CMP_STRIDE
    nslc = seq // SLC_BLOCK
    cs = np.arange(ncp)[None, :] * CMP_STRIDE
    ss = np.arange(nslc)[:, None] * SLC_BLOCK
    ov = (cs < ss + SLC_BLOCK) & (cs + CMP_BLOCK > ss) & (np.arange(ncp)[None, :] < ncp - 1)
    return jnp.asarray(ov.astype(np.float32)).astype(BF16)


def _odd_layer(x2d, c, ada_w, ada_b, mix_pre_g, mix_post_g, ffn_pre_g, ffn_post_g,
               w_in, cmp_k_pe, cmp_k_w1, cmp_k_w2, cmp_v_pe, cmp_v_w1, cmp_v_w2, w_out,
               router_w, exp_w_gate, exp_w_up, exp_w_down, *, bsz, seq):
    n, d = x2d.shape
    sh_m, sc_m, gt_m, sh_f, sc_f, gt_f = _ada_call(c, ada_w, ada_b)
    nmain = Q_WIDTH + 6 * KV_WIDTH
    w_main = w_in[:, :nmain].astype(BF16)
    w_gates = jnp.pad(w_in[:, nmain:], ((0, 0), (0, LANES - 3 * N_HEADS))).astype(BF16)
    proj, gates = _proj_call(x2d, mix_pre_g, sh_m, sc_m, w_main, w_gates, seq=seq,
                             q_cols=Q_WIDTH, q_scale=HEAD_DIM ** -0.5, name="l1_in_proj")

    nqb = seq // Q_BLOCK
    ncp = seq // CMP_STRIDE

    def kv_cols(k):
        o = Q_WIDTH + k * KV_WIDTH
        return proj[:, o:o + KV_WIDTH]

    def chunked(t):
        t = t.reshape(bsz, ncp, CMP_STRIDE, N_KV, HEAD_DIM).transpose(0, 3, 1, 2, 4)
        return t.reshape(bsz * N_KV, ncp, CMP_STRIDE * HEAD_DIM)

    def key_major_t(t):
        t = t.reshape(bsz, nqb, Q_BLOCK, N_KV, HEAD_DIM).transpose(0, 3, 1, 4, 2)
        return t.reshape(bsz * N_KV, nqb, HEAD_DIM, Q_BLOCK)

    kcmp, _ = _compress_call(chunked(kv_cols(0)), cmp_k_pe, cmp_k_w1, cmp_k_w2)
    _, vcmpt = _compress_call(chunked(kv_cols(1)), cmp_v_pe, cmp_v_w1, cmp_v_w2)
    slopes = _alibi_slope_rows()
    oc, sel = _attn_cmp_call(proj, kcmp, vcmpt, _overlap_t(seq), slopes, bsz=bsz, seq=seq)

    gt = gates[:, :3 * N_HEADS].reshape(bsz, nqb, Q_BLOCK, 3, N_KV, Q_PER_KV).transpose(0, 4, 1, 3, 5, 2)
    gt = gt.reshape(bsz, N_KV, nqb, 3, GQ_WIDTH)
    gt = jnp.pad(gt, ((0, 0), (0, 0), (0, 0), (0, SUBLANES - 3), (0, 0)))
    attn = _attn_sel_call(proj, key_major_t(kv_cols(3)), key_major_t(kv_cols(5)), sel, oc, gt, slopes,
                          bsz=bsz, seq=seq)

    x2, h2, rt = _outproj_router_call(attn, x2d, w_out.astype(BF16), mix_post_g, gt_m,
                                      ffn_pre_g, sh_f, sc_f, router_w, seq=seq)

    tm = min(1024, n)
    row_token, row_w, dest, tile_expert, n_tiles = _moe_routing(rt, tm=tm)
    hs = jnp.take(h2, row_token, axis=0)
    rows = _moe_call(tile_expert, n_tiles, hs, row_w.reshape(-1, 1), exp_w_gate.astype(BF16),
                     exp_w_up.astype(BF16), exp_w_down.astype(BF16), tm=tm)
    y0 = jnp.take(rows, dest[:, 0], axis=0)
    y1 = jnp.take(rows, dest[:, 1], axis=0)
    return _residual_norm_call(x2, y0, y1, ffn_post_g, gt_f, seq=seq)


def kernel(x, c, l0_ada_w, l0_ada_b, l0_mix_pre_g, l0_mix_post_g, l0_ffn_pre_g, l0_ffn_post_g, l0_w_in, l0_v_ln_g, l0_v_ln_b, l0_w_spatial, l0_b_spatial, l0_conv_w, l0_w_out, l0_ffn_w_gate, l0_ffn_w_up, l0_ffn_w_down, l1_ada_w, l1_ada_b, l1_mix_pre_g, l1_mix_post_g, l1_ffn_pre_g, l1_ffn_post_g, l1_w_in, l1_cmp_k_pe, l1_cmp_k_w1, l1_cmp_k_w2, l1_cmp_v_pe, l1_cmp_v_w1, l1_cmp_v_w2, l1_w_out, l1_router_w, l1_exp_w_gate, l1_exp_w_up, l1_exp_w_down):
    bsz, seq, d = x.shape
    x2d = x.reshape(bsz * seq, d)
    x2d = _even_layer(x2d, c, l0_ada_w, l0_ada_b, l0_mix_pre_g, l0_mix_post_g, l0_ffn_pre_g, l0_ffn_post_g,
                      l0_w_in, l0_v_ln_g, l0_v_ln_b, l0_w_spatial, l0_b_spatial, l0_conv_w, l0_w_out,
                      l0_ffn_w_gate, l0_ffn_w_up, l0_ffn_w_down, seq=seq)
    x2d = _odd_layer(x2d, c, l1_ada_w, l1_ada_b, l1_mix_pre_g, l1_mix_post_g, l1_ffn_pre_g, l1_ffn_post_g,
                     l1_w_in, l1_cmp_k_pe, l1_cmp_k_w1, l1_cmp_k_w2, l1_cmp_v_pe, l1_cmp_v_w1, l1_cmp_v_w2,
                     l1_w_out, l1_router_w, l1_exp_w_gate, l1_exp_w_up, l1_exp_w_down, bsz=bsz, seq=seq)
    return x2d.reshape(bsz, seq, d)
```

```python
import functools
import math

import numpy as np
import jax
import jax.numpy as jnp
from jax import lax
from jax.experimental import pallas as pl
from jax.experimental.pallas import tpu as pltpu

F32 = jnp.float32
BF16 = jnp.bfloat16
HIGHEST = lax.Precision.HIGHEST

EPS = 1e-6
NEG_INF = -1e30
FORCE_SCORE = 1e4

LANES = 128
SUBLANES = 8
VMEM_LIMIT_BYTES = 56 * 1024 * 1024

A_GROUPS = 8
A_WIDTH = 1024
B_WIDTH = 1024
CHUNK = 128
CONV_W = 3
N_HEADS = 16
HEAD_DIM = 128
N_KV = 4
Q_PER_KV = N_HEADS // N_KV
Q_WIDTH = N_HEADS * HEAD_DIM
KV_WIDTH = N_KV * HEAD_DIM
GQ_WIDTH = Q_PER_KV * HEAD_DIM
CMP_BLOCK = 32
CMP_STRIDE = 16
CMP_HIDDEN = 256
SLC_BLOCK = 64
N_SELECT = 16
WINDOW = 512
Q_BLOCK = 128
N_EXPERTS = 8
TOP_K = 2


def _cparams(semantics):
    return pltpu.CompilerParams(dimension_semantics=semantics, vmem_limit_bytes=VMEM_LIMIT_BYTES)


def _sigmoid(x):
    return 1.0 / (1.0 + jnp.exp(-x))


def _gelu_tanh(x):
    return 0.5 * x * (1.0 + jnp.tanh(0.7978845608028654 * (x + 0.044715 * (x * x * x))))


def _rms(x):
    return x * lax.rsqrt(jnp.mean(x * x, axis=-1, keepdims=True) + EPS)


def _tile(n, pref):
    if n <= pref:
        return n
    t = (pref // LANES) * LANES
    while n % t:
        t -= LANES
    assert t > 0, (n, pref)
    return t


def _ada_kernel(c_ref, w_ref, b_ref, o_ref):
    c = c_ref[...]
    s = c * _sigmoid(c)
    o_ref[...] = jnp.dot(s, w_ref[...], preferred_element_type=F32, precision=HIGHEST) + b_ref[...]


def _ada_call(c, w, b):
    bsz, d = c.shape
    n = w.shape[1]
    tn = _tile(n, 1024)
    cp = jnp.pad(c, ((0, SUBLANES - bsz), (0, 0)))
    out = pl.pallas_call(
        _ada_kernel,
        out_shape=jax.ShapeDtypeStruct((SUBLANES, n), F32),
        grid=(n // tn,),
        in_specs=[pl.BlockSpec((SUBLANES, d), lambda j: (0, 0)),
                  pl.BlockSpec((d, tn), lambda j: (0, j)),
                  pl.BlockSpec((1, tn), lambda j: (0, j))],
        out_specs=pl.BlockSpec((SUBLANES, tn), lambda j: (0, j)),
        compiler_params=_cparams(("parallel",)),
        name="ada_modulation",
    )(cp, w, b.reshape(1, n))
    m = out[:bsz]
    return [t.reshape(bsz, 1, d) for t in jnp.split(m, 6, axis=-1)]


def _proj_kernel(*refs, q_tiles, q_scale, with_gates):
    if with_gates:
        x_ref, g_ref, sh_ref, sc_ref, w_ref, wg_ref, o_ref, og_ref, h_ref = refs
    else:
        x_ref, g_ref, sh_ref, sc_ref, w_ref, o_ref, h_ref = refs
    j = pl.program_id(1)

    @pl.when(j == 0)
    def _():
        h = _rms(x_ref[...]) * g_ref[...]
        h = h * (1.0 + sc_ref[0]) + sh_ref[0]
        h_ref[...] = h.astype(BF16)
        if with_gates:
            og_ref[...] = _sigmoid(jnp.dot(h_ref[...], wg_ref[...], preferred_element_type=F32))

    acc = jnp.dot(h_ref[...], w_ref[...], preferred_element_type=F32)
    if q_tiles:
        acc = acc * jnp.where(j < q_tiles, jnp.float32(q_scale), jnp.float32(1.0))
    o_ref[...] = acc.astype(o_ref.dtype)


def _proj_call(x, g, shift, scale, w, wg=None, *, seq, q_cols=0, q_scale=1.0, name):
    n, d = x.shape
    nout = w.shape[1]
    tm = _tile(seq, 1024)
    tn = _tile(nout, 1024)
    assert q_cols % tn == 0
    tpb = seq // tm
    with_gates = wg is not None
    in_specs = [pl.BlockSpec((tm, d), lambda i, j: (i, 0)),
                pl.BlockSpec((1, d), lambda i, j: (0, 0)),
                pl.BlockSpec((1, 1, d), lambda i, j: (i // tpb, 0, 0)),
                pl.BlockSpec((1, 1, d), lambda i, j: (i // tpb, 0, 0)),
                pl.BlockSpec((d, tn), lambda i, j: (0, j))]
    args = [x, g.reshape(1, d), shift, scale, w]
    out_shape = [jax.ShapeDtypeStruct((n, nout), BF16)]
    out_specs = [pl.BlockSpec((tm, tn), lambda i, j: (i, j))]
    if with_gates:
        in_specs.append(pl.BlockSpec((d, LANES), lambda i, j: (0, 0)))
        args.append(wg)
        out_shape.append(jax.ShapeDtypeStruct((n, LANES), F32))
        out_specs.append(pl.BlockSpec((tm, LANES), lambda i, j: (i, 0)))
    out = pl.pallas_call(
        functools.partial(_proj_kernel, q_tiles=q_cols // tn, q_scale=q_scale, with_gates=with_gates),
        out_shape=out_shape,
        grid=(n // tm, nout // tn),
        in_specs=in_specs,
        out_specs=out_specs,
        scratch_shapes=[pltpu.VMEM((tm, d), BF16)],
        compiler_params=_cparams(("parallel", "arbitrary")),
        name=name,
    )(*args)
    return out if with_gates else out[0]


def _mixer0_kernel(p_ref, pgc_ref, pxb_ref, x_ref, vg_ref, vb_ref, wsp_ref, bspt_ref, cw_ref,
                   wout_ref, gpost_ref, gate_ref, o_ref, cat_ref, *, tm, tiles_per_batch):
    i = pl.program_id(0)
    nchunk = tm // CHUNK
    v = _gelu_tanh(p_ref[:, A_WIDTH:2 * A_WIDTH].astype(F32))
    mu = jnp.mean(v, axis=-1, keepdims=True)
    vc = v - mu
    var = jnp.mean(vc * vc, axis=-1, keepdims=True)
    vn = (vc * lax.rsqrt(var + EPS) * vg_ref[...] + vb_ref[...]).astype(BF16)
    row = lax.broadcasted_iota(jnp.int32, (CHUNK, CHUNK), 0)
    col = lax.broadcasted_iota(jnp.int32, (CHUNK, CHUNK), 1)
    causal = col <= row
    for h in range(A_GROUPS):
        cs = slice(h * LANES, (h + 1) * LANES)
        w = jnp.where(causal, wsp_ref[h], 0.0).astype(BF16)
        rhs = jnp.concatenate([vn[c * CHUNK:(c + 1) * CHUNK, cs] for c in range(nchunk)], axis=1)
        z = jnp.dot(w, rhs, preferred_element_type=F32) + bspt_ref[:, h:h + 1]
        for c in range(nchunk):
            rs = slice(c * CHUNK, (c + 1) * CHUNK)
            u = _gelu_tanh(p_ref[rs, cs].astype(F32))
            cat_ref[rs, cs] = (u * z[:, c * CHUNK:(c + 1) * CHUNK]).astype(BF16)
    o_gb, o_gc, o_xb = 2 * A_WIDTH, 2 * A_WIDTH + B_WIDTH, 2 * A_WIDTH + 2 * B_WIDTH
    zc = p_ref[:, o_gc:o_gc + B_WIDTH].astype(F32) * p_ref[:, o_xb:o_xb + B_WIDTH].astype(F32)
    prev = pgc_ref[...].astype(F32) * pxb_ref[...].astype(F32)
    prev = jnp.where(i % tiles_per_batch == 0, 0.0, prev)
    nprev = prev.shape[0]
    p1 = prev[nprev - 1:nprev, :]
    p2 = prev[nprev - 2:nprev - 1, :]
    rowi = lax.broadcasted_iota(jnp.int32, (tm, B_WIDTH), 0)
    r1 = jnp.where(rowi == 0, p1, pltpu.roll(zc, 1, 0))
    r2 = jnp.where(rowi == 0, p2, jnp.where(rowi == 1, p1, pltpu.roll(zc, 2, 0)))
    y = cw_ref[0:1, :] * r2 + cw_ref[1:2, :] * r1 + cw_ref[2:3, :] * zc
    cat_ref[:, A_WIDTH:] = (p_ref[:, o_gb:o_gb + B_WIDTH].astype(F32) * y).astype(BF16)
    yo = jnp.dot(cat_ref[...], wout_ref[...], preferred_element_type=F32)
    o_ref[...] = x_ref[...] + gate_ref[0] * (_rms(yo) * gpost_ref[...])


def _mixer0_call(proj, x, v_ln_g, v_ln_b, w_spatial, b_spatial, conv_w, w_out, g_post, gate, *, seq):
    n, d = x.shape
    tm = _tile(seq, 512)
    tpb = seq // tm
    pr = 16
    rpb = tm // pr
    wcat = A_WIDTH + B_WIDTH
    cwp = jnp.pad(conv_w, ((0, SUBLANES - CONV_W), (0, 0)))
    return pl.pallas_call(
        functools.partial(_mixer0_kernel, tm=tm, tiles_per_batch=tpb),
        out_shape=jax.ShapeDtypeStruct((n, d), F32),
        grid=(n // tm,),
        in_specs=[pl.BlockSpec((tm, proj.shape[1]), lambda i: (i, 0)),
                  pl.BlockSpec((pr, B_WIDTH), lambda i: (jnp.maximum(i * rpb - 1, 0), 3)),
                  pl.BlockSpec((pr, B_WIDTH), lambda i: (jnp.maximum(i * rpb - 1, 0), 4)),
                  pl.BlockSpec((tm, d), lambda i: (i, 0)),
                  pl.BlockSpec((1, A_WIDTH), lambda i: (0, 0)),
                  pl.BlockSpec((1, A_WIDTH), lambda i: (0, 0)),
                  pl.BlockSpec((A_GROUPS, CHUNK, CHUNK), lambda i: (0, 0, 0)),
                  pl.BlockSpec((CHUNK, A_GROUPS), lambda i: (0, 0)),
                  pl.BlockSpec((SUBLANES, B_WIDTH), lambda i: (0, 0)),
                  pl.BlockSpec((wcat, d), lambda i: (0, 0)),
                  pl.BlockSpec((1, d), lambda i: (0, 0)),
                  pl.BlockSpec((1, 1, d), lambda i: (i // tpb, 0, 0))],
        out_specs=pl.BlockSpec((tm, d), lambda i: (i, 0)),
        scratch_shapes=[pltpu.VMEM((tm, wcat), BF16)],
        compiler_params=_cparams(("parallel",)),
        name="l0_mixer",
    )(proj, proj, proj, x, v_ln_g.reshape(1, -1), v_ln_b.reshape(1, -1), w_spatial, b_spatial.T, cwp,
      w_out, g_post.reshape(1, d), gate)


def _ffn_kernel(x_ref, gpre_ref, sh_ref, sc_ref, wg_ref, wu_ref, wd_ref, gpost_ref, gate_ref, o_ref, h_ref):
    j = pl.program_id(1)

    @pl.when(j == 0)
    def _():
        h = _rms(x_ref[...]) * gpre_ref[...]
        h_ref[...] = (h * (1.0 + sc_ref[0]) + sh_ref[0]).astype(BF16)
        o_ref[...] = jnp.zeros_like(o_ref)

    h = h_ref[...]
    g = jnp.dot(h, wg_ref[...], preferred_element_type=F32)
    u = jnp.dot(h, wu_ref[...], preferred_element_type=F32)
    a = (g * _sigmoid(g) * u).astype(BF16)
    o_ref[...] += jnp.dot(a, wd_ref[...], preferred_element_type=F32)

    @pl.when(j == pl.num_programs(1) - 1)
    def _():
        o_ref[...] = x_ref[...] + gate_ref[0] * (_rms(o_ref[...]) * gpost_ref[...])


def _ffn_call(x, g_pre, shift, scale, wg, wu, wd, g_post, gate, *, seq):
    n, d = x.shape
    f = wg.shape[1]
    tm = _tile(seq, 1024)
    tf = _tile(f, 256)
    tpb = seq // tm
    return pl.pallas_call(
        _ffn_kernel,
        out_shape=jax.ShapeDtypeStruct((n, d), F32),
        grid=(n // tm, f // tf),
        in_specs=[pl.BlockSpec((tm, d), lambda i, j: (i, 0)),
                  pl.BlockSpec((1, d), lambda i, j: (0, 0)),
                  pl.BlockSpec((1, 1, d), lambda i, j: (i // tpb, 0, 0)),
                  pl.BlockSpec((1, 1, d), lambda i, j: (i // tpb, 0, 0)),
                  pl.BlockSpec((d, tf), lambda i, j: (0, j)),
                  pl.BlockSpec((d, tf), lambda i, j: (0, j)),
                  pl.BlockSpec((tf, d), lambda i, j: (j, 0)),
                  pl.BlockSpec((1, d), lambda i, j: (0, 0)),
                  pl.BlockSpec((1, 1, d), lambda i, j: (i // tpb, 0, 0))],
        out_specs=pl.BlockSpec((tm, d), lambda i, j: (i, 0)),
        scratch_shapes=[pltpu.VMEM((tm, d), BF16)],
        compiler_params=_cparams(("parallel", "arbitrary")),
        name="l0_ffn",
    )(x, g_pre.reshape(1, d), shift, scale, wg, wu, wd, g_post.reshape(1, d), gate)


def _compress_kernel(x_ref, pe_ref, w1_ref, w2_ref, o_ref, ot_ref):
    half = CMP_STRIDE * HEAD_DIM
    x = x_ref[0]
    nc = x.shape[0]
    a = jnp.dot(x, w1_ref[0:half, :], preferred_element_type=F32)
    b = jnp.dot(x, w1_ref[half:2 * half, :], preferred_element_type=F32)
    bias = jnp.dot(pe_ref[...], w1_ref[...], preferred_element_type=F32)[0:1, :]
    hid = _gelu_tanh(a + pltpu.roll(b, nc - 1, 0) + bias)
    out = jnp.dot(hid.astype(BF16), w2_ref[...], preferred_element_type=F32)
    rowi = lax.broadcasted_iota(jnp.int32, out.shape, 0)
    out = jnp.where(rowi < nc - 1, out, 0.0)
    o_ref[0] = out.astype(BF16)
    ot_ref[0] = out.T.astype(BF16)


def _compress_call(xc, pe, w1, w2):
    bg, nc, kdim = xc.shape
    pe_flat = jnp.pad(pe.reshape(1, -1), ((0, SUBLANES - 1), (0, 0))).astype(BF16)
    w1f = w1.reshape(CMP_BLOCK * HEAD_DIM, CMP_HIDDEN).astype(BF16)
    return pl.pallas_call(
        _compress_kernel,
        out_shape=[jax.ShapeDtypeStruct((bg, nc, HEAD_DIM), BF16),
                   jax.ShapeDtypeStruct((bg, HEAD_DIM, nc), BF16)],
        grid=(bg,),
        in_specs=[pl.BlockSpec((1, nc, kdim), lambda i: (i, 0, 0)),
                  pl.BlockSpec((SUBLANES, CMP_BLOCK * HEAD_DIM), lambda i: (0, 0)),
                  pl.BlockSpec((CMP_BLOCK * HEAD_DIM, CMP_HIDDEN), lambda i: (0, 0)),
                  pl.BlockSpec((CMP_HIDDEN, HEAD_DIM), lambda i: (0, 0))],
        out_specs=[pl.BlockSpec((1, nc, HEAD_DIM), lambda i: (i, 0, 0)),
                   pl.BlockSpec((1, HEAD_DIM, nc), lambda i: (i, 0, 0))],
        compiler_params=_cparams(("parallel",)),
        name="nsa_compress",
    )(xc, pe_flat, w1f, w2.astype(BF16))


def _q_transposed(q_ref):
    parts = []
    for r in range(Q_PER_KV):
        parts.append(q_ref[:, r * HEAD_DIM:(r + 1) * HEAD_DIM].astype(F32).T.astype(BF16))
    return jnp.concatenate(parts, axis=1)


def _attn_cmp_kernel(q_ref, kc_ref, vct_ref, ovt_ref, slope_ref, oc_ref, sel_ref, cnt_ref, imp_ref,
                     *, n_sel, n_class):
    qb = pl.program_id(2)
    q0 = qb * Q_BLOCK
    qt = _q_transposed(q_ref)
    ncp = kc_ref.shape[1]
    nslc = ovt_ref.shape[0]
    rows_per_class = ncp // n_class
    visible = (Q_BLOCK // CMP_STRIDE) * (qb + 1) - 1
    cls = jnp.minimum((visible - 1) // rows_per_class, n_class - 1)

    def compressed(n):
        s_all = jnp.dot(kc_ref[0, 0:n, :], qt, preferred_element_type=F32)
        isub = lax.broadcasted_iota(jnp.int32, (n, Q_BLOCK), 0)
        tl = lax.broadcasted_iota(jnp.int32, (n, Q_BLOCK), 1) + q0
        dist = tl - (isub * CMP_STRIDE + (CMP_BLOCK - 1))
        valid = dist >= 0
        distf = dist.astype(F32)
        psum = jnp.zeros((n, Q_BLOCK), F32)
        pparts = []
        for r in range(Q_PER_KV):
            ls = slice(r * Q_BLOCK, (r + 1) * Q_BLOCK)
            s = s_all[:, ls] - slope_ref[0, 0:1, ls] * distf
            s = jnp.where(valid, s, NEG_INF)
            m = jnp.max(s, axis=0, keepdims=True)
            e = jnp.exp(s - m)
            p = jnp.where(valid, e / jnp.sum(e, axis=0, keepdims=True), 0.0)
            psum = psum + p
            pparts.append(p.astype(BF16))
        pt = jnp.concatenate(pparts, axis=1)
        oc_ref[0, 0, 0] = jnp.dot(vct_ref[0, :, 0:n], pt, preferred_element_type=F32)
        hi = psum.astype(BF16)
        lo = (psum - hi.astype(F32)).astype(BF16)
        ovt = ovt_ref[:, 0:n]
        imp_ref[...] = (jnp.dot(ovt, hi, preferred_element_type=F32)
                        + jnp.dot(ovt, lo, preferred_element_type=F32))

    for k in range(n_class):
        pl.when(cls == k)(functools.partial(compressed, (k + 1) * rows_per_class))

    jb = lax.broadcasted_iota(jnp.int32, (nslc, Q_BLOCK), 0)
    cur = (lax.broadcasted_iota(jnp.int32, (nslc, Q_BLOCK), 1) + q0) // SLC_BLOCK
    forced = (jb == 0) | (jb == cur) | (jb == cur - 1)
    score = jnp.where(forced, FORCE_SCORE, imp_ref[...])
    score = jnp.where(jb <= cur, score, -1.0)
    picked = -3e38
    three_forced = qb > 0
    score = jnp.where(forced & three_forced, picked, score)

    def pick_next(_, work):
        m = jnp.max(work, axis=0, keepdims=True)
        first = jnp.min(jnp.where(work == m, jb, nslc), axis=0, keepdims=True)
        return jnp.where(jb == first, picked, work)

    work = lax.fori_loop(0, jnp.where(three_forced, n_sel - 3, n_sel), pick_next, score)
    chosen = (work == picked) & (jb <= cur)
    sel_ref[0, 0, 0] = jnp.where(chosen, 0.0, NEG_INF)
    flags = jnp.where(chosen, 1.0, 0.0).astype(BF16)
    cnt_ref[0, 0, 0] = lax.dot_general(jnp.ones((SUBLANES, Q_BLOCK), BF16), flags, (((1,), (1,)), ((), ())),
                                       preferred_element_type=F32)


def _attn_cmp_call(proj, kcmp, vcmpt, ovt, slopes, *, bsz, seq):
    nqb = seq // Q_BLOCK
    ncp = kcmp.shape[1]
    nslc = seq // SLC_BLOCK
    n_sel = min(N_SELECT, nslc)
    return pl.pallas_call(
        functools.partial(_attn_cmp_kernel, n_sel=n_sel, n_class=4),
        out_shape=[jax.ShapeDtypeStruct((bsz, N_KV, nqb, HEAD_DIM, GQ_WIDTH), F32),
                   jax.ShapeDtypeStruct((bsz, N_KV, nqb, nslc, Q_BLOCK), F32),
                   jax.ShapeDtypeStruct((bsz, N_KV, nqb, SUBLANES, nslc), F32)],
        grid=(bsz, N_KV, nqb),
        in_specs=[pl.BlockSpec((Q_BLOCK, GQ_WIDTH), lambda b, g, q: (b * nqb + q, g)),
                  pl.BlockSpec((1, ncp, HEAD_DIM), lambda b, g, q: (b * N_KV + g, 0, 0)),
                  pl.BlockSpec((1, HEAD_DIM, ncp), lambda b, g, q: (b * N_KV + g, 0, 0)),
                  pl.BlockSpec((nslc, ncp), lambda b, g, q: (0, 0)),
                  pl.BlockSpec((1, SUBLANES, GQ_WIDTH), lambda b, g, q: (g, 0, 0))],
        out_specs=[pl.BlockSpec((1, 1, 1, HEAD_DIM, GQ_WIDTH), lambda b, g, q: (b, g, q, 0, 0)),
                   pl.BlockSpec((1, 1, 1, nslc, Q_BLOCK), lambda b, g, q: (b, g, q, 0, 0)),
                   pl.BlockSpec((1, 1, 1, SUBLANES, nslc), lambda b, g, q: (b, g, q, 0, 0))],
        scratch_shapes=[pltpu.VMEM((nslc, Q_BLOCK), F32)],
        compiler_params=_cparams(("parallel", "parallel", "arbitrary")),
        name="nsa_compressed_select",
    )(proj, kcmp, vcmpt, ovt, slopes)


N_WTILES = WINDOW // Q_BLOCK + 1
HALF = GQ_WIDTH // 2
SEL_TILES = 4


def _attn_sel_kernel(bits_ref, q_ref, ks_ref, vst_ref, kw_ref, vwt_ref, selb_ref, oc_ref, gates_ref, slope_ref,
                     wb_ref, o_ref, qt_ref, m_ref, l_ref, acc_ref, os_ref, list_ref, *, words):
    b = pl.program_id(0)
    g = pl.program_id(1)
    qb = pl.program_id(2)
    nqb = pl.num_programs(2)
    qt_ref[...] = _q_transposed(q_ref)

    def reset():
        m_ref[...] = jnp.full(m_ref.shape, NEG_INF, F32)
        l_ref[...] = jnp.zeros(l_ref.shape, F32)
        acc_ref[...] = jnp.zeros(acc_ref.shape, F32)

    def key_tile(ref, t):
        return ref[0, pl.ds(pl.multiple_of(t * Q_BLOCK, Q_BLOCK), Q_BLOCK), :]

    def step(k_ref, vt_ref, tiles, widx, use_selection):
        nt = len(tiles)
        ks = [key_tile(k_ref, t) for t in tiles]
        vt = jnp.concatenate([vt_ref[0, t] for t in tiles], axis=1)
        for h in range(2):
            cs = slice(h * HALF, (h + 1) * HALF)
            us, offs = [], []
            mx = None
            for j in range(nt):
                s = jnp.dot(ks[j], qt_ref[:, cs], preferred_element_type=F32)
                u = s + wb_ref[0, widx[j], :, cs]
                if use_selection:
                    two = selb_ref[0, 0, 0, tiles[j]]
                    lo = jnp.broadcast_to(jnp.concatenate([two[0:1, :]] * 2, axis=1), (SLC_BLOCK, HALF))
                    hi = jnp.broadcast_to(jnp.concatenate([two[1:2, :]] * 2, axis=1), (SLC_BLOCK, HALF))
                    u = u + jnp.concatenate([lo, hi], axis=0)
                c = slope_ref[0, 0:1, cs] * ((tiles[j] - qb) * Q_BLOCK).astype(F32)
                cm = jnp.max(u, axis=0, keepdims=True) + c
                mx = cm if mx is None else jnp.maximum(mx, cm)
                us.append(u)
                offs.append(c)
            m_old = m_ref[0:1, cs]
            m_new = jnp.maximum(m_old, mx)
            es = [jnp.exp(us[j] + (offs[j] - m_new)) for j in range(nt)]
            lsum = es[0].sum(axis=0, keepdims=True)
            for j in range(1, nt):
                lsum = lsum + es[j].sum(axis=0, keepdims=True)
            e = jnp.concatenate([x.astype(BF16) for x in es], axis=0)
            alpha = jnp.exp(m_old - m_new)
            l_ref[0:1, cs] = alpha * l_ref[0:1, cs] + lsum
            acc_ref[:, cs] = alpha * acc_ref[:, cs] + jnp.dot(vt, e, preferred_element_type=F32)
            m_ref[0:1, cs] = m_new

    base = ((b * N_KV + g) * nqb + qb) * words

    def collect(p, n):
        list_ref[n] = p
        return n + ((bits_ref[base + (p >> 5)] >> (p & 31)) & 1)

    n_act = lax.fori_loop(0, qb, collect, 0)
    list_ref[n_act] = qb
    for j in range(1, SEL_TILES):
        list_ref[n_act + j] = -1

    reset()

    def sel_body(s, carry):
        idx = [list_ref[s * SEL_TILES + j] for j in range(SEL_TILES)]
        tiles = [jnp.maximum(i, 0) for i in idx]
        widx = [jnp.where(i < 0, N_WTILES, jnp.where(i == qb, N_WTILES - 1, 1)) for i in idx]
        step(ks_ref, vst_ref, tiles, widx, True)
        return carry

    lax.fori_loop(0, (n_act + SEL_TILES) // SEL_TILES, sel_body, 0)
    os_ref[...] = acc_ref[...] / l_ref[0:1, :]

    reset()
    kts = [qb - (N_WTILES - 1) + w for w in range(N_WTILES)]
    step(kw_ref, vwt_ref, [jnp.maximum(kt, 0) for kt in kts],
         [jnp.where(kt < 0, N_WTILES, w) for w, kt in enumerate(kts)], False)

    gt = gates_ref[0, 0, 0]
    ot = (gt[0:1, :] * oc_ref[0, 0, 0] + gt[1:2, :] * os_ref[...]
          + gt[2:3, :] * (acc_ref[...] / l_ref[0:1, :]))
    for r in range(Q_PER_KV):
        cs = slice(r * HEAD_DIM, (r + 1) * HEAD_DIM)
        o_ref[:, cs] = ot[:, cs].T.astype(o_ref.dtype)


def _window_bias_tiles():
    h = np.arange(1, N_HEADS + 1, dtype=np.float64)
    slope = np.exp2(-8.0 * h / N_HEADS).astype(np.float32).reshape(N_KV, 1, 1, Q_PER_KV, 1)
    sub = np.arange(Q_BLOCK, dtype=np.float32).reshape(1, 1, Q_BLOCK, 1, 1)
    tq = np.arange(Q_BLOCK).reshape(1, 1, 1, 1, Q_BLOCK)
    ks = np.arange(Q_BLOCK).reshape(1, 1, Q_BLOCK, 1, 1)
    mask = np.zeros((1, N_WTILES + 1, Q_BLOCK, 1, Q_BLOCK), np.float32)
    mask[0, 0] = np.where(tq[0, 0] < ks[0, 0], 0.0, NEG_INF)
    mask[0, N_WTILES - 1] = np.where(tq[0, 0] >= ks[0, 0], 0.0, NEG_INF)
    mask[0, N_WTILES] = NEG_INF
    out = (slope * sub).astype(np.float32) + mask
    return jnp.asarray(out.reshape(N_KV, N_WTILES + 1, Q_BLOCK, GQ_WIDTH))


def _attn_sel_call(bits, proj, vst, vwt, selb, oc, gates_t, slopes, wbias, *, bsz, seq):
    nqb = seq // Q_BLOCK
    n = bsz * seq
    proj3 = proj.reshape(bsz, seq, proj.shape[1])
    ks_col = (Q_WIDTH + 2 * KV_WIDTH) // HEAD_DIM
    kw_col = (Q_WIDTH + 4 * KV_WIDTH) // HEAD_DIM
    selb5 = selb.reshape(bsz, N_KV, nqb, nqb, 2, Q_BLOCK)
    words = bits.shape[0] // (bsz * N_KV * nqb)
    grid_spec = pltpu.PrefetchScalarGridSpec(
        num_scalar_prefetch=1,
        grid=(bsz, N_KV, nqb),
        in_specs=[pl.BlockSpec((Q_BLOCK, GQ_WIDTH), lambda b, g, q, s: (b * nqb + q, g)),
                  pl.BlockSpec((1, seq, HEAD_DIM), lambda b, g, q, s: (b, 0, ks_col + g)),
                  pl.BlockSpec((1, nqb, HEAD_DIM, Q_BLOCK), lambda b, g, q, s: (b * N_KV + g, 0, 0, 0)),
                  pl.BlockSpec((1, seq, HEAD_DIM), lambda b, g, q, s: (b, 0, kw_col + g)),
                  pl.BlockSpec((1, nqb, HEAD_DIM, Q_BLOCK), lambda b, g, q, s: (b * N_KV + g, 0, 0, 0)),
                  pl.BlockSpec((1, 1, 1, nqb, 2, Q_BLOCK), lambda b, g, q, s: (b, g, q, 0, 0, 0)),
                  pl.BlockSpec((1, 1, 1, HEAD_DIM, GQ_WIDTH), lambda b, g, q, s: (b, g, q, 0, 0)),
                  pl.BlockSpec((1, 1, 1, SUBLANES, GQ_WIDTH), lambda b, g, q, s: (b, g, q, 0, 0)),
                  pl.BlockSpec((1, SUBLANES, GQ_WIDTH), lambda b, g, q, s: (g, 0, 0)),
                  pl.BlockSpec((1, N_WTILES + 1, Q_BLOCK, GQ_WIDTH), lambda b, g, q, s: (g, 0, 0, 0))],
        out_specs=pl.BlockSpec((Q_BLOCK, GQ_WIDTH), lambda b, g, q, s: (b * nqb + q, g)),
        scratch_shapes=[pltpu.VMEM((HEAD_DIM, GQ_WIDTH), BF16),
                        pltpu.VMEM((SUBLANES, GQ_WIDTH), F32),
                        pltpu.VMEM((SUBLANES, GQ_WIDTH), F32),
                        pltpu.VMEM((HEAD_DIM, GQ_WIDTH), F32),
                        pltpu.VMEM((HEAD_DIM, GQ_WIDTH), F32),
                        pltpu.SMEM((nqb + SEL_TILES,), jnp.int32)],
    )
    return pl.pallas_call(
        functools.partial(_attn_sel_kernel, words=words),
        out_shape=jax.ShapeDtypeStruct((n, Q_WIDTH), BF16),
        grid_spec=grid_spec,
        compiler_params=_cparams(("parallel", "parallel", "arbitrary")),
        name="nsa_selected_window",
    )(bits, proj, proj3, vst, proj3, vwt, selb5, oc, gates_t, slopes, wbias)


def _active_pair_bits(cnt, *, bsz, seq):
    nqb = seq // Q_BLOCK
    words = (nqb + 31) // 32
    act = (cnt[:, :, :, 0, :] > 0.5).reshape(bsz, N_KV, nqb, nqb, 2).any(axis=-1)
    act = jnp.pad(act, ((0, 0), (0, 0), (0, 0), (0, words * 32 - nqb))).reshape(bsz, N_KV, nqb, words, 32)
    weights = jnp.left_shift(jnp.uint32(1), jnp.arange(32, dtype=jnp.uint32))
    packed = jnp.sum(act.astype(jnp.uint32) * weights, axis=-1, dtype=jnp.uint32)
    return lax.bitcast_convert_type(packed, jnp.int32).reshape(-1)


def _outproj_router_kernel(a_ref, x_ref, w_ref, gpost_ref, gate_ref, gpre_ref, sh_ref, sc_ref, wr_ref,
                           x2_ref, h2_ref, rt_ref):
    y = jnp.dot(a_ref[...], w_ref[...], preferred_element_type=F32)
    x2 = x_ref[...] + gate_ref[0] * (_rms(y) * gpost_ref[...])
    x2_ref[...] = x2
    h = _rms(x2) * gpre_ref[...]
    h = h * (1.0 + sc_ref[0]) + sh_ref[0]
    h2_ref[...] = h.astype(BF16)
    logits = jnp.dot(h, wr_ref[...], preferred_element_type=F32, precision=HIGHEST)
    lane = lax.broadcasted_iota(jnp.int32, logits.shape, 1)
    logits = jnp.where(lane < N_EXPERTS, logits, -jnp.inf)
    v1 = jnp.max(logits, axis=-1, keepdims=True)
    i1 = jnp.min(jnp.where(logits == v1, lane, LANES), axis=-1, keepdims=True)
    rest = jnp.where(lane == i1, -jnp.inf, logits)
    v2 = jnp.max(rest, axis=-1, keepdims=True)
    i2 = jnp.min(jnp.where(rest == v2, lane, LANES), axis=-1, keepdims=True)
    e2 = jnp.exp(v2 - v1)
    w1 = 1.0 / (1.0 + e2)
    w2 = e2 / (1.0 + e2)
    out = jnp.where(lane == 0, w1, jnp.where(lane == 1, w2, 0.0))
    out = jnp.where(lane == 2, i1.astype(F32), jnp.where(lane == 3, i2.astype(F32), out))
    rt_ref[...] = out


def _outproj_router_call(a, x, w_out, g_post, gate, g_pre, shift, scale, w_router, *, seq):
    n, d = x.shape
    tm = _tile(seq, 512)
    tpb = seq // tm
    wr = jnp.pad(w_router, ((0, 0), (0, LANES - w_router.shape[1])))
    row = lambda i: (i, 0)
    fixed = lambda i: (0, 0)
    per_b = lambda i: (i // tpb, 0, 0)
    return pl.pallas_call(
        _outproj_router_kernel,
        out_shape=[jax.ShapeDtypeStruct((n, d), F32),
                   jax.ShapeDtypeStruct((n, d), BF16),
                   jax.ShapeDtypeStruct((n, LANES), F32)],
        grid=(n // tm,),
        in_specs=[pl.BlockSpec((tm, a.shape[1]), row),
                  pl.BlockSpec((tm, d), row),
                  pl.BlockSpec(w_out.shape, fixed),
                  pl.BlockSpec((1, d), fixed),
                  pl.BlockSpec((1, 1, d), per_b),
                  pl.BlockSpec((1, d), fixed),
                  pl.BlockSpec((1, 1, d), per_b),
                  pl.BlockSpec((1, 1, d), per_b),
                  pl.BlockSpec((d, LANES), fixed)],
        out_specs=[pl.BlockSpec((tm, d), row), pl.BlockSpec((tm, d), row), pl.BlockSpec((tm, LANES), row)],
        compiler_params=_cparams(("parallel",)),
        name="l1_outproj_router",
    )(a, x, w_out, g_post.reshape(1, d), gate, g_pre.reshape(1, d), shift, scale, wr)


def _moe_kernel(te_ref, nt_ref, h_ref, rw_ref, wg_ref, wu_ref, wd_ref, o_ref):
    i = pl.program_id(0)
    j = pl.program_id(1)

    @pl.when(j == 0)
    def _():
        o_ref[...] = jnp.zeros_like(o_ref)

    @pl.when(i < nt_ref[0])
    def _():
        h = h_ref[...]
        g = jnp.dot(h, wg_ref[0], preferred_element_type=F32)
        u = jnp.dot(h, wu_ref[0], preferred_element_type=F32)
        a = (g * _sigmoid(g) * u).astype(BF16)
        o_ref[...] += jnp.dot(a, wd_ref[0], preferred_element_type=F32)

    @pl.when(j == pl.num_programs(1) - 1)
    def _():
        o_ref[...] = o_ref[...] * rw_ref[...]


def _moe_call(tile_expert, n_tiles, hs, row_w, wg, wu, wd, *, tm):
    mp, d = hs.shape
    f = wg.shape[2]
    tf = _tile(f, 512)
    nj = f // tf

    def wcol(i, j, nt):
        return jnp.where(i < nt[0], j, nj - 1)

    grid_spec = pltpu.PrefetchScalarGridSpec(
        num_scalar_prefetch=2,
        grid=(mp // tm, f // tf),
        in_specs=[pl.BlockSpec((tm, d), lambda i, j, te, nt: (i, 0)),
                  pl.BlockSpec((tm, 1), lambda i, j, te, nt: (i, 0)),
                  pl.BlockSpec((1, d, tf), lambda i, j, te, nt: (te[i], 0, wcol(i, j, nt))),
                  pl.BlockSpec((1, d, tf), lambda i, j, te, nt: (te[i], 0, wcol(i, j, nt))),
                  pl.BlockSpec((1, tf, d), lambda i, j, te, nt: (te[i], wcol(i, j, nt), 0))],
        out_specs=pl.BlockSpec((tm, d), lambda i, j, te, nt: (i, 0)),
    )
    return pl.pallas_call(
        _moe_kernel,
        out_shape=jax.ShapeDtypeStruct((mp, d), F32),
        grid_spec=grid_spec,
        compiler_params=_cparams(("arbitrary", "arbitrary")),
        name="l1_moe_experts",
    )(tile_expert, n_tiles, hs, row_w, wg, wu, wd)


def _residual_norm_kernel(x_ref, y0_ref, y1_ref, gpost_ref, gate_ref, o_ref):
    y = y0_ref[...] + y1_ref[...]
    o_ref[...] = x_ref[...] + gate_ref[0] * (_rms(y) * gpost_ref[...])


def _residual_norm_call(x, y0, y1, g_post, gate, *, seq):
    n, d = x.shape
    tm = _tile(seq, 512)
    tpb = seq // tm
    row = lambda i: (i, 0)
    return pl.pallas_call(
        _residual_norm_kernel,
        out_shape=jax.ShapeDtypeStruct((n, d), F32),
        grid=(n // tm,),
        in_specs=[pl.BlockSpec((tm, d), row), pl.BlockSpec((tm, d), row), pl.BlockSpec((tm, d), row),
                  pl.BlockSpec((1, d), lambda i: (0, 0)),
                  pl.BlockSpec((1, 1, d), lambda i: (i // tpb, 0, 0))],
        out_specs=pl.BlockSpec((tm, d), row),
        compiler_params=_cparams(("parallel",)),
        name="l1_moe_residual",
    )(x, y0, y1, g_post.reshape(1, d), gate)


def _even_layer(x2d, c, ada_w, ada_b, mix_pre_g, mix_post_g, ffn_pre_g, ffn_post_g,
                w_in, v_ln_g, v_ln_b, w_spatial, b_spatial, conv_w, w_out,
                ffn_w_gate, ffn_w_up, ffn_w_down, *, seq):
    sh_m, sc_m, gt_m, sh_f, sc_f, gt_f = _ada_call(c, ada_w, ada_b)
    proj = _proj_call(x2d, mix_pre_g, sh_m, sc_m, w_in.astype(BF16), seq=seq, name="l0_in_proj")
    x2d = _mixer0_call(proj, x2d, v_ln_g, v_ln_b, w_spatial, b_spatial, conv_w, w_out.astype(BF16),
                       mix_post_g, gt_m, seq=seq)
    return _ffn_call(x2d, ffn_pre_g, sh_f, sc_f, ffn_w_gate.astype(BF16), ffn_w_up.astype(BF16),
                     ffn_w_down.astype(BF16), ffn_post_g, gt_f, seq=seq)


def _moe_routing(rt, *, tm):
    n = rt.shape[0]
    na = n * TOP_K
    mp = na + N_EXPERTS * tm
    w_flat = rt[:, 0:TOP_K].reshape(na)
    e_flat = rt[:, TOP_K:2 * TOP_K].astype(jnp.int32).reshape(na)
    onehot = (e_flat[:, None] == jnp.arange(N_EXPERTS, dtype=jnp.int32)[None, :]).astype(jnp.int32)
    csum = jnp.cumsum(onehot, axis=0)
    counts = csum[-1]
    rank = jnp.sum(onehot * csum, axis=1) - 1
    padded = ((counts + tm - 1) // tm) * tm
    ends = jnp.cumsum(padded)
    starts = ends - padded
    dest = jnp.sum(onehot * starts[None, :], axis=1) + rank
    row_token = jnp.zeros((mp,), jnp.int32).at[dest].set(jnp.arange(na, dtype=jnp.int32) // TOP_K)
    row_w = jnp.zeros((mp,), F32).at[dest].set(w_flat)
    n_tiles = (ends[-1] // tm).astype(jnp.int32)
    tile_ids = jnp.arange(mp // tm, dtype=jnp.int32)
    tile_expert = jnp.sum((tile_ids[:, None] * tm >= ends[None, :]).astype(jnp.int32), axis=1)
    last = jnp.sum(((n_tiles - 1) * tm >= ends).astype(jnp.int32))
    tile_expert = jnp.where(tile_ids < n_tiles, tile_expert, last).astype(jnp.int32)
    return row_token, row_w, dest.reshape(n, TOP_K), tile_expert, n_tiles.reshape(1)


def _alibi_slope_rows():
    h = np.arange(1, N_HEADS + 1, dtype=np.float64)
    s = np.exp2(-8.0 * h / N_HEADS).astype(np.float32).reshape(N_KV, 1, Q_PER_KV, 1)
    rows = np.broadcast_to(s, (N_KV, SUBLANES, Q_PER_KV, Q_BLOCK)).reshape(N_KV, SUBLANES, GQ_WIDTH)
    return jnp.asarray(rows)


def _overlap_t(seq):
    ncp = seq // CMP_STRIDE
    nslc = seq // SLC_BLOCK
    cs = np.arange(ncp)[None, :] * CMP_STRIDE
    ss = np.arange(nslc)[:, None] * SLC_BLOCK
    ov = (cs < ss + SLC_BLOCK) & (cs + CMP_BLOCK > ss) & (np.arange(ncp)[None, :] < ncp - 1)
    return jnp.asarray(ov.astype(np.float32)).astype(BF16)


def _odd_layer(x2d, c, ada_w, ada_b, mix_pre_g, mix_post_g, ffn_pre_g, ffn_post_g,
               w_in, cmp_k_pe, cmp_k_w1, cmp_k_w2, cmp_v_pe, cmp_v_w1, cmp_v_w2, w_out,
               router_w, exp_w_gate, exp_w_up, exp_w_down, *, bsz, seq):
    n, d = x2d.shape
    sh_m, sc_m, gt_m, sh_f, sc_f, gt_f = _ada_call(c, ada_w, ada_b)
    nmain = Q_WIDTH + 6 * KV_WIDTH
    w_main = w_in[:, :nmain].astype(BF16)
    w_gates = jnp.pad(w_in[:, nmain:], ((0, 0), (0, LANES - 3 * N_HEADS))).astype(BF16)
    proj, gates = _proj_call(x2d, mix_pre_g, sh_m, sc_m, w_main, w_gates, seq=seq,
                             q_cols=Q_WIDTH, q_scale=HEAD_DIM ** -0.5, name="l1_in_proj")

    nqb = seq // Q_BLOCK
    ncp = seq // CMP_STRIDE

    def kv_cols(k):
        o = Q_WIDTH + k * KV_WIDTH
        return proj[:, o:o + KV_WIDTH]

    def chunked(t):
        t = t.reshape(bsz, ncp, CMP_STRIDE, N_KV, HEAD_DIM).transpose(0, 3, 1, 2, 4)
        return t.reshape(bsz * N_KV, ncp, CMP_STRIDE * HEAD_DIM)

    def key_major_t(t):
        t = t.reshape(bsz, nqb, Q_BLOCK, N_KV, HEAD_DIM).transpose(0, 3, 1, 4, 2)
        return t.reshape(bsz * N_KV, nqb, HEAD_DIM, Q_BLOCK)

    kcmp, _ = _compress_call(chunked(kv_cols(0)), cmp_k_pe, cmp_k_w1, cmp_k_w2)
    _, vcmpt = _compress_call(chunked(kv_cols(1)), cmp_v_pe, cmp_v_w1, cmp_v_w2)
    slopes = _alibi_slope_rows()
    oc, selb, cnt = _attn_cmp_call(proj, kcmp, vcmpt, _overlap_t(seq), slopes, bsz=bsz, seq=seq)
    bits = _active_pair_bits(cnt, bsz=bsz, seq=seq)

    gt = gates[:, :3 * N_HEADS].reshape(bsz, nqb, Q_BLOCK, 3, N_KV, Q_PER_KV).transpose(0, 4, 1, 3, 5, 2)
    gt = gt.reshape(bsz, N_KV, nqb, 3, GQ_WIDTH)
    gt = jnp.pad(gt, ((0, 0), (0, 0), (0, 0), (0, SUBLANES - 3), (0, 0)))
    attn = _attn_sel_call(bits, proj, key_major_t(kv_cols(3)), key_major_t(kv_cols(5)), selb, oc, gt, slopes,
                          _window_bias_tiles(), bsz=bsz, seq=seq)

    x2, h2, rt = _outproj_router_call(attn, x2d, w_out.astype(BF16), mix_post_g, gt_m,
                                      ffn_pre_g, sh_f, sc_f, router_w, seq=seq)

    tm = min(1024, n)
    row_token, row_w, dest, tile_expert, n_tiles = _moe_routing(rt, tm=tm)
    hs = jnp.take(h2, row_token, axis=0)
    rows = _moe_call(tile_expert, n_tiles, hs, row_w.reshape(-1, 1), exp_w_gate.astype(BF16),
                     exp_w_up.astype(BF16), exp_w_down.astype(BF16), tm=tm)
    y0 = jnp.take(rows, dest[:, 0], axis=0)
    y1 = jnp.take(rows, dest[:, 1], axis=0)
    return _residual_norm_call(x2, y0, y1, ffn_post_g, gt_f, seq=seq)


def kernel(x, c, l0_ada_w, l0_ada_b, l0_mix_pre_g, l0_mix_post_g, l0_ffn_pre_g, l0_ffn_post_g, l0_w_in, l0_v_ln_g, l0_v_ln_b, l0_w_spatial, l0_b_spatial, l0_conv_w, l0_w_out, l0_ffn_w_gate, l0_ffn_w_up, l0_ffn_w_down, l1_ada_w, l1_ada_b, l1_mix_pre_g, l1_mix_post_g, l1_ffn_pre_g, l1_ffn_post_g, l1_w_in, l1_cmp_k_pe, l1_cmp_k_w1, l1_cmp_k_w2, l1_cmp_v_pe, l1_cmp_v_w1, l1_cmp_v_w2, l1_w_out, l1_router_w, l1_exp_w_gate, l1_exp_w_up, l1_exp_w_down):
    bsz, seq, d = x.shape
    x2d = x.reshape(bsz * seq, d)
    x2d = _even_layer(x2d, c, l0_ada_w, l0_ada_b, l0_mix_pre_g, l0_mix_post_g, l0_ffn_pre_g, l0_ffn_post_g,
                      l0_w_in, l0_v_ln_g, l0_v_ln_b, l0_w_spatial, l0_b_spatial, l0_conv_w, l0_w_out,
                      l0_ffn_w_gate, l0_ffn_w_up, l0_ffn_w_down, seq=seq)
    x2d = _odd_layer(x2d, c, l1_ada_w, l1_ada_b, l1_mix_pre_g, l1_mix_post_g, l1_ffn_pre_g, l1_ffn_post_g,
                     l1_w_in, l1_cmp_k_pe, l1_cmp_k_w1, l1_cmp_k_w2, l1_cmp_v_pe, l1_cmp_v_w1, l1_cmp_v_w2,
                     l1_w_out, l1_router_w, l1_exp_w_gate, l1_exp_w_up, l1_exp_w_down, bsz=bsz, seq=seq)
    return x2d.reshape(bsz, seq, d)
```

```python
import functools
import math

import numpy as np
import jax
import jax.numpy as jnp
from jax import lax
from jax.experimental import pallas as pl
from jax.experimental.pallas import tpu as pltpu

F32 = jnp.float32
BF16 = jnp.bfloat16
HIGHEST = lax.Precision.HIGHEST

EPS = 1e-6
NEG_INF = -1e30
FORCE_SCORE = 1e4

LANES = 128
SUBLANES = 8
VMEM_LIMIT_BYTES = 56 * 1024 * 1024

A_GROUPS = 8
A_WIDTH = 1024
B_WIDTH = 1024
CHUNK = 128
CONV_W = 3
N_HEADS = 16
HEAD_DIM = 128
N_KV = 4
Q_PER_KV = N_HEADS // N_KV
Q_WIDTH = N_HEADS * HEAD_DIM
KV_WIDTH = N_KV * HEAD_DIM
GQ_WIDTH = Q_PER_KV * HEAD_DIM
CMP_BLOCK = 32
CMP_STRIDE = 16
CMP_HIDDEN = 256
SLC_BLOCK = 64
N_SELECT = 16
WINDOW = 512
Q_BLOCK = 128
N_EXPERTS = 8
TOP_K = 2


def _cparams(semantics):
    return pltpu.CompilerParams(dimension_semantics=semantics, vmem_limit_bytes=VMEM_LIMIT_BYTES)


def _sigmoid(x):
    return 1.0 / (1.0 + jnp.exp(-x))


def _gelu_tanh(x):
    return 0.5 * x * (1.0 + jnp.tanh(0.7978845608028654 * (x + 0.044715 * (x * x * x))))


def _rms(x):
    return x * lax.rsqrt(jnp.mean(x * x, axis=-1, keepdims=True) + EPS)


def _tile(n, pref):
    if n <= pref:
        return n
    t = (pref // LANES) * LANES
    while n % t:
        t -= LANES
    assert t > 0, (n, pref)
    return t


def _ada_kernel(c_ref, w_ref, b_ref, o_ref):
    c = c_ref[...]
    s = c * _sigmoid(c)
    o_ref[...] = jnp.dot(s, w_ref[...], preferred_element_type=F32, precision=HIGHEST) + b_ref[...]


def _ada_call(c, w, b):
    bsz, d = c.shape
    n = w.shape[1]
    tn = _tile(n, 1024)
    cp = jnp.pad(c, ((0, SUBLANES - bsz), (0, 0)))
    out = pl.pallas_call(
        _ada_kernel,
        out_shape=jax.ShapeDtypeStruct((SUBLANES, n), F32),
        grid=(n // tn,),
        in_specs=[pl.BlockSpec((SUBLANES, d), lambda j: (0, 0)),
                  pl.BlockSpec((d, tn), lambda j: (0, j)),
                  pl.BlockSpec((1, tn), lambda j: (0, j))],
        out_specs=pl.BlockSpec((SUBLANES, tn), lambda j: (0, j)),
        compiler_params=_cparams(("parallel",)),
        name="ada_modulation",
    )(cp, w, b.reshape(1, n))
    m = out[:bsz]
    return [t.reshape(bsz, 1, d) for t in jnp.split(m, 6, axis=-1)]


def _proj_kernel(*refs, q_tiles, q_scale, with_gates):
    if with_gates:
        x_ref, g_ref, sh_ref, sc_ref, w_ref, wg_ref, o_ref, og_ref, h_ref = refs
    else:
        x_ref, g_ref, sh_ref, sc_ref, w_ref, o_ref, h_ref = refs
    j = pl.program_id(1)

    @pl.when(j == 0)
    def _():
        h = _rms(x_ref[...]) * g_ref[...]
        h = h * (1.0 + sc_ref[0]) + sh_ref[0]
        h_ref[...] = h.astype(BF16)
        if with_gates:
            og_ref[...] = _sigmoid(jnp.dot(h_ref[...], wg_ref[...], preferred_element_type=F32))

    acc = jnp.dot(h_ref[...], w_ref[...], preferred_element_type=F32)
    if q_tiles:
        acc = acc * jnp.where(j < q_tiles, jnp.float32(q_scale), jnp.float32(1.0))
    o_ref[...] = acc.astype(o_ref.dtype)


def _proj_call(x, g, shift, scale, w, wg=None, *, seq, q_cols=0, q_scale=1.0, name):
    n, d = x.shape
    nout = w.shape[1]
    tm = _tile(seq, 1024)
    tn = _tile(nout, 1024)
    assert q_cols % tn == 0
    tpb = seq // tm
    with_gates = wg is not None
    in_specs = [pl.BlockSpec((tm, d), lambda i, j: (i, 0)),
                pl.BlockSpec((1, d), lambda i, j: (0, 0)),
                pl.BlockSpec((1, 1, d), lambda i, j: (i // tpb, 0, 0)),
                pl.BlockSpec((1, 1, d), lambda i, j: (i // tpb, 0, 0)),
                pl.BlockSpec((d, tn), lambda i, j: (0, j))]
    args = [x, g.reshape(1, d), shift, scale, w]
    out_shape = [jax.ShapeDtypeStruct((n, nout), BF16)]
    out_specs = [pl.BlockSpec((tm, tn), lambda i, j: (i, j))]
    if with_gates:
        in_specs.append(pl.BlockSpec((d, LANES), lambda i, j: (0, 0)))
        args.append(wg)
        out_shape.append(jax.ShapeDtypeStruct((n, LANES), F32))
        out_specs.append(pl.BlockSpec((tm, LANES), lambda i, j: (i, 0)))
    out = pl.pallas_call(
        functools.partial(_proj_kernel, q_tiles=q_cols // tn, q_scale=q_scale, with_gates=with_gates),
        out_shape=out_shape,
        grid=(n // tm, nout // tn),
        in_specs=in_specs,
        out_specs=out_specs,
        scratch_shapes=[pltpu.VMEM((tm, d), BF16)],
        compiler_params=_cparams(("parallel", "arbitrary")),
        name=name,
    )(*args)
    return out if with_gates else out[0]


def _mixer0_kernel(p_ref, pgc_ref, pxb_ref, x_ref, vg_ref, vb_ref, wsp_ref, bspt_ref, cw_ref,
                   wout_ref, gpost_ref, gate_ref, o_ref, cat_ref, *, tm, tiles_per_batch):
    i = pl.program_id(0)
    nchunk = tm // CHUNK
    v = _gelu_tanh(p_ref[:, A_WIDTH:2 * A_WIDTH].astype(F32))
    mu = jnp.mean(v, axis=-1, keepdims=True)
    vc = v - mu
    var = jnp.mean(vc * vc, axis=-1, keepdims=True)
    vn = (vc * lax.rsqrt(var + EPS) * vg_ref[...] + vb_ref[...]).astype(BF16)
    row = lax.broadcasted_iota(jnp.int32, (CHUNK, CHUNK), 0)
    col = lax.broadcasted_iota(jnp.int32, (CHUNK, CHUNK), 1)
    causal = col <= row
    for h in range(A_GROUPS):
        cs = slice(h * LANES, (h + 1) * LANES)
        w = jnp.where(causal, wsp_ref[h], 0.0).astype(BF16)
        rhs = jnp.concatenate([vn[c * CHUNK:(c + 1) * CHUNK, cs] for c in range(nchunk)], axis=1)
        z = jnp.dot(w, rhs, preferred_element_type=F32) + bspt_ref[:, h:h + 1]
        for c in range(nchunk):
            rs = slice(c * CHUNK, (c + 1) * CHUNK)
            u = _gelu_tanh(p_ref[rs, cs].astype(F32))
            cat_ref[rs, cs] = (u * z[:, c * CHUNK:(c + 1) * CHUNK]).astype(BF16)
    o_gb, o_gc, o_xb = 2 * A_WIDTH, 2 * A_WIDTH + B_WIDTH, 2 * A_WIDTH + 2 * B_WIDTH
    zc = p_ref[:, o_gc:o_gc + B_WIDTH].astype(F32) * p_ref[:, o_xb:o_xb + B_WIDTH].astype(F32)
    prev = pgc_ref[...].astype(F32) * pxb_ref[...].astype(F32)
    prev = jnp.where(i % tiles_per_batch == 0, 0.0, prev)
    nprev = prev.shape[0]
    p1 = prev[nprev - 1:nprev, :]
    p2 = prev[nprev - 2:nprev - 1, :]
    rowi = lax.broadcasted_iota(jnp.int32, (tm, B_WIDTH), 0)
    r1 = jnp.where(rowi == 0, p1, pltpu.roll(zc, 1, 0))
    r2 = jnp.where(rowi == 0, p2, jnp.where(rowi == 1, p1, pltpu.roll(zc, 2, 0)))
    y = cw_ref[0:1, :] * r2 + cw_ref[1:2, :] * r1 + cw_ref[2:3, :] * zc
    cat_ref[:, A_WIDTH:] = (p_ref[:, o_gb:o_gb + B_WIDTH].astype(F32) * y).astype(BF16)
    yo = jnp.dot(cat_ref[...], wout_ref[...], preferred_element_type=F32)
    o_ref[...] = x_ref[...] + gate_ref[0] * (_rms(yo) * gpost_ref[...])


def _mixer0_call(proj, x, v_ln_g, v_ln_b, w_spatial, b_spatial, conv_w, w_out, g_post, gate, *, seq):
    n, d = x.shape
    tm = _tile(seq, 512)
    tpb = seq // tm
    pr = 16
    rpb = tm // pr
    wcat = A_WIDTH + B_WIDTH
    cwp = jnp.pad(conv_w, ((0, SUBLANES - CONV_W), (0, 0)))
    return pl.pallas_call(
        functools.partial(_mixer0_kernel, tm=tm, tiles_per_batch=tpb),
        out_shape=jax.ShapeDtypeStruct((n, d), F32),
        grid=(n // tm,),
        in_specs=[pl.BlockSpec((tm, proj.shape[1]), lambda i: (i, 0)),
                  pl.BlockSpec((pr, B_WIDTH), lambda i: (jnp.maximum(i * rpb - 1, 0), 3)),
                  pl.BlockSpec((pr, B_WIDTH), lambda i: (jnp.maximum(i * rpb - 1, 0), 4)),
                  pl.BlockSpec((tm, d), lambda i: (i, 0)),
                  pl.BlockSpec((1, A_WIDTH), lambda i: (0, 0)),
                  pl.BlockSpec((1, A_WIDTH), lambda i: (0, 0)),
                  pl.BlockSpec((A_GROUPS, CHUNK, CHUNK), lambda i: (0, 0, 0)),
                  pl.BlockSpec((CHUNK, A_GROUPS), lambda i: (0, 0)),
                  pl.BlockSpec((SUBLANES, B_WIDTH), lambda i: (0, 0)),
                  pl.BlockSpec((wcat, d), lambda i: (0, 0)),
                  pl.BlockSpec((1, d), lambda i: (0, 0)),
                  pl.BlockSpec((1, 1, d), lambda i: (i // tpb, 0, 0))],
        out_specs=pl.BlockSpec((tm, d), lambda i: (i, 0)),
        scratch_shapes=[pltpu.VMEM((tm, wcat), BF16)],
        compiler_params=_cparams(("parallel",)),
        name="l0_mixer",
    )(proj, proj, proj, x, v_ln_g.reshape(1, -1), v_ln_b.reshape(1, -1), w_spatial, b_spatial.T, cwp,
      w_out, g_post.reshape(1, d), gate)


def _ffn_kernel(x_ref, gpre_ref, sh_ref, sc_ref, wg_ref, wu_ref, wd_ref, gpost_ref, gate_ref, o_ref, h_ref):
    j = pl.program_id(1)

    @pl.when(j == 0)
    def _():
        h = _rms(x_ref[...]) * gpre_ref[...]
        h_ref[...] = (h * (1.0 + sc_ref[0]) + sh_ref[0]).astype(BF16)
        o_ref[...] = jnp.zeros_like(o_ref)

    h = h_ref[...]
    g = jnp.dot(h, wg_ref[...], preferred_element_type=F32)
    u = jnp.dot(h, wu_ref[...], preferred_element_type=F32)
    a = (g * _sigmoid(g) * u).astype(BF16)
    o_ref[...] += jnp.dot(a, wd_ref[...], preferred_element_type=F32)

    @pl.when(j == pl.num_programs(1) - 1)
    def _():
        o_ref[...] = x_ref[...] + gate_ref[0] * (_rms(o_ref[...]) * gpost_ref[...])


def _ffn_call(x, g_pre, shift, scale, wg, wu, wd, g_post, gate, *, seq):
    n, d = x.shape
    f = wg.shape[1]
    tm = _tile(seq, 1024)
    tf = _tile(f, 256)
    tpb = seq // tm
    return pl.pallas_call(
        _ffn_kernel,
        out_shape=jax.ShapeDtypeStruct((n, d), F32),
        grid=(n // tm, f // tf),
        in_specs=[pl.BlockSpec((tm, d), lambda i, j: (i, 0)),
                  pl.BlockSpec((1, d), lambda i, j: (0, 0)),
                  pl.BlockSpec((1, 1, d), lambda i, j: (i // tpb, 0, 0)),
                  pl.BlockSpec((1, 1, d), lambda i, j: (i // tpb, 0, 0)),
                  pl.BlockSpec((d, tf), lambda i, j: (0, j)),
                  pl.BlockSpec((d, tf), lambda i, j: (0, j)),
                  pl.BlockSpec((tf, d), lambda i, j: (j, 0)),
                  pl.BlockSpec((1, d), lambda i, j: (0, 0)),
                  pl.BlockSpec((1, 1, d), lambda i, j: (i // tpb, 0, 0))],
        out_specs=pl.BlockSpec((tm, d), lambda i, j: (i, 0)),
        scratch_shapes=[pltpu.VMEM((tm, d), BF16)],
        compiler_params=_cparams(("parallel", "arbitrary")),
        name="l0_ffn",
    )(x, g_pre.reshape(1, d), shift, scale, wg, wu, wd, g_post.reshape(1, d), gate)


def _compress_kernel(x_ref, pe_ref, w1_ref, w2_ref, o_ref, ot_ref):
    half = CMP_STRIDE * HEAD_DIM
    x = x_ref[0]
    nc = x.shape[0]
    a = jnp.dot(x, w1_ref[0:half, :], preferred_element_type=F32)
    b = jnp.dot(x, w1_ref[half:2 * half, :], preferred_element_type=F32)
    bias = jnp.dot(pe_ref[...], w1_ref[...], preferred_element_type=F32)[0:1, :]
    hid = _gelu_tanh(a + pltpu.roll(b, nc - 1, 0) + bias)
    out = jnp.dot(hid.astype(BF16), w2_ref[...], preferred_element_type=F32)
    rowi = lax.broadcasted_iota(jnp.int32, out.shape, 0)
    out = jnp.where(rowi < nc - 1, out, 0.0)
    o_ref[0] = out.astype(BF16)
    ot_ref[0] = out.T.astype(BF16)


def _compress_call(xc, pe, w1, w2):
    bg, nc, kdim = xc.shape
    pe_flat = jnp.pad(pe.reshape(1, -1), ((0, SUBLANES - 1), (0, 0))).astype(BF16)
    w1f = w1.reshape(CMP_BLOCK * HEAD_DIM, CMP_HIDDEN).astype(BF16)
    return pl.pallas_call(
        _compress_kernel,
        out_shape=[jax.ShapeDtypeStruct((bg, nc, HEAD_DIM), BF16),
                   jax.ShapeDtypeStruct((bg, HEAD_DIM, nc), BF16)],
        grid=(bg,),
        in_specs=[pl.BlockSpec((1, nc, kdim), lambda i: (i, 0, 0)),
                  pl.BlockSpec((SUBLANES, CMP_BLOCK * HEAD_DIM), lambda i: (0, 0)),
                  pl.BlockSpec((CMP_BLOCK * HEAD_DIM, CMP_HIDDEN), lambda i: (0, 0)),
                  pl.BlockSpec((CMP_HIDDEN, HEAD_DIM), lambda i: (0, 0))],
        out_specs=[pl.BlockSpec((1, nc, HEAD_DIM), lambda i: (i, 0, 0)),
                   pl.BlockSpec((1, HEAD_DIM, nc), lambda i: (i, 0, 0))],
        compiler_params=_cparams(("parallel",)),
        name="nsa_compress",
    )(xc, pe_flat, w1f, w2.astype(BF16))


def _q_transposed(q_ref):
    parts = []
    for r in range(Q_PER_KV):
        parts.append(q_ref[:, r * HEAD_DIM:(r + 1) * HEAD_DIM].astype(F32).T.astype(BF16))
    return jnp.concatenate(parts, axis=1)


def _attn_cmp_kernel(q_ref, kc_ref, vct_ref, ovt_ref, slope_ref, oc_ref, sel_ref, cnt_ref, imp_ref,
                     *, n_sel, n_class):
    qb = pl.program_id(2)
    q0 = qb * Q_BLOCK
    qt = _q_transposed(q_ref)
    ncp = kc_ref.shape[1]
    nslc = ovt_ref.shape[0]
    rows_per_class = ncp // n_class
    visible = (Q_BLOCK // CMP_STRIDE) * (qb + 1) - 1
    cls = jnp.minimum((visible - 1) // rows_per_class, n_class - 1)

    def compressed(n):
        s_all = jnp.dot(kc_ref[0, 0:n, :], qt, preferred_element_type=F32)
        isub = lax.broadcasted_iota(jnp.int32, (n, Q_BLOCK), 0)
        tl = lax.broadcasted_iota(jnp.int32, (n, Q_BLOCK), 1) + q0
        dist = tl - (isub * CMP_STRIDE + (CMP_BLOCK - 1))
        valid = dist >= 0
        distf = dist.astype(F32)
        psum = jnp.zeros((n, Q_BLOCK), F32)
        pparts = []
        for r in range(Q_PER_KV):
            ls = slice(r * Q_BLOCK, (r + 1) * Q_BLOCK)
            s = s_all[:, ls] - slope_ref[0, 0:1, ls] * distf
            s = jnp.where(valid, s, NEG_INF)
            m = jnp.max(s, axis=0, keepdims=True)
            e = jnp.exp(s - m)
            p = jnp.where(valid, e / jnp.sum(e, axis=0, keepdims=True), 0.0)
            psum = psum + p
            pparts.append(p.astype(BF16))
        pt = jnp.concatenate(pparts, axis=1)
        oc_ref[0, 0, 0] = jnp.dot(vct_ref[0, :, 0:n], pt, preferred_element_type=F32)
        hi = psum.astype(BF16)
        lo = (psum - hi.astype(F32)).astype(BF16)
        ovt = ovt_ref[:, 0:n]
        imp_ref[...] = (jnp.dot(ovt, hi, preferred_element_type=F32)
                        + jnp.dot(ovt, lo, preferred_element_type=F32))

    for k in range(n_class):
        pl.when(cls == k)(functools.partial(compressed, (k + 1) * rows_per_class))

    jb = lax.broadcasted_iota(jnp.int32, (nslc, Q_BLOCK), 0)
    cur = (lax.broadcasted_iota(jnp.int32, (nslc, Q_BLOCK), 1) + q0) // SLC_BLOCK
    forced = (jb == 0) | (jb == cur) | (jb == cur - 1)
    score = jnp.where(forced, FORCE_SCORE, imp_ref[...])
    score = jnp.where(jb <= cur, score, -1.0)
    picked = -3e38
    three_forced = qb > 0
    score = jnp.where(forced & three_forced, picked, score)

    def pick_next(_, work):
        m = jnp.max(work, axis=0, keepdims=True)
        first = jnp.min(jnp.where(work == m, jb, nslc), axis=0, keepdims=True)
        return jnp.where(jb == first, picked, work)

    work = lax.fori_loop(0, jnp.where(three_forced, n_sel - 3, n_sel), pick_next, score)
    chosen = (work == picked) & (jb <= cur)
    sel_ref[0, 0, 0] = jnp.where(chosen, 0.0, NEG_INF)
    flags = jnp.where(chosen, 1.0, 0.0).astype(BF16)
    cnt_ref[0, 0, 0] = lax.dot_general(jnp.ones((SUBLANES, Q_BLOCK), BF16), flags, (((1,), (1,)), ((), ())),
                                       preferred_element_type=F32)


def _attn_cmp_call(proj, kcmp, vcmpt, ovt, slopes, *, bsz, seq):
    nqb = seq // Q_BLOCK
    ncp = kcmp.shape[1]
    nslc = seq // SLC_BLOCK
    n_sel = min(N_SELECT, nslc)
    return pl.pallas_call(
        functools.partial(_attn_cmp_kernel, n_sel=n_sel, n_class=4),
        out_shape=[jax.ShapeDtypeStruct((bsz, N_KV, nqb, HEAD_DIM, GQ_WIDTH), F32),
                   jax.ShapeDtypeStruct((bsz, N_KV, nqb, nslc, Q_BLOCK), F32),
                   jax.ShapeDtypeStruct((bsz, N_KV, nqb, SUBLANES, nslc), F32)],
        grid=(bsz, N_KV, nqb),
        in_specs=[pl.BlockSpec((Q_BLOCK, GQ_WIDTH), lambda b, g, q: (b * nqb + q, g)),
                  pl.BlockSpec((1, ncp, HEAD_DIM), lambda b, g, q: (b * N_KV + g, 0, 0)),
                  pl.BlockSpec((1, HEAD_DIM, ncp), lambda b, g, q: (b * N_KV + g, 0, 0)),
                  pl.BlockSpec((nslc, ncp), lambda b, g, q: (0, 0)),
                  pl.BlockSpec((1, SUBLANES, GQ_WIDTH), lambda b, g, q: (g, 0, 0))],
        out_specs=[pl.BlockSpec((1, 1, 1, HEAD_DIM, GQ_WIDTH), lambda b, g, q: (b, g, q, 0, 0)),
                   pl.BlockSpec((1, 1, 1, nslc, Q_BLOCK), lambda b, g, q: (b, g, q, 0, 0)),
                   pl.BlockSpec((1, 1, 1, SUBLANES, nslc), lambda b, g, q: (b, g, q, 0, 0))],
        scratch_shapes=[pltpu.VMEM((nslc, Q_BLOCK), F32)],
        compiler_params=_cparams(("parallel", "parallel", "arbitrary")),
        name="nsa_compressed_select",
    )(proj, kcmp, vcmpt, ovt, slopes)


N_WTILES = WINDOW // Q_BLOCK + 1
HALF = GQ_WIDTH // 2
SEL_TILES = 4


def _attn_sel_kernel(bits_ref, q_ref, ks_ref, vst_ref, kw_ref, vwt_ref, selb_ref, oc_ref, gates_ref, slope_ref,
                     wb_ref, o_ref, qt_ref, m_ref, l_ref, acc_ref, os_ref, list_ref, *, words):
    b = pl.program_id(0)
    g = pl.program_id(1)
    qb = pl.program_id(2)
    nqb = pl.num_programs(2)
    qt_ref[...] = _q_transposed(q_ref)

    def reset():
        m_ref[...] = jnp.full(m_ref.shape, NEG_INF, F32)
        l_ref[...] = jnp.zeros(l_ref.shape, F32)
        acc_ref[...] = jnp.zeros(acc_ref.shape, F32)

    def key_tile(ref, t):
        return ref[0, pl.ds(pl.multiple_of(t * Q_BLOCK, Q_BLOCK), Q_BLOCK), :]

    def step(k_ref, vt_ref, tiles, widx, use_selection):
        nt = len(tiles)
        ks = [key_tile(k_ref, t) for t in tiles]
        vt = jnp.concatenate([vt_ref[0, t] for t in tiles], axis=1)
        for h in range(2):
            cs = slice(h * HALF, (h + 1) * HALF)
            us, offs = [], []
            mx = None
            for j in range(nt):
                s = jnp.dot(ks[j], qt_ref[:, cs], preferred_element_type=F32)
                u = s + wb_ref[0, widx[j], :, cs]
                if use_selection:
                    two = selb_ref[0, 0, 0, tiles[j]]
                    lo = jnp.broadcast_to(jnp.concatenate([two[0:1, :]] * 2, axis=1), (SLC_BLOCK, HALF))
                    hi = jnp.broadcast_to(jnp.concatenate([two[1:2, :]] * 2, axis=1), (SLC_BLOCK, HALF))
                    u = u + jnp.concatenate([lo, hi], axis=0)
                c = slope_ref[0, 0:1, cs] * ((tiles[j] - qb) * Q_BLOCK).astype(F32)
                cm = jnp.max(u, axis=0, keepdims=True) + c
                mx = cm if mx is None else jnp.maximum(mx, cm)
                us.append(u)
                offs.append(c)
            m_old = m_ref[0:1, cs]
            m_new = jnp.maximum(m_old, mx)
            es = [jnp.exp(us[j] + (offs[j] - m_new)) for j in range(nt)]
            lsum = es[0].sum(axis=0, keepdims=True)
            for j in range(1, nt):
                lsum = lsum + es[j].sum(axis=0, keepdims=True)
            e = jnp.concatenate([x.astype(BF16) for x in es], axis=0)
            alpha = jnp.exp(m_old - m_new)
            l_ref[0:1, cs] = alpha * l_ref[0:1, cs] + lsum
            acc_ref[:, cs] = alpha * acc_ref[:, cs] + jnp.dot(vt, e, preferred_element_type=F32)
            m_ref[0:1, cs] = m_new

    base = ((b * N_KV + g) * nqb + qb) * (words + 1)

    def collect(p, n):
        list_ref[n] = p
        return n + ((bits_ref[base + 1 + (p >> 5)] >> (p & 31)) & 1)

    list_ref[0] = 0
    n_first = jnp.where(qb > 0, bits_ref[base + 1] & 1, 0)
    n_act = lax.fori_loop(jnp.maximum(bits_ref[base], 1), qb, collect, n_first)
    list_ref[n_act] = qb
    for j in range(1, SEL_TILES):
        list_ref[n_act + j] = -1

    reset()

    def sel_body(s, carry):
        idx = [list_ref[s * SEL_TILES + j] for j in range(SEL_TILES)]
        tiles = [jnp.maximum(i, 0) for i in idx]
        widx = [jnp.where(i < 0, N_WTILES, jnp.where(i == qb, N_WTILES - 1, 1)) for i in idx]
        step(ks_ref, vst_ref, tiles, widx, True)
        return carry

    lax.fori_loop(0, (n_act + SEL_TILES) // SEL_TILES, sel_body, 0)
    os_ref[...] = acc_ref[...] / l_ref[0:1, :]

    reset()
    kts = [qb - (N_WTILES - 1) + w for w in range(N_WTILES)]
    step(kw_ref, vwt_ref, [jnp.maximum(kt, 0) for kt in kts],
         [jnp.where(kt < 0, N_WTILES, w) for w, kt in enumerate(kts)], False)

    gt = gates_ref[0, 0, 0]
    ot = (gt[0:1, :] * oc_ref[0, 0, 0] + gt[1:2, :] * os_ref[...]
          + gt[2:3, :] * (acc_ref[...] / l_ref[0:1, :]))
    for r in range(Q_PER_KV):
        cs = slice(r * HEAD_DIM, (r + 1) * HEAD_DIM)
        o_ref[:, cs] = ot[:, cs].T.astype(o_ref.dtype)


def _window_bias_tiles():
    h = np.arange(1, N_HEADS + 1, dtype=np.float64)
    slope = np.exp2(-8.0 * h / N_HEADS).astype(np.float32).reshape(N_KV, 1, 1, Q_PER_KV, 1)
    sub = np.arange(Q_BLOCK, dtype=np.float32).reshape(1, 1, Q_BLOCK, 1, 1)
    tq = np.arange(Q_BLOCK).reshape(1, 1, 1, 1, Q_BLOCK)
    ks = np.arange(Q_BLOCK).reshape(1, 1, Q_BLOCK, 1, 1)
    mask = np.zeros((1, N_WTILES + 1, Q_BLOCK, 1, Q_BLOCK), np.float32)
    mask[0, 0] = np.where(tq[0, 0] < ks[0, 0], 0.0, NEG_INF)
    mask[0, N_WTILES - 1] = np.where(tq[0, 0] >= ks[0, 0], 0.0, NEG_INF)
    mask[0, N_WTILES] = NEG_INF
    out = (slope * sub).astype(np.float32) + mask
    return jnp.asarray(out.reshape(N_KV, N_WTILES + 1, Q_BLOCK, GQ_WIDTH))


def _attn_sel_call(bits, proj, vst, vwt, selb, oc, gates_t, slopes, wbias, *, bsz, seq):
    nqb = seq // Q_BLOCK
    n = bsz * seq
    proj3 = proj.reshape(bsz, seq, proj.shape[1])
    ks_col = (Q_WIDTH + 2 * KV_WIDTH) // HEAD_DIM
    kw_col = (Q_WIDTH + 4 * KV_WIDTH) // HEAD_DIM
    selb5 = selb.reshape(bsz, N_KV, nqb, nqb, 2, Q_BLOCK)
    words = bits.shape[0] // (bsz * N_KV * nqb) - 1
    grid_spec = pltpu.PrefetchScalarGridSpec(
        num_scalar_prefetch=1,
        grid=(bsz, N_KV, nqb),
        in_specs=[pl.BlockSpec((Q_BLOCK, GQ_WIDTH), lambda b, g, q, s: (b * nqb + q, g)),
                  pl.BlockSpec((1, seq, HEAD_DIM), lambda b, g, q, s: (b, 0, ks_col + g)),
                  pl.BlockSpec((1, nqb, HEAD_DIM, Q_BLOCK), lambda b, g, q, s: (b * N_KV + g, 0, 0, 0)),
                  pl.BlockSpec((1, seq, HEAD_DIM), lambda b, g, q, s: (b, 0, kw_col + g)),
                  pl.BlockSpec((1, nqb, HEAD_DIM, Q_BLOCK), lambda b, g, q, s: (b * N_KV + g, 0, 0, 0)),
                  pl.BlockSpec((1, 1, 1, nqb, 2, Q_BLOCK), lambda b, g, q, s: (b, g, q, 0, 0, 0)),
                  pl.BlockSpec((1, 1, 1, HEAD_DIM, GQ_WIDTH), lambda b, g, q, s: (b, g, q, 0, 0)),
                  pl.BlockSpec((1, 1, 1, SUBLANES, GQ_WIDTH), lambda b, g, q, s: (b, g, q, 0, 0)),
                  pl.BlockSpec((1, SUBLANES, GQ_WIDTH), lambda b, g, q, s: (g, 0, 0)),
                  pl.BlockSpec((1, N_WTILES + 1, Q_BLOCK, GQ_WIDTH), lambda b, g, q, s: (g, 0, 0, 0))],
        out_specs=pl.BlockSpec((Q_BLOCK, GQ_WIDTH), lambda b, g, q, s: (b * nqb + q, g)),
        scratch_shapes=[pltpu.VMEM((HEAD_DIM, GQ_WIDTH), BF16),
                        pltpu.VMEM((SUBLANES, GQ_WIDTH), F32),
                        pltpu.VMEM((SUBLANES, GQ_WIDTH), F32),
                        pltpu.VMEM((HEAD_DIM, GQ_WIDTH), F32),
                        pltpu.VMEM((HEAD_DIM, GQ_WIDTH), F32),
                        pltpu.SMEM((nqb + SEL_TILES,), jnp.int32)],
    )
    return pl.pallas_call(
        functools.partial(_attn_sel_kernel, words=words),
        out_shape=jax.ShapeDtypeStruct((n, Q_WIDTH), BF16),
        grid_spec=grid_spec,
        compiler_params=_cparams(("parallel", "parallel", "arbitrary")),
        name="nsa_selected_window",
    )(bits, proj, proj3, vst, proj3, vwt, selb5, oc, gates_t, slopes, wbias)


def _active_pair_bits(cnt, *, bsz, seq):
    nqb = seq // Q_BLOCK
    words = (nqb + 31) // 32
    act = (cnt[:, :, :, 0, :] > 0.5).reshape(bsz, N_KV, nqb, nqb, 2).any(axis=-1)
    tile_id = jnp.arange(nqb, dtype=jnp.int32)
    first = jnp.min(jnp.where(act & (tile_id >= 1), tile_id, nqb), axis=-1, keepdims=True)
    act = jnp.pad(act, ((0, 0), (0, 0), (0, 0), (0, words * 32 - nqb))).reshape(bsz, N_KV, nqb, words, 32)
    weights = jnp.left_shift(jnp.uint32(1), jnp.arange(32, dtype=jnp.uint32))
    packed = jnp.sum(act.astype(jnp.uint32) * weights, axis=-1, dtype=jnp.uint32)
    packed = lax.bitcast_convert_type(packed, jnp.int32)
    return jnp.concatenate([first, packed], axis=-1).reshape(-1)


def _outproj_router_kernel(a_ref, x_ref, w_ref, gpost_ref, gate_ref, gpre_ref, sh_ref, sc_ref, wr_ref, wrlo_ref,
                           x2_ref, h2_ref, rt_ref):
    y = jnp.dot(a_ref[...], w_ref[...], preferred_element_type=F32)
    x2 = x_ref[...] + gate_ref[0] * (_rms(y) * gpost_ref[...])
    x2_ref[...] = x2
    h = _rms(x2) * gpre_ref[...]
    h = h * (1.0 + sc_ref[0]) + sh_ref[0]
    hb = h.astype(BF16)
    h2_ref[...] = hb
    h_lo = (h - hb.astype(F32)).astype(BF16)
    logits = (jnp.dot(hb, wr_ref[...], preferred_element_type=F32)
              + jnp.dot(h_lo, wr_ref[...], preferred_element_type=F32)
              + jnp.dot(hb, wrlo_ref[...], preferred_element_type=F32))
    lane = lax.broadcasted_iota(jnp.int32, logits.shape, 1)
    logits = jnp.where(lane < N_EXPERTS, logits, -jnp.inf)
    v1 = jnp.max(logits, axis=-1, keepdims=True)
    i1 = jnp.min(jnp.where(logits == v1, lane, LANES), axis=-1, keepdims=True)
    rest = jnp.where(lane == i1, -jnp.inf, logits)
    v2 = jnp.max(rest, axis=-1, keepdims=True)
    i2 = jnp.min(jnp.where(rest == v2, lane, LANES), axis=-1, keepdims=True)
    e2 = jnp.exp(v2 - v1)
    w1 = 1.0 / (1.0 + e2)
    w2 = e2 / (1.0 + e2)
    out = jnp.where(lane == 0, w1, jnp.where(lane == 1, w2, 0.0))
    out = jnp.where(lane == 2, i1.astype(F32), jnp.where(lane == 3, i2.astype(F32), out))
    rt_ref[...] = out


def _outproj_router_call(a, x, w_out, g_post, gate, g_pre, shift, scale, w_router, *, seq):
    n, d = x.shape
    tm = _tile(seq, 512)
    tpb = seq // tm
    wr32 = jnp.pad(w_router, ((0, 0), (0, LANES - w_router.shape[1])))
    wr = wr32.astype(BF16)
    wr_lo = (wr32 - wr.astype(F32)).astype(BF16)
    row = lambda i: (i, 0)
    fixed = lambda i: (0, 0)
    per_b = lambda i: (i // tpb, 0, 0)
    return pl.pallas_call(
        _outproj_router_kernel,
        out_shape=[jax.ShapeDtypeStruct((n, d), F32),
                   jax.ShapeDtypeStruct((n, d), BF16),
                   jax.ShapeDtypeStruct((n, LANES), F32)],
        grid=(n // tm,),
        in_specs=[pl.BlockSpec((tm, a.shape[1]), row),
                  pl.BlockSpec((tm, d), row),
                  pl.BlockSpec(w_out.shape, fixed),
                  pl.BlockSpec((1, d), fixed),
                  pl.BlockSpec((1, 1, d), per_b),
                  pl.BlockSpec((1, d), fixed),
                  pl.BlockSpec((1, 1, d), per_b),
                  pl.BlockSpec((1, 1, d), per_b),
                  pl.BlockSpec((d, LANES), fixed),
                  pl.BlockSpec((d, LANES), fixed)],
        out_specs=[pl.BlockSpec((tm, d), row), pl.BlockSpec((tm, d), row), pl.BlockSpec((tm, LANES), row)],
        compiler_params=_cparams(("parallel",)),
        name="l1_outproj_router",
    )(a, x, w_out, g_post.reshape(1, d), gate, g_pre.reshape(1, d), shift, scale, wr, wr_lo)


def _moe_kernel(te_ref, nt_ref, h_ref, wg_ref, wu_ref, wd_ref, o_ref, acc_ref):
    i = pl.program_id(0)
    j = pl.program_id(1)

    @pl.when(j == 0)
    def _():
        acc_ref[...] = jnp.zeros_like(acc_ref)

    @pl.when(i < nt_ref[0])
    def _():
        h = h_ref[...]
        g = jnp.dot(h, wg_ref[0], preferred_element_type=F32)
        u = jnp.dot(h, wu_ref[0], preferred_element_type=F32)
        a = (g * _sigmoid(g) * u).astype(BF16)
        acc_ref[...] += jnp.dot(a, wd_ref[0], preferred_element_type=F32)

    @pl.when(j == pl.num_programs(1) - 1)
    def _():
        o_ref[...] = acc_ref[...].astype(o_ref.dtype)


def _moe_call(tile_expert, n_tiles, hs, wg, wu, wd, *, tm):
    mp, d = hs.shape
    f = wg.shape[2]
    tf = _tile(f, 512)
    nj = f // tf

    def wcol(i, j, nt):
        return jnp.where(i < nt[0], j, nj - 1)

    grid_spec = pltpu.PrefetchScalarGridSpec(
        num_scalar_prefetch=2,
        grid=(mp // tm, f // tf),
        in_specs=[pl.BlockSpec((tm, d), lambda i, j, te, nt: (i, 0)),
                  pl.BlockSpec((1, d, tf), lambda i, j, te, nt: (te[i], 0, wcol(i, j, nt))),
                  pl.BlockSpec((1, d, tf), lambda i, j, te, nt: (te[i], 0, wcol(i, j, nt))),
                  pl.BlockSpec((1, tf, d), lambda i, j, te, nt: (te[i], wcol(i, j, nt), 0))],
        out_specs=pl.BlockSpec((tm, d), lambda i, j, te, nt: (i, 0)),
        scratch_shapes=[pltpu.VMEM((tm, d), F32)],
    )
    return pl.pallas_call(
        _moe_kernel,
        out_shape=jax.ShapeDtypeStruct((mp, d), BF16),
        grid_spec=grid_spec,
        compiler_params=_cparams(("arbitrary", "arbitrary")),
        name="l1_moe_experts",
    )(tile_expert, n_tiles, hs, wg, wu, wd)


def _residual_norm_kernel(x_ref, y0_ref, y1_ref, rt_ref, gpost_ref, gate_ref, o_ref):
    y = rt_ref[:, 0:1] * y0_ref[...].astype(F32) + rt_ref[:, 1:2] * y1_ref[...].astype(F32)
    o_ref[...] = x_ref[...] + gate_ref[0] * (_rms(y) * gpost_ref[...])


def _residual_norm_call(x, y0, y1, rt, g_post, gate, *, seq):
    n, d = x.shape
    tm = _tile(seq, 512)
    tpb = seq // tm
    row = lambda i: (i, 0)
    return pl.pallas_call(
        _residual_norm_kernel,
        out_shape=jax.ShapeDtypeStruct((n, d), F32),
        grid=(n // tm,),
        in_specs=[pl.BlockSpec((tm, d), row), pl.BlockSpec((tm, d), row), pl.BlockSpec((tm, d), row),
                  pl.BlockSpec((tm, LANES), row),
                  pl.BlockSpec((1, d), lambda i: (0, 0)),
                  pl.BlockSpec((1, 1, d), lambda i: (i // tpb, 0, 0))],
        out_specs=pl.BlockSpec((tm, d), row),
        compiler_params=_cparams(("parallel",)),
        name="l1_moe_residual",
    )(x, y0, y1, rt, g_post.reshape(1, d), gate)


CAST_BLOCK_BYTES = 8 * 1024 * 1024


def _cast_kernel(x_ref, o_ref):
    o_ref[...] = x_ref[...].astype(o_ref.dtype)


def _to_bf16(w):
    cols = w.shape[-1]
    rows = w.size // cols
    tr = rows
    while tr * cols * 4 > CAST_BLOCK_BYTES and tr % 2 == 0 and (tr // 2) % 16 == 0:
        tr //= 2
    out = pl.pallas_call(
        _cast_kernel,
        out_shape=jax.ShapeDtypeStruct((rows, cols), BF16),
        grid=(rows // tr,),
        in_specs=[pl.BlockSpec((tr, cols), lambda i: (i, 0))],
        out_specs=pl.BlockSpec((tr, cols), lambda i: (i, 0)),
        compiler_params=_cparams(("parallel",)),
        name="weight_cast",
    )(w.reshape(rows, cols))
    return out.reshape(w.shape)


def _even_layer(x2d, c, ada_w, ada_b, mix_pre_g, mix_post_g, ffn_pre_g, ffn_post_g,
                w_in, v_ln_g, v_ln_b, w_spatial, b_spatial, conv_w, w_out,
                ffn_w_gate, ffn_w_up, ffn_w_down, *, seq):
    sh_m, sc_m, gt_m, sh_f, sc_f, gt_f = _ada_call(c, ada_w, ada_b)
    proj = _proj_call(x2d, mix_pre_g, sh_m, sc_m, _to_bf16(w_in), seq=seq, name="l0_in_proj")
    x2d = _mixer0_call(proj, x2d, v_ln_g, v_ln_b, w_spatial, b_spatial, conv_w, _to_bf16(w_out),
                       mix_post_g, gt_m, seq=seq)
    return _ffn_call(x2d, ffn_pre_g, sh_f, sc_f, _to_bf16(ffn_w_gate), _to_bf16(ffn_w_up),
                     _to_bf16(ffn_w_down), ffn_post_g, gt_f, seq=seq)


def _moe_routing(rt, *, tm):
    n = rt.shape[0]
    na = n * TOP_K
    mp = na + N_EXPERTS * tm
    e_flat = rt[:, TOP_K:2 * TOP_K].astype(jnp.int32).reshape(na)
    onehot = (e_flat[:, None] == jnp.arange(N_EXPERTS, dtype=jnp.int32)[None, :]).astype(jnp.int32)
    csum = jnp.cumsum(onehot, axis=0)
    counts = csum[-1]
    rank = jnp.sum(onehot * csum, axis=1) - 1
    padded = ((counts + tm - 1) // tm) * tm
    ends = jnp.cumsum(padded)
    starts = ends - padded
    dest = jnp.sum(onehot * starts[None, :], axis=1) + rank
    order = jnp.argsort(e_flat, stable=True).astype(jnp.int32)
    rows = jnp.arange(mp, dtype=jnp.int32)
    row_e = jnp.minimum(jnp.sum((rows[:, None] >= ends[None, :]).astype(jnp.int32), axis=1), N_EXPERTS - 1)
    slot = rows - starts[row_e]
    src = (jnp.cumsum(counts) - counts)[row_e] + slot
    row_token = jnp.where(slot < counts[row_e], order[jnp.clip(src, 0, na - 1)] // TOP_K, 0)
    n_tiles = (ends[-1] // tm).astype(jnp.int32)
    tile_ids = jnp.arange(mp // tm, dtype=jnp.int32)
    tile_expert = jnp.sum((tile_ids[:, None] * tm >= ends[None, :]).astype(jnp.int32), axis=1)
    last = jnp.sum(((n_tiles - 1) * tm >= ends).astype(jnp.int32))
    tile_expert = jnp.where(tile_ids < n_tiles, tile_expert, last).astype(jnp.int32)
    return row_token, dest.reshape(n, TOP_K), tile_expert, n_tiles.reshape(1)


def _alibi_slope_rows():
    h = np.arange(1, N_HEADS + 1, dtype=np.float64)
    s = np.exp2(-8.0 * h / N_HEADS).astype(np.float32).reshape(N_KV, 1, Q_PER_KV, 1)
    rows = np.broadcast_to(s, (N_KV, SUBLANES, Q_PER_KV, Q_BLOCK)).reshape(N_KV, SUBLANES, GQ_WIDTH)
    return jnp.asarray(rows)


def _overlap_t(seq):
    ncp = seq // CMP_STRIDE
    nslc = seq // SLC_BLOCK
    cs = np.arange(ncp)[None, :] * CMP_STRIDE
    ss = np.arange(nslc)[:, None] * SLC_BLOCK
    ov = (cs < ss + SLC_BLOCK) & (cs + CMP_BLOCK > ss) & (np.arange(ncp)[None, :] < ncp - 1)
    return jnp.asarray(ov.astype(np.float32)).astype(BF16)


def _odd_layer(x2d, c, ada_w, ada_b, mix_pre_g, mix_post_g, ffn_pre_g, ffn_post_g,
               w_in, cmp_k_pe, cmp_k_w1, cmp_k_w2, cmp_v_pe, cmp_v_w1, cmp_v_w2, w_out,
               router_w, exp_w_gate, exp_w_up, exp_w_down, *, bsz, seq):
    n, d = x2d.shape
    sh_m, sc_m, gt_m, sh_f, sc_f, gt_f = _ada_call(c, ada_w, ada_b)
    nmain = Q_WIDTH + 6 * KV_WIDTH
    w_main = w_in[:, :nmain].astype(BF16)
    w_gates = jnp.pad(w_in[:, nmain:], ((0, 0), (0, LANES - 3 * N_HEADS))).astype(BF16)
    proj, gates = _proj_call(x2d, mix_pre_g, sh_m, sc_m, w_main, w_gates, seq=seq,
                             q_cols=Q_WIDTH, q_scale=HEAD_DIM ** -0.5, name="l1_in_proj")

    nqb = seq // Q_BLOCK
    ncp = seq // CMP_STRIDE

    def kv_cols(k):
        o = Q_WIDTH + k * KV_WIDTH
        return proj[:, o:o + KV_WIDTH]

    def chunked(t):
        t = t.reshape(bsz, ncp, CMP_STRIDE, N_KV, HEAD_DIM).transpose(0, 3, 1, 2, 4)
        return t.reshape(bsz * N_KV, ncp, CMP_STRIDE * HEAD_DIM)

    def key_major_t(t):
        t = t.reshape(bsz, nqb, Q_BLOCK, N_KV, HEAD_DIM).transpose(0, 3, 1, 4, 2)
        return t.reshape(bsz * N_KV, nqb, HEAD_DIM, Q_BLOCK)

    kcmp, _ = _compress_call(chunked(kv_cols(0)), cmp_k_pe, cmp_k_w1, cmp_k_w2)
    _, vcmpt = _compress_call(chunked(kv_cols(1)), cmp_v_pe, cmp_v_w1, cmp_v_w2)
    slopes = _alibi_slope_rows()
    oc, selb, cnt = _attn_cmp_call(proj, kcmp, vcmpt, _overlap_t(seq), slopes, bsz=bsz, seq=seq)
    bits = _active_pair_bits(cnt, bsz=bsz, seq=seq)

    gt = gates[:, :3 * N_HEADS].reshape(bsz, nqb, Q_BLOCK, 3, N_KV, Q_PER_KV).transpose(0, 4, 1, 3, 5, 2)
    gt = gt.reshape(bsz, N_KV, nqb, 3, GQ_WIDTH)
    gt = jnp.pad(gt, ((0, 0), (0, 0), (0, 0), (0, SUBLANES - 3), (0, 0)))
    attn = _attn_sel_call(bits, proj, key_major_t(kv_cols(3)), key_major_t(kv_cols(5)), selb, oc, gt, slopes,
                          _window_bias_tiles(), bsz=bsz, seq=seq)

    x2, h2, rt = _outproj_router_call(attn, x2d, _to_bf16(w_out), mix_post_g, gt_m,
                                      ffn_pre_g, sh_f, sc_f, router_w, seq=seq)

    tm = min(1024, n)
    row_token, dest, tile_expert, n_tiles = _moe_routing(rt, tm=tm)
    hs = jnp.take(h2, row_token, axis=0)
    rows = _moe_call(tile_expert, n_tiles, hs, _to_bf16(exp_w_gate), _to_bf16(exp_w_up),
                     _to_bf16(exp_w_down), tm=tm)
    y0 = jnp.take(rows, dest[:, 0], axis=0)
    y1 = jnp.take(rows, dest[:, 1], axis=0)
    return _residual_norm_call(x2, y0, y1, rt, ffn_post_g, gt_f, seq=seq)


def kernel(x, c, l0_ada_w, l0_ada_b, l0_mix_pre_g, l0_mix_post_g, l0_ffn_pre_g, l0_ffn_post_g, l0_w_in, l0_v_ln_g, l0_v_ln_b, l0_w_spatial, l0_b_spatial, l0_conv_w, l0_w_out, l0_ffn_w_gate, l0_ffn_w_up, l0_ffn_w_down, l1_ada_w, l1_ada_b, l1_mix_pre_g, l1_mix_post_g, l1_ffn_pre_g, l1_ffn_post_g, l1_w_in, l1_cmp_k_pe, l1_cmp_k_w1, l1_cmp_k_w2, l1_cmp_v_pe, l1_cmp_v_w1, l1_cmp_v_w2, l1_w_out, l1_router_w, l1_exp_w_gate, l1_exp_w_up, l1_exp_w_down):
    bsz, seq, d = x.shape
    x2d = x.reshape(bsz * seq, d)
    x2d = _even_layer(x2d, c, l0_ada_w, l0_ada_b, l0_mix_pre_g, l0_mix_post_g, l0_ffn_pre_g, l0_ffn_post_g,
                      l0_w_in, l0_v_ln_g, l0_v_ln_b, l0_w_spatial, l0_b_spatial, l0_conv_w, l0_w_out,
                      l0_ffn_w_gate, l0_ffn_w_up, l0_ffn_w_down, seq=seq)
    x2d = _odd_layer(x2d, c, l1_ada_w, l1_ada_b, l1_mix_pre_g, l1_mix_post_g, l1_ffn_pre_g, l1_ffn_post_g,
                     l1_w_in, l1_cmp_k_pe, l1_cmp_k_w1, l1_cmp_k_w2, l1_cmp_v_pe, l1_cmp_v_w1, l1_cmp_v_w2,
                     l1_w_out, l1_router_w, l1_exp_w_gate, l1_exp_w_up, l1_exp_w_down, bsz=bsz, seq=seq)
    return x2d.reshape(bsz, seq, d)
```

```python
import functools
import math

import numpy as np
import jax
import jax.numpy as jnp
from jax import lax
from jax.experimental import pallas as pl
from jax.experimental.pallas import tpu as pltpu

F32 = jnp.float32
BF16 = jnp.bfloat16
HIGHEST = lax.Precision.HIGHEST

EPS = 1e-6
NEG_INF = -1e30
FORCE_SCORE = 1e4

LANES = 128
SUBLANES = 8
VMEM_LIMIT_BYTES = 56 * 1024 * 1024

A_GROUPS = 8
A_WIDTH = 1024
B_WIDTH = 1024
CHUNK = 128
CONV_W = 3
N_HEADS = 16
HEAD_DIM = 128
N_KV = 4
Q_PER_KV = N_HEADS // N_KV
Q_WIDTH = N_HEADS * HEAD_DIM
KV_WIDTH = N_KV * HEAD_DIM
GQ_WIDTH = Q_PER_KV * HEAD_DIM
CMP_BLOCK = 32
CMP_STRIDE = 16
CMP_HIDDEN = 256
SLC_BLOCK = 64
N_SELECT = 16
WINDOW = 512
Q_BLOCK = 128
N_EXPERTS = 8
TOP_K = 2


def _cparams(semantics):
    return pltpu.CompilerParams(dimension_semantics=semantics, vmem_limit_bytes=VMEM_LIMIT_BYTES)


def _sigmoid(x):
    return 1.0 / (1.0 + jnp.exp(-x))


def _gelu_tanh(x):
    return 0.5 * x * (1.0 + jnp.tanh(0.7978845608028654 * (x + 0.044715 * (x * x * x))))


def _rms(x):
    return x * lax.rsqrt(jnp.mean(x * x, axis=-1, keepdims=True) + EPS)


def _tile(n, pref):
    if n <= pref:
        return n
    t = (pref // LANES) * LANES
    while n % t:
        t -= LANES
    assert t > 0, (n, pref)
    return t


def _ada_kernel(c_ref, w_ref, b_ref, o_ref):
    c = c_ref[...]
    s = c * _sigmoid(c)
    o_ref[...] = jnp.dot(s, w_ref[...], preferred_element_type=F32, precision=HIGHEST) + b_ref[...]


def _ada_call(c, w, b):
    bsz, d = c.shape
    n = w.shape[1]
    tn = _tile(n, 1024)
    cp = jnp.pad(c, ((0, SUBLANES - bsz), (0, 0)))
    out = pl.pallas_call(
        _ada_kernel,
        out_shape=jax.ShapeDtypeStruct((SUBLANES, n), F32),
        grid=(n // tn,),
        in_specs=[pl.BlockSpec((SUBLANES, d), lambda j: (0, 0)),
                  pl.BlockSpec((d, tn), lambda j: (0, j)),
                  pl.BlockSpec((1, tn), lambda j: (0, j))],
        out_specs=pl.BlockSpec((SUBLANES, tn), lambda j: (0, j)),
        compiler_params=_cparams(("parallel",)),
        name="ada_modulation",
    )(cp, w, b.reshape(1, n))
    m = out[:bsz]
    return [t.reshape(bsz, 1, d) for t in jnp.split(m, 6, axis=-1)]


def _proj_kernel(*refs, q_tiles, q_scale, with_gates):
    if with_gates:
        x_ref, g_ref, sh_ref, sc_ref, w_ref, wg_ref, o_ref, og_ref, h_ref = refs
    else:
        x_ref, g_ref, sh_ref, sc_ref, w_ref, o_ref, h_ref = refs
    j = pl.program_id(1)

    @pl.when(j == 0)
    def _():
        h = _rms(x_ref[...]) * g_ref[...]
        h = h * (1.0 + sc_ref[0]) + sh_ref[0]
        h_ref[...] = h.astype(BF16)
        if with_gates:
            og_ref[...] = _sigmoid(jnp.dot(h_ref[...], wg_ref[...], preferred_element_type=F32))

    acc = jnp.dot(h_ref[...], w_ref[...], preferred_element_type=F32)
    if q_tiles:
        acc = acc * jnp.where(j < q_tiles, jnp.float32(q_scale), jnp.float32(1.0))
    o_ref[...] = acc.astype(o_ref.dtype)


def _proj_call(x, g, shift, scale, w, wg=None, *, seq, q_cols=0, q_scale=1.0, name):
    n, d = x.shape
    nout = w.shape[1]
    tm = _tile(seq, 1024)
    tn = _tile(nout, 1024)
    assert q_cols % tn == 0
    tpb = seq // tm
    with_gates = wg is not None
    in_specs = [pl.BlockSpec((tm, d), lambda i, j: (i, 0)),
                pl.BlockSpec((1, d), lambda i, j: (0, 0)),
                pl.BlockSpec((1, 1, d), lambda i, j: (i // tpb, 0, 0)),
                pl.BlockSpec((1, 1, d), lambda i, j: (i // tpb, 0, 0)),
                pl.BlockSpec((d, tn), lambda i, j: (0, j))]
    args = [x, g.reshape(1, d), shift, scale, w]
    out_shape = [jax.ShapeDtypeStruct((n, nout), BF16)]
    out_specs = [pl.BlockSpec((tm, tn), lambda i, j: (i, j))]
    if with_gates:
        in_specs.append(pl.BlockSpec((d, LANES), lambda i, j: (0, 0)))
        args.append(wg)
        out_shape.append(jax.ShapeDtypeStruct((n, LANES), F32))
        out_specs.append(pl.BlockSpec((tm, LANES), lambda i, j: (i, 0)))
    out = pl.pallas_call(
        functools.partial(_proj_kernel, q_tiles=q_cols // tn, q_scale=q_scale, with_gates=with_gates),
        out_shape=out_shape,
        grid=(n // tm, nout // tn),
        in_specs=in_specs,
        out_specs=out_specs,
        scratch_shapes=[pltpu.VMEM((tm, d), BF16)],
        compiler_params=_cparams(("parallel", "arbitrary")),
        name=name,
    )(*args)
    return out if with_gates else out[0]


def _mixer0_kernel(p_ref, pgc_ref, pxb_ref, x_ref, vg_ref, vb_ref, wsp_ref, bspt_ref, cw_ref,
                   wout_ref, gpost_ref, gate_ref, o_ref, cat_ref, *, tm, tiles_per_batch):
    i = pl.program_id(0)
    nchunk = tm // CHUNK
    v = _gelu_tanh(p_ref[:, A_WIDTH:2 * A_WIDTH].astype(F32))
    mu = jnp.mean(v, axis=-1, keepdims=True)
    vc = v - mu
    var = jnp.mean(vc * vc, axis=-1, keepdims=True)
    vn = (vc * lax.rsqrt(var + EPS) * vg_ref[...] + vb_ref[...]).astype(BF16)
    row = lax.broadcasted_iota(jnp.int32, (CHUNK, CHUNK), 0)
    col = lax.broadcasted_iota(jnp.int32, (CHUNK, CHUNK), 1)
    causal = col <= row
    for h in range(A_GROUPS):
        cs = slice(h * LANES, (h + 1) * LANES)
        w = jnp.where(causal, wsp_ref[h], 0.0).astype(BF16)
        rhs = jnp.concatenate([vn[c * CHUNK:(c + 1) * CHUNK, cs] for c in range(nchunk)], axis=1)
        z = jnp.dot(w, rhs, preferred_element_type=F32) + bspt_ref[:, h:h + 1]
        for c in range(nchunk):
            rs = slice(c * CHUNK, (c + 1) * CHUNK)
            u = _gelu_tanh(p_ref[rs, cs].astype(F32))
            cat_ref[rs, cs] = (u * z[:, c * CHUNK:(c + 1) * CHUNK]).astype(BF16)
    o_gb, o_gc, o_xb = 2 * A_WIDTH, 2 * A_WIDTH + B_WIDTH, 2 * A_WIDTH + 2 * B_WIDTH
    zc = p_ref[:, o_gc:o_gc + B_WIDTH].astype(F32) * p_ref[:, o_xb:o_xb + B_WIDTH].astype(F32)
    prev = pgc_ref[...].astype(F32) * pxb_ref[...].astype(F32)
    prev = jnp.where(i % tiles_per_batch == 0, 0.0, prev)
    nprev = prev.shape[0]
    p1 = prev[nprev - 1:nprev, :]
    p2 = prev[nprev - 2:nprev - 1, :]
    rowi = lax.broadcasted_iota(jnp.int32, (tm, B_WIDTH), 0)
    r1 = jnp.where(rowi == 0, p1, pltpu.roll(zc, 1, 0))
    r2 = jnp.where(rowi == 0, p2, jnp.where(rowi == 1, p1, pltpu.roll(zc, 2, 0)))
    y = cw_ref[0:1, :] * r2 + cw_ref[1:2, :] * r1 + cw_ref[2:3, :] * zc
    cat_ref[:, A_WIDTH:] = (p_ref[:, o_gb:o_gb + B_WIDTH].astype(F32) * y).astype(BF16)
    yo = jnp.dot(cat_ref[...], wout_ref[...], preferred_element_type=F32)
    o_ref[...] = x_ref[...] + gate_ref[0] * (_rms(yo) * gpost_ref[...])


def _mixer0_call(proj, x, v_ln_g, v_ln_b, w_spatial, b_spatial, conv_w, w_out, g_post, gate, *, seq):
    n, d = x.shape
    tm = _tile(seq, 512)
    tpb = seq // tm
    pr = 16
    rpb = tm // pr
    wcat = A_WIDTH + B_WIDTH
    cwp = jnp.pad(conv_w, ((0, SUBLANES - CONV_W), (0, 0)))
    return pl.pallas_call(
        functools.partial(_mixer0_kernel, tm=tm, tiles_per_batch=tpb),
        out_shape=jax.ShapeDtypeStruct((n, d), F32),
        grid=(n // tm,),
        in_specs=[pl.BlockSpec((tm, proj.shape[1]), lambda i: (i, 0)),
                  pl.BlockSpec((pr, B_WIDTH), lambda i: (jnp.maximum(i * rpb - 1, 0), 3)),
                  pl.BlockSpec((pr, B_WIDTH), lambda i: (jnp.maximum(i * rpb - 1, 0), 4)),
                  pl.BlockSpec((tm, d), lambda i: (i, 0)),
                  pl.BlockSpec((1, A_WIDTH), lambda i: (0, 0)),
                  pl.BlockSpec((1, A_WIDTH), lambda i: (0, 0)),
                  pl.BlockSpec((A_GROUPS, CHUNK, CHUNK), lambda i: (0, 0, 0)),
                  pl.BlockSpec((CHUNK, A_GROUPS), lambda i: (0, 0)),
                  pl.BlockSpec((SUBLANES, B_WIDTH), lambda i: (0, 0)),
                  pl.BlockSpec((wcat, d), lambda i: (0, 0)),
                  pl.BlockSpec((1, d), lambda i: (0, 0)),
                  pl.BlockSpec((1, 1, d), lambda i: (i // tpb, 0, 0))],
        out_specs=pl.BlockSpec((tm, d), lambda i: (i, 0)),
        scratch_shapes=[pltpu.VMEM((tm, wcat), BF16)],
        compiler_params=_cparams(("parallel",)),
        name="l0_mixer",
    )(proj, proj, proj, x, v_ln_g.reshape(1, -1), v_ln_b.reshape(1, -1), w_spatial, b_spatial.T, cwp,
      w_out, g_post.reshape(1, d), gate)


def _ffn_kernel(x_ref, gpre_ref, sh_ref, sc_ref, wg_ref, wu_ref, wd_ref, gpost_ref, gate_ref, o_ref, h_ref):
    j = pl.program_id(1)

    @pl.when(j == 0)
    def _():
        h = _rms(x_ref[...]) * gpre_ref[...]
        h_ref[...] = (h * (1.0 + sc_ref[0]) + sh_ref[0]).astype(BF16)
        o_ref[...] = jnp.zeros_like(o_ref)

    h = h_ref[...]
    g = jnp.dot(h, wg_ref[...], preferred_element_type=F32)
    u = jnp.dot(h, wu_ref[...], preferred_element_type=F32)
    a = (g * _sigmoid(g) * u).astype(BF16)
    o_ref[...] += jnp.dot(a, wd_ref[...], preferred_element_type=F32)

    @pl.when(j == pl.num_programs(1) - 1)
    def _():
        o_ref[...] = x_ref[...] + gate_ref[0] * (_rms(o_ref[...]) * gpost_ref[...])


def _ffn_call(x, g_pre, shift, scale, wg, wu, wd, g_post, gate, *, seq):
    n, d = x.shape
    f = wg.shape[1]
    tm = _tile(seq, 512)
    tf = _tile(f, 1024)
    tpb = seq // tm
    return pl.pallas_call(
        _ffn_kernel,
        out_shape=jax.ShapeDtypeStruct((n, d), F32),
        grid=(n // tm, f // tf),
        in_specs=[pl.BlockSpec((tm, d), lambda i, j: (i, 0)),
                  pl.BlockSpec((1, d), lambda i, j: (0, 0)),
                  pl.BlockSpec((1, 1, d), lambda i, j: (i // tpb, 0, 0)),
                  pl.BlockSpec((1, 1, d), lambda i, j: (i // tpb, 0, 0)),
                  pl.BlockSpec((d, tf), lambda i, j: (0, j)),
                  pl.BlockSpec((d, tf), lambda i, j: (0, j)),
                  pl.BlockSpec((tf, d), lambda i, j: (j, 0)),
                  pl.BlockSpec((1, d), lambda i, j: (0, 0)),
                  pl.BlockSpec((1, 1, d), lambda i, j: (i // tpb, 0, 0))],
        out_specs=pl.BlockSpec((tm, d), lambda i, j: (i, 0)),
        scratch_shapes=[pltpu.VMEM((tm, d), BF16)],
        compiler_params=_cparams(("parallel", "arbitrary")),
        name="l0_ffn",
    )(x, g_pre.reshape(1, d), shift, scale, wg, wu, wd, g_post.reshape(1, d), gate)


def _compress_kernel(x_ref, pe_ref, w1_ref, w2_ref, o_ref, ot_ref):
    half = CMP_STRIDE * HEAD_DIM
    x = x_ref[0]
    nc = x.shape[0]
    a = jnp.dot(x, w1_ref[0:half, :], preferred_element_type=F32)
    b = jnp.dot(x, w1_ref[half:2 * half, :], preferred_element_type=F32)
    bias = jnp.dot(pe_ref[...], w1_ref[...], preferred_element_type=F32)[0:1, :]
    hid = _gelu_tanh(a + pltpu.roll(b, nc - 1, 0) + bias)
    out = jnp.dot(hid.astype(BF16), w2_ref[...], preferred_element_type=F32)
    rowi = lax.broadcasted_iota(jnp.int32, out.shape, 0)
    out = jnp.where(rowi < nc - 1, out, 0.0)
    o_ref[0] = out.astype(BF16)
    ot_ref[0] = out.T.astype(BF16)


def _compress_call(xc, pe, w1, w2):
    bg, nc, kdim = xc.shape
    pe_flat = jnp.pad(pe.reshape(1, -1), ((0, SUBLANES - 1), (0, 0))).astype(BF16)
    w1f = w1.reshape(CMP_BLOCK * HEAD_DIM, CMP_HIDDEN).astype(BF16)
    return pl.pallas_call(
        _compress_kernel,
        out_shape=[jax.ShapeDtypeStruct((bg, nc, HEAD_DIM), BF16),
                   jax.ShapeDtypeStruct((bg, HEAD_DIM, nc), BF16)],
        grid=(bg,),
        in_specs=[pl.BlockSpec((1, nc, kdim), lambda i: (i, 0, 0)),
                  pl.BlockSpec((SUBLANES, CMP_BLOCK * HEAD_DIM), lambda i: (0, 0)),
                  pl.BlockSpec((CMP_BLOCK * HEAD_DIM, CMP_HIDDEN), lambda i: (0, 0)),
                  pl.BlockSpec((CMP_HIDDEN, HEAD_DIM), lambda i: (0, 0))],
        out_specs=[pl.BlockSpec((1, nc, HEAD_DIM), lambda i: (i, 0, 0)),
                   pl.BlockSpec((1, HEAD_DIM, nc), lambda i: (i, 0, 0))],
        compiler_params=_cparams(("parallel",)),
        name="nsa_compress",
    )(xc, pe_flat, w1f, w2.astype(BF16))


def _q_transposed(q_ref):
    parts = []
    for r in range(Q_PER_KV):
        parts.append(q_ref[:, r * HEAD_DIM:(r + 1) * HEAD_DIM].astype(F32).T.astype(BF16))
    return jnp.concatenate(parts, axis=1)


def _attn_cmp_kernel(q_ref, kc_ref, vct_ref, ovt_ref, slope_ref, oc_ref, sel_ref, cnt_ref, imp_ref,
                     *, n_sel, n_class):
    qb = pl.program_id(2)
    q0 = qb * Q_BLOCK
    qt = _q_transposed(q_ref)
    ncp = kc_ref.shape[1]
    nslc = ovt_ref.shape[0]
    rows_per_class = ncp // n_class
    visible = (Q_BLOCK // CMP_STRIDE) * (qb + 1) - 1
    cls = jnp.minimum((visible - 1) // rows_per_class, n_class - 1)

    def compressed(n):
        s_all = jnp.dot(kc_ref[0, 0:n, :], qt, preferred_element_type=F32)
        isub = lax.broadcasted_iota(jnp.int32, (n, Q_BLOCK), 0)
        tl = lax.broadcasted_iota(jnp.int32, (n, Q_BLOCK), 1) + q0
        dist = tl - (isub * CMP_STRIDE + (CMP_BLOCK - 1))
        valid = dist >= 0
        distf = dist.astype(F32)
        psum = jnp.zeros((n, Q_BLOCK), F32)
        pparts = []
        for r in range(Q_PER_KV):
            ls = slice(r * Q_BLOCK, (r + 1) * Q_BLOCK)
            s = s_all[:, ls] - slope_ref[0, 0:1, ls] * distf
            s = jnp.where(valid, s, NEG_INF)
            m = jnp.max(s, axis=0, keepdims=True)
            e = jnp.exp(s - m)
            p = jnp.where(valid, e / jnp.sum(e, axis=0, keepdims=True), 0.0)
            psum = psum + p
            pparts.append(p.astype(BF16))
        pt = jnp.concatenate(pparts, axis=1)
        oc_ref[0, 0, 0] = jnp.dot(vct_ref[0, :, 0:n], pt, preferred_element_type=F32)
        hi = psum.astype(BF16)
        lo = (psum - hi.astype(F32)).astype(BF16)
        ovt = ovt_ref[:, 0:n]
        imp_ref[...] = (jnp.dot(ovt, hi, preferred_element_type=F32)
                        + jnp.dot(ovt, lo, preferred_element_type=F32))

    for k in range(n_class):
        pl.when(cls == k)(functools.partial(compressed, (k + 1) * rows_per_class))

    jb = lax.broadcasted_iota(jnp.int32, (nslc, Q_BLOCK), 0)
    cur = (lax.broadcasted_iota(jnp.int32, (nslc, Q_BLOCK), 1) + q0) // SLC_BLOCK
    forced = (jb == 0) | (jb == cur) | (jb == cur - 1)
    score = jnp.where(forced, FORCE_SCORE, imp_ref[...])
    score = jnp.where(jb <= cur, score, -1.0)
    picked = -3e38
    three_forced = qb > 0
    score = jnp.where(forced & three_forced, picked, score)

    def pick_next(_, work):
        m = jnp.max(work, axis=0, keepdims=True)
        first = jnp.min(jnp.where(work == m, jb, nslc), axis=0, keepdims=True)
        return jnp.where(jb == first, picked, work)

    work = lax.fori_loop(0, jnp.where(three_forced, n_sel - 3, n_sel), pick_next, score)
    chosen = (work == picked) & (jb <= cur)
    sel_ref[0, 0, 0] = jnp.where(chosen, 0.0, NEG_INF)
    flags = jnp.where(chosen, 1.0, 0.0).astype(BF16)
    cnt_ref[0, 0, 0] = lax.dot_general(jnp.ones((SUBLANES, Q_BLOCK), BF16), flags, (((1,), (1,)), ((), ())),
                                       preferred_element_type=F32)


def _attn_cmp_call(proj, kcmp, vcmpt, ovt, slopes, *, bsz, seq):
    nqb = seq // Q_BLOCK
    ncp = kcmp.shape[1]
    nslc = seq // SLC_BLOCK
    n_sel = min(N_SELECT, nslc)
    return pl.pallas_call(
        functools.partial(_attn_cmp_kernel, n_sel=n_sel, n_class=4),
        out_shape=[jax.ShapeDtypeStruct((bsz, N_KV, nqb, HEAD_DIM, GQ_WIDTH), F32),
                   jax.ShapeDtypeStruct((bsz, N_KV, nqb, nslc, Q_BLOCK), F32),
                   jax.ShapeDtypeStruct((bsz, N_KV, nqb, SUBLANES, nslc), F32)],
        grid=(bsz, N_KV, nqb),
        in_specs=[pl.BlockSpec((Q_BLOCK, GQ_WIDTH), lambda b, g, q: (b * nqb + q, g)),
                  pl.BlockSpec((1, ncp, HEAD_DIM), lambda b, g, q: (b * N_KV + g, 0, 0)),
                  pl.BlockSpec((1, HEAD_DIM, ncp), lambda b, g, q: (b * N_KV + g, 0, 0)),
                  pl.BlockSpec((nslc, ncp), lambda b, g, q: (0, 0)),
                  pl.BlockSpec((1, SUBLANES, GQ_WIDTH), lambda b, g, q: (g, 0, 0))],
        out_specs=[pl.BlockSpec((1, 1, 1, HEAD_DIM, GQ_WIDTH), lambda b, g, q: (b, g, q, 0, 0)),
                   pl.BlockSpec((1, 1, 1, nslc, Q_BLOCK), lambda b, g, q: (b, g, q, 0, 0)),
                   pl.BlockSpec((1, 1, 1, SUBLANES, nslc), lambda b, g, q: (b, g, q, 0, 0))],
        scratch_shapes=[pltpu.VMEM((nslc, Q_BLOCK), F32)],
        compiler_params=_cparams(("parallel", "parallel", "arbitrary")),
        name="nsa_compressed_select",
    )(proj, kcmp, vcmpt, ovt, slopes)


N_WTILES = WINDOW // Q_BLOCK + 1
HALF = GQ_WIDTH // 2
SEL_TILES = 4


def _attn_sel_kernel(bits_ref, q_ref, ks_ref, vst_ref, kw_ref, vwt_ref, selb_ref, oc_ref, gates_ref, slope_ref,
                     wb_ref, o_ref, qt_ref, m_ref, l_ref, acc_ref, os_ref, list_ref, *, words):
    b = pl.program_id(0)
    g = pl.program_id(1)
    qb = pl.program_id(2)
    nqb = pl.num_programs(2)
    qt_ref[...] = _q_transposed(q_ref)

    def reset():
        m_ref[...] = jnp.full(m_ref.shape, NEG_INF, F32)
        l_ref[...] = jnp.zeros(l_ref.shape, F32)
        acc_ref[...] = jnp.zeros(acc_ref.shape, F32)

    def key_tile(ref, t):
        return ref[0, pl.ds(pl.multiple_of(t * Q_BLOCK, Q_BLOCK), Q_BLOCK), :]

    def step(k_ref, vt_ref, tiles, widx, use_selection):
        nt = len(tiles)
        ks = [key_tile(k_ref, t) for t in tiles]
        vt = jnp.concatenate([vt_ref[0, t] for t in tiles], axis=1)
        for h in range(2):
            cs = slice(h * HALF, (h + 1) * HALF)
            us, offs = [], []
            mx = None
            for j in range(nt):
                s = jnp.dot(ks[j], qt_ref[:, cs], preferred_element_type=F32)
                u = s + wb_ref[0, widx[j], :, cs]
                if use_selection:
                    two = selb_ref[0, 0, 0, tiles[j]]
                    lo = jnp.broadcast_to(jnp.concatenate([two[0:1, :]] * 2, axis=1), (SLC_BLOCK, HALF))
                    hi = jnp.broadcast_to(jnp.concatenate([two[1:2, :]] * 2, axis=1), (SLC_BLOCK, HALF))
                    u = u + jnp.concatenate([lo, hi], axis=0)
                c = slope_ref[0, 0:1, cs] * ((tiles[j] - qb) * Q_BLOCK).astype(F32)
                cm = jnp.max(u, axis=0, keepdims=True) + c
                mx = cm if mx is None else jnp.maximum(mx, cm)
                us.append(u)
                offs.append(c)
            m_old = m_ref[0:1, cs]
            m_new = jnp.maximum(m_old, mx)
            es = [jnp.exp(us[j] + (offs[j] - m_new)) for j in range(nt)]
            lsum = es[0].sum(axis=0, keepdims=True)
            for j in range(1, nt):
                lsum = lsum + es[j].sum(axis=0, keepdims=True)
            e = jnp.concatenate([x.astype(BF16) for x in es], axis=0)
            alpha = jnp.exp(m_old - m_new)
            l_ref[0:1, cs] = alpha * l_ref[0:1, cs] + lsum
            acc_ref[:, cs] = alpha * acc_ref[:, cs] + jnp.dot(vt, e, preferred_element_type=F32)
            m_ref[0:1, cs] = m_new

    base = ((b * N_KV + g) * nqb + qb) * (words + 1)

    def collect(p, n):
        list_ref[n] = p
        return n + ((bits_ref[base + 1 + (p >> 5)] >> (p & 31)) & 1)

    list_ref[0] = 0
    n_first = jnp.where(qb > 0, bits_ref[base + 1] & 1, 0)
    n_act = lax.fori_loop(jnp.maximum(bits_ref[base], 1), qb, collect, n_first)
    list_ref[n_act] = qb
    for j in range(1, SEL_TILES):
        list_ref[n_act + j] = -1

    reset()

    def sel_body(s, carry):
        idx = [list_ref[s * SEL_TILES + j] for j in range(SEL_TILES)]
        tiles = [jnp.maximum(i, 0) for i in idx]
        widx = [jnp.where(i < 0, N_WTILES, jnp.where(i == qb, N_WTILES - 1, 1)) for i in idx]
        step(ks_ref, vst_ref, tiles, widx, True)
        return carry

    lax.fori_loop(0, (n_act + SEL_TILES) // SEL_TILES, sel_body, 0)
    os_ref[...] = acc_ref[...] / l_ref[0:1, :]

    reset()
    kts = [qb - (N_WTILES - 1) + w for w in range(N_WTILES)]
    step(kw_ref, vwt_ref, [jnp.maximum(kt, 0) for kt in kts],
         [jnp.where(kt < 0, N_WTILES, w) for w, kt in enumerate(kts)], False)

    gt = gates_ref[0, 0, 0]
    ot = (gt[0:1, :] * oc_ref[0, 0, 0] + gt[1:2, :] * os_ref[...]
          + gt[2:3, :] * (acc_ref[...] / l_ref[0:1, :]))
    for r in range(Q_PER_KV):
        cs = slice(r * HEAD_DIM, (r + 1) * HEAD_DIM)
        o_ref[:, cs] = ot[:, cs].T.astype(o_ref.dtype)


def _window_bias_tiles():
    h = np.arange(1, N_HEADS + 1, dtype=np.float64)
    slope = np.exp2(-8.0 * h / N_HEADS).astype(np.float32).reshape(N_KV, 1, 1, Q_PER_KV, 1)
    sub = np.arange(Q_BLOCK, dtype=np.float32).reshape(1, 1, Q_BLOCK, 1, 1)
    tq = np.arange(Q_BLOCK).reshape(1, 1, 1, 1, Q_BLOCK)
    ks = np.arange(Q_BLOCK).reshape(1, 1, Q_BLOCK, 1, 1)
    mask = np.zeros((1, N_WTILES + 1, Q_BLOCK, 1, Q_BLOCK), np.float32)
    mask[0, 0] = np.where(tq[0, 0] < ks[0, 0], 0.0, NEG_INF)
    mask[0, N_WTILES - 1] = np.where(tq[0, 0] >= ks[0, 0], 0.0, NEG_INF)
    mask[0, N_WTILES] = NEG_INF
    out = (slope * sub).astype(np.float32) + mask
    return jnp.asarray(out.reshape(N_KV, N_WTILES + 1, Q_BLOCK, GQ_WIDTH))


def _attn_sel_call(bits, proj, vst, vwt, selb, oc, gates_t, slopes, wbias, *, bsz, seq):
    nqb = seq // Q_BLOCK
    n = bsz * seq
    proj3 = proj.reshape(bsz, seq, proj.shape[1])
    ks_col = (Q_WIDTH + 2 * KV_WIDTH) // HEAD_DIM
    kw_col = (Q_WIDTH + 4 * KV_WIDTH) // HEAD_DIM
    selb5 = selb.reshape(bsz, N_KV, nqb, nqb, 2, Q_BLOCK)
    words = bits.shape[0] // (bsz * N_KV * nqb) - 1
    grid_spec = pltpu.PrefetchScalarGridSpec(
        num_scalar_prefetch=1,
        grid=(bsz, N_KV, nqb),
        in_specs=[pl.BlockSpec((Q_BLOCK, GQ_WIDTH), lambda b, g, q, s: (b * nqb + q, g)),
                  pl.BlockSpec((1, seq, HEAD_DIM), lambda b, g, q, s: (b, 0, ks_col + g)),
                  pl.BlockSpec((1, nqb, HEAD_DIM, Q_BLOCK), lambda b, g, q, s: (b * N_KV + g, 0, 0, 0)),
                  pl.BlockSpec((1, seq, HEAD_DIM), lambda b, g, q, s: (b, 0, kw_col + g)),
                  pl.BlockSpec((1, nqb, HEAD_DIM, Q_BLOCK), lambda b, g, q, s: (b * N_KV + g, 0, 0, 0)),
                  pl.BlockSpec((1, 1, 1, nqb, 2, Q_BLOCK), lambda b, g, q, s: (b, g, q, 0, 0, 0)),
                  pl.BlockSpec((1, 1, 1, HEAD_DIM, GQ_WIDTH), lambda b, g, q, s: (b, g, q, 0, 0)),
                  pl.BlockSpec((1, 1, 1, SUBLANES, GQ_WIDTH), lambda b, g, q, s: (b, g, q, 0, 0)),
                  pl.BlockSpec((1, SUBLANES, GQ_WIDTH), lambda b, g, q, s: (g, 0, 0)),
                  pl.BlockSpec((1, N_WTILES + 1, Q_BLOCK, GQ_WIDTH), lambda b, g, q, s: (g, 0, 0, 0))],
        out_specs=pl.BlockSpec((Q_BLOCK, GQ_WIDTH), lambda b, g, q, s: (b * nqb + q, g)),
        scratch_shapes=[pltpu.VMEM((HEAD_DIM, GQ_WIDTH), BF16),
                        pltpu.VMEM((SUBLANES, GQ_WIDTH), F32),
                        pltpu.VMEM((SUBLANES, GQ_WIDTH), F32),
                        pltpu.VMEM((HEAD_DIM, GQ_WIDTH), F32),
                        pltpu.VMEM((HEAD_DIM, GQ_WIDTH), F32),
                        pltpu.SMEM((nqb + SEL_TILES,), jnp.int32)],
    )
    return pl.pallas_call(
        functools.partial(_attn_sel_kernel, words=words),
        out_shape=jax.ShapeDtypeStruct((n, Q_WIDTH), BF16),
        grid_spec=grid_spec,
        compiler_params=_cparams(("parallel", "parallel", "arbitrary")),
        name="nsa_selected_window",
    )(bits, proj, proj3, vst, proj3, vwt, selb5, oc, gates_t, slopes, wbias)


def _active_pair_bits(cnt, *, bsz, seq):
    nqb = seq // Q_BLOCK
    words = (nqb + 31) // 32
    act = (cnt[:, :, :, 0, :] > 0.5).reshape(bsz, N_KV, nqb, nqb, 2).any(axis=-1)
    tile_id = jnp.arange(nqb, dtype=jnp.int32)
    first = jnp.min(jnp.where(act & (tile_id >= 1), tile_id, nqb), axis=-1, keepdims=True)
    act = jnp.pad(act, ((0, 0), (0, 0), (0, 0), (0, words * 32 - nqb))).reshape(bsz, N_KV, nqb, words, 32)
    weights = jnp.left_shift(jnp.uint32(1), jnp.arange(32, dtype=jnp.uint32))
    packed = jnp.sum(act.astype(jnp.uint32) * weights, axis=-1, dtype=jnp.uint32)
    packed = lax.bitcast_convert_type(packed, jnp.int32)
    return jnp.concatenate([first, packed], axis=-1).reshape(-1)


def _outproj_router_kernel(a_ref, x_ref, w_ref, gpost_ref, gate_ref, gpre_ref, sh_ref, sc_ref, wr_ref, wrlo_ref,
                           x2_ref, h2_ref, rt_ref):
    y = jnp.dot(a_ref[...], w_ref[...], preferred_element_type=F32)
    x2 = x_ref[...] + gate_ref[0] * (_rms(y) * gpost_ref[...])
    x2_ref[...] = x2
    h = _rms(x2) * gpre_ref[...]
    h = h * (1.0 + sc_ref[0]) + sh_ref[0]
    hb = h.astype(BF16)
    h2_ref[...] = hb
    h_lo = (h - hb.astype(F32)).astype(BF16)
    logits = (jnp.dot(hb, wr_ref[...], preferred_element_type=F32)
              + jnp.dot(h_lo, wr_ref[...], preferred_element_type=F32)
              + jnp.dot(hb, wrlo_ref[...], preferred_element_type=F32))
    lane = lax.broadcasted_iota(jnp.int32, logits.shape, 1)
    logits = jnp.where(lane < N_EXPERTS, logits, -jnp.inf)
    v1 = jnp.max(logits, axis=-1, keepdims=True)
    i1 = jnp.min(jnp.where(logits == v1, lane, LANES), axis=-1, keepdims=True)
    rest = jnp.where(lane == i1, -jnp.inf, logits)
    v2 = jnp.max(rest, axis=-1, keepdims=True)
    i2 = jnp.min(jnp.where(rest == v2, lane, LANES), axis=-1, keepdims=True)
    e2 = jnp.exp(v2 - v1)
    w1 = 1.0 / (1.0 + e2)
    w2 = e2 / (1.0 + e2)
    out = jnp.where(lane == 0, w1, jnp.where(lane == 1, w2, 0.0))
    out = jnp.where(lane == 2, i1.astype(F32), jnp.where(lane == 3, i2.astype(F32), out))
    rt_ref[...] = out


def _outproj_router_call(a, x, w_out, g_post, gate, g_pre, shift, scale, w_router, *, seq):
    n, d = x.shape
    tm = _tile(seq, 512)
    tpb = seq // tm
    wr32 = jnp.pad(w_router, ((0, 0), (0, LANES - w_router.shape[1])))
    wr = wr32.astype(BF16)
    wr_lo = (wr32 - wr.astype(F32)).astype(BF16)
    row = lambda i: (i, 0)
    fixed = lambda i: (0, 0)
    per_b = lambda i: (i // tpb, 0, 0)
    return pl.pallas_call(
        _outproj_router_kernel,
        out_shape=[jax.ShapeDtypeStruct((n, d), F32),
                   jax.ShapeDtypeStruct((n, d), BF16),
                   jax.ShapeDtypeStruct((n, LANES), F32)],
        grid=(n // tm,),
        in_specs=[pl.BlockSpec((tm, a.shape[1]), row),
                  pl.BlockSpec((tm, d), row),
                  pl.BlockSpec(w_out.shape, fixed),
                  pl.BlockSpec((1, d), fixed),
                  pl.BlockSpec((1, 1, d), per_b),
                  pl.BlockSpec((1, d), fixed),
                  pl.BlockSpec((1, 1, d), per_b),
                  pl.BlockSpec((1, 1, d), per_b),
                  pl.BlockSpec((d, LANES), fixed),
                  pl.BlockSpec((d, LANES), fixed)],
        out_specs=[pl.BlockSpec((tm, d), row), pl.BlockSpec((tm, d), row), pl.BlockSpec((tm, LANES), row)],
        compiler_params=_cparams(("parallel",)),
        name="l1_outproj_router",
    )(a, x, w_out, g_post.reshape(1, d), gate, g_pre.reshape(1, d), shift, scale, wr, wr_lo)


def _moe_kernel(te_ref, nt_ref, h_ref, wg_ref, wu_ref, wd_ref, o_ref, acc_ref):
    i = pl.program_id(0)
    j = pl.program_id(1)

    @pl.when(j == 0)
    def _():
        acc_ref[...] = jnp.zeros_like(acc_ref)

    @pl.when(i < nt_ref[0])
    def _():
        h = h_ref[...]
        g = jnp.dot(h, wg_ref[0], preferred_element_type=F32)
        u = jnp.dot(h, wu_ref[0], preferred_element_type=F32)
        a = (g * _sigmoid(g) * u).astype(BF16)
        acc_ref[...] += jnp.dot(a, wd_ref[0], preferred_element_type=F32)

    @pl.when(j == pl.num_programs(1) - 1)
    def _():
        o_ref[...] = acc_ref[...].astype(o_ref.dtype)


def _moe_call(tile_expert, n_tiles, hs, wg, wu, wd, *, tm):
    mp, d = hs.shape
    f = wg.shape[2]
    tf = _tile(f, 512)
    nj = f // tf

    def wcol(i, j, nt):
        return jnp.where(i < nt[0], j, nj - 1)

    grid_spec = pltpu.PrefetchScalarGridSpec(
        num_scalar_prefetch=2,
        grid=(mp // tm, f // tf),
        in_specs=[pl.BlockSpec((tm, d), lambda i, j, te, nt: (i, 0)),
                  pl.BlockSpec((1, d, tf), lambda i, j, te, nt: (te[i], 0, wcol(i, j, nt))),
                  pl.BlockSpec((1, d, tf), lambda i, j, te, nt: (te[i], 0, wcol(i, j, nt))),
                  pl.BlockSpec((1, tf, d), lambda i, j, te, nt: (te[i], wcol(i, j, nt), 0))],
        out_specs=pl.BlockSpec((tm, d), lambda i, j, te, nt: (i, 0)),
        scratch_shapes=[pltpu.VMEM((tm, d), F32)],
    )
    return pl.pallas_call(
        _moe_kernel,
        out_shape=jax.ShapeDtypeStruct((mp, d), BF16),
        grid_spec=grid_spec,
        compiler_params=_cparams(("arbitrary", "arbitrary")),
        name="l1_moe_experts",
    )(tile_expert, n_tiles, hs, wg, wu, wd)


def _residual_norm_kernel(x_ref, y0_ref, y1_ref, rt_ref, gpost_ref, gate_ref, o_ref):
    y = rt_ref[:, 0:1] * y0_ref[...].astype(F32) + rt_ref[:, 1:2] * y1_ref[...].astype(F32)
    o_ref[...] = x_ref[...] + gate_ref[0] * (_rms(y) * gpost_ref[...])


def _residual_norm_call(x, y0, y1, rt, g_post, gate, *, seq):
    n, d = x.shape
    tm = _tile(seq, 512)
    tpb = seq // tm
    row = lambda i: (i, 0)
    return pl.pallas_call(
        _residual_norm_kernel,
        out_shape=jax.ShapeDtypeStruct((n, d), F32),
        grid=(n // tm,),
        in_specs=[pl.BlockSpec((tm, d), row), pl.BlockSpec((tm, d), row), pl.BlockSpec((tm, d), row),
                  pl.BlockSpec((tm, LANES), row),
                  pl.BlockSpec((1, d), lambda i: (0, 0)),
                  pl.BlockSpec((1, 1, d), lambda i: (i // tpb, 0, 0))],
        out_specs=pl.BlockSpec((tm, d), row),
        compiler_params=_cparams(("parallel",)),
        name="l1_moe_residual",
    )(x, y0, y1, rt, g_post.reshape(1, d), gate)


CAST_BLOCK_BYTES = 8 * 1024 * 1024
CAST_BLOCK_COLS = 2048


def _cast_kernel(x_ref, o_ref):
    o_ref[...] = x_ref[...].astype(o_ref.dtype)


def _to_bf16(w):
    cols = w.shape[-1]
    rows = w.size // cols
    tc = _tile(cols, CAST_BLOCK_COLS)
    tr = rows
    while tr * tc * 4 > CAST_BLOCK_BYTES and tr % 2 == 0 and (tr // 2) % 16 == 0:
        tr //= 2
    out = pl.pallas_call(
        _cast_kernel,
        out_shape=jax.ShapeDtypeStruct((rows, cols), BF16),
        grid=(rows // tr, cols // tc),
        in_specs=[pl.BlockSpec((tr, tc), lambda i, j: (i, j))],
        out_specs=pl.BlockSpec((tr, tc), lambda i, j: (i, j)),
        compiler_params=_cparams(("parallel", "parallel")),
        name="weight_cast",
    )(w.reshape(rows, cols))
    return out.reshape(w.shape)


def _even_layer(x2d, c, ada_w, ada_b, mix_pre_g, mix_post_g, ffn_pre_g, ffn_post_g,
                w_in, v_ln_g, v_ln_b, w_spatial, b_spatial, conv_w, w_out,
                ffn_w_gate, ffn_w_up, ffn_w_down, *, seq):
    sh_m, sc_m, gt_m, sh_f, sc_f, gt_f = _ada_call(c, ada_w, ada_b)
    proj = _proj_call(x2d, mix_pre_g, sh_m, sc_m, _to_bf16(w_in), seq=seq, name="l0_in_proj")
    x2d = _mixer0_call(proj, x2d, v_ln_g, v_ln_b, w_spatial, b_spatial, conv_w, _to_bf16(w_out),
                       mix_post_g, gt_m, seq=seq)
    return _ffn_call(x2d, ffn_pre_g, sh_f, sc_f, _to_bf16(ffn_w_gate), _to_bf16(ffn_w_up),
                     _to_bf16(ffn_w_down), ffn_post_g, gt_f, seq=seq)


def _moe_routing(rt, *, tm):
    n = rt.shape[0]
    na = n * TOP_K
    mp = na + N_EXPERTS * tm
    e_flat = rt[:, TOP_K:2 * TOP_K].astype(jnp.int32).reshape(na)
    onehot = (e_flat[:, None] == jnp.arange(N_EXPERTS, dtype=jnp.int32)[None, :]).astype(jnp.int32)
    csum = jnp.cumsum(onehot, axis=0)
    counts = csum[-1]
    rank = jnp.sum(onehot * csum, axis=1) - 1
    padded = ((counts + tm - 1) // tm) * tm
    ends = jnp.cumsum(padded)
    starts = ends - padded
    dest = jnp.sum(onehot * starts[None, :], axis=1) + rank
    order = jnp.argsort(e_flat, stable=True).astype(jnp.int32)
    rows = jnp.arange(mp, dtype=jnp.int32)
    row_e = jnp.minimum(jnp.sum((rows[:, None] >= ends[None, :]).astype(jnp.int32), axis=1), N_EXPERTS - 1)
    slot = rows - starts[row_e]
    src = (jnp.cumsum(counts) - counts)[row_e] + slot
    row_token = jnp.where(slot < counts[row_e], order[jnp.clip(src, 0, na - 1)] // TOP_K, 0)
    n_tiles = (ends[-1] // tm).astype(jnp.int32)
    tile_ids = jnp.arange(mp // tm, dtype=jnp.int32)
    tile_expert = jnp.sum((tile_ids[:, None] * tm >= ends[None, :]).astype(jnp.int32), axis=1)
    last = jnp.sum(((n_tiles - 1) * tm >= ends).astype(jnp.int32))
    tile_expert = jnp.where(tile_ids < n_tiles, tile_expert, last).astype(jnp.int32)
    return row_token, dest.reshape(n, TOP_K), tile_expert, n_tiles.reshape(1)


def _alibi_slope_rows():
    h = np.arange(1, N_HEADS + 1, dtype=np.float64)
    s = np.exp2(-8.0 * h / N_HEADS).astype(np.float32).reshape(N_KV, 1, Q_PER_KV, 1)
    rows = np.broadcast_to(s, (N_KV, SUBLANES, Q_PER_KV, Q_BLOCK)).reshape(N_KV, SUBLANES, GQ_WIDTH)
    return jnp.asarray(rows)


def _overlap_t(seq):
    ncp = seq // CMP_STRIDE
    nslc = seq // SLC_BLOCK
    cs = np.arange(ncp)[None, :] * CMP_STRIDE
    ss = np.arange(nslc)[:, None] * SLC_BLOCK
    ov = (cs < ss + SLC_BLOCK) & (cs + CMP_BLOCK > ss) & (np.arange(ncp)[None, :] < ncp - 1)
    return jnp.asarray(ov.astype(np.float32)).astype(BF16)


def _odd_layer(x2d, c, ada_w, ada_b, mix_pre_g, mix_post_g, ffn_pre_g, ffn_post_g,
               w_in, cmp_k_pe, cmp_k_w1, cmp_k_w2, cmp_v_pe, cmp_v_w1, cmp_v_w2, w_out,
               router_w, exp_w_gate, exp_w_up, exp_w_down, *, bsz, seq):
    n, d = x2d.shape
    sh_m, sc_m, gt_m, sh_f, sc_f, gt_f = _ada_call(c, ada_w, ada_b)
    nmain = Q_WIDTH + 6 * KV_WIDTH
    w_main = w_in[:, :nmain].astype(BF16)
    w_gates = jnp.pad(w_in[:, nmain:], ((0, 0), (0, LANES - 3 * N_HEADS))).astype(BF16)
    proj, gates = _proj_call(x2d, mix_pre_g, sh_m, sc_m, w_main, w_gates, seq=seq,
                             q_cols=Q_WIDTH, q_scale=HEAD_DIM ** -0.5, name="l1_in_proj")

    nqb = seq // Q_BLOCK
    ncp = seq // CMP_STRIDE

    def kv_cols(k):
        o = Q_WIDTH + k * KV_WIDTH
        return proj[:, o:o + KV_WIDTH]

    def chunked(t):
        t = t.reshape(bsz, ncp, CMP_STRIDE, N_KV, HEAD_DIM).transpose(0, 3, 1, 2, 4)
        return t.reshape(bsz * N_KV, ncp, CMP_STRIDE * HEAD_DIM)

    def key_major_t(t):
        t = t.reshape(bsz, nqb, Q_BLOCK, N_KV, HEAD_DIM).transpose(0, 3, 1, 4, 2)
        return t.reshape(bsz * N_KV, nqb, HEAD_DIM, Q_BLOCK)

    kcmp, _ = _compress_call(chunked(kv_cols(0)), cmp_k_pe, cmp_k_w1, cmp_k_w2)
    _, vcmpt = _compress_call(chunked(kv_cols(1)), cmp_v_pe, cmp_v_w1, cmp_v_w2)
    slopes = _alibi_slope_rows()
    oc, selb, cnt = _attn_cmp_call(proj, kcmp, vcmpt, _overlap_t(seq), slopes, bsz=bsz, seq=seq)
    bits = _active_pair_bits(cnt, bsz=bsz, seq=seq)

    gt = gates[:, :3 * N_HEADS].reshape(bsz, nqb, Q_BLOCK, 3, N_KV, Q_PER_KV).transpose(0, 4, 1, 3, 5, 2)
    gt = gt.reshape(bsz, N_KV, nqb, 3, GQ_WIDTH)
    gt = jnp.pad(gt, ((0, 0), (0, 0), (0, 0), (0, SUBLANES - 3), (0, 0)))
    attn = _attn_sel_call(bits, proj, key_major_t(kv_cols(3)), key_major_t(kv_cols(5)), selb, oc, gt, slopes,
                          _window_bias_tiles(), bsz=bsz, seq=seq)

    x2, h2, rt = _outproj_router_call(attn, x2d, _to_bf16(w_out), mix_post_g, gt_m,
                                      ffn_pre_g, sh_f, sc_f, router_w, seq=seq)

    tm = min(1024, n)
    row_token, dest, tile_expert, n_tiles = _moe_routing(rt, tm=tm)
    hs = jnp.take(h2, row_token, axis=0)
    rows = _moe_call(tile_expert, n_tiles, hs, _to_bf16(exp_w_gate), _to_bf16(exp_w_up),
                     _to_bf16(exp_w_down), tm=tm)
    y0 = jnp.take(rows, dest[:, 0], axis=0)
    y1 = jnp.take(rows, dest[:, 1], axis=0)
    return _residual_norm_call(x2, y0, y1, rt, ffn_post_g, gt_f, seq=seq)


def kernel(x, c, l0_ada_w, l0_ada_b, l0_mix_pre_g, l0_mix_post_g, l0_ffn_pre_g, l0_ffn_post_g, l0_w_in, l0_v_ln_g, l0_v_ln_b, l0_w_spatial, l0_b_spatial, l0_conv_w, l0_w_out, l0_ffn_w_gate, l0_ffn_w_up, l0_ffn_w_down, l1_ada_w, l1_ada_b, l1_mix_pre_g, l1_mix_post_g, l1_ffn_pre_g, l1_ffn_post_g, l1_w_in, l1_cmp_k_pe, l1_cmp_k_w1, l1_cmp_k_w2, l1_cmp_v_pe, l1_cmp_v_w1, l1_cmp_v_w2, l1_w_out, l1_router_w, l1_exp_w_gate, l1_exp_w_up, l1_exp_w_down):
    bsz, seq, d = x.shape
    x2d = x.reshape(bsz * seq, d)
    x2d = _even_layer(x2d, c, l0_ada_w, l0_ada_b, l0_mix_pre_g, l0_mix_post_g, l0_ffn_pre_g, l0_ffn_post_g,
                      l0_w_in, l0_v_ln_g, l0_v_ln_b, l0_w_spatial, l0_b_spatial, l0_conv_w, l0_w_out,
                      l0_ffn_w_gate, l0_ffn_w_up, l0_ffn_w_down, seq=seq)
    x2d = _odd_layer(x2d, c, l1_ada_w, l1_ada_b, l1_mix_pre_g, l1_mix_post_g, l1_ffn_pre_g, l1_ffn_post_g,
                     l1_w_in, l1_cmp_k_pe, l1_cmp_k_w1, l1_cmp_k_w2, l1_cmp_v_pe, l1_cmp_v_w1, l1_cmp_v_w2,
                     l1_w_out, l1_router_w, l1_exp_w_gate, l1_exp_w_up, l1_exp_w_down, bsz=bsz, seq=seq)
    return x2d.reshape(bsz, seq, d)
```

```python
import functools
import math

import numpy as np
import jax
import jax.numpy as jnp
from jax import lax
from jax.experimental import pallas as pl
from jax.experimental.pallas import tpu as pltpu

F32 = jnp.float32
BF16 = jnp.bfloat16
HIGHEST = lax.Precision.HIGHEST

EPS = 1e-6
NEG_INF = -1e30
FORCE_SCORE = 1e4

LANES = 128
SUBLANES = 8
VMEM_LIMIT_BYTES = 56 * 1024 * 1024

A_GROUPS = 8
A_WIDTH = 1024
B_WIDTH = 1024
CHUNK = 128
CONV_W = 3
N_HEADS = 16
HEAD_DIM = 128
N_KV = 4
Q_PER_KV = N_HEADS // N_KV
Q_WIDTH = N_HEADS * HEAD_DIM
KV_WIDTH = N_KV * HEAD_DIM
GQ_WIDTH = Q_PER_KV * HEAD_DIM
CMP_BLOCK = 32
CMP_STRIDE = 16
CMP_HIDDEN = 256
SLC_BLOCK = 64
N_SELECT = 16
WINDOW = 512
Q_BLOCK = 128
N_EXPERTS = 8
TOP_K = 2


def _cparams(semantics):
    return pltpu.CompilerParams(dimension_semantics=semantics, vmem_limit_bytes=VMEM_LIMIT_BYTES)


def _sigmoid(x):
    return 1.0 / (1.0 + jnp.exp(-x))


def _gelu_tanh(x):
    return 0.5 * x * (1.0 + jnp.tanh(0.7978845608028654 * (x + 0.044715 * (x * x * x))))


def _rms(x):
    return x * lax.rsqrt(jnp.mean(x * x, axis=-1, keepdims=True) + EPS)


def _tile(n, pref):
    if n <= pref:
        return n
    t = (pref // LANES) * LANES
    while n % t:
        t -= LANES
    assert t > 0, (n, pref)
    return t


def _ada_kernel(c_ref, w_ref, b_ref, o_ref):
    c = c_ref[...]
    s = c * _sigmoid(c)
    o_ref[...] = jnp.dot(s, w_ref[...], preferred_element_type=F32, precision=HIGHEST) + b_ref[...]


def _ada_call(c, w, b):
    bsz, d = c.shape
    n = w.shape[1]
    tn = _tile(n, 1024)
    cp = jnp.pad(c, ((0, SUBLANES - bsz), (0, 0)))
    out = pl.pallas_call(
        _ada_kernel,
        out_shape=jax.ShapeDtypeStruct((SUBLANES, n), F32),
        grid=(n // tn,),
        in_specs=[pl.BlockSpec((SUBLANES, d), lambda j: (0, 0)),
                  pl.BlockSpec((d, tn), lambda j: (0, j)),
                  pl.BlockSpec((1, tn), lambda j: (0, j))],
        out_specs=pl.BlockSpec((SUBLANES, tn), lambda j: (0, j)),
        compiler_params=_cparams(("parallel",)),
        name="ada_modulation",
    )(cp, w, b.reshape(1, n))
    m = out[:bsz]
    return [t.reshape(bsz, 1, d) for t in jnp.split(m, 6, axis=-1)]


def _proj_kernel(*refs, q_tiles, q_scale, with_gates):
    if with_gates:
        x_ref, g_ref, sh_ref, sc_ref, w_ref, wg_ref, o_ref, og_ref, h_ref = refs
    else:
        x_ref, g_ref, sh_ref, sc_ref, w_ref, o_ref, h_ref = refs
    j = pl.program_id(1)

    @pl.when(j == 0)
    def _():
        h = _rms(x_ref[...]) * g_ref[...]
        h = h * (1.0 + sc_ref[0]) + sh_ref[0]
        h_ref[...] = h.astype(BF16)
        if with_gates:
            og_ref[...] = _sigmoid(jnp.dot(h_ref[...], wg_ref[...], preferred_element_type=F32))

    acc = jnp.dot(h_ref[...], w_ref[...], preferred_element_type=F32)
    if q_tiles:
        acc = acc * jnp.where(j < q_tiles, jnp.float32(q_scale), jnp.float32(1.0))
    o_ref[...] = acc.astype(o_ref.dtype)


def _proj_call(x, g, shift, scale, w, wg=None, *, seq, q_cols=0, q_scale=1.0, name):
    n, d = x.shape
    nout = w.shape[1]
    tm = _tile(seq, 1024)
    tn = _tile(nout, 1024)
    assert q_cols % tn == 0
    tpb = seq // tm
    with_gates = wg is not None
    in_specs = [pl.BlockSpec((tm, d), lambda i, j: (i, 0)),
                pl.BlockSpec((1, d), lambda i, j: (0, 0)),
                pl.BlockSpec((1, 1, d), lambda i, j: (i // tpb, 0, 0)),
                pl.BlockSpec((1, 1, d), lambda i, j: (i // tpb, 0, 0)),
                pl.BlockSpec((d, tn), lambda i, j: (0, j))]
    args = [x, g.reshape(1, d), shift, scale, w]
    out_shape = [jax.ShapeDtypeStruct((n, nout), BF16)]
    out_specs = [pl.BlockSpec((tm, tn), lambda i, j: (i, j))]
    if with_gates:
        in_specs.append(pl.BlockSpec((d, LANES), lambda i, j: (0, 0)))
        args.append(wg)
        out_shape.append(jax.ShapeDtypeStruct((n, LANES), F32))
        out_specs.append(pl.BlockSpec((tm, LANES), lambda i, j: (i, 0)))
    out = pl.pallas_call(
        functools.partial(_proj_kernel, q_tiles=q_cols // tn, q_scale=q_scale, with_gates=with_gates),
        out_shape=out_shape,
        grid=(n // tm, nout // tn),
        in_specs=in_specs,
        out_specs=out_specs,
        scratch_shapes=[pltpu.VMEM((tm, d), BF16)],
        compiler_params=_cparams(("parallel", "arbitrary")),
        name=name,
    )(*args)
    return out if with_gates else out[0]


def _mixer0_kernel(p_ref, pgc_ref, pxb_ref, x_ref, vg_ref, vb_ref, wsp_ref, bspt_ref, cw_ref,
                   wout_ref, gpost_ref, gate_ref, o_ref, cat_ref, *, tm, tiles_per_batch):
    i = pl.program_id(0)
    nchunk = tm // CHUNK
    v = _gelu_tanh(p_ref[:, A_WIDTH:2 * A_WIDTH].astype(F32))
    mu = jnp.mean(v, axis=-1, keepdims=True)
    vc = v - mu
    var = jnp.mean(vc * vc, axis=-1, keepdims=True)
    vn = (vc * lax.rsqrt(var + EPS) * vg_ref[...] + vb_ref[...]).astype(BF16)
    row = lax.broadcasted_iota(jnp.int32, (CHUNK, CHUNK), 0)
    col = lax.broadcasted_iota(jnp.int32, (CHUNK, CHUNK), 1)
    causal = col <= row
    for h in range(A_GROUPS):
        cs = slice(h * LANES, (h + 1) * LANES)
        w = jnp.where(causal, wsp_ref[h], 0.0).astype(BF16)
        rhs = jnp.concatenate([vn[c * CHUNK:(c + 1) * CHUNK, cs] for c in range(nchunk)], axis=1)
        z = jnp.dot(w, rhs, preferred_element_type=F32) + bspt_ref[:, h:h + 1]
        for c in range(nchunk):
            rs = slice(c * CHUNK, (c + 1) * CHUNK)
            u = _gelu_tanh(p_ref[rs, cs].astype(F32))
            cat_ref[rs, cs] = (u * z[:, c * CHUNK:(c + 1) * CHUNK]).astype(BF16)
    o_gb, o_gc, o_xb = 2 * A_WIDTH, 2 * A_WIDTH + B_WIDTH, 2 * A_WIDTH + 2 * B_WIDTH
    zc = p_ref[:, o_gc:o_gc + B_WIDTH].astype(F32) * p_ref[:, o_xb:o_xb + B_WIDTH].astype(F32)
    prev = pgc_ref[...].astype(F32) * pxb_ref[...].astype(F32)
    prev = jnp.where(i % tiles_per_batch == 0, 0.0, prev)
    nprev = prev.shape[0]
    p1 = prev[nprev - 1:nprev, :]
    p2 = prev[nprev - 2:nprev - 1, :]
    rowi = lax.broadcasted_iota(jnp.int32, (tm, B_WIDTH), 0)
    r1 = jnp.where(rowi == 0, p1, pltpu.roll(zc, 1, 0))
    r2 = jnp.where(rowi == 0, p2, jnp.where(rowi == 1, p1, pltpu.roll(zc, 2, 0)))
    y = cw_ref[0:1, :] * r2 + cw_ref[1:2, :] * r1 + cw_ref[2:3, :] * zc
    cat_ref[:, A_WIDTH:] = (p_ref[:, o_gb:o_gb + B_WIDTH].astype(F32) * y).astype(BF16)
    yo = jnp.dot(cat_ref[...], wout_ref[...], preferred_element_type=F32)
    o_ref[...] = x_ref[...] + gate_ref[0] * (_rms(yo) * gpost_ref[...])


def _mixer0_call(proj, x, v_ln_g, v_ln_b, w_spatial, b_spatial, conv_w, w_out, g_post, gate, *, seq):
    n, d = x.shape
    tm = _tile(seq, 512)
    tpb = seq // tm
    pr = 16
    rpb = tm // pr
    wcat = A_WIDTH + B_WIDTH
    cwp = jnp.pad(conv_w, ((0, SUBLANES - CONV_W), (0, 0)))
    return pl.pallas_call(
        functools.partial(_mixer0_kernel, tm=tm, tiles_per_batch=tpb),
        out_shape=jax.ShapeDtypeStruct((n, d), F32),
        grid=(n // tm,),
        in_specs=[pl.BlockSpec((tm, proj.shape[1]), lambda i: (i, 0)),
                  pl.BlockSpec((pr, B_WIDTH), lambda i: (jnp.maximum(i * rpb - 1, 0), 3)),
                  pl.BlockSpec((pr, B_WIDTH), lambda i: (jnp.maximum(i * rpb - 1, 0), 4)),
                  pl.BlockSpec((tm, d), lambda i: (i, 0)),
                  pl.BlockSpec((1, A_WIDTH), lambda i: (0, 0)),
                  pl.BlockSpec((1, A_WIDTH), lambda i: (0, 0)),
                  pl.BlockSpec((A_GROUPS, CHUNK, CHUNK), lambda i: (0, 0, 0)),
                  pl.BlockSpec((CHUNK, A_GROUPS), lambda i: (0, 0)),
                  pl.BlockSpec((SUBLANES, B_WIDTH), lambda i: (0, 0)),
                  pl.BlockSpec((wcat, d), lambda i: (0, 0)),
                  pl.BlockSpec((1, d), lambda i: (0, 0)),
                  pl.BlockSpec((1, 1, d), lambda i: (i // tpb, 0, 0))],
        out_specs=pl.BlockSpec((tm, d), lambda i: (i, 0)),
        scratch_shapes=[pltpu.VMEM((tm, wcat), BF16)],
        compiler_params=_cparams(("parallel",)),
        name="l0_mixer",
    )(proj, proj, proj, x, v_ln_g.reshape(1, -1), v_ln_b.reshape(1, -1), w_spatial, b_spatial.T, cwp,
      w_out, g_post.reshape(1, d), gate)


def _ffn_kernel(x_ref, gpre_ref, sh_ref, sc_ref, wg_ref, wu_ref, wd_ref, gpost_ref, gate_ref, o_ref, h_ref):
    j = pl.program_id(1)

    @pl.when(j == 0)
    def _():
        h = _rms(x_ref[...]) * gpre_ref[...]
        h_ref[...] = (h * (1.0 + sc_ref[0]) + sh_ref[0]).astype(BF16)
        o_ref[...] = jnp.zeros_like(o_ref)

    h = h_ref[...]
    g = jnp.dot(h, wg_ref[...], preferred_element_type=F32)
    u = jnp.dot(h, wu_ref[...], preferred_element_type=F32)
    a = (g * _sigmoid(g) * u).astype(BF16)
    o_ref[...] += jnp.dot(a, wd_ref[...], preferred_element_type=F32)

    @pl.when(j == pl.num_programs(1) - 1)
    def _():
        o_ref[...] = x_ref[...] + gate_ref[0] * (_rms(o_ref[...]) * gpost_ref[...])


def _ffn_call(x, g_pre, shift, scale, wg, wu, wd, g_post, gate, *, seq):
    n, d = x.shape
    f = wg.shape[1]
    tm = _tile(seq, 512)
    tf = _tile(f, 1024)
    tpb = seq // tm
    return pl.pallas_call(
        _ffn_kernel,
        out_shape=jax.ShapeDtypeStruct((n, d), F32),
        grid=(n // tm, f // tf),
        in_specs=[pl.BlockSpec((tm, d), lambda i, j: (i, 0)),
                  pl.BlockSpec((1, d), lambda i, j: (0, 0)),
                  pl.BlockSpec((1, 1, d), lambda i, j: (i // tpb, 0, 0)),
                  pl.BlockSpec((1, 1, d), lambda i, j: (i // tpb, 0, 0)),
                  pl.BlockSpec((d, tf), lambda i, j: (0, j)),
                  pl.BlockSpec((d, tf), lambda i, j: (0, j)),
                  pl.BlockSpec((tf, d), lambda i, j: (j, 0)),
                  pl.BlockSpec((1, d), lambda i, j: (0, 0)),
                  pl.BlockSpec((1, 1, d), lambda i, j: (i // tpb, 0, 0))],
        out_specs=pl.BlockSpec((tm, d), lambda i, j: (i, 0)),
        scratch_shapes=[pltpu.VMEM((tm, d), BF16)],
        compiler_params=_cparams(("parallel", "arbitrary")),
        name="l0_ffn",
    )(x, g_pre.reshape(1, d), shift, scale, wg, wu, wd, g_post.reshape(1, d), gate)


def _compress_kernel(x_ref, pe_ref, w1_ref, w2_ref, o_ref, ot_ref):
    half = CMP_STRIDE * HEAD_DIM
    x = x_ref[0]
    nc = x.shape[0]
    a = jnp.dot(x, w1_ref[0:half, :], preferred_element_type=F32)
    b = jnp.dot(x, w1_ref[half:2 * half, :], preferred_element_type=F32)
    bias = jnp.dot(pe_ref[...], w1_ref[...], preferred_element_type=F32)[0:1, :]
    hid = _gelu_tanh(a + pltpu.roll(b, nc - 1, 0) + bias)
    out = jnp.dot(hid.astype(BF16), w2_ref[...], preferred_element_type=F32)
    rowi = lax.broadcasted_iota(jnp.int32, out.shape, 0)
    out = jnp.where(rowi < nc - 1, out, 0.0)
    o_ref[0] = out.astype(BF16)
    ot_ref[0] = out.T.astype(BF16)


def _compress_call(xc, pe, w1, w2):
    bg, nc, kdim = xc.shape
    pe_flat = jnp.pad(pe.reshape(1, -1), ((0, SUBLANES - 1), (0, 0))).astype(BF16)
    w1f = w1.reshape(CMP_BLOCK * HEAD_DIM, CMP_HIDDEN).astype(BF16)
    return pl.pallas_call(
        _compress_kernel,
        out_shape=[jax.ShapeDtypeStruct((bg, nc, HEAD_DIM), BF16),
                   jax.ShapeDtypeStruct((bg, HEAD_DIM, nc), BF16)],
        grid=(bg,),
        in_specs=[pl.BlockSpec((1, nc, kdim), lambda i: (i, 0, 0)),
                  pl.BlockSpec((SUBLANES, CMP_BLOCK * HEAD_DIM), lambda i: (0, 0)),
                  pl.BlockSpec((CMP_BLOCK * HEAD_DIM, CMP_HIDDEN), lambda i: (0, 0)),
                  pl.BlockSpec((CMP_HIDDEN, HEAD_DIM), lambda i: (0, 0))],
        out_specs=[pl.BlockSpec((1, nc, HEAD_DIM), lambda i: (i, 0, 0)),
                   pl.BlockSpec((1, HEAD_DIM, nc), lambda i: (i, 0, 0))],
        compiler_params=_cparams(("parallel",)),
        name="nsa_compress",
    )(xc, pe_flat, w1f, w2.astype(BF16))


def _q_transposed(q_ref):
    parts = []
    for r in range(Q_PER_KV):
        parts.append(q_ref[:, r * HEAD_DIM:(r + 1) * HEAD_DIM].astype(F32).T.astype(BF16))
    return jnp.concatenate(parts, axis=1)


def _attn_cmp_kernel(q_ref, kc_ref, vct_ref, ovt_ref, slope_ref, oc_ref, sel_ref, cnt_ref, imp_ref,
                     *, n_sel, n_class):
    qb = pl.program_id(2)
    q0 = qb * Q_BLOCK
    qt = _q_transposed(q_ref)
    ncp = kc_ref.shape[1]
    nslc = ovt_ref.shape[0]
    rows_per_class = ncp // n_class
    visible = (Q_BLOCK // CMP_STRIDE) * (qb + 1) - 1
    cls = jnp.minimum((visible - 1) // rows_per_class, n_class - 1)

    def compressed(n):
        s_all = jnp.dot(kc_ref[0, 0:n, :], qt, preferred_element_type=F32)
        isub = lax.broadcasted_iota(jnp.int32, (n, Q_BLOCK), 0)
        tl = lax.broadcasted_iota(jnp.int32, (n, Q_BLOCK), 1) + q0
        dist = tl - (isub * CMP_STRIDE + (CMP_BLOCK - 1))
        valid = dist >= 0
        distf = dist.astype(F32)
        psum = jnp.zeros((n, Q_BLOCK), F32)
        pparts = []
        for r in range(Q_PER_KV):
            ls = slice(r * Q_BLOCK, (r + 1) * Q_BLOCK)
            s = s_all[:, ls] - slope_ref[0, 0:1, ls] * distf
            s = jnp.where(valid, s, NEG_INF)
            m = jnp.max(s, axis=0, keepdims=True)
            e = jnp.exp(s - m)
            p = jnp.where(valid, e / jnp.sum(e, axis=0, keepdims=True), 0.0)
            psum = psum + p
            pparts.append(p.astype(BF16))
        pt = jnp.concatenate(pparts, axis=1)
        oc_ref[0, 0, 0] = jnp.dot(vct_ref[0, :, 0:n], pt, preferred_element_type=F32)
        hi = psum.astype(BF16)
        lo = (psum - hi.astype(F32)).astype(BF16)
        ovt = ovt_ref[:, 0:n]
        imp_ref[...] = (jnp.dot(ovt, hi, preferred_element_type=F32)
                        + jnp.dot(ovt, lo, preferred_element_type=F32))

    for k in range(n_class):
        pl.when(cls == k)(functools.partial(compressed, (k + 1) * rows_per_class))

    jb = lax.broadcasted_iota(jnp.int32, (nslc, Q_BLOCK), 0)
    cur = (lax.broadcasted_iota(jnp.int32, (nslc, Q_BLOCK), 1) + q0) // SLC_BLOCK
    forced = (jb == 0) | (jb == cur) | (jb == cur - 1)
    score = jnp.where(forced, FORCE_SCORE, imp_ref[...])
    score = jnp.where(jb <= cur, score, -1.0)
    picked = -3e38
    three_forced = qb > 0
    score = jnp.where(forced & three_forced, picked, score)

    def pick_next(_, work):
        m = jnp.max(work, axis=0, keepdims=True)
        first = jnp.min(jnp.where(work == m, jb, nslc), axis=0, keepdims=True)
        return jnp.where(jb == first, picked, work)

    work = lax.fori_loop(0, jnp.where(three_forced, n_sel - 3, n_sel), pick_next, score)
    chosen = (work == picked) & (jb <= cur)
    sel_ref[0, 0, 0] = jnp.where(chosen, 0.0, NEG_INF)
    flags = jnp.where(chosen, 1.0, 0.0).astype(BF16)
    cnt_ref[0, 0, 0] = lax.dot_general(jnp.ones((SUBLANES, Q_BLOCK), BF16), flags, (((1,), (1,)), ((), ())),
                                       preferred_element_type=F32)


def _attn_cmp_call(proj, kcmp, vcmpt, ovt, slopes, *, bsz, seq):
    nqb = seq // Q_BLOCK
    ncp = kcmp.shape[1]
    nslc = seq // SLC_BLOCK
    n_sel = min(N_SELECT, nslc)
    return pl.pallas_call(
        functools.partial(_attn_cmp_kernel, n_sel=n_sel, n_class=4),
        out_shape=[jax.ShapeDtypeStruct((bsz, N_KV, nqb, HEAD_DIM, GQ_WIDTH), F32),
                   jax.ShapeDtypeStruct((bsz, N_KV, nqb, nslc, Q_BLOCK), F32),
                   jax.ShapeDtypeStruct((bsz, N_KV, nqb, SUBLANES, nslc), F32)],
        grid=(bsz, N_KV, nqb),
        in_specs=[pl.BlockSpec((Q_BLOCK, GQ_WIDTH), lambda b, g, q: (b * nqb + q, g)),
                  pl.BlockSpec((1, ncp, HEAD_DIM), lambda b, g, q: (b * N_KV + g, 0, 0)),
                  pl.BlockSpec((1, HEAD_DIM, ncp), lambda b, g, q: (b * N_KV + g, 0, 0)),
                  pl.BlockSpec((nslc, ncp), lambda b, g, q: (0, 0)),
                  pl.BlockSpec((1, SUBLANES, GQ_WIDTH), lambda b, g, q: (g, 0, 0))],
        out_specs=[pl.BlockSpec((1, 1, 1, HEAD_DIM, GQ_WIDTH), lambda b, g, q: (b, g, q, 0, 0)),
                   pl.BlockSpec((1, 1, 1, nslc, Q_BLOCK), lambda b, g, q: (b, g, q, 0, 0)),
                   pl.BlockSpec((1, 1, 1, SUBLANES, nslc), lambda b, g, q: (b, g, q, 0, 0))],
        scratch_shapes=[pltpu.VMEM((nslc, Q_BLOCK), F32)],
        compiler_params=_cparams(("parallel", "parallel", "arbitrary")),
        name="nsa_compressed_select",
    )(proj, kcmp, vcmpt, ovt, slopes)


N_WTILES = WINDOW // Q_BLOCK + 1
HALF = GQ_WIDTH // 2
STEP_TILES = N_WTILES


def _attn_sel_kernel(bits_ref, q_ref, ks_ref, vst_ref, kw_ref, vwt_ref, selb_ref, oc_ref, gates_ref, slope_ref,
                     wb_ref, o_ref, qt_ref, m_ref, l_ref, acc_ref, u0_ref, u1_ref, list_ref, *, words):
    b = pl.program_id(0)
    g = pl.program_id(1)
    qb = pl.program_id(2)
    nqb = pl.num_programs(2)
    qt_ref[...] = _q_transposed(q_ref)
    m_ref[...] = jnp.full(m_ref.shape, NEG_INF, F32)
    l_ref[...] = jnp.zeros(l_ref.shape, F32)
    acc_ref[...] = jnp.zeros(acc_ref.shape, F32)

    def key_tile(ref, t):
        return ref[0, pl.ds(pl.multiple_of(t * Q_BLOCK, Q_BLOCK), Q_BLOCK), :]

    def plan(t):
        is_win = t == 0
        out = []
        for j in range(STEP_TILES):
            kt = qb - (N_WTILES - 1) + j
            idx = list_ref[jnp.maximum(t - 1, 0) * STEP_TILES + j]
            w_bias = jnp.where(kt < 0, N_WTILES, j)
            s_bias = jnp.where(idx < 0, N_WTILES, jnp.where(idx == qb, N_WTILES - 1, 1))
            out.append((jnp.where(is_win, jnp.maximum(kt, 0), jnp.maximum(idx, 0)),
                        jnp.where(is_win, w_bias, s_bias)))
        return is_win, out

    def scores(t, u_ref):
        is_win, tiles = plan(t)
        for j, (tile, widx) in enumerate(tiles):
            k = jnp.where(is_win, key_tile(kw_ref, tile), key_tile(ks_ref, tile))
            two = jnp.where(is_win, 0.0, selb_ref[0, 0, 0, tile])
            lo = jnp.broadcast_to(jnp.concatenate([two[0:1, :]] * Q_PER_KV, axis=1), (SLC_BLOCK, GQ_WIDTH))
            hi = jnp.broadcast_to(jnp.concatenate([two[1:2, :]] * Q_PER_KV, axis=1), (SLC_BLOCK, GQ_WIDTH))
            s = jnp.dot(k, qt_ref[...], preferred_element_type=F32)
            u_ref[j * Q_BLOCK:(j + 1) * Q_BLOCK, :] = s + wb_ref[0, widx] + jnp.concatenate([lo, hi], axis=0)

    def update(t, u_ref):
        is_win, tiles = plan(t)
        st = jnp.where(is_win, 0, 1)
        vt = jnp.concatenate([jnp.where(is_win, vwt_ref[0, tile], vst_ref[0, tile]) for tile, _ in tiles], axis=1)
        for h in range(2):
            cs = slice(h * HALF, (h + 1) * HALF)
            offs = [slope_ref[0, 0:1, cs] * ((tile - qb) * Q_BLOCK).astype(F32) for tile, _ in tiles]
            mx = None
            for j in range(STEP_TILES):
                cm = jnp.max(u_ref[j * Q_BLOCK:(j + 1) * Q_BLOCK, cs], axis=0, keepdims=True) + offs[j]
                mx = cm if mx is None else jnp.maximum(mx, cm)
            m_old = m_ref[st, 0:1, cs]
            m_new = jnp.maximum(m_old, mx)
            es = [jnp.exp(u_ref[j * Q_BLOCK:(j + 1) * Q_BLOCK, cs] + (offs[j] - m_new)) for j in range(STEP_TILES)]
            lsum = es[0].sum(axis=0, keepdims=True)
            for j in range(1, STEP_TILES):
                lsum = lsum + es[j].sum(axis=0, keepdims=True)
            e = jnp.concatenate([x.astype(BF16) for x in es], axis=0)
            alpha = jnp.exp(m_old - m_new)
            l_ref[st, 0:1, cs] = alpha * l_ref[st, 0:1, cs] + lsum
            acc_ref[st, :, cs] = alpha * acc_ref[st, :, cs] + jnp.dot(vt, e, preferred_element_type=F32)
            m_ref[st, 0:1, cs] = m_new

    base = ((b * N_KV + g) * nqb + qb) * (words + 1)

    def collect(p, n):
        list_ref[n] = p
        return n + ((bits_ref[base + 1 + (p >> 5)] >> (p & 31)) & 1)

    list_ref[0] = 0
    n_first = jnp.where(qb > 0, bits_ref[base + 1] & 1, 0)
    n_act = lax.fori_loop(jnp.maximum(bits_ref[base], 1), qb, collect, n_first)
    list_ref[n_act] = qb
    for j in range(1, 3 * STEP_TILES):
        list_ref[n_act + j] = -1

    n_updates = 1 + (n_act + STEP_TILES) // STEP_TILES
    scores(0, u0_ref)

    def pair(i, carry):
        scores(2 * i + 1, u1_ref)
        update(2 * i, u0_ref)
        scores(2 * i + 2, u0_ref)
        update(2 * i + 1, u1_ref)
        return carry

    lax.fori_loop(0, n_updates // 2, pair, 0)

    @pl.when(n_updates % 2 == 1)
    def _():
        update(n_updates - 1, u0_ref)

    gt = gates_ref[0, 0, 0]
    ot = (gt[0:1, :] * oc_ref[0, 0, 0] + gt[1:2, :] * (acc_ref[1] / l_ref[1, 0:1, :])
          + gt[2:3, :] * (acc_ref[0] / l_ref[0, 0:1, :]))
    for r in range(Q_PER_KV):
        cs = slice(r * HEAD_DIM, (r + 1) * HEAD_DIM)
        o_ref[:, cs] = ot[:, cs].T.astype(o_ref.dtype)


def _window_bias_tiles():
    h = np.arange(1, N_HEADS + 1, dtype=np.float64)
    slope = np.exp2(-8.0 * h / N_HEADS).astype(np.float32).reshape(N_KV, 1, 1, Q_PER_KV, 1)
    sub = np.arange(Q_BLOCK, dtype=np.float32).reshape(1, 1, Q_BLOCK, 1, 1)
    tq = np.arange(Q_BLOCK).reshape(1, 1, 1, 1, Q_BLOCK)
    ks = np.arange(Q_BLOCK).reshape(1, 1, Q_BLOCK, 1, 1)
    mask = np.zeros((1, N_WTILES + 1, Q_BLOCK, 1, Q_BLOCK), np.float32)
    mask[0, 0] = np.where(tq[0, 0] < ks[0, 0], 0.0, NEG_INF)
    mask[0, N_WTILES - 1] = np.where(tq[0, 0] >= ks[0, 0], 0.0, NEG_INF)
    mask[0, N_WTILES] = NEG_INF
    out = (slope * sub).astype(np.float32) + mask
    return jnp.asarray(out.reshape(N_KV, N_WTILES + 1, Q_BLOCK, GQ_WIDTH))


def _attn_sel_call(bits, proj, vst, vwt, selb, oc, gates_t, slopes, wbias, *, bsz, seq):
    nqb = seq // Q_BLOCK
    n = bsz * seq
    proj3 = proj.reshape(bsz, seq, proj.shape[1])
    ks_col = (Q_WIDTH + 2 * KV_WIDTH) // HEAD_DIM
    kw_col = (Q_WIDTH + 4 * KV_WIDTH) // HEAD_DIM
    selb5 = selb.reshape(bsz, N_KV, nqb, nqb, 2, Q_BLOCK)
    words = bits.shape[0] // (bsz * N_KV * nqb) - 1
    grid_spec = pltpu.PrefetchScalarGridSpec(
        num_scalar_prefetch=1,
        grid=(bsz, N_KV, nqb),
        in_specs=[pl.BlockSpec((Q_BLOCK, GQ_WIDTH), lambda b, g, q, s: (b * nqb + q, g)),
                  pl.BlockSpec((1, seq, HEAD_DIM), lambda b, g, q, s: (b, 0, ks_col + g)),
                  pl.BlockSpec((1, nqb, HEAD_DIM, Q_BLOCK), lambda b, g, q, s: (b * N_KV + g, 0, 0, 0)),
                  pl.BlockSpec((1, seq, HEAD_DIM), lambda b, g, q, s: (b, 0, kw_col + g)),
                  pl.BlockSpec((1, nqb, HEAD_DIM, Q_BLOCK), lambda b, g, q, s: (b * N_KV + g, 0, 0, 0)),
                  pl.BlockSpec((1, 1, 1, nqb, 2, Q_BLOCK), lambda b, g, q, s: (b, g, q, 0, 0, 0)),
                  pl.BlockSpec((1, 1, 1, HEAD_DIM, GQ_WIDTH), lambda b, g, q, s: (b, g, q, 0, 0)),
                  pl.BlockSpec((1, 1, 1, SUBLANES, GQ_WIDTH), lambda b, g, q, s: (b, g, q, 0, 0)),
                  pl.BlockSpec((1, SUBLANES, GQ_WIDTH), lambda b, g, q, s: (g, 0, 0)),
                  pl.BlockSpec((1, N_WTILES + 1, Q_BLOCK, GQ_WIDTH), lambda b, g, q, s: (g, 0, 0, 0))],
        out_specs=pl.BlockSpec((Q_BLOCK, GQ_WIDTH), lambda b, g, q, s: (b * nqb + q, g)),
        scratch_shapes=[pltpu.VMEM((HEAD_DIM, GQ_WIDTH), BF16),
                        pltpu.VMEM((2, SUBLANES, GQ_WIDTH), F32),
                        pltpu.VMEM((2, SUBLANES, GQ_WIDTH), F32),
                        pltpu.VMEM((2, HEAD_DIM, GQ_WIDTH), F32),
                        pltpu.VMEM((STEP_TILES * Q_BLOCK, GQ_WIDTH), F32),
                        pltpu.VMEM((STEP_TILES * Q_BLOCK, GQ_WIDTH), F32),
                        pltpu.SMEM((nqb + 3 * STEP_TILES,), jnp.int32)],
    )
    return pl.pallas_call(
        functools.partial(_attn_sel_kernel, words=words),
        out_shape=jax.ShapeDtypeStruct((n, Q_WIDTH), BF16),
        grid_spec=grid_spec,
        compiler_params=_cparams(("parallel", "parallel", "arbitrary")),
        name="nsa_selected_window",
    )(bits, proj, proj3, vst, proj3, vwt, selb5, oc, gates_t, slopes, wbias)


def _active_pair_bits(cnt, *, bsz, seq):
    nqb = seq // Q_BLOCK
    words = (nqb + 31) // 32
    act = (cnt[:, :, :, 0, :] > 0.5).reshape(bsz, N_KV, nqb, nqb, 2).any(axis=-1)
    tile_id = jnp.arange(nqb, dtype=jnp.int32)
    first = jnp.min(jnp.where(act & (tile_id >= 1), tile_id, nqb), axis=-1, keepdims=True)
    act = jnp.pad(act, ((0, 0), (0, 0), (0, 0), (0, words * 32 - nqb))).reshape(bsz, N_KV, nqb, words, 32)
    weights = jnp.left_shift(jnp.uint32(1), jnp.arange(32, dtype=jnp.uint32))
    packed = jnp.sum(act.astype(jnp.uint32) * weights, axis=-1, dtype=jnp.uint32)
    packed = lax.bitcast_convert_type(packed, jnp.int32)
    return jnp.concatenate([first, packed], axis=-1).reshape(-1)


def _outproj_router_kernel(a_ref, x_ref, w_ref, gpost_ref, gate_ref, gpre_ref, sh_ref, sc_ref, wr_ref, wrlo_ref,
                           x2_ref, h2_ref, rt_ref):
    y = jnp.dot(a_ref[...], w_ref[...], preferred_element_type=F32)
    x2 = x_ref[...] + gate_ref[0] * (_rms(y) * gpost_ref[...])
    x2_ref[...] = x2
    h = _rms(x2) * gpre_ref[...]
    h = h * (1.0 + sc_ref[0]) + sh_ref[0]
    hb = h.astype(BF16)
    h2_ref[...] = hb
    h_lo = (h - hb.astype(F32)).astype(BF16)
    logits = (jnp.dot(hb, wr_ref[...], preferred_element_type=F32)
              + jnp.dot(h_lo, wr_ref[...], preferred_element_type=F32)
              + jnp.dot(hb, wrlo_ref[...], preferred_element_type=F32))
    lane = lax.broadcasted_iota(jnp.int32, logits.shape, 1)
    logits = jnp.where(lane < N_EXPERTS, logits, -jnp.inf)
    v1 = jnp.max(logits, axis=-1, keepdims=True)
    i1 = jnp.min(jnp.where(logits == v1, lane, LANES), axis=-1, keepdims=True)
    rest = jnp.where(lane == i1, -jnp.inf, logits)
    v2 = jnp.max(rest, axis=-1, keepdims=True)
    i2 = jnp.min(jnp.where(rest == v2, lane, LANES), axis=-1, keepdims=True)
    e2 = jnp.exp(v2 - v1)
    w1 = 1.0 / (1.0 + e2)
    w2 = e2 / (1.0 + e2)
    out = jnp.where(lane == 0, w1, jnp.where(lane == 1, w2, 0.0))
    out = jnp.where(lane == 2, i1.astype(F32), jnp.where(lane == 3, i2.astype(F32), out))
    rt_ref[...] = out


def _outproj_router_call(a, x, w_out, g_post, gate, g_pre, shift, scale, w_router, *, seq):
    n, d = x.shape
    tm = _tile(seq, 512)
    tpb = seq // tm
    wr32 = jnp.pad(w_router, ((0, 0), (0, LANES - w_router.shape[1])))
    wr = wr32.astype(BF16)
    wr_lo = (wr32 - wr.astype(F32)).astype(BF16)
    row = lambda i: (i, 0)
    fixed = lambda i: (0, 0)
    per_b = lambda i: (i // tpb, 0, 0)
    return pl.pallas_call(
        _outproj_router_kernel,
        out_shape=[jax.ShapeDtypeStruct((n, d), F32),
                   jax.ShapeDtypeStruct((n, d), BF16),
                   jax.ShapeDtypeStruct((n, LANES), F32)],
        grid=(n // tm,),
        in_specs=[pl.BlockSpec((tm, a.shape[1]), row),
                  pl.BlockSpec((tm, d), row),
                  pl.BlockSpec(w_out.shape, fixed),
                  pl.BlockSpec((1, d), fixed),
                  pl.BlockSpec((1, 1, d), per_b),
                  pl.BlockSpec((1, d), fixed),
                  pl.BlockSpec((1, 1, d), per_b),
                  pl.BlockSpec((1, 1, d), per_b),
                  pl.BlockSpec((d, LANES), fixed),
                  pl.BlockSpec((d, LANES), fixed)],
        out_specs=[pl.BlockSpec((tm, d), row), pl.BlockSpec((tm, d), row), pl.BlockSpec((tm, LANES), row)],
        compiler_params=_cparams(("parallel",)),
        name="l1_outproj_router",
    )(a, x, w_out, g_post.reshape(1, d), gate, g_pre.reshape(1, d), shift, scale, wr, wr_lo)


def _moe_kernel(te_ref, nt_ref, h_ref, wg_ref, wu_ref, wd_ref, o_ref, acc_ref):
    i = pl.program_id(0)
    j = pl.program_id(1)

    @pl.when(j == 0)
    def _():
        acc_ref[...] = jnp.zeros_like(acc_ref)

    @pl.when(i < nt_ref[0])
    def _():
        h = h_ref[...]
        g = jnp.dot(h, wg_ref[0], preferred_element_type=F32)
        u = jnp.dot(h, wu_ref[0], preferred_element_type=F32)
        a = (g * _sigmoid(g) * u).astype(BF16)
        acc_ref[...] += jnp.dot(a, wd_ref[0], preferred_element_type=F32)

    @pl.when(j == pl.num_programs(1) - 1)
    def _():
        o_ref[...] = acc_ref[...].astype(o_ref.dtype)


def _moe_call(tile_expert, n_tiles, hs, wg, wu, wd, *, tm):
    mp, d = hs.shape
    f = wg.shape[2]
    tf = _tile(f, 512)
    nj = f // tf

    def wcol(i, j, nt):
        return jnp.where(i < nt[0], j, nj - 1)

    grid_spec = pltpu.PrefetchScalarGridSpec(
        num_scalar_prefetch=2,
        grid=(mp // tm, f // tf),
        in_specs=[pl.BlockSpec((tm, d), lambda i, j, te, nt: (i, 0)),
                  pl.BlockSpec((1, d, tf), lambda i, j, te, nt: (te[i], 0, wcol(i, j, nt))),
                  pl.BlockSpec((1, d, tf), lambda i, j, te, nt: (te[i], 0, wcol(i, j, nt))),
                  pl.BlockSpec((1, tf, d), lambda i, j, te, nt: (te[i], wcol(i, j, nt), 0))],
        out_specs=pl.BlockSpec((tm, d), lambda i, j, te, nt: (i, 0)),
        scratch_shapes=[pltpu.VMEM((tm, d), F32)],
    )
    return pl.pallas_call(
        _moe_kernel,
        out_shape=jax.ShapeDtypeStruct((mp, d), BF16),
        grid_spec=grid_spec,
        compiler_params=_cparams(("arbitrary", "arbitrary")),
        name="l1_moe_experts",
    )(tile_expert, n_tiles, hs, wg, wu, wd)


def _residual_norm_kernel(x_ref, y0_ref, y1_ref, rt_ref, gpost_ref, gate_ref, o_ref):
    y = rt_ref[:, 0:1] * y0_ref[...].astype(F32) + rt_ref[:, 1:2] * y1_ref[...].astype(F32)
    o_ref[...] = x_ref[...] + gate_ref[0] * (_rms(y) * gpost_ref[...])


def _residual_norm_call(x, y0, y1, rt, g_post, gate, *, seq):
    n, d = x.shape
    tm = _tile(seq, 512)
    tpb = seq // tm
    row = lambda i: (i, 0)
    return pl.pallas_call(
        _residual_norm_kernel,
        out_shape=jax.ShapeDtypeStruct((n, d), F32),
        grid=(n // tm,),
        in_specs=[pl.BlockSpec((tm, d), row), pl.BlockSpec((tm, d), row), pl.BlockSpec((tm, d), row),
                  pl.BlockSpec((tm, LANES), row),
                  pl.BlockSpec((1, d), lambda i: (0, 0)),
                  pl.BlockSpec((1, 1, d), lambda i: (i // tpb, 0, 0))],
        out_specs=pl.BlockSpec((tm, d), row),
        compiler_params=_cparams(("parallel",)),
        name="l1_moe_residual",
    )(x, y0, y1, rt, g_post.reshape(1, d), gate)


CAST_BLOCK_BYTES = 8 * 1024 * 1024
CAST_BLOCK_COLS = 2048


def _cast_kernel(x_ref, o_ref):
    o_ref[...] = x_ref[...].astype(o_ref.dtype)


def _to_bf16(w):
    cols = w.shape[-1]
    rows = w.size // cols
    tc = _tile(cols, CAST_BLOCK_COLS)
    tr = rows
    while tr * tc * 4 > CAST_BLOCK_BYTES and tr % 2 == 0 and (tr // 2) % 16 == 0:
        tr //= 2
    out = pl.pallas_call(
        _cast_kernel,
        out_shape=jax.ShapeDtypeStruct((rows, cols), BF16),
        grid=(rows // tr, cols // tc),
        in_specs=[pl.BlockSpec((tr, tc), lambda i, j: (i, j))],
        out_specs=pl.BlockSpec((tr, tc), lambda i, j: (i, j)),
        compiler_params=_cparams(("parallel", "parallel")),
        name="weight_cast",
    )(w.reshape(rows, cols))
    return out.reshape(w.shape)


def _even_layer(x2d, c, ada_w, ada_b, mix_pre_g, mix_post_g, ffn_pre_g, ffn_post_g,
                w_in, v_ln_g, v_ln_b, w_spatial, b_spatial, conv_w, w_out,
                ffn_w_gate, ffn_w_up, ffn_w_down, *, seq):
    sh_m, sc_m, gt_m, sh_f, sc_f, gt_f = _ada_call(c, ada_w, ada_b)
    proj = _proj_call(x2d, mix_pre_g, sh_m, sc_m, _to_bf16(w_in), seq=seq, name="l0_in_proj")
    x2d = _mixer0_call(proj, x2d, v_ln_g, v_ln_b, w_spatial, b_spatial, conv_w, _to_bf16(w_out),
                       mix_post_g, gt_m, seq=seq)
    return _ffn_call(x2d, ffn_pre_g, sh_f, sc_f, _to_bf16(ffn_w_gate), _to_bf16(ffn_w_up),
                     _to_bf16(ffn_w_down), ffn_post_g, gt_f, seq=seq)


def _moe_routing(rt, *, tm):
    n = rt.shape[0]
    na = n * TOP_K
    mp = na + N_EXPERTS * tm
    e_flat = rt[:, TOP_K:2 * TOP_K].astype(jnp.int32).reshape(na)
    onehot = (e_flat[:, None] == jnp.arange(N_EXPERTS, dtype=jnp.int32)[None, :]).astype(jnp.int32)
    csum = jnp.cumsum(onehot, axis=0)
    counts = csum[-1]
    rank = jnp.sum(onehot * csum, axis=1) - 1
    padded = ((counts + tm - 1) // tm) * tm
    ends = jnp.cumsum(padded)
    starts = ends - padded
    dest = jnp.sum(onehot * starts[None, :], axis=1) + rank
    order = jnp.argsort(e_flat, stable=True).astype(jnp.int32)
    rows = jnp.arange(mp, dtype=jnp.int32)
    row_e = jnp.minimum(jnp.sum((rows[:, None] >= ends[None, :]).astype(jnp.int32), axis=1), N_EXPERTS - 1)
    slot = rows - starts[row_e]
    src = (jnp.cumsum(counts) - counts)[row_e] + slot
    row_token = jnp.where(slot < counts[row_e], order[jnp.clip(src, 0, na - 1)] // TOP_K, 0)
    n_tiles = (ends[-1] // tm).astype(jnp.int32)
    tile_ids = jnp.arange(mp // tm, dtype=jnp.int32)
    tile_expert = jnp.sum((tile_ids[:, None] * tm >= ends[None, :]).astype(jnp.int32), axis=1)
    last = jnp.sum(((n_tiles - 1) * tm >= ends).astype(jnp.int32))
    tile_expert = jnp.where(tile_ids < n_tiles, tile_expert, last).astype(jnp.int32)
    return row_token, dest.reshape(n, TOP_K), tile_expert, n_tiles.reshape(1)


def _alibi_slope_rows():
    h = np.arange(1, N_HEADS + 1, dtype=np.float64)
    s = np.exp2(-8.0 * h / N_HEADS).astype(np.float32).reshape(N_KV, 1, Q_PER_KV, 1)
    rows = np.broadcast_to(s, (N_KV, SUBLANES, Q_PER_KV, Q_BLOCK)).reshape(N_KV, SUBLANES, GQ_WIDTH)
    return jnp.asarray(rows)


def _overlap_t(seq):
    ncp = seq // CMP_STRIDE
    nslc = seq // SLC_BLOCK
    cs = np.arange(ncp)[None, :] * CMP_STRIDE
    ss = np.arange(nslc)[:, None] * SLC_BLOCK
    ov = (cs < ss + SLC_BLOCK) & (cs + CMP_BLOCK > ss) & (np.arange(ncp)[None, :] < ncp - 1)
    return jnp.asarray(ov.astype(np.float32)).astype(BF16)


def _odd_layer(x2d, c, ada_w, ada_b, mix_pre_g, mix_post_g, ffn_pre_g, ffn_post_g,
               w_in, cmp_k_pe, cmp_k_w1, cmp_k_w2, cmp_v_pe, cmp_v_w1, cmp_v_w2, w_out,
               router_w, exp_w_gate, exp_w_up, exp_w_down, *, bsz, seq):
    n, d = x2d.shape
    sh_m, sc_m, gt_m, sh_f, sc_f, gt_f = _ada_call(c, ada_w, ada_b)
    nmain = Q_WIDTH + 6 * KV_WIDTH
    w_main = w_in[:, :nmain].astype(BF16)
    w_gates = jnp.pad(w_in[:, nmain:], ((0, 0), (0, LANES - 3 * N_HEADS))).astype(BF16)
    proj, gates = _proj_call(x2d, mix_pre_g, sh_m, sc_m, w_main, w_gates, seq=seq,
                             q_cols=Q_WIDTH, q_scale=HEAD_DIM ** -0.5, name="l1_in_proj")

    nqb = seq // Q_BLOCK
    ncp = seq // CMP_STRIDE

    def kv_cols(k):
        o = Q_WIDTH + k * KV_WIDTH
        return proj[:, o:o + KV_WIDTH]

    def chunked(t):
        t = t.reshape(bsz, ncp, CMP_STRIDE, N_KV, HEAD_DIM).transpose(0, 3, 1, 2, 4)
        return t.reshape(bsz * N_KV, ncp, CMP_STRIDE * HEAD_DIM)

    def key_major_t(t):
        t = t.reshape(bsz, nqb, Q_BLOCK, N_KV, HEAD_DIM).transpose(0, 3, 1, 4, 2)
        return t.reshape(bsz * N_KV, nqb, HEAD_DIM, Q_BLOCK)

    kcmp, _ = _compress_call(chunked(kv_cols(0)), cmp_k_pe, cmp_k_w1, cmp_k_w2)
    _, vcmpt = _compress_call(chunked(kv_cols(1)), cmp_v_pe, cmp_v_w1, cmp_v_w2)
    slopes = _alibi_slope_rows()
    oc, selb, cnt = _attn_cmp_call(proj, kcmp, vcmpt, _overlap_t(seq), slopes, bsz=bsz, seq=seq)
    bits = _active_pair_bits(cnt, bsz=bsz, seq=seq)

    gt = gates[:, :3 * N_HEADS].reshape(bsz, nqb, Q_BLOCK, 3, N_KV, Q_PER_KV).transpose(0, 4, 1, 3, 5, 2)
    gt = gt.reshape(bsz, N_KV, nqb, 3, GQ_WIDTH)
    gt = jnp.pad(gt, ((0, 0), (0, 0), (0, 0), (0, SUBLANES - 3), (0, 0)))
    attn = _attn_sel_call(bits, proj, key_major_t(kv_cols(3)), key_major_t(kv_cols(5)), selb, oc, gt, slopes,
                          _window_bias_tiles(), bsz=bsz, seq=seq)

    x2, h2, rt = _outproj_router_call(attn, x2d, _to_bf16(w_out), mix_post_g, gt_m,
                                      ffn_pre_g, sh_f, sc_f, router_w, seq=seq)

    tm = min(1024, n)
    row_token, dest, tile_expert, n_tiles = _moe_routing(rt, tm=tm)
    hs = jnp.take(h2, row_token, axis=0)
    rows = _moe_call(tile_expert, n_tiles, hs, _to_bf16(exp_w_gate), _to_bf16(exp_w_up),
                     _to_bf16(exp_w_down), tm=tm)
    y0 = jnp.take(rows, dest[:, 0], axis=0)
    y1 = jnp.take(rows, dest[:, 1], axis=0)
    return _residual_norm_call(x2, y0, y1, rt, ffn_post_g, gt_f, seq=seq)


def kernel(x, c, l0_ada_w, l0_ada_b, l0_mix_pre_g, l0_mix_post_g, l0_ffn_pre_g, l0_ffn_post_g, l0_w_in, l0_v_ln_g, l0_v_ln_b, l0_w_spatial, l0_b_spatial, l0_conv_w, l0_w_out, l0_ffn_w_gate, l0_ffn_w_up, l0_ffn_w_down, l1_ada_w, l1_ada_b, l1_mix_pre_g, l1_mix_post_g, l1_ffn_pre_g, l1_ffn_post_g, l1_w_in, l1_cmp_k_pe, l1_cmp_k_w1, l1_cmp_k_w2, l1_cmp_v_pe, l1_cmp_v_w1, l1_cmp_v_w2, l1_w_out, l1_router_w, l1_exp_w_gate, l1_exp_w_up, l1_exp_w_down):
    bsz, seq, d = x.shape
    x2d = x.reshape(bsz * seq, d)
    x2d = _even_layer(x2d, c, l0_ada_w, l0_ada_b, l0_mix_pre_g, l0_mix_post_g, l0_ffn_pre_g, l0_ffn_post_g,
                      l0_w_in, l0_v_ln_g, l0_v_ln_b, l0_w_spatial, l0_b_spatial, l0_conv_w, l0_w_out,
                      l0_ffn_w_gate, l0_ffn_w_up, l0_ffn_w_down, seq=seq)
    x2d = _odd_layer(x2d, c, l1_ada_w, l1_ada_b, l1_mix_pre_g, l1_mix_post_g, l1_ffn_pre_g, l1_ffn_post_g,
                     l1_w_in, l1_cmp_k_pe, l1_cmp_k_w1, l1_cmp_k_w2, l1_cmp_v_pe, l1_cmp_v_w1, l1_cmp_v_w2,
                     l1_w_out, l1_router_w, l1_exp_w_gate, l1_exp_w_up, l1_exp_w_down, bsz=bsz, seq=seq)
    return x2d.reshape(bsz, seq, d)
```

```python
import functools
import math

import numpy as np
import jax
import jax.numpy as jnp
from jax import lax
from jax.experimental import pallas as pl
from jax.experimental.pallas import tpu as pltpu

F32 = jnp.float32
BF16 = jnp.bfloat16
HIGHEST = lax.Precision.HIGHEST

EPS = 1e-6
LOG2E = math.log2(math.e)
NEG_INF = -1e30
FORCE_SCORE = 1e4

LANES = 128
SUBLANES = 8
VMEM_LIMIT_BYTES = 56 * 1024 * 1024

A_GROUPS = 8
A_WIDTH = 1024
B_WIDTH = 1024
CHUNK = 128
CONV_W = 3
N_HEADS = 16
HEAD_DIM = 128
N_KV = 4
Q_PER_KV = N_HEADS // N_KV
Q_WIDTH = N_HEADS * HEAD_DIM
KV_WIDTH = N_KV * HEAD_DIM
GQ_WIDTH = Q_PER_KV * HEAD_DIM
CMP_BLOCK = 32
CMP_STRIDE = 16
CMP_HIDDEN = 256
SLC_BLOCK = 64
N_SELECT = 16
WINDOW = 512
Q_BLOCK = 128
N_EXPERTS = 8
TOP_K = 2


def _cparams(semantics):
    return pltpu.CompilerParams(dimension_semantics=semantics, vmem_limit_bytes=VMEM_LIMIT_BYTES)


def _sigmoid(x):
    return 1.0 / (1.0 + jnp.exp(-x))


def _gelu_tanh(x):
    return 0.5 * x * (1.0 + jnp.tanh(0.7978845608028654 * (x + 0.044715 * (x * x * x))))


def _rms(x):
    return x * lax.rsqrt(jnp.mean(x * x, axis=-1, keepdims=True) + EPS)


def _tile(n, pref):
    if n <= pref:
        return n
    t = (pref // LANES) * LANES
    while n % t:
        t -= LANES
    assert t > 0, (n, pref)
    return t


def _ada_kernel(c_ref, w_ref, b_ref, o_ref):
    c = c_ref[...]
    s = c * _sigmoid(c)
    o_ref[...] = jnp.dot(s, w_ref[...], preferred_element_type=F32, precision=HIGHEST) + b_ref[...]


def _ada_call(c, w, b):
    bsz, d = c.shape
    n = w.shape[1]
    tn = _tile(n, 1024)
    cp = jnp.pad(c, ((0, SUBLANES - bsz), (0, 0)))
    out = pl.pallas_call(
        _ada_kernel,
        out_shape=jax.ShapeDtypeStruct((SUBLANES, n), F32),
        grid=(n // tn,),
        in_specs=[pl.BlockSpec((SUBLANES, d), lambda j: (0, 0)),
                  pl.BlockSpec((d, tn), lambda j: (0, j)),
                  pl.BlockSpec((1, tn), lambda j: (0, j))],
        out_specs=pl.BlockSpec((SUBLANES, tn), lambda j: (0, j)),
        compiler_params=_cparams(("parallel",)),
        name="ada_modulation",
    )(cp, w, b.reshape(1, n))
    m = out[:bsz]
    return [t.reshape(bsz, 1, d) for t in jnp.split(m, 6, axis=-1)]


def _proj_kernel(*refs, q_tiles, q_scale, with_gates):
    if with_gates:
        x_ref, g_ref, sh_ref, sc_ref, w_ref, wg_ref, o_ref, og_ref, h_ref = refs
    else:
        x_ref, g_ref, sh_ref, sc_ref, w_ref, o_ref, h_ref = refs
    j = pl.program_id(1)

    @pl.when(j == 0)
    def _():
        h = _rms(x_ref[...]) * g_ref[...]
        h = h * (1.0 + sc_ref[0]) + sh_ref[0]
        h_ref[...] = h.astype(BF16)
        if with_gates:
            og_ref[...] = _sigmoid(jnp.dot(h_ref[...], wg_ref[...], preferred_element_type=F32))

    acc = jnp.dot(h_ref[...], w_ref[...], preferred_element_type=F32)
    if q_tiles:
        acc = acc * jnp.where(j < q_tiles, jnp.float32(q_scale), jnp.float32(1.0))
    o_ref[...] = acc.astype(o_ref.dtype)


def _proj_call(x, g, shift, scale, w, wg=None, *, seq, q_cols=0, q_scale=1.0, name):
    n, d = x.shape
    nout = w.shape[1]
    tm = _tile(seq, 1024)
    tn = _tile(nout, 1024)
    assert q_cols % tn == 0
    tpb = seq // tm
    with_gates = wg is not None
    in_specs = [pl.BlockSpec((tm, d), lambda i, j: (i, 0)),
                pl.BlockSpec((1, d), lambda i, j: (0, 0)),
                pl.BlockSpec((1, 1, d), lambda i, j: (i // tpb, 0, 0)),
                pl.BlockSpec((1, 1, d), lambda i, j: (i // tpb, 0, 0)),
                pl.BlockSpec((d, tn), lambda i, j: (0, j))]
    args = [x, g.reshape(1, d), shift, scale, w]
    out_shape = [jax.ShapeDtypeStruct((n, nout), BF16)]
    out_specs = [pl.BlockSpec((tm, tn), lambda i, j: (i, j))]
    if with_gates:
        in_specs.append(pl.BlockSpec((d, LANES), lambda i, j: (0, 0)))
        args.append(wg)
        out_shape.append(jax.ShapeDtypeStruct((n, LANES), F32))
        out_specs.append(pl.BlockSpec((tm, LANES), lambda i, j: (i, 0)))
    out = pl.pallas_call(
        functools.partial(_proj_kernel, q_tiles=q_cols // tn, q_scale=q_scale, with_gates=with_gates),
        out_shape=out_shape,
        grid=(n // tm, nout // tn),
        in_specs=in_specs,
        out_specs=out_specs,
        scratch_shapes=[pltpu.VMEM((tm, d), BF16)],
        compiler_params=_cparams(("parallel", "arbitrary")),
        name=name,
    )(*args)
    return out if with_gates else out[0]


def _mixer0_kernel(p_ref, pgc_ref, pxb_ref, x_ref, vg_ref, vb_ref, wsp_ref, bspt_ref, cw_ref,
                   wout_ref, gpost_ref, gate_ref, o_ref, cat_ref, *, tm, tiles_per_batch):
    i = pl.program_id(0)
    nchunk = tm // CHUNK
    v = _gelu_tanh(p_ref[:, A_WIDTH:2 * A_WIDTH].astype(F32))
    mu = jnp.mean(v, axis=-1, keepdims=True)
    vc = v - mu
    var = jnp.mean(vc * vc, axis=-1, keepdims=True)
    vn = (vc * lax.rsqrt(var + EPS) * vg_ref[...] + vb_ref[...]).astype(BF16)
    row = lax.broadcasted_iota(jnp.int32, (CHUNK, CHUNK), 0)
    col = lax.broadcasted_iota(jnp.int32, (CHUNK, CHUNK), 1)
    causal = col <= row
    for h in range(A_GROUPS):
        cs = slice(h * LANES, (h + 1) * LANES)
        w = jnp.where(causal, wsp_ref[h], 0.0).astype(BF16)
        rhs = jnp.concatenate([vn[c * CHUNK:(c + 1) * CHUNK, cs] for c in range(nchunk)], axis=1)
        z = jnp.dot(w, rhs, preferred_element_type=F32) + bspt_ref[:, h:h + 1]
        for c in range(nchunk):
            rs = slice(c * CHUNK, (c + 1) * CHUNK)
            u = _gelu_tanh(p_ref[rs, cs].astype(F32))
            cat_ref[rs, cs] = (u * z[:, c * CHUNK:(c + 1) * CHUNK]).astype(BF16)
    o_gb, o_gc, o_xb = 2 * A_WIDTH, 2 * A_WIDTH + B_WIDTH, 2 * A_WIDTH + 2 * B_WIDTH
    zc = p_ref[:, o_gc:o_gc + B_WIDTH].astype(F32) * p_ref[:, o_xb:o_xb + B_WIDTH].astype(F32)
    prev = pgc_ref[...].astype(F32) * pxb_ref[...].astype(F32)
    prev = jnp.where(i % tiles_per_batch == 0, 0.0, prev)
    nprev = prev.shape[0]
    p1 = prev[nprev - 1:nprev, :]
    p2 = prev[nprev - 2:nprev - 1, :]
    rowi = lax.broadcasted_iota(jnp.int32, (tm, B_WIDTH), 0)
    r1 = jnp.where(rowi == 0, p1, pltpu.roll(zc, 1, 0))
    r2 = jnp.where(rowi == 0, p2, jnp.where(rowi == 1, p1, pltpu.roll(zc, 2, 0)))
    y = cw_ref[0:1, :] * r2 + cw_ref[1:2, :] * r1 + cw_ref[2:3, :] * zc
    cat_ref[:, A_WIDTH:] = (p_ref[:, o_gb:o_gb + B_WIDTH].astype(F32) * y).astype(BF16)
    yo = jnp.dot(cat_ref[...], wout_ref[...], preferred_element_type=F32)
    o_ref[...] = x_ref[...] + gate_ref[0] * (_rms(yo) * gpost_ref[...])


def _mixer0_call(proj, x, v_ln_g, v_ln_b, w_spatial, b_spatial, conv_w, w_out, g_post, gate, *, seq):
    n, d = x.shape
    tm = _tile(seq, 512)
    tpb = seq // tm
    pr = 16
    rpb = tm // pr
    wcat = A_WIDTH + B_WIDTH
    cwp = jnp.pad(conv_w, ((0, SUBLANES - CONV_W), (0, 0)))
    return pl.pallas_call(
        functools.partial(_mixer0_kernel, tm=tm, tiles_per_batch=tpb),
        out_shape=jax.ShapeDtypeStruct((n, d), F32),
        grid=(n // tm,),
        in_specs=[pl.BlockSpec((tm, proj.shape[1]), lambda i: (i, 0)),
                  pl.BlockSpec((pr, B_WIDTH), lambda i: (jnp.maximum(i * rpb - 1, 0), 3)),
                  pl.BlockSpec((pr, B_WIDTH), lambda i: (jnp.maximum(i * rpb - 1, 0), 4)),
                  pl.BlockSpec((tm, d), lambda i: (i, 0)),
                  pl.BlockSpec((1, A_WIDTH), lambda i: (0, 0)),
                  pl.BlockSpec((1, A_WIDTH), lambda i: (0, 0)),
                  pl.BlockSpec((A_GROUPS, CHUNK, CHUNK), lambda i: (0, 0, 0)),
                  pl.BlockSpec((CHUNK, A_GROUPS), lambda i: (0, 0)),
                  pl.BlockSpec((SUBLANES, B_WIDTH), lambda i: (0, 0)),
                  pl.BlockSpec((wcat, d), lambda i: (0, 0)),
                  pl.BlockSpec((1, d), lambda i: (0, 0)),
                  pl.BlockSpec((1, 1, d), lambda i: (i // tpb, 0, 0))],
        out_specs=pl.BlockSpec((tm, d), lambda i: (i, 0)),
        scratch_shapes=[pltpu.VMEM((tm, wcat), BF16)],
        compiler_params=_cparams(("parallel",)),
        name="l0_mixer",
    )(proj, proj, proj, x, v_ln_g.reshape(1, -1), v_ln_b.reshape(1, -1), w_spatial, b_spatial.T, cwp,
      w_out, g_post.reshape(1, d), gate)


def _ffn_kernel(x_ref, gpre_ref, sh_ref, sc_ref, wg_ref, wu_ref, wd_ref, gpost_ref, gate_ref, o_ref, h_ref):
    j = pl.program_id(1)

    @pl.when(j == 0)
    def _():
        h = _rms(x_ref[...]) * gpre_ref[...]
        h_ref[...] = (h * (1.0 + sc_ref[0]) + sh_ref[0]).astype(BF16)
        o_ref[...] = jnp.zeros_like(o_ref)

    h = h_ref[...]
    g = jnp.dot(h, wg_ref[...], preferred_element_type=F32)
    u = jnp.dot(h, wu_ref[...], preferred_element_type=F32)
    a = (g * _sigmoid(g) * u).astype(BF16)
    o_ref[...] += jnp.dot(a, wd_ref[...], preferred_element_type=F32)

    @pl.when(j == pl.num_programs(1) - 1)
    def _():
        o_ref[...] = x_ref[...] + gate_ref[0] * (_rms(o_ref[...]) * gpost_ref[...])


def _ffn_call(x, g_pre, shift, scale, wg, wu, wd, g_post, gate, *, seq):
    n, d = x.shape
    f = wg.shape[1]
    tm = _tile(seq, 512)
    tf = _tile(f, 1024)
    tpb = seq // tm
    return pl.pallas_call(
        _ffn_kernel,
        out_shape=jax.ShapeDtypeStruct((n, d), F32),
        grid=(n // tm, f // tf),
        in_specs=[pl.BlockSpec((tm, d), lambda i, j: (i, 0)),
                  pl.BlockSpec((1, d), lambda i, j: (0, 0)),
                  pl.BlockSpec((1, 1, d), lambda i, j: (i // tpb, 0, 0)),
                  pl.BlockSpec((1, 1, d), lambda i, j: (i // tpb, 0, 0)),
                  pl.BlockSpec((d, tf), lambda i, j: (0, j)),
                  pl.BlockSpec((d, tf), lambda i, j: (0, j)),
                  pl.BlockSpec((tf, d), lambda i, j: (j, 0)),
                  pl.BlockSpec((1, d), lambda i, j: (0, 0)),
                  pl.BlockSpec((1, 1, d), lambda i, j: (i // tpb, 0, 0))],
        out_specs=pl.BlockSpec((tm, d), lambda i, j: (i, 0)),
        scratch_shapes=[pltpu.VMEM((tm, d), BF16)],
        compiler_params=_cparams(("parallel", "arbitrary")),
        name="l0_ffn",
    )(x, g_pre.reshape(1, d), shift, scale, wg, wu, wd, g_post.reshape(1, d), gate)


def _compress_kernel(x_ref, pe_ref, w1_ref, w2_ref, o_ref, ot_ref):
    half = CMP_STRIDE * HEAD_DIM
    x = x_ref[0]
    nc = x.shape[0]
    a = jnp.dot(x, w1_ref[0:half, :], preferred_element_type=F32)
    b = jnp.dot(x, w1_ref[half:2 * half, :], preferred_element_type=F32)
    bias = jnp.dot(pe_ref[...], w1_ref[...], preferred_element_type=F32)[0:1, :]
    hid = _gelu_tanh(a + pltpu.roll(b, nc - 1, 0) + bias)
    out = jnp.dot(hid.astype(BF16), w2_ref[...], preferred_element_type=F32)
    rowi = lax.broadcasted_iota(jnp.int32, out.shape, 0)
    out = jnp.where(rowi < nc - 1, out, 0.0)
    o_ref[0] = out.astype(BF16)
    ot_ref[0] = out.T.astype(BF16)


def _compress_call(xc, pe, w1, w2):
    bg, nc, kdim = xc.shape
    pe_flat = jnp.pad(pe.reshape(1, -1), ((0, SUBLANES - 1), (0, 0))).astype(BF16)
    w1f = w1.reshape(CMP_BLOCK * HEAD_DIM, CMP_HIDDEN).astype(BF16)
    return pl.pallas_call(
        _compress_kernel,
        out_shape=[jax.ShapeDtypeStruct((bg, nc, HEAD_DIM), BF16),
                   jax.ShapeDtypeStruct((bg, HEAD_DIM, nc), BF16)],
        grid=(bg,),
        in_specs=[pl.BlockSpec((1, nc, kdim), lambda i: (i, 0, 0)),
                  pl.BlockSpec((SUBLANES, CMP_BLOCK * HEAD_DIM), lambda i: (0, 0)),
                  pl.BlockSpec((CMP_BLOCK * HEAD_DIM, CMP_HIDDEN), lambda i: (0, 0)),
                  pl.BlockSpec((CMP_HIDDEN, HEAD_DIM), lambda i: (0, 0))],
        out_specs=[pl.BlockSpec((1, nc, HEAD_DIM), lambda i: (i, 0, 0)),
                   pl.BlockSpec((1, HEAD_DIM, nc), lambda i: (i, 0, 0))],
        compiler_params=_cparams(("parallel",)),
        name="nsa_compress",
    )(xc, pe_flat, w1f, w2.astype(BF16))


def _q_transposed(q_ref):
    parts = []
    for r in range(Q_PER_KV):
        parts.append(q_ref[:, r * HEAD_DIM:(r + 1) * HEAD_DIM].astype(F32).T.astype(BF16))
    return jnp.concatenate(parts, axis=1)


def _attn_cmp_kernel(q_ref, kc_ref, vct_ref, ovt_ref, slope_ref, oc_ref, sel_ref, cnt_ref, imp_ref,
                     *, n_sel, n_class):
    qb = pl.program_id(2)
    q0 = qb * Q_BLOCK
    qt = _q_transposed(q_ref)
    ncp = kc_ref.shape[1]
    nslc = ovt_ref.shape[0]
    rows_per_class = ncp // n_class
    visible = (Q_BLOCK // CMP_STRIDE) * (qb + 1) - 1
    cls = jnp.minimum((visible - 1) // rows_per_class, n_class - 1)

    def compressed(n):
        s_all = jnp.dot(kc_ref[0, 0:n, :], qt, preferred_element_type=F32)
        isub = lax.broadcasted_iota(jnp.int32, (n, Q_BLOCK), 0)
        tl = lax.broadcasted_iota(jnp.int32, (n, Q_BLOCK), 1) + q0
        dist = tl - (isub * CMP_STRIDE + (CMP_BLOCK - 1))
        valid = dist >= 0
        distf = dist.astype(F32)
        psum = jnp.zeros((n, Q_BLOCK), F32)
        pparts = []
        for r in range(Q_PER_KV):
            ls = slice(r * Q_BLOCK, (r + 1) * Q_BLOCK)
            s = s_all[:, ls] - slope_ref[0, 0:1, ls] * distf
            s = jnp.where(valid, s, NEG_INF)
            m = jnp.max(s, axis=0, keepdims=True)
            e = jnp.exp2(s - m)
            inv = jnp.where(m > 0.5 * NEG_INF, 1.0 / jnp.sum(e, axis=0, keepdims=True), 0.0)
            p = e * inv
            psum = psum + p
            pparts.append(p.astype(BF16))
        pt = jnp.concatenate(pparts, axis=1)
        oc_ref[0, 0, 0] = jnp.dot(vct_ref[0, :, 0:n], pt, preferred_element_type=F32)
        hi = psum.astype(BF16)
        lo = (psum - hi.astype(F32)).astype(BF16)
        ovt = ovt_ref[:, 0:n]
        imp_ref[...] = (jnp.dot(ovt, hi, preferred_element_type=F32)
                        + jnp.dot(ovt, lo, preferred_element_type=F32))

    for k in range(n_class):
        pl.when(cls == k)(functools.partial(compressed, (k + 1) * rows_per_class))

    jb = lax.broadcasted_iota(jnp.int32, (nslc, Q_BLOCK), 0)
    cur = (lax.broadcasted_iota(jnp.int32, (nslc, Q_BLOCK), 1) + q0) // SLC_BLOCK
    forced = (jb == 0) | (jb == cur) | (jb == cur - 1)
    score = jnp.where(forced, FORCE_SCORE, imp_ref[...])
    score = jnp.where(jb <= cur, score, -1.0)
    picked = -3e38
    three_forced = qb > 0
    score = jnp.where(forced & three_forced, picked, score)

    def pick_next(_, work):
        m = jnp.max(work, axis=0, keepdims=True)
        first = jnp.min(jnp.where(work == m, jb, nslc), axis=0, keepdims=True)
        return jnp.where(jb == first, picked, work)

    work = lax.fori_loop(0, jnp.where(three_forced, n_sel - 3, n_sel), pick_next, score)
    chosen = (work == picked) & (jb <= cur)
    sel_ref[0, 0, 0] = jnp.where(chosen, 0.0, NEG_INF)
    flags = jnp.where(chosen, 1.0, 0.0).astype(BF16)
    cnt_ref[0, 0, 0] = lax.dot_general(jnp.ones((SUBLANES, Q_BLOCK), BF16), flags, (((1,), (1,)), ((), ())),
                                       preferred_element_type=F32)


def _attn_cmp_call(proj, kcmp, vcmpt, ovt, slopes, *, bsz, seq):
    nqb = seq // Q_BLOCK
    ncp = kcmp.shape[1]
    nslc = seq // SLC_BLOCK
    n_sel = min(N_SELECT, nslc)
    return pl.pallas_call(
        functools.partial(_attn_cmp_kernel, n_sel=n_sel, n_class=4),
        out_shape=[jax.ShapeDtypeStruct((bsz, N_KV, nqb, HEAD_DIM, GQ_WIDTH), F32),
                   jax.ShapeDtypeStruct((bsz, N_KV, nqb, nslc, Q_BLOCK), F32),
                   jax.ShapeDtypeStruct((bsz, N_KV, nqb, SUBLANES, nslc), F32)],
        grid=(bsz, N_KV, nqb),
        in_specs=[pl.BlockSpec((Q_BLOCK, GQ_WIDTH), lambda b, g, q: (b * nqb + q, g)),
                  pl.BlockSpec((1, ncp, HEAD_DIM), lambda b, g, q: (b * N_KV + g, 0, 0)),
                  pl.BlockSpec((1, HEAD_DIM, ncp), lambda b, g, q: (b * N_KV + g, 0, 0)),
                  pl.BlockSpec((nslc, ncp), lambda b, g, q: (0, 0)),
                  pl.BlockSpec((1, SUBLANES, GQ_WIDTH), lambda b, g, q: (g, 0, 0))],
        out_specs=[pl.BlockSpec((1, 1, 1, HEAD_DIM, GQ_WIDTH), lambda b, g, q: (b, g, q, 0, 0)),
                   pl.BlockSpec((1, 1, 1, nslc, Q_BLOCK), lambda b, g, q: (b, g, q, 0, 0)),
                   pl.BlockSpec((1, 1, 1, SUBLANES, nslc), lambda b, g, q: (b, g, q, 0, 0))],
        scratch_shapes=[pltpu.VMEM((nslc, Q_BLOCK), F32)],
        compiler_params=_cparams(("parallel", "parallel", "arbitrary")),
        name="nsa_compressed_select",
    )(proj, kcmp, vcmpt, ovt, slopes)


N_WTILES = WINDOW // Q_BLOCK + 1
HALF = GQ_WIDTH // 2
STEP_TILES = N_WTILES


def _attn_sel_kernel(bits_ref, q_ref, ks_ref, vst_ref, kw_ref, vwt_ref, selb_ref, oc_ref, gates_ref, slope_ref,
                     wb_ref, o_ref, qt_ref, m_ref, l_ref, acc_ref, u0_ref, u1_ref, list_ref, *, words):
    b = pl.program_id(0)
    g = pl.program_id(1)
    qb = pl.program_id(2)
    nqb = pl.num_programs(2)
    qt_ref[...] = _q_transposed(q_ref)
    m_ref[...] = jnp.full(m_ref.shape, NEG_INF, F32)
    l_ref[...] = jnp.zeros(l_ref.shape, F32)
    acc_ref[...] = jnp.zeros(acc_ref.shape, F32)

    def key_tile(ref, t):
        return ref[0, pl.ds(pl.multiple_of(t * Q_BLOCK, Q_BLOCK), Q_BLOCK), :]

    def plan(t):
        is_win = t == 0
        out = []
        for j in range(STEP_TILES):
            kt = qb - (N_WTILES - 1) + j
            idx = list_ref[jnp.maximum(t - 1, 0) * STEP_TILES + j]
            w_bias = jnp.where(kt < 0, N_WTILES, j)
            s_bias = jnp.where(idx < 0, N_WTILES, jnp.where(idx == qb, N_WTILES - 1, 1))
            out.append((jnp.where(is_win, jnp.maximum(kt, 0), jnp.maximum(idx, 0)),
                        jnp.where(is_win, w_bias, s_bias)))
        return is_win, out

    def scores(t, u_ref):
        is_win, tiles = plan(t)
        for j, (tile, widx) in enumerate(tiles):
            k = jnp.where(is_win, key_tile(kw_ref, tile), key_tile(ks_ref, tile))
            two = jnp.where(is_win, 0.0, selb_ref[0, 0, 0, tile])
            lo = jnp.broadcast_to(jnp.concatenate([two[0:1, :]] * Q_PER_KV, axis=1), (SLC_BLOCK, GQ_WIDTH))
            hi = jnp.broadcast_to(jnp.concatenate([two[1:2, :]] * Q_PER_KV, axis=1), (SLC_BLOCK, GQ_WIDTH))
            s = jnp.dot(k, qt_ref[...], preferred_element_type=F32)
            u_ref[j * Q_BLOCK:(j + 1) * Q_BLOCK, :] = s + wb_ref[0, widx] + jnp.concatenate([lo, hi], axis=0)

    def update(t, u_ref):
        is_win, tiles = plan(t)
        st = jnp.where(is_win, 0, 1)
        vt = jnp.concatenate([jnp.where(is_win, vwt_ref[0, tile], vst_ref[0, tile]) for tile, _ in tiles], axis=1)
        for h in range(2):
            cs = slice(h * HALF, (h + 1) * HALF)
            offs = [slope_ref[0, 0:1, cs] * ((tile - qb) * Q_BLOCK).astype(F32) for tile, _ in tiles]
            mx = None
            for j in range(STEP_TILES):
                cm = jnp.max(u_ref[j * Q_BLOCK:(j + 1) * Q_BLOCK, cs], axis=0, keepdims=True) + offs[j]
                mx = cm if mx is None else jnp.maximum(mx, cm)
            m_old = m_ref[st, 0:1, cs]
            m_new = jnp.maximum(m_old, mx)
            es = [jnp.exp2(u_ref[j * Q_BLOCK:(j + 1) * Q_BLOCK, cs] + (offs[j] - m_new)) for j in range(STEP_TILES)]
            lsum = es[0].sum(axis=0, keepdims=True)
            for j in range(1, STEP_TILES):
                lsum = lsum + es[j].sum(axis=0, keepdims=True)
            e = jnp.concatenate([x.astype(BF16) for x in es], axis=0)
            alpha = jnp.exp2(m_old - m_new)
            l_ref[st, 0:1, cs] = alpha * l_ref[st, 0:1, cs] + lsum
            acc_ref[st, :, cs] = alpha * acc_ref[st, :, cs] + jnp.dot(vt, e, preferred_element_type=F32)
            m_ref[st, 0:1, cs] = m_new

    base = ((b * N_KV + g) * nqb + qb) * (words + 1)

    def collect(p, n):
        list_ref[n] = p
        return n + ((bits_ref[base + 1 + (p >> 5)] >> (p & 31)) & 1)

    list_ref[0] = 0
    n_first = jnp.where(qb > 0, bits_ref[base + 1] & 1, 0)
    n_act = lax.fori_loop(jnp.maximum(bits_ref[base], 1), qb, collect, n_first)
    list_ref[n_act] = qb
    for j in range(1, 3 * STEP_TILES):
        list_ref[n_act + j] = -1

    n_updates = 1 + (n_act + STEP_TILES) // STEP_TILES
    scores(0, u0_ref)

    def pair(i, carry):
        scores(2 * i + 1, u1_ref)
        update(2 * i, u0_ref)
        scores(2 * i + 2, u0_ref)
        update(2 * i + 1, u1_ref)
        return carry

    lax.fori_loop(0, n_updates // 2, pair, 0)

    @pl.when(n_updates % 2 == 1)
    def _():
        update(n_updates - 1, u0_ref)

    gt = gates_ref[0, 0, 0]
    ot = (gt[0:1, :] * oc_ref[0, 0, 0] + gt[1:2, :] * (acc_ref[1] / l_ref[1, 0:1, :])
          + gt[2:3, :] * (acc_ref[0] / l_ref[0, 0:1, :]))
    for r in range(Q_PER_KV):
        cs = slice(r * HEAD_DIM, (r + 1) * HEAD_DIM)
        o_ref[:, cs] = ot[:, cs].T.astype(o_ref.dtype)


def _window_bias_tiles():
    h = np.arange(1, N_HEADS + 1, dtype=np.float64)
    slope = (LOG2E * np.exp2(-8.0 * h / N_HEADS)).astype(np.float32).reshape(N_KV, 1, 1, Q_PER_KV, 1)
    sub = np.arange(Q_BLOCK, dtype=np.float32).reshape(1, 1, Q_BLOCK, 1, 1)
    tq = np.arange(Q_BLOCK).reshape(1, 1, 1, 1, Q_BLOCK)
    ks = np.arange(Q_BLOCK).reshape(1, 1, Q_BLOCK, 1, 1)
    mask = np.zeros((1, N_WTILES + 1, Q_BLOCK, 1, Q_BLOCK), np.float32)
    mask[0, 0] = np.where(tq[0, 0] < ks[0, 0], 0.0, NEG_INF)
    mask[0, N_WTILES - 1] = np.where(tq[0, 0] >= ks[0, 0], 0.0, NEG_INF)
    mask[0, N_WTILES] = NEG_INF
    out = (slope * sub).astype(np.float32) + mask
    return jnp.asarray(out.reshape(N_KV, N_WTILES + 1, Q_BLOCK, GQ_WIDTH))


def _attn_sel_call(bits, proj, vst, vwt, selb, oc, gates_t, slopes, wbias, *, bsz, seq):
    nqb = seq // Q_BLOCK
    n = bsz * seq
    proj3 = proj.reshape(bsz, seq, proj.shape[1])
    ks_col = (Q_WIDTH + 2 * KV_WIDTH) // HEAD_DIM
    kw_col = (Q_WIDTH + 4 * KV_WIDTH) // HEAD_DIM
    selb5 = selb.reshape(bsz, N_KV, nqb, nqb, 2, Q_BLOCK)
    words = bits.shape[0] // (bsz * N_KV * nqb) - 1
    grid_spec = pltpu.PrefetchScalarGridSpec(
        num_scalar_prefetch=1,
        grid=(bsz, N_KV, nqb),
        in_specs=[pl.BlockSpec((Q_BLOCK, GQ_WIDTH), lambda b, g, q, s: (b * nqb + q, g)),
                  pl.BlockSpec((1, seq, HEAD_DIM), lambda b, g, q, s: (b, 0, ks_col + g)),
                  pl.BlockSpec((1, nqb, HEAD_DIM, Q_BLOCK), lambda b, g, q, s: (b * N_KV + g, 0, 0, 0)),
                  pl.BlockSpec((1, seq, HEAD_DIM), lambda b, g, q, s: (b, 0, kw_col + g)),
                  pl.BlockSpec((1, nqb, HEAD_DIM, Q_BLOCK), lambda b, g, q, s: (b * N_KV + g, 0, 0, 0)),
                  pl.BlockSpec((1, 1, 1, nqb, 2, Q_BLOCK), lambda b, g, q, s: (b, g, q, 0, 0, 0)),
                  pl.BlockSpec((1, 1, 1, HEAD_DIM, GQ_WIDTH), lambda b, g, q, s: (b, g, q, 0, 0)),
                  pl.BlockSpec((1, 1, 1, SUBLANES, GQ_WIDTH), lambda b, g, q, s: (b, g, q, 0, 0)),
                  pl.BlockSpec((1, SUBLANES, GQ_WIDTH), lambda b, g, q, s: (g, 0, 0)),
                  pl.BlockSpec((1, N_WTILES + 1, Q_BLOCK, GQ_WIDTH), lambda b, g, q, s: (g, 0, 0, 0))],
        out_specs=pl.BlockSpec((Q_BLOCK, GQ_WIDTH), lambda b, g, q, s: (b * nqb + q, g)),
        scratch_shapes=[pltpu.VMEM((HEAD_DIM, GQ_WIDTH), BF16),
                        pltpu.VMEM((2, SUBLANES, GQ_WIDTH), F32),
                        pltpu.VMEM((2, SUBLANES, GQ_WIDTH), F32),
                        pltpu.VMEM((2, HEAD_DIM, GQ_WIDTH), F32),
                        pltpu.VMEM((STEP_TILES * Q_BLOCK, GQ_WIDTH), F32),
                        pltpu.VMEM((STEP_TILES * Q_BLOCK, GQ_WIDTH), F32),
                        pltpu.SMEM((nqb + 3 * STEP_TILES,), jnp.int32)],
    )
    return pl.pallas_call(
        functools.partial(_attn_sel_kernel, words=words),
        out_shape=jax.ShapeDtypeStruct((n, Q_WIDTH), BF16),
        grid_spec=grid_spec,
        compiler_params=_cparams(("parallel", "parallel", "arbitrary")),
        name="nsa_selected_window",
    )(bits, proj, proj3, vst, proj3, vwt, selb5, oc, gates_t, slopes, wbias)


def _active_pair_bits(cnt, *, bsz, seq):
    nqb = seq // Q_BLOCK
    words = (nqb + 31) // 32
    act = (cnt[:, :, :, 0, :] > 0.5).reshape(bsz, N_KV, nqb, nqb, 2).any(axis=-1)
    tile_id = jnp.arange(nqb, dtype=jnp.int32)
    first = jnp.min(jnp.where(act & (tile_id >= 1), tile_id, nqb), axis=-1, keepdims=True)
    act = jnp.pad(act, ((0, 0), (0, 0), (0, 0), (0, words * 32 - nqb))).reshape(bsz, N_KV, nqb, words, 32)
    weights = jnp.left_shift(jnp.uint32(1), jnp.arange(32, dtype=jnp.uint32))
    packed = jnp.sum(act.astype(jnp.uint32) * weights, axis=-1, dtype=jnp.uint32)
    packed = lax.bitcast_convert_type(packed, jnp.int32)
    return jnp.concatenate([first, packed], axis=-1).reshape(-1)


def _outproj_router_kernel(a_ref, x_ref, w_ref, gpost_ref, gate_ref, gpre_ref, sh_ref, sc_ref, wr_ref, wrlo_ref,
                           x2_ref, h2_ref, rt_ref):
    y = jnp.dot(a_ref[...], w_ref[...], preferred_element_type=F32)
    x2 = x_ref[...] + gate_ref[0] * (_rms(y) * gpost_ref[...])
    x2_ref[...] = x2
    h = _rms(x2) * gpre_ref[...]
    h = h * (1.0 + sc_ref[0]) + sh_ref[0]
    hb = h.astype(BF16)
    h2_ref[...] = hb
    h_lo = (h - hb.astype(F32)).astype(BF16)
    logits = (jnp.dot(hb, wr_ref[...], preferred_element_type=F32)
              + jnp.dot(h_lo, wr_ref[...], preferred_element_type=F32)
              + jnp.dot(hb, wrlo_ref[...], preferred_element_type=F32))
    lane = lax.broadcasted_iota(jnp.int32, logits.shape, 1)
    logits = jnp.where(lane < N_EXPERTS, logits, -jnp.inf)
    v1 = jnp.max(logits, axis=-1, keepdims=True)
    i1 = jnp.min(jnp.where(logits == v1, lane, LANES), axis=-1, keepdims=True)
    rest = jnp.where(lane == i1, -jnp.inf, logits)
    v2 = jnp.max(rest, axis=-1, keepdims=True)
    i2 = jnp.min(jnp.where(rest == v2, lane, LANES), axis=-1, keepdims=True)
    e2 = jnp.exp(v2 - v1)
    w1 = 1.0 / (1.0 + e2)
    w2 = e2 / (1.0 + e2)
    out = jnp.where(lane == 0, w1, jnp.where(lane == 1, w2, 0.0))
    out = jnp.where(lane == 2, i1.astype(F32), jnp.where(lane == 3, i2.astype(F32), out))
    rt_ref[...] = out


def _outproj_router_call(a, x, w_out, g_post, gate, g_pre, shift, scale, w_router, *, seq):
    n, d = x.shape
    tm = _tile(seq, 512)
    tpb = seq // tm
    wr32 = jnp.pad(w_router, ((0, 0), (0, LANES - w_router.shape[1])))
    wr = wr32.astype(BF16)
    wr_lo = (wr32 - wr.astype(F32)).astype(BF16)
    row = lambda i: (i, 0)
    fixed = lambda i: (0, 0)
    per_b = lambda i: (i // tpb, 0, 0)
    return pl.pallas_call(
        _outproj_router_kernel,
        out_shape=[jax.ShapeDtypeStruct((n, d), F32),
                   jax.ShapeDtypeStruct((n, d), BF16),
                   jax.ShapeDtypeStruct((n, LANES), F32)],
        grid=(n // tm,),
        in_specs=[pl.BlockSpec((tm, a.shape[1]), row),
                  pl.BlockSpec((tm, d), row),
                  pl.BlockSpec(w_out.shape, fixed),
                  pl.BlockSpec((1, d), fixed),
                  pl.BlockSpec((1, 1, d), per_b),
                  pl.BlockSpec((1, d), fixed),
                  pl.BlockSpec((1, 1, d), per_b),
                  pl.BlockSpec((1, 1, d), per_b),
                  pl.BlockSpec((d, LANES), fixed),
                  pl.BlockSpec((d, LANES), fixed)],
        out_specs=[pl.BlockSpec((tm, d), row), pl.BlockSpec((tm, d), row), pl.BlockSpec((tm, LANES), row)],
        compiler_params=_cparams(("parallel",)),
        name="l1_outproj_router",
    )(a, x, w_out, g_post.reshape(1, d), gate, g_pre.reshape(1, d), shift, scale, wr, wr_lo)


MOE_SUB_ROWS = 256


def _moe_kernel(te_ref, nt_ref, rows_ref, h_ref, wg_ref, wu_ref, wd_ref, o_ref, acc_ref):
    i = pl.program_id(0)
    j = pl.program_id(1)
    tm = h_ref.shape[0]
    n_rows = rows_ref[i]

    @pl.when(j == 0)
    def _():
        acc_ref[...] = jnp.zeros_like(acc_ref)

    def swiglu_rows(rs):
        h = h_ref[rs, :]
        g = jnp.dot(h, wg_ref[0], preferred_element_type=F32)
        u = jnp.dot(h, wu_ref[0], preferred_element_type=F32)
        a = (g * _sigmoid(g) * u).astype(BF16)
        acc_ref[rs, :] += jnp.dot(a, wd_ref[0], preferred_element_type=F32)

    @pl.when(n_rows == tm)
    def _():
        swiglu_rows(slice(None))

    @pl.when((n_rows > 0) & (n_rows < tm))
    def _():
        for sb in range(tm // MOE_SUB_ROWS):
            pl.when(n_rows > sb * MOE_SUB_ROWS)(
                functools.partial(swiglu_rows, slice(sb * MOE_SUB_ROWS, (sb + 1) * MOE_SUB_ROWS)))

    @pl.when(j == pl.num_programs(1) - 1)
    def _():
        o_ref[...] = acc_ref[...].astype(o_ref.dtype)


def _moe_call(tile_expert, n_tiles, tile_rows, hs, wg, wu, wd, *, tm):
    mp, d = hs.shape
    f = wg.shape[2]
    tf = _tile(f, 512)
    nj = f // tf

    def wcol(i, j, nt):
        return jnp.where(i < nt[0], j, nj - 1)

    grid_spec = pltpu.PrefetchScalarGridSpec(
        num_scalar_prefetch=3,
        grid=(mp // tm, f // tf),
        in_specs=[pl.BlockSpec((tm, d), lambda i, j, te, nt, tr: (i, 0)),
                  pl.BlockSpec((1, d, tf), lambda i, j, te, nt, tr: (te[i], 0, wcol(i, j, nt))),
                  pl.BlockSpec((1, d, tf), lambda i, j, te, nt, tr: (te[i], 0, wcol(i, j, nt))),
                  pl.BlockSpec((1, tf, d), lambda i, j, te, nt, tr: (te[i], wcol(i, j, nt), 0))],
        out_specs=pl.BlockSpec((tm, d), lambda i, j, te, nt, tr: (i, 0)),
        scratch_shapes=[pltpu.VMEM((tm, d), F32)],
    )
    return pl.pallas_call(
        _moe_kernel,
        out_shape=jax.ShapeDtypeStruct((mp, d), BF16),
        grid_spec=grid_spec,
        compiler_params=_cparams(("arbitrary", "arbitrary")),
        name="l1_moe_experts",
    )(tile_expert, n_tiles, tile_rows, hs, wg, wu, wd)


def _residual_norm_kernel(x_ref, y0_ref, y1_ref, rt_ref, gpost_ref, gate_ref, o_ref):
    y = rt_ref[:, 0:1] * y0_ref[...].astype(F32) + rt_ref[:, 1:2] * y1_ref[...].astype(F32)
    o_ref[...] = x_ref[...] + gate_ref[0] * (_rms(y) * gpost_ref[...])


def _residual_norm_call(x, y0, y1, rt, g_post, gate, *, seq):
    n, d = x.shape
    tm = _tile(seq, 512)
    tpb = seq // tm
    row = lambda i: (i, 0)
    return pl.pallas_call(
        _residual_norm_kernel,
        out_shape=jax.ShapeDtypeStruct((n, d), F32),
        grid=(n // tm,),
        in_specs=[pl.BlockSpec((tm, d), row), pl.BlockSpec((tm, d), row), pl.BlockSpec((tm, d), row),
                  pl.BlockSpec((tm, LANES), row),
                  pl.BlockSpec((1, d), lambda i: (0, 0)),
                  pl.BlockSpec((1, 1, d), lambda i: (i // tpb, 0, 0))],
        out_specs=pl.BlockSpec((tm, d), row),
        compiler_params=_cparams(("parallel",)),
        name="l1_moe_residual",
    )(x, y0, y1, rt, g_post.reshape(1, d), gate)


CAST_BLOCK_BYTES = 8 * 1024 * 1024
CAST_BLOCK_COLS = 2048


def _cast_kernel(x_ref, o_ref):
    o_ref[...] = x_ref[...].astype(o_ref.dtype)


def _to_bf16(w):
    cols = w.shape[-1]
    rows = w.size // cols
    tc = _tile(cols, CAST_BLOCK_COLS)
    tr = rows
    while tr * tc * 4 > CAST_BLOCK_BYTES and tr % 2 == 0 and (tr // 2) % 16 == 0:
        tr //= 2
    out = pl.pallas_call(
        _cast_kernel,
        out_shape=jax.ShapeDtypeStruct((rows, cols), BF16),
        grid=(rows // tr, cols // tc),
        in_specs=[pl.BlockSpec((tr, tc), lambda i, j: (i, j))],
        out_specs=pl.BlockSpec((tr, tc), lambda i, j: (i, j)),
        compiler_params=_cparams(("parallel", "parallel")),
        name="weight_cast",
    )(w.reshape(rows, cols))
    return out.reshape(w.shape)


def _even_layer(x2d, c, ada_w, ada_b, mix_pre_g, mix_post_g, ffn_pre_g, ffn_post_g,
                w_in, v_ln_g, v_ln_b, w_spatial, b_spatial, conv_w, w_out,
                ffn_w_gate, ffn_w_up, ffn_w_down, *, seq):
    sh_m, sc_m, gt_m, sh_f, sc_f, gt_f = _ada_call(c, ada_w, ada_b)
    proj = _proj_call(x2d, mix_pre_g, sh_m, sc_m, _to_bf16(w_in), seq=seq, name="l0_in_proj")
    x2d = _mixer0_call(proj, x2d, v_ln_g, v_ln_b, w_spatial, b_spatial, conv_w, _to_bf16(w_out),
                       mix_post_g, gt_m, seq=seq)
    return _ffn_call(x2d, ffn_pre_g, sh_f, sc_f, _to_bf16(ffn_w_gate), _to_bf16(ffn_w_up),
                     _to_bf16(ffn_w_down), ffn_post_g, gt_f, seq=seq)


def _moe_routing(rt, *, tm):
    n = rt.shape[0]
    na = n * TOP_K
    mp = na + N_EXPERTS * tm
    e_flat = rt[:, TOP_K:2 * TOP_K].astype(jnp.int32).reshape(na)
    onehot = (e_flat[:, None] == jnp.arange(N_EXPERTS, dtype=jnp.int32)[None, :]).astype(jnp.int32)
    csum = jnp.cumsum(onehot, axis=0)
    counts = csum[-1]
    rank = jnp.sum(onehot * csum, axis=1) - 1
    padded = ((counts + tm - 1) // tm) * tm
    ends = jnp.cumsum(padded)
    starts = ends - padded
    dest = jnp.sum(onehot * starts[None, :], axis=1) + rank
    order = jnp.argsort(e_flat, stable=True).astype(jnp.int32)
    rows = jnp.arange(mp, dtype=jnp.int32)
    row_e = jnp.minimum(jnp.sum((rows[:, None] >= ends[None, :]).astype(jnp.int32), axis=1), N_EXPERTS - 1)
    slot = rows - starts[row_e]
    src = (jnp.cumsum(counts) - counts)[row_e] + slot
    row_token = jnp.where(slot < counts[row_e], order[jnp.clip(src, 0, na - 1)] // TOP_K, 0)
    n_tiles = (ends[-1] // tm).astype(jnp.int32)
    tile_ids = jnp.arange(mp // tm, dtype=jnp.int32)
    tile_expert = jnp.sum((tile_ids[:, None] * tm >= ends[None, :]).astype(jnp.int32), axis=1)
    last = jnp.sum(((n_tiles - 1) * tm >= ends).astype(jnp.int32))
    tile_expert = jnp.where(tile_ids < n_tiles, tile_expert, last).astype(jnp.int32)
    tile_rows = jnp.clip(starts[tile_expert] + counts[tile_expert] - tile_ids * tm, 0, tm)
    tile_rows = jnp.where(tile_ids < n_tiles, tile_rows, 0).astype(jnp.int32)
    return row_token, dest.reshape(n, TOP_K), tile_expert, n_tiles.reshape(1), tile_rows


def _alibi_slope_rows():
    h = np.arange(1, N_HEADS + 1, dtype=np.float64)
    s = (LOG2E * np.exp2(-8.0 * h / N_HEADS)).astype(np.float32).reshape(N_KV, 1, Q_PER_KV, 1)
    rows = np.broadcast_to(s, (N_KV, SUBLANES, Q_PER_KV, Q_BLOCK)).reshape(N_KV, SUBLANES, GQ_WIDTH)
    return jnp.asarray(rows)


def _overlap_t(seq):
    ncp = seq // CMP_STRIDE
    nslc = seq // SLC_BLOCK
    cs = np.arange(ncp)[None, :] * CMP_STRIDE
    ss = np.arange(nslc)[:, None] * SLC_BLOCK
    ov = (cs < ss + SLC_BLOCK) & (cs + CMP_BLOCK > ss) & (np.arange(ncp)[None, :] < ncp - 1)
    return jnp.asarray(ov.astype(np.float32)).astype(BF16)


def _odd_layer(x2d, c, ada_w, ada_b, mix_pre_g, mix_post_g, ffn_pre_g, ffn_post_g,
               w_in, cmp_k_pe, cmp_k_w1, cmp_k_w2, cmp_v_pe, cmp_v_w1, cmp_v_w2, w_out,
               router_w, exp_w_gate, exp_w_up, exp_w_down, *, bsz, seq):
    n, d = x2d.shape
    sh_m, sc_m, gt_m, sh_f, sc_f, gt_f = _ada_call(c, ada_w, ada_b)
    nmain = Q_WIDTH + 6 * KV_WIDTH
    w_main = w_in[:, :nmain].astype(BF16)
    w_gates = jnp.pad(w_in[:, nmain:], ((0, 0), (0, LANES - 3 * N_HEADS))).astype(BF16)
    proj, gates = _proj_call(x2d, mix_pre_g, sh_m, sc_m, w_main, w_gates, seq=seq,
                             q_cols=Q_WIDTH, q_scale=LOG2E * HEAD_DIM ** -0.5, name="l1_in_proj")

    nqb = seq // Q_BLOCK
    ncp = seq // CMP_STRIDE

    def kv_cols(k):
        o = Q_WIDTH + k * KV_WIDTH
        return proj[:, o:o + KV_WIDTH]

    def chunked(t):
        t = t.reshape(bsz, ncp, CMP_STRIDE, N_KV, HEAD_DIM).transpose(0, 3, 1, 2, 4)
        return t.reshape(bsz * N_KV, ncp, CMP_STRIDE * HEAD_DIM)

    def key_major_t(t):
        t = t.reshape(bsz, nqb, Q_BLOCK, N_KV, HEAD_DIM).transpose(0, 3, 1, 4, 2)
        return t.reshape(bsz * N_KV, nqb, HEAD_DIM, Q_BLOCK)

    kcmp, _ = _compress_call(chunked(kv_cols(0)), cmp_k_pe, cmp_k_w1, cmp_k_w2)
    _, vcmpt = _compress_call(chunked(kv_cols(1)), cmp_v_pe, cmp_v_w1, cmp_v_w2)
    slopes = _alibi_slope_rows()
    oc, selb, cnt = _attn_cmp_call(proj, kcmp, vcmpt, _overlap_t(seq), slopes, bsz=bsz, seq=seq)
    bits = _active_pair_bits(cnt, bsz=bsz, seq=seq)

    gt = gates[:, :3 * N_HEADS].reshape(bsz, nqb, Q_BLOCK, 3, N_KV, Q_PER_KV).transpose(0, 4, 1, 3, 5, 2)
    gt = gt.reshape(bsz, N_KV, nqb, 3, GQ_WIDTH)
    gt = jnp.pad(gt, ((0, 0), (0, 0), (0, 0), (0, SUBLANES - 3), (0, 0)))
    attn = _attn_sel_call(bits, proj, key_major_t(kv_cols(3)), key_major_t(kv_cols(5)), selb, oc, gt, slopes,
                          _window_bias_tiles(), bsz=bsz, seq=seq)

    x2, h2, rt = _outproj_router_call(attn, x2d, _to_bf16(w_out), mix_post_g, gt_m,
                                      ffn_pre_g, sh_f, sc_f, router_w, seq=seq)

    tm = min(1024, n)
    row_token, dest, tile_expert, n_tiles, tile_rows = _moe_routing(rt, tm=tm)
    hs = jnp.take(h2, row_token, axis=0)
    rows = _moe_call(tile_expert, n_tiles, tile_rows, hs, _to_bf16(exp_w_gate), _to_bf16(exp_w_up),
                     _to_bf16(exp_w_down), tm=tm)
    y0 = jnp.take(rows, dest[:, 0], axis=0)
    y1 = jnp.take(rows, dest[:, 1], axis=0)
    return _residual_norm_call(x2, y0, y1, rt, ffn_post_g, gt_f, seq=seq)


def kernel(x, c, l0_ada_w, l0_ada_b, l0_mix_pre_g, l0_mix_post_g, l0_ffn_pre_g, l0_ffn_post_g, l0_w_in, l0_v_ln_g, l0_v_ln_b, l0_w_spatial, l0_b_spatial, l0_conv_w, l0_w_out, l0_ffn_w_gate, l0_ffn_w_up, l0_ffn_w_down, l1_ada_w, l1_ada_b, l1_mix_pre_g, l1_mix_post_g, l1_ffn_pre_g, l1_ffn_post_g, l1_w_in, l1_cmp_k_pe, l1_cmp_k_w1, l1_cmp_k_w2, l1_cmp_v_pe, l1_cmp_v_w1, l1_cmp_v_w2, l1_w_out, l1_router_w, l1_exp_w_gate, l1_exp_w_up, l1_exp_w_down):
    bsz, seq, d = x.shape
    x2d = x.reshape(bsz * seq, d)
    x2d = _even_layer(x2d, c, l0_ada_w, l0_ada_b, l0_mix_pre_g, l0_mix_post_g, l0_ffn_pre_g, l0_ffn_post_g,
                      l0_w_in, l0_v_ln_g, l0_v_ln_b, l0_w_spatial, l0_b_spatial, l0_conv_w, l0_w_out,
                      l0_ffn_w_gate, l0_ffn_w_up, l0_ffn_w_down, seq=seq)
    x2d = _odd_layer(x2d, c, l1_ada_w, l1_ada_b, l1_mix_pre_g, l1_mix_post_g, l1_ffn_pre_g, l1_ffn_post_g,
                     l1_w_in, l1_cmp_k_pe, l1_cmp_k_w1, l1_cmp_k_w2, l1_cmp_v_pe, l1_cmp_v_w1, l1_cmp_v_w2,
                     l1_w_out, l1_router_w, l1_exp_w_gate, l1_exp_w_up, l1_exp_w_down, bsz=bsz, seq=seq)
    return x2d.reshape(bsz, seq, d)
```

```python
import functools
import math

import numpy as np
import jax
import jax.numpy as jnp
from jax import lax
from jax.experimental import pallas as pl
from jax.experimental.pallas import tpu as pltpu

F32 = jnp.float32
BF16 = jnp.bfloat16
HIGHEST = lax.Precision.HIGHEST

EPS = 1e-6
LOG2E = math.log2(math.e)
NEG_INF = -1e30
FORCE_SCORE = 1e4

LANES = 128
SUBLANES = 8
VMEM_LIMIT_BYTES = 56 * 1024 * 1024

A_GROUPS = 8
A_WIDTH = 1024
B_WIDTH = 1024
CHUNK = 128
CONV_W = 3
N_HEADS = 16
HEAD_DIM = 128
N_KV = 4
Q_PER_KV = N_HEADS // N_KV
Q_WIDTH = N_HEADS * HEAD_DIM
KV_WIDTH = N_KV * HEAD_DIM
GQ_WIDTH = Q_PER_KV * HEAD_DIM
CMP_BLOCK = 32
CMP_STRIDE = 16
CMP_HIDDEN = 256
SLC_BLOCK = 64
N_SELECT = 16
WINDOW = 512
Q_BLOCK = 128
N_EXPERTS = 8
TOP_K = 2


def _cparams(semantics):
    return pltpu.CompilerParams(dimension_semantics=semantics, vmem_limit_bytes=VMEM_LIMIT_BYTES)


def _sigmoid(x):
    return 1.0 / (1.0 + jnp.exp(-x))


def _gelu_tanh(x):
    return 0.5 * x * (1.0 + jnp.tanh(0.7978845608028654 * (x + 0.044715 * (x * x * x))))


def _rms(x):
    return x * lax.rsqrt(jnp.mean(x * x, axis=-1, keepdims=True) + EPS)


def _tile(n, pref):
    if n <= pref:
        return n
    t = (pref // LANES) * LANES
    while n % t:
        t -= LANES
    assert t > 0, (n, pref)
    return t


def _ada_kernel(c_ref, w_ref, b_ref, o_ref):
    c = c_ref[...]
    s = c * _sigmoid(c)
    o_ref[...] = jnp.dot(s, w_ref[...], preferred_element_type=F32, precision=HIGHEST) + b_ref[...]


def _ada_call(c, w, b):
    bsz, d = c.shape
    n = w.shape[1]
    tn = _tile(n, 1024)
    cp = jnp.pad(c, ((0, SUBLANES - bsz), (0, 0)))
    out = pl.pallas_call(
        _ada_kernel,
        out_shape=jax.ShapeDtypeStruct((SUBLANES, n), F32),
        grid=(n // tn,),
        in_specs=[pl.BlockSpec((SUBLANES, d), lambda j: (0, 0)),
                  pl.BlockSpec((d, tn), lambda j: (0, j)),
                  pl.BlockSpec((1, tn), lambda j: (0, j))],
        out_specs=pl.BlockSpec((SUBLANES, tn), lambda j: (0, j)),
        compiler_params=_cparams(("parallel",)),
        name="ada_modulation",
    )(cp, w, b.reshape(1, n))
    m = out[:bsz]
    return [t.reshape(bsz, 1, d) for t in jnp.split(m, 6, axis=-1)]


def _proj_kernel(*refs, q_tiles, q_scale, with_gates):
    if with_gates:
        x_ref, g_ref, sh_ref, sc_ref, w_ref, wg_ref, o_ref, og_ref, h_ref = refs
    else:
        x_ref, g_ref, sh_ref, sc_ref, w_ref, o_ref, h_ref = refs
    j = pl.program_id(1)

    @pl.when(j == 0)
    def _():
        h = _rms(x_ref[...]) * g_ref[...]
        h = h * (1.0 + sc_ref[0]) + sh_ref[0]
        h_ref[...] = h.astype(BF16)
        if with_gates:
            og_ref[...] = _sigmoid(jnp.dot(h_ref[...], wg_ref[...], preferred_element_type=F32))

    acc = jnp.dot(h_ref[...], w_ref[...], preferred_element_type=F32)
    if q_tiles:
        acc = acc * jnp.where(j < q_tiles, jnp.float32(q_scale), jnp.float32(1.0))
    o_ref[...] = acc.astype(o_ref.dtype)


def _proj_call(x, g, shift, scale, w, wg=None, *, seq, q_cols=0, q_scale=1.0, name):
    n, d = x.shape
    nout = w.shape[1]
    tm = _tile(seq, 1024)
    tn = _tile(nout, 1024)
    assert q_cols % tn == 0
    tpb = seq // tm
    with_gates = wg is not None
    in_specs = [pl.BlockSpec((tm, d), lambda i, j: (i, 0)),
                pl.BlockSpec((1, d), lambda i, j: (0, 0)),
                pl.BlockSpec((1, 1, d), lambda i, j: (i // tpb, 0, 0)),
                pl.BlockSpec((1, 1, d), lambda i, j: (i // tpb, 0, 0)),
                pl.BlockSpec((d, tn), lambda i, j: (0, j))]
    args = [x, g.reshape(1, d), shift, scale, w]
    out_shape = [jax.ShapeDtypeStruct((n, nout), BF16)]
    out_specs = [pl.BlockSpec((tm, tn), lambda i, j: (i, j))]
    if with_gates:
        in_specs.append(pl.BlockSpec((d, LANES), lambda i, j: (0, 0)))
        args.append(wg)
        out_shape.append(jax.ShapeDtypeStruct((n, LANES), F32))
        out_specs.append(pl.BlockSpec((tm, LANES), lambda i, j: (i, 0)))
    out = pl.pallas_call(
        functools.partial(_proj_kernel, q_tiles=q_cols // tn, q_scale=q_scale, with_gates=with_gates),
        out_shape=out_shape,
        grid=(n // tm, nout // tn),
        in_specs=in_specs,
        out_specs=out_specs,
        scratch_shapes=[pltpu.VMEM((tm, d), BF16)],
        compiler_params=_cparams(("parallel", "arbitrary")),
        name=name,
    )(*args)
    return out if with_gates else out[0]


def _mixer0_kernel(p_ref, pgc_ref, pxb_ref, x_ref, vg_ref, vb_ref, wsp_ref, bspt_ref, cw_ref,
                   wout_ref, gpost_ref, gate_ref, o_ref, cat_ref, *, tm, tiles_per_batch):
    i = pl.program_id(0)
    nchunk = tm // CHUNK
    v = _gelu_tanh(p_ref[:, A_WIDTH:2 * A_WIDTH].astype(F32))
    mu = jnp.mean(v, axis=-1, keepdims=True)
    vc = v - mu
    var = jnp.mean(vc * vc, axis=-1, keepdims=True)
    vn = (vc * lax.rsqrt(var + EPS) * vg_ref[...] + vb_ref[...]).astype(BF16)
    row = lax.broadcasted_iota(jnp.int32, (CHUNK, CHUNK), 0)
    col = lax.broadcasted_iota(jnp.int32, (CHUNK, CHUNK), 1)
    causal = col <= row
    for h in range(A_GROUPS):
        cs = slice(h * LANES, (h + 1) * LANES)
        w = jnp.where(causal, wsp_ref[h], 0.0).astype(BF16)
        rhs = jnp.concatenate([vn[c * CHUNK:(c + 1) * CHUNK, cs] for c in range(nchunk)], axis=1)
        z = jnp.dot(w, rhs, preferred_element_type=F32) + bspt_ref[:, h:h + 1]
        for c in range(nchunk):
            rs = slice(c * CHUNK, (c + 1) * CHUNK)
            u = _gelu_tanh(p_ref[rs, cs].astype(F32))
            cat_ref[rs, cs] = (u * z[:, c * CHUNK:(c + 1) * CHUNK]).astype(BF16)
    o_gb, o_gc, o_xb = 2 * A_WIDTH, 2 * A_WIDTH + B_WIDTH, 2 * A_WIDTH + 2 * B_WIDTH
    zc = p_ref[:, o_gc:o_gc + B_WIDTH].astype(F32) * p_ref[:, o_xb:o_xb + B_WIDTH].astype(F32)
    prev = pgc_ref[...].astype(F32) * pxb_ref[...].astype(F32)
    prev = jnp.where(i % tiles_per_batch == 0, 0.0, prev)
    nprev = prev.shape[0]
    p1 = prev[nprev - 1:nprev, :]
    p2 = prev[nprev - 2:nprev - 1, :]
    rowi = lax.broadcasted_iota(jnp.int32, (tm, B_WIDTH), 0)
    r1 = jnp.where(rowi == 0, p1, pltpu.roll(zc, 1, 0))
    r2 = jnp.where(rowi == 0, p2, jnp.where(rowi == 1, p1, pltpu.roll(zc, 2, 0)))
    y = cw_ref[0:1, :] * r2 + cw_ref[1:2, :] * r1 + cw_ref[2:3, :] * zc
    cat_ref[:, A_WIDTH:] = (p_ref[:, o_gb:o_gb + B_WIDTH].astype(F32) * y).astype(BF16)
    yo = jnp.dot(cat_ref[...], wout_ref[...], preferred_element_type=F32)
    o_ref[...] = x_ref[...] + gate_ref[0] * (_rms(yo) * gpost_ref[...])


def _mixer0_call(proj, x, v_ln_g, v_ln_b, w_spatial, b_spatial, conv_w, w_out, g_post, gate, *, seq):
    n, d = x.shape
    tm = _tile(seq, 512)
    tpb = seq // tm
    pr = 16
    rpb = tm // pr
    wcat = A_WIDTH + B_WIDTH
    cwp = jnp.pad(conv_w, ((0, SUBLANES - CONV_W), (0, 0)))
    return pl.pallas_call(
        functools.partial(_mixer0_kernel, tm=tm, tiles_per_batch=tpb),
        out_shape=jax.ShapeDtypeStruct((n, d), F32),
        grid=(n // tm,),
        in_specs=[pl.BlockSpec((tm, proj.shape[1]), lambda i: (i, 0)),
                  pl.BlockSpec((pr, B_WIDTH), lambda i: (jnp.maximum(i * rpb - 1, 0), 3)),
                  pl.BlockSpec((pr, B_WIDTH), lambda i: (jnp.maximum(i * rpb - 1, 0), 4)),
                  pl.BlockSpec((tm, d), lambda i: (i, 0)),
                  pl.BlockSpec((1, A_WIDTH), lambda i: (0, 0)),
                  pl.BlockSpec((1, A_WIDTH), lambda i: (0, 0)),
                  pl.BlockSpec((A_GROUPS, CHUNK, CHUNK), lambda i: (0, 0, 0)),
                  pl.BlockSpec((CHUNK, A_GROUPS), lambda i: (0, 0)),
                  pl.BlockSpec((SUBLANES, B_WIDTH), lambda i: (0, 0)),
                  pl.BlockSpec((wcat, d), lambda i: (0, 0)),
                  pl.BlockSpec((1, d), lambda i: (0, 0)),
                  pl.BlockSpec((1, 1, d), lambda i: (i // tpb, 0, 0))],
        out_specs=pl.BlockSpec((tm, d), lambda i: (i, 0)),
        scratch_shapes=[pltpu.VMEM((tm, wcat), BF16)],
        compiler_params=_cparams(("parallel",)),
        name="l0_mixer",
    )(proj, proj, proj, x, v_ln_g.reshape(1, -1), v_ln_b.reshape(1, -1), w_spatial, b_spatial.T, cwp,
      w_out, g_post.reshape(1, d), gate)


def _ffn_kernel(x_ref, gpre_ref, sh_ref, sc_ref, wg_ref, wu_ref, wd_ref, gpost_ref, gate_ref, o_ref, h_ref):
    j = pl.program_id(1)

    @pl.when(j == 0)
    def _():
        h = _rms(x_ref[...]) * gpre_ref[...]
        h_ref[...] = (h * (1.0 + sc_ref[0]) + sh_ref[0]).astype(BF16)
        o_ref[...] = jnp.zeros_like(o_ref)

    h = h_ref[...]
    g = jnp.dot(h, wg_ref[...], preferred_element_type=F32)
    u = jnp.dot(h, wu_ref[...], preferred_element_type=F32)
    a = (g * _sigmoid(g) * u).astype(BF16)
    o_ref[...] += jnp.dot(a, wd_ref[...], preferred_element_type=F32)

    @pl.when(j == pl.num_programs(1) - 1)
    def _():
        o_ref[...] = x_ref[...] + gate_ref[0] * (_rms(o_ref[...]) * gpost_ref[...])


def _ffn_call(x, g_pre, shift, scale, wg, wu, wd, g_post, gate, *, seq):
    n, d = x.shape
    f = wg.shape[1]
    tm = _tile(seq, 512)
    tf = _tile(f, 1024)
    tpb = seq // tm
    return pl.pallas_call(
        _ffn_kernel,
        out_shape=jax.ShapeDtypeStruct((n, d), F32),
        grid=(n // tm, f // tf),
        in_specs=[pl.BlockSpec((tm, d), lambda i, j: (i, 0)),
                  pl.BlockSpec((1, d), lambda i, j: (0, 0)),
                  pl.BlockSpec((1, 1, d), lambda i, j: (i // tpb, 0, 0)),
                  pl.BlockSpec((1, 1, d), lambda i, j: (i // tpb, 0, 0)),
                  pl.BlockSpec((d, tf), lambda i, j: (0, j)),
                  pl.BlockSpec((d, tf), lambda i, j: (0, j)),
                  pl.BlockSpec((tf, d), lambda i, j: (j, 0)),
                  pl.BlockSpec((1, d), lambda i, j: (0, 0)),
                  pl.BlockSpec((1, 1, d), lambda i, j: (i // tpb, 0, 0))],
        out_specs=pl.BlockSpec((tm, d), lambda i, j: (i, 0)),
        scratch_shapes=[pltpu.VMEM((tm, d), BF16)],
        compiler_params=_cparams(("parallel", "arbitrary")),
        name="l0_ffn",
    )(x, g_pre.reshape(1, d), shift, scale, wg, wu, wd, g_post.reshape(1, d), gate)


def _compress_kernel(x_ref, pe_ref, w1_ref, w2_ref, o_ref, ot_ref):
    half = CMP_STRIDE * HEAD_DIM
    x = x_ref[0]
    nc = x.shape[0]
    a = jnp.dot(x, w1_ref[0:half, :], preferred_element_type=F32)
    b = jnp.dot(x, w1_ref[half:2 * half, :], preferred_element_type=F32)
    bias = jnp.dot(pe_ref[...], w1_ref[...], preferred_element_type=F32)[0:1, :]
    hid = _gelu_tanh(a + pltpu.roll(b, nc - 1, 0) + bias)
    out = jnp.dot(hid.astype(BF16), w2_ref[...], preferred_element_type=F32)
    rowi = lax.broadcasted_iota(jnp.int32, out.shape, 0)
    out = jnp.where(rowi < nc - 1, out, 0.0)
    o_ref[0] = out.astype(BF16)
    ot_ref[0] = out.T.astype(BF16)


def _compress_call(xc, pe, w1, w2):
    bg, nc, kdim = xc.shape
    pe_flat = jnp.pad(pe.reshape(1, -1), ((0, SUBLANES - 1), (0, 0))).astype(BF16)
    w1f = w1.reshape(CMP_BLOCK * HEAD_DIM, CMP_HIDDEN).astype(BF16)
    return pl.pallas_call(
        _compress_kernel,
        out_shape=[jax.ShapeDtypeStruct((bg, nc, HEAD_DIM), BF16),
                   jax.ShapeDtypeStruct((bg, HEAD_DIM, nc), BF16)],
        grid=(bg,),
        in_specs=[pl.BlockSpec((1, nc, kdim), lambda i: (i, 0, 0)),
                  pl.BlockSpec((SUBLANES, CMP_BLOCK * HEAD_DIM), lambda i: (0, 0)),
                  pl.BlockSpec((CMP_BLOCK * HEAD_DIM, CMP_HIDDEN), lambda i: (0, 0)),
                  pl.BlockSpec((CMP_HIDDEN, HEAD_DIM), lambda i: (0, 0))],
        out_specs=[pl.BlockSpec((1, nc, HEAD_DIM), lambda i: (i, 0, 0)),
                   pl.BlockSpec((1, HEAD_DIM, nc), lambda i: (i, 0, 0))],
        compiler_params=_cparams(("parallel",)),
        name="nsa_compress",
    )(xc, pe_flat, w1f, w2.astype(BF16))


def _q_transposed(q_ref):
    parts = []
    for r in range(Q_PER_KV):
        parts.append(q_ref[:, r * HEAD_DIM:(r + 1) * HEAD_DIM].astype(F32).T.astype(BF16))
    return jnp.concatenate(parts, axis=1)


def _attn_cmp_kernel(q_ref, kc_ref, vct_ref, ovt_ref, slope_ref, oc_ref, sel_ref, cnt_ref, imp_ref,
                     *, n_sel, n_class):
    qb = pl.program_id(2)
    q0 = qb * Q_BLOCK
    qt = _q_transposed(q_ref)
    ncp = kc_ref.shape[1]
    nslc = ovt_ref.shape[0]
    rows_per_class = ncp // n_class
    visible = (Q_BLOCK // CMP_STRIDE) * (qb + 1) - 1
    cls = jnp.minimum((visible - 1) // rows_per_class, n_class - 1)

    def compressed(n):
        s_all = jnp.dot(kc_ref[0, 0:n, :], qt, preferred_element_type=F32)
        isub = lax.broadcasted_iota(jnp.int32, (n, Q_BLOCK), 0)
        tl = lax.broadcasted_iota(jnp.int32, (n, Q_BLOCK), 1) + q0
        dist = tl - (isub * CMP_STRIDE + (CMP_BLOCK - 1))
        valid = dist >= 0
        distf = dist.astype(F32)
        psum = jnp.zeros((n, Q_BLOCK), F32)
        pparts = []
        for r in range(Q_PER_KV):
            ls = slice(r * Q_BLOCK, (r + 1) * Q_BLOCK)
            s = s_all[:, ls] - slope_ref[0, 0:1, ls] * distf
            s = jnp.where(valid, s, NEG_INF)
            m = jnp.max(s, axis=0, keepdims=True)
            e = jnp.exp2(s - m)
            inv = jnp.where(m > 0.5 * NEG_INF, 1.0 / jnp.sum(e, axis=0, keepdims=True), 0.0)
            p = e * inv
            psum = psum + p
            pparts.append(p.astype(BF16))
        pt = jnp.concatenate(pparts, axis=1)
        oc_ref[0, 0, 0] = jnp.dot(vct_ref[0, :, 0:n], pt,
                                  preferred_element_type=F32).astype(oc_ref.dtype)
        hi = psum.astype(BF16)
        lo = (psum - hi.astype(F32)).astype(BF16)
        ovt = ovt_ref[:, 0:n]
        imp_ref[...] = (jnp.dot(ovt, hi, preferred_element_type=F32)
                        + jnp.dot(ovt, lo, preferred_element_type=F32))

    for k in range(n_class):
        pl.when(cls == k)(functools.partial(compressed, (k + 1) * rows_per_class))

    jb = lax.broadcasted_iota(jnp.int32, (nslc, Q_BLOCK), 0)
    cur = (lax.broadcasted_iota(jnp.int32, (nslc, Q_BLOCK), 1) + q0) // SLC_BLOCK
    forced = (jb == 0) | (jb == cur) | (jb == cur - 1)
    score = jnp.where(forced, FORCE_SCORE, imp_ref[...])
    score = jnp.where(jb <= cur, score, -1.0)
    picked = -3e38
    three_forced = qb > 0
    score = jnp.where(forced & three_forced, picked, score)

    def pick_next(_, work):
        m = jnp.max(work, axis=0, keepdims=True)
        first = jnp.min(jnp.where(work == m, jb, nslc), axis=0, keepdims=True)
        return jnp.where(jb == first, picked, work)

    work = lax.fori_loop(0, jnp.where(three_forced, n_sel - 3, n_sel), pick_next, score)
    chosen = (work == picked) & (jb <= cur)
    sel_ref[0, 0, 0] = jnp.where(chosen, 0.0, NEG_INF)
    flags = jnp.where(chosen, 1.0, 0.0).astype(BF16)
    cnt_ref[0, 0, 0] = lax.dot_general(jnp.ones((SUBLANES, Q_BLOCK), BF16), flags, (((1,), (1,)), ((), ())),
                                       preferred_element_type=F32)


def _attn_cmp_call(proj, kcmp, vcmpt, ovt, slopes, *, bsz, seq):
    nqb = seq // Q_BLOCK
    ncp = kcmp.shape[1]
    nslc = seq // SLC_BLOCK
    n_sel = min(N_SELECT, nslc)
    return pl.pallas_call(
        functools.partial(_attn_cmp_kernel, n_sel=n_sel, n_class=4),
        out_shape=[jax.ShapeDtypeStruct((bsz, N_KV, nqb, HEAD_DIM, GQ_WIDTH), BF16),
                   jax.ShapeDtypeStruct((bsz, N_KV, nqb, nslc, Q_BLOCK), F32),
                   jax.ShapeDtypeStruct((bsz, N_KV, nqb, SUBLANES, nslc), F32)],
        grid=(bsz, N_KV, nqb),
        in_specs=[pl.BlockSpec((Q_BLOCK, GQ_WIDTH), lambda b, g, q: (b * nqb + q, g)),
                  pl.BlockSpec((1, ncp, HEAD_DIM), lambda b, g, q: (b * N_KV + g, 0, 0)),
                  pl.BlockSpec((1, HEAD_DIM, ncp), lambda b, g, q: (b * N_KV + g, 0, 0)),
                  pl.BlockSpec((nslc, ncp), lambda b, g, q: (0, 0)),
                  pl.BlockSpec((1, SUBLANES, GQ_WIDTH), lambda b, g, q: (g, 0, 0))],
        out_specs=[pl.BlockSpec((1, 1, 1, HEAD_DIM, GQ_WIDTH), lambda b, g, q: (b, g, q, 0, 0)),
                   pl.BlockSpec((1, 1, 1, nslc, Q_BLOCK), lambda b, g, q: (b, g, q, 0, 0)),
                   pl.BlockSpec((1, 1, 1, SUBLANES, nslc), lambda b, g, q: (b, g, q, 0, 0))],
        scratch_shapes=[pltpu.VMEM((nslc, Q_BLOCK), F32)],
        compiler_params=_cparams(("parallel", "parallel", "arbitrary")),
        name="nsa_compressed_select",
    )(proj, kcmp, vcmpt, ovt, slopes)


N_WTILES = WINDOW // Q_BLOCK + 1
HALF = GQ_WIDTH // 2
STEP_TILES = N_WTILES


def _attn_sel_kernel(bits_ref, q_ref, ks_ref, vs_ref, kw_ref, vw_ref, selb_ref, oc_ref, gates_ref, slope_ref,
                     wb_ref, o_ref, qt_ref, m_ref, l_ref, acc_ref, u0_ref, u1_ref, list_ref, *, words):
    b = pl.program_id(0)
    g = pl.program_id(1)
    qb = pl.program_id(2)
    nqb = pl.num_programs(2)
    qt_ref[...] = _q_transposed(q_ref)
    m_ref[...] = jnp.full(m_ref.shape, NEG_INF, F32)
    l_ref[...] = jnp.zeros(l_ref.shape, F32)
    acc_ref[...] = jnp.zeros(acc_ref.shape, F32)

    def key_tile(ref, t):
        return ref[0, pl.ds(pl.multiple_of(t * Q_BLOCK, Q_BLOCK), Q_BLOCK), :]

    def plan(t):
        is_win = t == 0
        out = []
        for j in range(STEP_TILES):
            kt = qb - (N_WTILES - 1) + j
            idx = list_ref[jnp.maximum(t - 1, 0) * STEP_TILES + j]
            w_bias = jnp.where(kt < 0, N_WTILES, j)
            s_bias = jnp.where(idx < 0, N_WTILES, jnp.where(idx == qb, N_WTILES - 1, 1))
            out.append((jnp.where(is_win, jnp.maximum(kt, 0), jnp.maximum(idx, 0)),
                        jnp.where(is_win, w_bias, s_bias)))
        return is_win, out

    def scores(t, u_ref):
        is_win, tiles = plan(t)
        for j, (tile, widx) in enumerate(tiles):
            k = jnp.where(is_win, key_tile(kw_ref, tile), key_tile(ks_ref, tile))
            two = jnp.where(is_win, 0.0, selb_ref[0, 0, 0, tile])
            lo = jnp.broadcast_to(jnp.concatenate([two[0:1, :]] * Q_PER_KV, axis=1), (SLC_BLOCK, GQ_WIDTH))
            hi = jnp.broadcast_to(jnp.concatenate([two[1:2, :]] * Q_PER_KV, axis=1), (SLC_BLOCK, GQ_WIDTH))
            s = jnp.dot(k, qt_ref[...], preferred_element_type=F32)
            u_ref[j * Q_BLOCK:(j + 1) * Q_BLOCK, :] = s + wb_ref[0, widx] + jnp.concatenate([lo, hi], axis=0)

    def update(t, u_ref):
        is_win, tiles = plan(t)
        st = jnp.where(is_win, 0, 1)
        v = jnp.concatenate([jnp.where(is_win, key_tile(vw_ref, tile), key_tile(vs_ref, tile))
                             for tile, _ in tiles], axis=0)
        for h in range(2):
            cs = slice(h * HALF, (h + 1) * HALF)
            offs = [slope_ref[0, 0:1, cs] * ((tile - qb) * Q_BLOCK).astype(F32) for tile, _ in tiles]
            mx = None
            for j in range(STEP_TILES):
                cm = jnp.max(u_ref[j * Q_BLOCK:(j + 1) * Q_BLOCK, cs], axis=0, keepdims=True) + offs[j]
                mx = cm if mx is None else jnp.maximum(mx, cm)
            m_old = m_ref[st, 0:1, cs]
            m_new = jnp.maximum(m_old, mx)
            es = [jnp.exp2(u_ref[j * Q_BLOCK:(j + 1) * Q_BLOCK, cs] + (offs[j] - m_new)) for j in range(STEP_TILES)]
            lsum = es[0].sum(axis=0, keepdims=True)
            for j in range(1, STEP_TILES):
                lsum = lsum + es[j].sum(axis=0, keepdims=True)
            e = jnp.concatenate([x.astype(BF16) for x in es], axis=0)
            alpha = jnp.exp2(m_old - m_new)
            l_ref[st, 0:1, cs] = alpha * l_ref[st, 0:1, cs] + lsum
            pv = lax.dot_general(v, e, (((0,), (0,)), ((), ())), preferred_element_type=F32)
            acc_ref[st, :, cs] = alpha * acc_ref[st, :, cs] + pv
            m_ref[st, 0:1, cs] = m_new

    base = ((b * N_KV + g) * nqb + qb) * (words + 1)

    def collect(p, n):
        list_ref[n] = p
        return n + ((bits_ref[base + 1 + (p >> 5)] >> (p & 31)) & 1)

    list_ref[0] = 0
    n_first = jnp.where(qb > 0, bits_ref[base + 1] & 1, 0)
    n_act = lax.fori_loop(jnp.maximum(bits_ref[base], 1), qb, collect, n_first)
    list_ref[n_act] = qb
    for j in range(1, 3 * STEP_TILES):
        list_ref[n_act + j] = -1

    n_updates = 1 + (n_act + STEP_TILES) // STEP_TILES
    scores(0, u0_ref)

    def pair(i, carry):
        scores(2 * i + 1, u1_ref)
        update(2 * i, u0_ref)
        scores(2 * i + 2, u0_ref)
        update(2 * i + 1, u1_ref)
        return carry

    lax.fori_loop(0, n_updates // 2, pair, 0)

    @pl.when(n_updates % 2 == 1)
    def _():
        update(n_updates - 1, u0_ref)

    gt = gates_ref[0, 0, 0]
    ot = (gt[0:1, :] * oc_ref[0, 0, 0].astype(F32) + gt[1:2, :] * (acc_ref[1] / l_ref[1, 0:1, :])
          + gt[2:3, :] * (acc_ref[0] / l_ref[0, 0:1, :]))
    for r in range(Q_PER_KV):
        cs = slice(r * HEAD_DIM, (r + 1) * HEAD_DIM)
        o_ref[:, cs] = ot[:, cs].T.astype(o_ref.dtype)


def _window_bias_tiles():
    h = np.arange(1, N_HEADS + 1, dtype=np.float64)
    slope = (LOG2E * np.exp2(-8.0 * h / N_HEADS)).astype(np.float32).reshape(N_KV, 1, 1, Q_PER_KV, 1)
    sub = np.arange(Q_BLOCK, dtype=np.float32).reshape(1, 1, Q_BLOCK, 1, 1)
    tq = np.arange(Q_BLOCK).reshape(1, 1, 1, 1, Q_BLOCK)
    ks = np.arange(Q_BLOCK).reshape(1, 1, Q_BLOCK, 1, 1)
    mask = np.zeros((1, N_WTILES + 1, Q_BLOCK, 1, Q_BLOCK), np.float32)
    mask[0, 0] = np.where(tq[0, 0] < ks[0, 0], 0.0, NEG_INF)
    mask[0, N_WTILES - 1] = np.where(tq[0, 0] >= ks[0, 0], 0.0, NEG_INF)
    mask[0, N_WTILES] = NEG_INF
    out = (slope * sub).astype(np.float32) + mask
    return jnp.asarray(out.reshape(N_KV, N_WTILES + 1, Q_BLOCK, GQ_WIDTH))


def _attn_sel_call(bits, proj, selb, oc, gates_t, slopes, wbias, *, bsz, seq):
    nqb = seq // Q_BLOCK
    n = bsz * seq
    proj3 = proj.reshape(bsz, seq, proj.shape[1])
    ks_col = (Q_WIDTH + 2 * KV_WIDTH) // HEAD_DIM
    kw_col = (Q_WIDTH + 4 * KV_WIDTH) // HEAD_DIM
    vs_col = ks_col + N_KV
    vw_col = kw_col + N_KV
    selb5 = selb.reshape(bsz, N_KV, nqb, nqb, 2, Q_BLOCK)
    words = bits.shape[0] // (bsz * N_KV * nqb) - 1
    grid_spec = pltpu.PrefetchScalarGridSpec(
        num_scalar_prefetch=1,
        grid=(bsz, N_KV, nqb),
        in_specs=[pl.BlockSpec((Q_BLOCK, GQ_WIDTH), lambda b, g, q, s: (b * nqb + q, g)),
                  pl.BlockSpec((1, seq, HEAD_DIM), lambda b, g, q, s: (b, 0, ks_col + g)),
                  pl.BlockSpec((1, seq, HEAD_DIM), lambda b, g, q, s: (b, 0, vs_col + g)),
                  pl.BlockSpec((1, seq, HEAD_DIM), lambda b, g, q, s: (b, 0, kw_col + g)),
                  pl.BlockSpec((1, seq, HEAD_DIM), lambda b, g, q, s: (b, 0, vw_col + g)),
                  pl.BlockSpec((1, 1, 1, nqb, 2, Q_BLOCK), lambda b, g, q, s: (b, g, q, 0, 0, 0)),
                  pl.BlockSpec((1, 1, 1, HEAD_DIM, GQ_WIDTH), lambda b, g, q, s: (b, g, q, 0, 0)),
                  pl.BlockSpec((1, 1, 1, SUBLANES, GQ_WIDTH), lambda b, g, q, s: (b, g, q, 0, 0)),
                  pl.BlockSpec((1, SUBLANES, GQ_WIDTH), lambda b, g, q, s: (g, 0, 0)),
                  pl.BlockSpec((1, N_WTILES + 1, Q_BLOCK, GQ_WIDTH), lambda b, g, q, s: (g, 0, 0, 0))],
        out_specs=pl.BlockSpec((Q_BLOCK, GQ_WIDTH), lambda b, g, q, s: (b * nqb + q, g)),
        scratch_shapes=[pltpu.VMEM((HEAD_DIM, GQ_WIDTH), BF16),
                        pltpu.VMEM((2, SUBLANES, GQ_WIDTH), F32),
                        pltpu.VMEM((2, SUBLANES, GQ_WIDTH), F32),
                        pltpu.VMEM((2, HEAD_DIM, GQ_WIDTH), F32),
                        pltpu.VMEM((STEP_TILES * Q_BLOCK, GQ_WIDTH), F32),
                        pltpu.VMEM((STEP_TILES * Q_BLOCK, GQ_WIDTH), F32),
                        pltpu.SMEM((nqb + 3 * STEP_TILES,), jnp.int32)],
    )
    return pl.pallas_call(
        functools.partial(_attn_sel_kernel, words=words),
        out_shape=jax.ShapeDtypeStruct((n, Q_WIDTH), BF16),
        grid_spec=grid_spec,
        compiler_params=_cparams(("parallel", "parallel", "arbitrary")),
        name="nsa_selected_window",
    )(bits, proj, proj3, proj3, proj3, proj3, selb5, oc, gates_t, slopes, wbias)


def _active_pair_bits(cnt, *, bsz, seq):
    nqb = seq // Q_BLOCK
    words = (nqb + 31) // 32
    act = (cnt[:, :, :, 0, :] > 0.5).reshape(bsz, N_KV, nqb, nqb, 2).any(axis=-1)
    tile_id = jnp.arange(nqb, dtype=jnp.int32)
    first = jnp.min(jnp.where(act & (tile_id >= 1), tile_id, nqb), axis=-1, keepdims=True)
    act = jnp.pad(act, ((0, 0), (0, 0), (0, 0), (0, words * 32 - nqb))).reshape(bsz, N_KV, nqb, words, 32)
    weights = jnp.left_shift(jnp.uint32(1), jnp.arange(32, dtype=jnp.uint32))
    packed = jnp.sum(act.astype(jnp.uint32) * weights, axis=-1, dtype=jnp.uint32)
    packed = lax.bitcast_convert_type(packed, jnp.int32)
    return jnp.concatenate([first, packed], axis=-1).reshape(-1)


def _outproj_router_kernel(a_ref, x_ref, w_ref, gpost_ref, gate_ref, gpre_ref, sh_ref, sc_ref, wr_ref, wrlo_ref,
                           x2_ref, h2_ref, rt_ref):
    y = jnp.dot(a_ref[...], w_ref[...], preferred_element_type=F32)
    x2 = x_ref[...] + gate_ref[0] * (_rms(y) * gpost_ref[...])
    x2_ref[...] = x2
    h = _rms(x2) * gpre_ref[...]
    h = h * (1.0 + sc_ref[0]) + sh_ref[0]
    hb = h.astype(BF16)
    h2_ref[...] = hb
    h_lo = (h - hb.astype(F32)).astype(BF16)
    logits = (jnp.dot(hb, wr_ref[...], preferred_element_type=F32)
              + jnp.dot(h_lo, wr_ref[...], preferred_element_type=F32)
              + jnp.dot(hb, wrlo_ref[...], preferred_element_type=F32))
    lane = lax.broadcasted_iota(jnp.int32, logits.shape, 1)
    logits = jnp.where(lane < N_EXPERTS, logits, -jnp.inf)
    v1 = jnp.max(logits, axis=-1, keepdims=True)
    i1 = jnp.min(jnp.where(logits == v1, lane, LANES), axis=-1, keepdims=True)
    rest = jnp.where(lane == i1, -jnp.inf, logits)
    v2 = jnp.max(rest, axis=-1, keepdims=True)
    i2 = jnp.min(jnp.where(rest == v2, lane, LANES), axis=-1, keepdims=True)
    e2 = jnp.exp(v2 - v1)
    w1 = 1.0 / (1.0 + e2)
    w2 = e2 / (1.0 + e2)
    out = jnp.where(lane == 0, w1, jnp.where(lane == 1, w2, 0.0))
    out = jnp.where(lane == 2, i1.astype(F32), jnp.where(lane == 3, i2.astype(F32), out))
    rt_ref[...] = out


def _outproj_router_call(a, x, w_out, g_post, gate, g_pre, shift, scale, w_router, *, seq):
    n, d = x.shape
    tm = _tile(seq, 512)
    tpb = seq // tm
    wr32 = jnp.pad(w_router, ((0, 0), (0, LANES - w_router.shape[1])))
    wr = wr32.astype(BF16)
    wr_lo = (wr32 - wr.astype(F32)).astype(BF16)
    row = lambda i: (i, 0)
    fixed = lambda i: (0, 0)
    per_b = lambda i: (i // tpb, 0, 0)
    return pl.pallas_call(
        _outproj_router_kernel,
        out_shape=[jax.ShapeDtypeStruct((n, d), F32),
                   jax.ShapeDtypeStruct((n, d), BF16),
                   jax.ShapeDtypeStruct((n, LANES), F32)],
        grid=(n // tm,),
        in_specs=[pl.BlockSpec((tm, a.shape[1]), row),
                  pl.BlockSpec((tm, d), row),
                  pl.BlockSpec(w_out.shape, fixed),
                  pl.BlockSpec((1, d), fixed),
                  pl.BlockSpec((1, 1, d), per_b),
                  pl.BlockSpec((1, d), fixed),
                  pl.BlockSpec((1, 1, d), per_b),
                  pl.BlockSpec((1, 1, d), per_b),
                  pl.BlockSpec((d, LANES), fixed),
                  pl.BlockSpec((d, LANES), fixed)],
        out_specs=[pl.BlockSpec((tm, d), row), pl.BlockSpec((tm, d), row), pl.BlockSpec((tm, LANES), row)],
        compiler_params=_cparams(("parallel",)),
        name="l1_outproj_router",
    )(a, x, w_out, g_post.reshape(1, d), gate, g_pre.reshape(1, d), shift, scale, wr, wr_lo)


MOE_SUB_ROWS = 256


def _moe_kernel(te_ref, nt_ref, rows_ref, h_ref, wg_ref, wu_ref, wd_ref, o_ref, acc_ref):
    i = pl.program_id(0)
    j = pl.program_id(1)
    tm = h_ref.shape[0]
    n_rows = rows_ref[i]

    @pl.when(j == 0)
    def _():
        acc_ref[...] = jnp.zeros_like(acc_ref)

    def swiglu_rows(rs):
        h = h_ref[rs, :]
        g = jnp.dot(h, wg_ref[0], preferred_element_type=F32)
        u = jnp.dot(h, wu_ref[0], preferred_element_type=F32)
        a = (g * _sigmoid(g) * u).astype(BF16)
        acc_ref[rs, :] += jnp.dot(a, wd_ref[0], preferred_element_type=F32)

    @pl.when(n_rows == tm)
    def _():
        swiglu_rows(slice(None))

    @pl.when((n_rows > 0) & (n_rows < tm))
    def _():
        for sb in range(tm // MOE_SUB_ROWS):
            pl.when(n_rows > sb * MOE_SUB_ROWS)(
                functools.partial(swiglu_rows, slice(sb * MOE_SUB_ROWS, (sb + 1) * MOE_SUB_ROWS)))

    @pl.when(j == pl.num_programs(1) - 1)
    def _():
        o_ref[...] = acc_ref[...].astype(o_ref.dtype)


def _moe_call(tile_expert, n_tiles, tile_rows, hs, wg, wu, wd, *, tm):
    mp, d = hs.shape
    f = wg.shape[2]
    tf = _tile(f, 512)
    nj = f // tf

    def wcol(i, j, nt):
        return jnp.where(i < nt[0], j, nj - 1)

    grid_spec = pltpu.PrefetchScalarGridSpec(
        num_scalar_prefetch=3,
        grid=(mp // tm, f // tf),
        in_specs=[pl.BlockSpec((tm, d), lambda i, j, te, nt, tr: (i, 0)),
                  pl.BlockSpec((1, d, tf), lambda i, j, te, nt, tr: (te[i], 0, wcol(i, j, nt))),
                  pl.BlockSpec((1, d, tf), lambda i, j, te, nt, tr: (te[i], 0, wcol(i, j, nt))),
                  pl.BlockSpec((1, tf, d), lambda i, j, te, nt, tr: (te[i], wcol(i, j, nt), 0))],
        out_specs=pl.BlockSpec((tm, d), lambda i, j, te, nt, tr: (i, 0)),
        scratch_shapes=[pltpu.VMEM((tm, d), F32)],
    )
    return pl.pallas_call(
        _moe_kernel,
        out_shape=jax.ShapeDtypeStruct((mp, d), BF16),
        grid_spec=grid_spec,
        compiler_params=_cparams(("arbitrary", "arbitrary")),
        name="l1_moe_experts",
    )(tile_expert, n_tiles, tile_rows, hs, wg, wu, wd)


def _residual_norm_kernel(x_ref, y0_ref, y1_ref, rt_ref, gpost_ref, gate_ref, o_ref):
    y = rt_ref[:, 0:1] * y0_ref[...].astype(F32) + rt_ref[:, 1:2] * y1_ref[...].astype(F32)
    o_ref[...] = x_ref[...] + gate_ref[0] * (_rms(y) * gpost_ref[...])


def _residual_norm_call(x, y0, y1, rt, g_post, gate, *, seq):
    n, d = x.shape
    tm = _tile(seq, 512)
    tpb = seq // tm
    row = lambda i: (i, 0)
    return pl.pallas_call(
        _residual_norm_kernel,
        out_shape=jax.ShapeDtypeStruct((n, d), F32),
        grid=(n // tm,),
        in_specs=[pl.BlockSpec((tm, d), row), pl.BlockSpec((tm, d), row), pl.BlockSpec((tm, d), row),
                  pl.BlockSpec((tm, LANES), row),
                  pl.BlockSpec((1, d), lambda i: (0, 0)),
                  pl.BlockSpec((1, 1, d), lambda i: (i // tpb, 0, 0))],
        out_specs=pl.BlockSpec((tm, d), row),
        compiler_params=_cparams(("parallel",)),
        name="l1_moe_residual",
    )(x, y0, y1, rt, g_post.reshape(1, d), gate)


CAST_BLOCK_BYTES = 8 * 1024 * 1024
CAST_BLOCK_COLS = 2048


def _cast_kernel(x_ref, o_ref):
    o_ref[...] = x_ref[...].astype(o_ref.dtype)


def _to_bf16(w):
    cols = w.shape[-1]
    rows = w.size // cols
    tc = _tile(cols, CAST_BLOCK_COLS)
    tr = rows
    while tr * tc * 4 > CAST_BLOCK_BYTES and tr % 2 == 0 and (tr // 2) % 16 == 0:
        tr //= 2
    out = pl.pallas_call(
        _cast_kernel,
        out_shape=jax.ShapeDtypeStruct((rows, cols), BF16),
        grid=(rows // tr, cols // tc),
        in_specs=[pl.BlockSpec((tr, tc), lambda i, j: (i, j))],
        out_specs=pl.BlockSpec((tr, tc), lambda i, j: (i, j)),
        compiler_params=_cparams(("parallel", "parallel")),
        name="weight_cast",
    )(w.reshape(rows, cols))
    return out.reshape(w.shape)


def _even_layer(x2d, c, ada_w, ada_b, mix_pre_g, mix_post_g, ffn_pre_g, ffn_post_g,
                w_in, v_ln_g, v_ln_b, w_spatial, b_spatial, conv_w, w_out,
                ffn_w_gate, ffn_w_up, ffn_w_down, *, seq):
    sh_m, sc_m, gt_m, sh_f, sc_f, gt_f = _ada_call(c, ada_w, ada_b)
    proj = _proj_call(x2d, mix_pre_g, sh_m, sc_m, _to_bf16(w_in), seq=seq, name="l0_in_proj")
    x2d = _mixer0_call(proj, x2d, v_ln_g, v_ln_b, w_spatial, b_spatial, conv_w, _to_bf16(w_out),
                       mix_post_g, gt_m, seq=seq)
    return _ffn_call(x2d, ffn_pre_g, sh_f, sc_f, _to_bf16(ffn_w_gate), _to_bf16(ffn_w_up),
                     _to_bf16(ffn_w_down), ffn_post_g, gt_f, seq=seq)


def _moe_routing(rt, *, tm):
    n = rt.shape[0]
    na = n * TOP_K
    mp = na + N_EXPERTS * tm
    e_flat = rt[:, TOP_K:2 * TOP_K].astype(jnp.int32).reshape(na)
    onehot = (e_flat[:, None] == jnp.arange(N_EXPERTS, dtype=jnp.int32)[None, :]).astype(jnp.int32)
    csum = jnp.cumsum(onehot, axis=0)
    counts = csum[-1]
    rank = jnp.sum(onehot * csum, axis=1) - 1
    padded = ((counts + tm - 1) // tm) * tm
    ends = jnp.cumsum(padded)
    starts = ends - padded
    dest = jnp.sum(onehot * starts[None, :], axis=1) + rank
    order = jnp.argsort(e_flat, stable=True).astype(jnp.int32)
    rows = jnp.arange(mp, dtype=jnp.int32)
    row_e = jnp.minimum(jnp.sum((rows[:, None] >= ends[None, :]).astype(jnp.int32), axis=1), N_EXPERTS - 1)
    slot = rows - starts[row_e]
    src = (jnp.cumsum(counts) - counts)[row_e] + slot
    row_token = jnp.where(slot < counts[row_e], order[jnp.clip(src, 0, na - 1)] // TOP_K, 0)
    n_tiles = (ends[-1] // tm).astype(jnp.int32)
    tile_ids = jnp.arange(mp // tm, dtype=jnp.int32)
    tile_expert = jnp.sum((tile_ids[:, None] * tm >= ends[None, :]).astype(jnp.int32), axis=1)
    last = jnp.sum(((n_tiles - 1) * tm >= ends).astype(jnp.int32))
    tile_expert = jnp.where(tile_ids < n_tiles, tile_expert, last).astype(jnp.int32)
    tile_rows = jnp.clip(starts[tile_expert] + counts[tile_expert] - tile_ids * tm, 0, tm)
    tile_rows = jnp.where(tile_ids < n_tiles, tile_rows, 0).astype(jnp.int32)
    return row_token, dest.reshape(n, TOP_K), tile_expert, n_tiles.reshape(1), tile_rows


def _alibi_slope_rows():
    h = np.arange(1, N_HEADS + 1, dtype=np.float64)
    s = (LOG2E * np.exp2(-8.0 * h / N_HEADS)).astype(np.float32).reshape(N_KV, 1, Q_PER_KV, 1)
    rows = np.broadcast_to(s, (N_KV, SUBLANES, Q_PER_KV, Q_BLOCK)).reshape(N_KV, SUBLANES, GQ_WIDTH)
    return jnp.asarray(rows)


def _overlap_t(seq):
    ncp = seq // CMP_STRIDE
    nslc = seq // SLC_BLOCK
    cs = np.arange(ncp)[None, :] * CMP_STRIDE
    ss = np.arange(nslc)[:, None] * SLC_BLOCK
    ov = (cs < ss + SLC_BLOCK) & (cs + CMP_BLOCK > ss) & (np.arange(ncp)[None, :] < ncp - 1)
    return jnp.asarray(ov.astype(np.float32)).astype(BF16)


def _odd_layer(x2d, c, ada_w, ada_b, mix_pre_g, mix_post_g, ffn_pre_g, ffn_post_g,
               w_in, cmp_k_pe, cmp_k_w1, cmp_k_w2, cmp_v_pe, cmp_v_w1, cmp_v_w2, w_out,
               router_w, exp_w_gate, exp_w_up, exp_w_down, *, bsz, seq):
    n, d = x2d.shape
    sh_m, sc_m, gt_m, sh_f, sc_f, gt_f = _ada_call(c, ada_w, ada_b)
    nmain = Q_WIDTH + 6 * KV_WIDTH
    w_main = w_in[:, :nmain].astype(BF16)
    w_gates = jnp.pad(w_in[:, nmain:], ((0, 0), (0, LANES - 3 * N_HEADS))).astype(BF16)
    proj, gates = _proj_call(x2d, mix_pre_g, sh_m, sc_m, w_main, w_gates, seq=seq,
                             q_cols=Q_WIDTH, q_scale=LOG2E * HEAD_DIM ** -0.5, name="l1_in_proj")

    nqb = seq // Q_BLOCK
    ncp = seq // CMP_STRIDE

    def kv_cols(k):
        o = Q_WIDTH + k * KV_WIDTH
        return proj[:, o:o + KV_WIDTH]

    def chunked(t):
        t = t.reshape(bsz, ncp, CMP_STRIDE, N_KV, HEAD_DIM).transpose(0, 3, 1, 2, 4)
        return t.reshape(bsz * N_KV, ncp, CMP_STRIDE * HEAD_DIM)

    kcmp, _ = _compress_call(chunked(kv_cols(0)), cmp_k_pe, cmp_k_w1, cmp_k_w2)
    _, vcmpt = _compress_call(chunked(kv_cols(1)), cmp_v_pe, cmp_v_w1, cmp_v_w2)
    slopes = _alibi_slope_rows()
    oc, selb, cnt = _attn_cmp_call(proj, kcmp, vcmpt, _overlap_t(seq), slopes, bsz=bsz, seq=seq)
    bits = _active_pair_bits(cnt, bsz=bsz, seq=seq)

    gt = gates[:, :3 * N_HEADS].reshape(bsz, nqb, Q_BLOCK, 3, N_KV, Q_PER_KV).transpose(0, 4, 1, 3, 5, 2)
    gt = gt.reshape(bsz, N_KV, nqb, 3, GQ_WIDTH)
    gt = jnp.pad(gt, ((0, 0), (0, 0), (0, 0), (0, SUBLANES - 3), (0, 0)))
    attn = _attn_sel_call(bits, proj, selb, oc, gt, slopes, _window_bias_tiles(), bsz=bsz, seq=seq)

    x2, h2, rt = _outproj_router_call(attn, x2d, _to_bf16(w_out), mix_post_g, gt_m,
                                      ffn_pre_g, sh_f, sc_f, router_w, seq=seq)

    tm = min(1024, n)
    row_token, dest, tile_expert, n_tiles, tile_rows = _moe_routing(rt, tm=tm)
    hs = jnp.take(h2, row_token, axis=0)
    rows = _moe_call(tile_expert, n_tiles, tile_rows, hs, _to_bf16(exp_w_gate), _to_bf16(exp_w_up),
                     _to_bf16(exp_w_down), tm=tm)
    y0 = jnp.take(rows, dest[:, 0], axis=0)
    y1 = jnp.take(rows, dest[:, 1], axis=0)
    return _residual_norm_call(x2, y0, y1, rt, ffn_post_g, gt_f, seq=seq)


def kernel(x, c, l0_ada_w, l0_ada_b, l0_mix_pre_g, l0_mix_post_g, l0_ffn_pre_g, l0_ffn_post_g, l0_w_in, l0_v_ln_g, l0_v_ln_b, l0_w_spatial, l0_b_spatial, l0_conv_w, l0_w_out, l0_ffn_w_gate, l0_ffn_w_up, l0_ffn_w_down, l1_ada_w, l1_ada_b, l1_mix_pre_g, l1_mix_post_g, l1_ffn_pre_g, l1_ffn_post_g, l1_w_in, l1_cmp_k_pe, l1_cmp_k_w1, l1_cmp_k_w2, l1_cmp_v_pe, l1_cmp_v_w1, l1_cmp_v_w2, l1_w_out, l1_router_w, l1_exp_w_gate, l1_exp_w_up, l1_exp_w_down):
    bsz, seq, d = x.shape
    x2d = x.reshape(bsz * seq, d)
    x2d = _even_layer(x2d, c, l0_ada_w, l0_ada_b, l0_mix_pre_g, l0_mix_post_g, l0_ffn_pre_g, l0_ffn_post_g,
                      l0_w_in, l0_v_ln_g, l0_v_ln_b, l0_w_spatial, l0_b_spatial, l0_conv_w, l0_w_out,
                      l0_ffn_w_gate, l0_ffn_w_up, l0_ffn_w_down, seq=seq)
    x2d = _odd_layer(x2d, c, l1_ada_w, l1_ada_b, l1_mix_pre_g, l1_mix_post_g, l1_ffn_pre_g, l1_ffn_post_g,
                     l1_w_in, l1_cmp_k_pe, l1_cmp_k_w1, l1_cmp_k_w2, l1_cmp_v_pe, l1_cmp_v_w1, l1_cmp_v_w2,
                     l1_w_out, l1_router_w, l1_exp_w_gate, l1_exp_w_up, l1_exp_w_down, bsz=bsz, seq=seq)
    return x2d.reshape(bsz, seq, d)
```

```python
import functools
import math

import numpy as np
import jax
import jax.numpy as jnp
from jax import lax
from jax.experimental import pallas as pl
from jax.experimental.pallas import tpu as pltpu

F32 = jnp.float32
BF16 = jnp.bfloat16
HIGHEST = lax.Precision.HIGHEST

EPS = 1e-6
LOG2E = math.log2(math.e)
NEG_INF = -1e30
FORCE_SCORE = 1e4

LANES = 128
SUBLANES = 8
VMEM_LIMIT_BYTES = 56 * 1024 * 1024

A_GROUPS = 8
A_WIDTH = 1024
B_WIDTH = 1024
CHUNK = 128
CONV_W = 3
N_HEADS = 16
HEAD_DIM = 128
N_KV = 4
Q_PER_KV = N_HEADS // N_KV
Q_WIDTH = N_HEADS * HEAD_DIM
KV_WIDTH = N_KV * HEAD_DIM
GQ_WIDTH = Q_PER_KV * HEAD_DIM
CMP_BLOCK = 32
CMP_STRIDE = 16
CMP_HIDDEN = 256
SLC_BLOCK = 64
N_SELECT = 16
WINDOW = 512
Q_BLOCK = 128
N_EXPERTS = 8
TOP_K = 2


def _cparams(semantics):
    return pltpu.CompilerParams(dimension_semantics=semantics, vmem_limit_bytes=VMEM_LIMIT_BYTES)


def _sigmoid(x):
    return 1.0 / (1.0 + jnp.exp(-x))


def _gelu_tanh(x):
    return 0.5 * x * (1.0 + jnp.tanh(0.7978845608028654 * (x + 0.044715 * (x * x * x))))


def _rms(x):
    return x * lax.rsqrt(jnp.mean(x * x, axis=-1, keepdims=True) + EPS)


def _tile(n, pref):
    if n <= pref:
        return n
    t = (pref // LANES) * LANES
    while n % t:
        t -= LANES
    assert t > 0, (n, pref)
    return t


def _ada_kernel(c_ref, w_ref, b_ref, o_ref):
    c = c_ref[...]
    s = c * _sigmoid(c)
    o_ref[...] = jnp.dot(s, w_ref[...], preferred_element_type=F32, precision=HIGHEST) + b_ref[...]


def _ada_call(c, w, b):
    bsz, d = c.shape
    n = w.shape[1]
    tn = _tile(n, 1024)
    cp = jnp.pad(c, ((0, SUBLANES - bsz), (0, 0)))
    out = pl.pallas_call(
        _ada_kernel,
        out_shape=jax.ShapeDtypeStruct((SUBLANES, n), F32),
        grid=(n // tn,),
        in_specs=[pl.BlockSpec((SUBLANES, d), lambda j: (0, 0)),
                  pl.BlockSpec((d, tn), lambda j: (0, j)),
                  pl.BlockSpec((1, tn), lambda j: (0, j))],
        out_specs=pl.BlockSpec((SUBLANES, tn), lambda j: (0, j)),
        compiler_params=_cparams(("parallel",)),
        name="ada_modulation",
    )(cp, w, b.reshape(1, n))
    m = out[:bsz]
    return [t.reshape(bsz, 1, d) for t in jnp.split(m, 6, axis=-1)]


def _proj_kernel(*refs, q_tiles, q_scale, with_gates):
    if with_gates:
        x_ref, g_ref, sh_ref, sc_ref, w_ref, wg_ref, o_ref, og_ref, h_ref = refs
    else:
        x_ref, g_ref, sh_ref, sc_ref, w_ref, o_ref, h_ref = refs
    j = pl.program_id(1)

    @pl.when(j == 0)
    def _():
        h = _rms(x_ref[...]) * g_ref[...]
        h = h * (1.0 + sc_ref[0]) + sh_ref[0]
        h_ref[...] = h.astype(BF16)
        if with_gates:
            og_ref[...] = _sigmoid(jnp.dot(h_ref[...], wg_ref[...], preferred_element_type=F32))

    acc = jnp.dot(h_ref[...], w_ref[...], preferred_element_type=F32)
    if q_tiles:
        acc = acc * jnp.where(j < q_tiles, jnp.float32(q_scale), jnp.float32(1.0))
    o_ref[...] = acc.astype(o_ref.dtype)


def _proj_call(x, g, shift, scale, w, wg=None, *, seq, q_cols=0, q_scale=1.0, name):
    n, d = x.shape
    nout = w.shape[1]
    tm = _tile(seq, 1024)
    tn = _tile(nout, 1024)
    assert q_cols % tn == 0
    tpb = seq // tm
    with_gates = wg is not None
    in_specs = [pl.BlockSpec((tm, d), lambda i, j: (i, 0)),
                pl.BlockSpec((1, d), lambda i, j: (0, 0)),
                pl.BlockSpec((1, 1, d), lambda i, j: (i // tpb, 0, 0)),
                pl.BlockSpec((1, 1, d), lambda i, j: (i // tpb, 0, 0)),
                pl.BlockSpec((d, tn), lambda i, j: (0, j))]
    args = [x, g.reshape(1, d), shift, scale, w]
    out_shape = [jax.ShapeDtypeStruct((n, nout), BF16)]
    out_specs = [pl.BlockSpec((tm, tn), lambda i, j: (i, j))]
    if with_gates:
        in_specs.append(pl.BlockSpec((d, LANES), lambda i, j: (0, 0)))
        args.append(wg)
        out_shape.append(jax.ShapeDtypeStruct((n, LANES), F32))
        out_specs.append(pl.BlockSpec((tm, LANES), lambda i, j: (i, 0)))
    out = pl.pallas_call(
        functools.partial(_proj_kernel, q_tiles=q_cols // tn, q_scale=q_scale, with_gates=with_gates),
        out_shape=out_shape,
        grid=(n // tm, nout // tn),
        in_specs=in_specs,
        out_specs=out_specs,
        scratch_shapes=[pltpu.VMEM((tm, d), BF16)],
        compiler_params=_cparams(("parallel", "arbitrary")),
        name=name,
    )(*args)
    return out if with_gates else out[0]


def _mixer0_kernel(p_ref, pgc_ref, pxb_ref, x_ref, vg_ref, vb_ref, wsp_ref, bspt_ref, cw_ref,
                   wout_ref, gpost_ref, gate_ref, o_ref, cat_ref, *, tm, tiles_per_batch):
    i = pl.program_id(0)
    nchunk = tm // CHUNK
    v = _gelu_tanh(p_ref[:, A_WIDTH:2 * A_WIDTH].astype(F32))
    mu = jnp.mean(v, axis=-1, keepdims=True)
    vc = v - mu
    var = jnp.mean(vc * vc, axis=-1, keepdims=True)
    vn = (vc * lax.rsqrt(var + EPS) * vg_ref[...] + vb_ref[...]).astype(BF16)
    row = lax.broadcasted_iota(jnp.int32, (CHUNK, CHUNK), 0)
    col = lax.broadcasted_iota(jnp.int32, (CHUNK, CHUNK), 1)
    causal = col <= row
    for h in range(A_GROUPS):
        cs = slice(h * LANES, (h + 1) * LANES)
        w = jnp.where(causal, wsp_ref[h], 0.0).astype(BF16)
        rhs = jnp.concatenate([vn[c * CHUNK:(c + 1) * CHUNK, cs] for c in range(nchunk)], axis=1)
        z = jnp.dot(w, rhs, preferred_element_type=F32) + bspt_ref[:, h:h + 1]
        for c in range(nchunk):
            rs = slice(c * CHUNK, (c + 1) * CHUNK)
            u = _gelu_tanh(p_ref[rs, cs].astype(F32))
            cat_ref[rs, cs] = (u * z[:, c * CHUNK:(c + 1) * CHUNK]).astype(BF16)
    o_gb, o_gc, o_xb = 2 * A_WIDTH, 2 * A_WIDTH + B_WIDTH, 2 * A_WIDTH + 2 * B_WIDTH
    zc = p_ref[:, o_gc:o_gc + B_WIDTH].astype(F32) * p_ref[:, o_xb:o_xb + B_WIDTH].astype(F32)
    prev = pgc_ref[...].astype(F32) * pxb_ref[...].astype(F32)
    prev = jnp.where(i % tiles_per_batch == 0, 0.0, prev)
    nprev = prev.shape[0]
    p1 = prev[nprev - 1:nprev, :]
    p2 = prev[nprev - 2:nprev - 1, :]
    rowi = lax.broadcasted_iota(jnp.int32, (tm, B_WIDTH), 0)
    r1 = jnp.where(rowi == 0, p1, pltpu.roll(zc, 1, 0))
    r2 = jnp.where(rowi == 0, p2, jnp.where(rowi == 1, p1, pltpu.roll(zc, 2, 0)))
    y = cw_ref[0:1, :] * r2 + cw_ref[1:2, :] * r1 + cw_ref[2:3, :] * zc
    cat_ref[:, A_WIDTH:] = (p_ref[:, o_gb:o_gb + B_WIDTH].astype(F32) * y).astype(BF16)
    yo = jnp.dot(cat_ref[...], wout_ref[...], preferred_element_type=F32)
    o_ref[...] = x_ref[...] + gate_ref[0] * (_rms(yo) * gpost_ref[...])


def _mixer0_call(proj, x, v_ln_g, v_ln_b, w_spatial, b_spatial, conv_w, w_out, g_post, gate, *, seq):
    n, d = x.shape
    tm = _tile(seq, 512)
    tpb = seq // tm
    pr = 16
    rpb = tm // pr
    wcat = A_WIDTH + B_WIDTH
    cwp = jnp.pad(conv_w, ((0, SUBLANES - CONV_W), (0, 0)))
    return pl.pallas_call(
        functools.partial(_mixer0_kernel, tm=tm, tiles_per_batch=tpb),
        out_shape=jax.ShapeDtypeStruct((n, d), F32),
        grid=(n // tm,),
        in_specs=[pl.BlockSpec((tm, proj.shape[1]), lambda i: (i, 0)),
                  pl.BlockSpec((pr, B_WIDTH), lambda i: (jnp.maximum(i * rpb - 1, 0), 3)),
                  pl.BlockSpec((pr, B_WIDTH), lambda i: (jnp.maximum(i * rpb - 1, 0), 4)),
                  pl.BlockSpec((tm, d), lambda i: (i, 0)),
                  pl.BlockSpec((1, A_WIDTH), lambda i: (0, 0)),
                  pl.BlockSpec((1, A_WIDTH), lambda i: (0, 0)),
                  pl.BlockSpec((A_GROUPS, CHUNK, CHUNK), lambda i: (0, 0, 0)),
                  pl.BlockSpec((CHUNK, A_GROUPS), lambda i: (0, 0)),
                  pl.BlockSpec((SUBLANES, B_WIDTH), lambda i: (0, 0)),
                  pl.BlockSpec((wcat, d), lambda i: (0, 0)),
                  pl.BlockSpec((1, d), lambda i: (0, 0)),
                  pl.BlockSpec((1, 1, d), lambda i: (i // tpb, 0, 0))],
        out_specs=pl.BlockSpec((tm, d), lambda i: (i, 0)),
        scratch_shapes=[pltpu.VMEM((tm, wcat), BF16)],
        compiler_params=_cparams(("parallel",)),
        name="l0_mixer",
    )(proj, proj, proj, x, v_ln_g.reshape(1, -1), v_ln_b.reshape(1, -1), w_spatial, b_spatial.T, cwp,
      w_out, g_post.reshape(1, d), gate)


def _ffn_kernel(x_ref, gpre_ref, sh_ref, sc_ref, wg_ref, wu_ref, wd_ref, gpost_ref, gate_ref, o_ref, h_ref):
    j = pl.program_id(1)

    @pl.when(j == 0)
    def _():
        h = _rms(x_ref[...]) * gpre_ref[...]
        h_ref[...] = (h * (1.0 + sc_ref[0]) + sh_ref[0]).astype(BF16)
        o_ref[...] = jnp.zeros_like(o_ref)

    h = h_ref[...]
    g = jnp.dot(h, wg_ref[...], preferred_element_type=F32)
    u = jnp.dot(h, wu_ref[...], preferred_element_type=F32)
    a = (g * _sigmoid(g) * u).astype(BF16)
    o_ref[...] += jnp.dot(a, wd_ref[...], preferred_element_type=F32)

    @pl.when(j == pl.num_programs(1) - 1)
    def _():
        o_ref[...] = x_ref[...] + gate_ref[0] * (_rms(o_ref[...]) * gpost_ref[...])


def _ffn_call(x, g_pre, shift, scale, wg, wu, wd, g_post, gate, *, seq):
    n, d = x.shape
    f = wg.shape[1]
    tm = _tile(seq, 1024)
    tf = _tile(f, 512)
    tpb = seq // tm
    return pl.pallas_call(
        _ffn_kernel,
        out_shape=jax.ShapeDtypeStruct((n, d), F32),
        grid=(n // tm, f // tf),
        in_specs=[pl.BlockSpec((tm, d), lambda i, j: (i, 0), pipeline_mode=pl.Buffered(1)),
                  pl.BlockSpec((1, d), lambda i, j: (0, 0)),
                  pl.BlockSpec((1, 1, d), lambda i, j: (i // tpb, 0, 0)),
                  pl.BlockSpec((1, 1, d), lambda i, j: (i // tpb, 0, 0)),
                  pl.BlockSpec((d, tf), lambda i, j: (0, j)),
                  pl.BlockSpec((d, tf), lambda i, j: (0, j)),
                  pl.BlockSpec((tf, d), lambda i, j: (j, 0)),
                  pl.BlockSpec((1, d), lambda i, j: (0, 0)),
                  pl.BlockSpec((1, 1, d), lambda i, j: (i // tpb, 0, 0))],
        out_specs=pl.BlockSpec((tm, d), lambda i, j: (i, 0)),
        scratch_shapes=[pltpu.VMEM((tm, d), BF16)],
        compiler_params=_cparams(("parallel", "arbitrary")),
        name="l0_ffn",
    )(x, g_pre.reshape(1, d), shift, scale, wg, wu, wd, g_post.reshape(1, d), gate)


def _compress_kernel(x_ref, pe_ref, w1_ref, w2_ref, o_ref, ot_ref):
    half = CMP_STRIDE * HEAD_DIM
    x = x_ref[0]
    nc = x.shape[0]
    a = jnp.dot(x, w1_ref[0:half, :], preferred_element_type=F32)
    b = jnp.dot(x, w1_ref[half:2 * half, :], preferred_element_type=F32)
    bias = jnp.dot(pe_ref[...], w1_ref[...], preferred_element_type=F32)[0:1, :]
    hid = _gelu_tanh(a + pltpu.roll(b, nc - 1, 0) + bias)
    out = jnp.dot(hid.astype(BF16), w2_ref[...], preferred_element_type=F32)
    rowi = lax.broadcasted_iota(jnp.int32, out.shape, 0)
    out = jnp.where(rowi < nc - 1, out, 0.0)
    o_ref[0] = out.astype(BF16)
    ot_ref[0] = out.T.astype(BF16)


def _compress_call(xc, pe, w1, w2):
    bg, nc, kdim = xc.shape
    pe_flat = jnp.pad(pe.reshape(1, -1), ((0, SUBLANES - 1), (0, 0))).astype(BF16)
    w1f = w1.reshape(CMP_BLOCK * HEAD_DIM, CMP_HIDDEN).astype(BF16)
    return pl.pallas_call(
        _compress_kernel,
        out_shape=[jax.ShapeDtypeStruct((bg, nc, HEAD_DIM), BF16),
                   jax.ShapeDtypeStruct((bg, HEAD_DIM, nc), BF16)],
        grid=(bg,),
        in_specs=[pl.BlockSpec((1, nc, kdim), lambda i: (i, 0, 0)),
                  pl.BlockSpec((SUBLANES, CMP_BLOCK * HEAD_DIM), lambda i: (0, 0)),
                  pl.BlockSpec((CMP_BLOCK * HEAD_DIM, CMP_HIDDEN), lambda i: (0, 0)),
                  pl.BlockSpec((CMP_HIDDEN, HEAD_DIM), lambda i: (0, 0))],
        out_specs=[pl.BlockSpec((1, nc, HEAD_DIM), lambda i: (i, 0, 0)),
                   pl.BlockSpec((1, HEAD_DIM, nc), lambda i: (i, 0, 0))],
        compiler_params=_cparams(("parallel",)),
        name="nsa_compress",
    )(xc, pe_flat, w1f, w2.astype(BF16))


def _q_transposed(q_ref):
    parts = []
    for r in range(Q_PER_KV):
        parts.append(q_ref[:, r * HEAD_DIM:(r + 1) * HEAD_DIM].astype(F32).T.astype(BF16))
    return jnp.concatenate(parts, axis=1)


def _attn_cmp_kernel(q_ref, kc_ref, vct_ref, ovt_ref, slope_ref, oc_ref, sel_ref, cnt_ref, imp_ref,
                     *, n_sel, n_class):
    qb = pl.program_id(2)
    q0 = qb * Q_BLOCK
    qt = _q_transposed(q_ref)
    ncp = kc_ref.shape[1]
    nslc = ovt_ref.shape[0]
    rows_per_class = ncp // n_class
    visible = (Q_BLOCK // CMP_STRIDE) * (qb + 1) - 1
    cls = jnp.minimum((visible - 1) // rows_per_class, n_class - 1)

    def compressed(n):
        s_all = jnp.dot(kc_ref[0, 0:n, :], qt, preferred_element_type=F32)
        isub = lax.broadcasted_iota(jnp.int32, (n, Q_BLOCK), 0)
        tl = lax.broadcasted_iota(jnp.int32, (n, Q_BLOCK), 1) + q0
        dist = tl - (isub * CMP_STRIDE + (CMP_BLOCK - 1))
        valid = dist >= 0
        distf = dist.astype(F32)
        psum = jnp.zeros((n, Q_BLOCK), F32)
        pparts = []
        for r in range(Q_PER_KV):
            ls = slice(r * Q_BLOCK, (r + 1) * Q_BLOCK)
            s = s_all[:, ls] - slope_ref[0, 0:1, ls] * distf
            s = jnp.where(valid, s, NEG_INF)
            m = jnp.max(s, axis=0, keepdims=True)
            e = jnp.exp2(s - m)
            inv = jnp.where(m > 0.5 * NEG_INF, 1.0 / jnp.sum(e, axis=0, keepdims=True), 0.0)
            p = e * inv
            psum = psum + p
            pparts.append(p.astype(BF16))
        pt = jnp.concatenate(pparts, axis=1)
        oc_ref[0, 0, 0] = jnp.dot(vct_ref[0, :, 0:n], pt,
                                  preferred_element_type=F32).astype(oc_ref.dtype)
        hi = psum.astype(BF16)
        lo = (psum - hi.astype(F32)).astype(BF16)
        ovt = ovt_ref[:, 0:n]
        imp_ref[...] = (jnp.dot(ovt, hi, preferred_element_type=F32)
                        + jnp.dot(ovt, lo, preferred_element_type=F32))

    for k in range(n_class):
        pl.when(cls == k)(functools.partial(compressed, (k + 1) * rows_per_class))

    jb = lax.broadcasted_iota(jnp.int32, (nslc, Q_BLOCK), 0)
    cur = (lax.broadcasted_iota(jnp.int32, (nslc, Q_BLOCK), 1) + q0) // SLC_BLOCK
    forced = (jb == 0) | (jb == cur) | (jb == cur - 1)
    score = jnp.where(forced, FORCE_SCORE, imp_ref[...])
    score = jnp.where(jb <= cur, score, -1.0)
    picked = -3e38
    three_forced = qb > 0
    score = jnp.where(forced & three_forced, picked, score)

    def pick_next(_, work):
        m = jnp.max(work, axis=0, keepdims=True)
        first = jnp.min(jnp.where(work == m, jb, nslc), axis=0, keepdims=True)
        return jnp.where(jb == first, picked, work)

    work = lax.fori_loop(0, jnp.where(three_forced, n_sel - 3, n_sel), pick_next, score)
    chosen = (work == picked) & (jb <= cur)
    sel_ref[0, 0, 0] = jnp.where(chosen, 0.0, NEG_INF)
    flags = jnp.where(chosen, 1.0, 0.0).astype(BF16)
    cnt_ref[0, 0, 0] = lax.dot_general(jnp.ones((SUBLANES, Q_BLOCK), BF16), flags, (((1,), (1,)), ((), ())),
                                       preferred_element_type=F32)


def _attn_cmp_call(proj, kcmp, vcmpt, ovt, slopes, *, bsz, seq):
    nqb = seq // Q_BLOCK
    ncp = kcmp.shape[1]
    nslc = seq // SLC_BLOCK
    n_sel = min(N_SELECT, nslc)
    return pl.pallas_call(
        functools.partial(_attn_cmp_kernel, n_sel=n_sel, n_class=8),
        out_shape=[jax.ShapeDtypeStruct((bsz, N_KV, nqb, HEAD_DIM, GQ_WIDTH), BF16),
                   jax.ShapeDtypeStruct((bsz, N_KV, nqb, nslc, Q_BLOCK), F32),
                   jax.ShapeDtypeStruct((bsz, N_KV, nqb, SUBLANES, nslc), F32)],
        grid=(bsz, N_KV, nqb),
        in_specs=[pl.BlockSpec((Q_BLOCK, GQ_WIDTH), lambda b, g, q: (b * nqb + q, g)),
                  pl.BlockSpec((1, ncp, HEAD_DIM), lambda b, g, q: (b * N_KV + g, 0, 0)),
                  pl.BlockSpec((1, HEAD_DIM, ncp), lambda b, g, q: (b * N_KV + g, 0, 0)),
                  pl.BlockSpec((nslc, ncp), lambda b, g, q: (0, 0)),
                  pl.BlockSpec((1, SUBLANES, GQ_WIDTH), lambda b, g, q: (g, 0, 0))],
        out_specs=[pl.BlockSpec((1, 1, 1, HEAD_DIM, GQ_WIDTH), lambda b, g, q: (b, g, q, 0, 0)),
                   pl.BlockSpec((1, 1, 1, nslc, Q_BLOCK), lambda b, g, q: (b, g, q, 0, 0)),
                   pl.BlockSpec((1, 1, 1, SUBLANES, nslc), lambda b, g, q: (b, g, q, 0, 0))],
        scratch_shapes=[pltpu.VMEM((nslc, Q_BLOCK), F32)],
        compiler_params=_cparams(("parallel", "parallel", "arbitrary")),
        name="nsa_compressed_select",
    )(proj, kcmp, vcmpt, ovt, slopes)


N_WTILES = WINDOW // Q_BLOCK + 1
HALF = GQ_WIDTH // 2
STEP_TILES = N_WTILES


def _attn_sel_kernel(bits_ref, q_ref, ks_ref, vs_ref, kw_ref, vw_ref, selb_ref, oc_ref, gates_ref, slope_ref,
                     wb_ref, o_ref, qt_ref, m_ref, l_ref, acc_ref, u0_ref, u1_ref, list_ref, *, words):
    b = pl.program_id(0)
    g = pl.program_id(1)
    qb = pl.program_id(2)
    nqb = pl.num_programs(2)
    qt_ref[...] = _q_transposed(q_ref)
    m_ref[...] = jnp.full(m_ref.shape, NEG_INF, F32)
    l_ref[...] = jnp.zeros(l_ref.shape, F32)
    acc_ref[...] = jnp.zeros(acc_ref.shape, F32)

    def key_tile(ref, t):
        return ref[0, pl.ds(pl.multiple_of(t * Q_BLOCK, Q_BLOCK), Q_BLOCK), :]

    def plan(t):
        is_win = t == 0
        out = []
        for j in range(STEP_TILES):
            kt = qb - (N_WTILES - 1) + j
            idx = list_ref[jnp.maximum(t - 1, 0) * STEP_TILES + j]
            w_bias = jnp.where(kt < 0, N_WTILES, j)
            s_bias = jnp.where(idx < 0, N_WTILES, jnp.where(idx == qb, N_WTILES - 1, 1))
            out.append((jnp.where(is_win, jnp.maximum(kt, 0), jnp.maximum(idx, 0)),
                        jnp.where(is_win, w_bias, s_bias)))
        return is_win, out

    def scores(t, u_ref):
        is_win, tiles = plan(t)
        for j, (tile, widx) in enumerate(tiles):
            k = jnp.where(is_win, key_tile(kw_ref, tile), key_tile(ks_ref, tile))
            two = jnp.where(is_win, 0.0, selb_ref[0, 0, 0, tile])
            lo = jnp.broadcast_to(jnp.concatenate([two[0:1, :]] * Q_PER_KV, axis=1), (SLC_BLOCK, GQ_WIDTH))
            hi = jnp.broadcast_to(jnp.concatenate([two[1:2, :]] * Q_PER_KV, axis=1), (SLC_BLOCK, GQ_WIDTH))
            s = jnp.dot(k, qt_ref[...], preferred_element_type=F32)
            u_ref[j * Q_BLOCK:(j + 1) * Q_BLOCK, :] = s + wb_ref[0, widx] + jnp.concatenate([lo, hi], axis=0)

    def update(t, u_ref):
        is_win, tiles = plan(t)
        st = jnp.where(is_win, 0, 1)
        v = jnp.concatenate([jnp.where(is_win, key_tile(vw_ref, tile), key_tile(vs_ref, tile))
                             for tile, _ in tiles], axis=0)
        for h in range(2):
            cs = slice(h * HALF, (h + 1) * HALF)
            offs = [slope_ref[0, 0:1, cs] * ((tile - qb) * Q_BLOCK).astype(F32) for tile, _ in tiles]
            mx = None
            for j in range(STEP_TILES):
                cm = jnp.max(u_ref[j * Q_BLOCK:(j + 1) * Q_BLOCK, cs], axis=0, keepdims=True) + offs[j]
                mx = cm if mx is None else jnp.maximum(mx, cm)
            m_old = m_ref[st, 0:1, cs]
            m_new = jnp.maximum(m_old, mx)
            es = [jnp.exp2(u_ref[j * Q_BLOCK:(j + 1) * Q_BLOCK, cs] + (offs[j] - m_new)) for j in range(STEP_TILES)]
            lsum = es[0].sum(axis=0, keepdims=True)
            for j in range(1, STEP_TILES):
                lsum = lsum + es[j].sum(axis=0, keepdims=True)
            e = jnp.concatenate([x.astype(BF16) for x in es], axis=0)
            alpha = jnp.exp2(m_old - m_new)
            l_ref[st, 0:1, cs] = alpha * l_ref[st, 0:1, cs] + lsum
            pv = lax.dot_general(v, e, (((0,), (0,)), ((), ())), preferred_element_type=F32)
            acc_ref[st, :, cs] = alpha * acc_ref[st, :, cs] + pv
            m_ref[st, 0:1, cs] = m_new

    base = ((b * N_KV + g) * nqb + qb) * (words + 1)

    def collect(p, n):
        list_ref[n] = p
        return n + ((bits_ref[base + 1 + (p >> 5)] >> (p & 31)) & 1)

    list_ref[0] = 0
    n_first = jnp.where(qb > 0, bits_ref[base + 1] & 1, 0)
    n_act = lax.fori_loop(jnp.maximum(bits_ref[base], 1), qb, collect, n_first)
    list_ref[n_act] = qb
    for j in range(1, 3 * STEP_TILES):
        list_ref[n_act + j] = -1

    n_updates = 1 + (n_act + STEP_TILES) // STEP_TILES
    scores(0, u0_ref)

    def pair(i, carry):
        scores(2 * i + 1, u1_ref)
        update(2 * i, u0_ref)
        scores(2 * i + 2, u0_ref)
        update(2 * i + 1, u1_ref)
        return carry

    lax.fori_loop(0, n_updates // 2, pair, 0)

    @pl.when(n_updates % 2 == 1)
    def _():
        update(n_updates - 1, u0_ref)

    gt = gates_ref[0, 0, 0]
    ot = (gt[0:1, :] * oc_ref[0, 0, 0].astype(F32) + gt[1:2, :] * (acc_ref[1] / l_ref[1, 0:1, :])
          + gt[2:3, :] * (acc_ref[0] / l_ref[0, 0:1, :]))
    for r in range(Q_PER_KV):
        cs = slice(r * HEAD_DIM, (r + 1) * HEAD_DIM)
        o_ref[:, cs] = ot[:, cs].T.astype(o_ref.dtype)


def _window_bias_tiles():
    h = np.arange(1, N_HEADS + 1, dtype=np.float64)
    slope = (LOG2E * np.exp2(-8.0 * h / N_HEADS)).astype(np.float32).reshape(N_KV, 1, 1, Q_PER_KV, 1)
    sub = np.arange(Q_BLOCK, dtype=np.float32).reshape(1, 1, Q_BLOCK, 1, 1)
    tq = np.arange(Q_BLOCK).reshape(1, 1, 1, 1, Q_BLOCK)
    ks = np.arange(Q_BLOCK).reshape(1, 1, Q_BLOCK, 1, 1)
    mask = np.zeros((1, N_WTILES + 1, Q_BLOCK, 1, Q_BLOCK), np.float32)
    mask[0, 0] = np.where(tq[0, 0] < ks[0, 0], 0.0, NEG_INF)
    mask[0, N_WTILES - 1] = np.where(tq[0, 0] >= ks[0, 0], 0.0, NEG_INF)
    mask[0, N_WTILES] = NEG_INF
    out = (slope * sub).astype(np.float32) + mask
    return jnp.asarray(out.reshape(N_KV, N_WTILES + 1, Q_BLOCK, GQ_WIDTH))


def _attn_sel_call(bits, proj, selb, oc, gates_t, slopes, wbias, *, bsz, seq):
    nqb = seq // Q_BLOCK
    n = bsz * seq
    proj3 = proj.reshape(bsz, seq, proj.shape[1])
    ks_col = (Q_WIDTH + 2 * KV_WIDTH) // HEAD_DIM
    kw_col = (Q_WIDTH + 4 * KV_WIDTH) // HEAD_DIM
    vs_col = ks_col + N_KV
    vw_col = kw_col + N_KV
    selb5 = selb.reshape(bsz, N_KV, nqb, nqb, 2, Q_BLOCK)
    words = bits.shape[0] // (bsz * N_KV * nqb) - 1
    grid_spec = pltpu.PrefetchScalarGridSpec(
        num_scalar_prefetch=1,
        grid=(bsz, N_KV, nqb),
        in_specs=[pl.BlockSpec((Q_BLOCK, GQ_WIDTH), lambda b, g, q, s: (b * nqb + q, g)),
                  pl.BlockSpec((1, seq, HEAD_DIM), lambda b, g, q, s: (b, 0, ks_col + g)),
                  pl.BlockSpec((1, seq, HEAD_DIM), lambda b, g, q, s: (b, 0, vs_col + g)),
                  pl.BlockSpec((1, seq, HEAD_DIM), lambda b, g, q, s: (b, 0, kw_col + g)),
                  pl.BlockSpec((1, seq, HEAD_DIM), lambda b, g, q, s: (b, 0, vw_col + g)),
                  pl.BlockSpec((1, 1, 1, nqb, 2, Q_BLOCK), lambda b, g, q, s: (b, g, q, 0, 0, 0)),
                  pl.BlockSpec((1, 1, 1, HEAD_DIM, GQ_WIDTH), lambda b, g, q, s: (b, g, q, 0, 0)),
                  pl.BlockSpec((1, 1, 1, SUBLANES, GQ_WIDTH), lambda b, g, q, s: (b, g, q, 0, 0)),
                  pl.BlockSpec((1, SUBLANES, GQ_WIDTH), lambda b, g, q, s: (g, 0, 0)),
                  pl.BlockSpec((1, N_WTILES + 1, Q_BLOCK, GQ_WIDTH), lambda b, g, q, s: (g, 0, 0, 0))],
        out_specs=pl.BlockSpec((Q_BLOCK, GQ_WIDTH), lambda b, g, q, s: (b * nqb + q, g)),
        scratch_shapes=[pltpu.VMEM((HEAD_DIM, GQ_WIDTH), BF16),
                        pltpu.VMEM((2, SUBLANES, GQ_WIDTH), F32),
                        pltpu.VMEM((2, SUBLANES, GQ_WIDTH), F32),
                        pltpu.VMEM((2, HEAD_DIM, GQ_WIDTH), F32),
                        pltpu.VMEM((STEP_TILES * Q_BLOCK, GQ_WIDTH), F32),
                        pltpu.VMEM((STEP_TILES * Q_BLOCK, GQ_WIDTH), F32),
                        pltpu.SMEM((nqb + 3 * STEP_TILES,), jnp.int32)],
    )
    return pl.pallas_call(
        functools.partial(_attn_sel_kernel, words=words),
        out_shape=jax.ShapeDtypeStruct((n, Q_WIDTH), BF16),
        grid_spec=grid_spec,
        compiler_params=_cparams(("parallel", "parallel", "arbitrary")),
        name="nsa_selected_window",
    )(bits, proj, proj3, proj3, proj3, proj3, selb5, oc, gates_t, slopes, wbias)


def _active_pair_bits(cnt, *, bsz, seq):
    nqb = seq // Q_BLOCK
    words = (nqb + 31) // 32
    act = (cnt[:, :, :, 0, :] > 0.5).reshape(bsz, N_KV, nqb, nqb, 2).any(axis=-1)
    tile_id = jnp.arange(nqb, dtype=jnp.int32)
    first = jnp.min(jnp.where(act & (tile_id >= 1), tile_id, nqb), axis=-1, keepdims=True)
    act = jnp.pad(act, ((0, 0), (0, 0), (0, 0), (0, words * 32 - nqb))).reshape(bsz, N_KV, nqb, words, 32)
    weights = jnp.left_shift(jnp.uint32(1), jnp.arange(32, dtype=jnp.uint32))
    packed = jnp.sum(act.astype(jnp.uint32) * weights, axis=-1, dtype=jnp.uint32)
    packed = lax.bitcast_convert_type(packed, jnp.int32)
    return jnp.concatenate([first, packed], axis=-1).reshape(-1)


def _outproj_router_kernel(a_ref, x_ref, w_ref, gpost_ref, gate_ref, gpre_ref, sh_ref, sc_ref, wr_ref, wrlo_ref,
                           x2_ref, h2_ref, rt_ref):
    y = jnp.dot(a_ref[...], w_ref[...], preferred_element_type=F32)
    x2 = x_ref[...] + gate_ref[0] * (_rms(y) * gpost_ref[...])
    x2_ref[...] = x2
    h = _rms(x2) * gpre_ref[...]
    h = h * (1.0 + sc_ref[0]) + sh_ref[0]
    hb = h.astype(BF16)
    h2_ref[...] = hb
    h_lo = (h - hb.astype(F32)).astype(BF16)
    logits = (jnp.dot(hb, wr_ref[...], preferred_element_type=F32)
              + jnp.dot(h_lo, wr_ref[...], preferred_element_type=F32)
              + jnp.dot(hb, wrlo_ref[...], preferred_element_type=F32))
    lane = lax.broadcasted_iota(jnp.int32, logits.shape, 1)
    logits = jnp.where(lane < N_EXPERTS, logits, -jnp.inf)
    v1 = jnp.max(logits, axis=-1, keepdims=True)
    i1 = jnp.min(jnp.where(logits == v1, lane, LANES), axis=-1, keepdims=True)
    rest = jnp.where(lane == i1, -jnp.inf, logits)
    v2 = jnp.max(rest, axis=-1, keepdims=True)
    i2 = jnp.min(jnp.where(rest == v2, lane, LANES), axis=-1, keepdims=True)
    e2 = jnp.exp(v2 - v1)
    w1 = 1.0 / (1.0 + e2)
    w2 = e2 / (1.0 + e2)
    out = jnp.where(lane == 0, w1, jnp.where(lane == 1, w2, 0.0))
    out = jnp.where(lane == 2, i1.astype(F32), jnp.where(lane == 3, i2.astype(F32), out))
    rt_ref[...] = out


def _outproj_router_call(a, x, w_out, g_post, gate, g_pre, shift, scale, w_router, *, seq):
    n, d = x.shape
    tm = _tile(seq, 512)
    tpb = seq // tm
    wr32 = jnp.pad(w_router, ((0, 0), (0, LANES - w_router.shape[1])))
    wr = wr32.astype(BF16)
    wr_lo = (wr32 - wr.astype(F32)).astype(BF16)
    row = lambda i: (i, 0)
    fixed = lambda i: (0, 0)
    per_b = lambda i: (i // tpb, 0, 0)
    return pl.pallas_call(
        _outproj_router_kernel,
        out_shape=[jax.ShapeDtypeStruct((n, d), F32),
                   jax.ShapeDtypeStruct((n, d), BF16),
                   jax.ShapeDtypeStruct((n, LANES), F32)],
        grid=(n // tm,),
        in_specs=[pl.BlockSpec((tm, a.shape[1]), row),
                  pl.BlockSpec((tm, d), row),
                  pl.BlockSpec(w_out.shape, fixed),
                  pl.BlockSpec((1, d), fixed),
                  pl.BlockSpec((1, 1, d), per_b),
                  pl.BlockSpec((1, d), fixed),
                  pl.BlockSpec((1, 1, d), per_b),
                  pl.BlockSpec((1, 1, d), per_b),
                  pl.BlockSpec((d, LANES), fixed),
                  pl.BlockSpec((d, LANES), fixed)],
        out_specs=[pl.BlockSpec((tm, d), row), pl.BlockSpec((tm, d), row), pl.BlockSpec((tm, LANES), row)],
        compiler_params=_cparams(("parallel",)),
        name="l1_outproj_router",
    )(a, x, w_out, g_post.reshape(1, d), gate, g_pre.reshape(1, d), shift, scale, wr, wr_lo)


MOE_SUB_ROWS = 256


def _moe_kernel(te_ref, nt_ref, rows_ref, h_ref, wg_ref, wu_ref, wd_ref, o_ref, acc_ref):
    i = pl.program_id(0)
    j = pl.program_id(1)
    tm = h_ref.shape[0]
    n_rows = rows_ref[i]

    @pl.when(j == 0)
    def _():
        acc_ref[...] = jnp.zeros_like(acc_ref)

    def swiglu_rows(rs):
        h = h_ref[rs, :]
        g = jnp.dot(h, wg_ref[0], preferred_element_type=F32)
        u = jnp.dot(h, wu_ref[0], preferred_element_type=F32)
        a = (g * _sigmoid(g) * u).astype(BF16)
        acc_ref[rs, :] += jnp.dot(a, wd_ref[0], preferred_element_type=F32)

    @pl.when(n_rows == tm)
    def _():
        swiglu_rows(slice(None))

    @pl.when((n_rows > 0) & (n_rows < tm))
    def _():
        for sb in range(tm // MOE_SUB_ROWS):
            pl.when(n_rows > sb * MOE_SUB_ROWS)(
                functools.partial(swiglu_rows, slice(sb * MOE_SUB_ROWS, (sb + 1) * MOE_SUB_ROWS)))

    @pl.when(j == pl.num_programs(1) - 1)
    def _():
        o_ref[...] = acc_ref[...].astype(o_ref.dtype)


def _moe_call(tile_expert, n_tiles, tile_rows, hs, wg, wu, wd, *, tm):
    mp, d = hs.shape
    f = wg.shape[2]
    tf = _tile(f, 512)
    nj = f // tf

    def wcol(i, j, nt):
        return jnp.where(i < nt[0], j, nj - 1)

    grid_spec = pltpu.PrefetchScalarGridSpec(
        num_scalar_prefetch=3,
        grid=(mp // tm, f // tf),
        in_specs=[pl.BlockSpec((tm, d), lambda i, j, te, nt, tr: (i, 0)),
                  pl.BlockSpec((1, d, tf), lambda i, j, te, nt, tr: (te[i], 0, wcol(i, j, nt))),
                  pl.BlockSpec((1, d, tf), lambda i, j, te, nt, tr: (te[i], 0, wcol(i, j, nt))),
                  pl.BlockSpec((1, tf, d), lambda i, j, te, nt, tr: (te[i], wcol(i, j, nt), 0))],
        out_specs=pl.BlockSpec((tm, d), lambda i, j, te, nt, tr: (i, 0)),
        scratch_shapes=[pltpu.VMEM((tm, d), F32)],
    )
    return pl.pallas_call(
        _moe_kernel,
        out_shape=jax.ShapeDtypeStruct((mp, d), BF16),
        grid_spec=grid_spec,
        compiler_params=_cparams(("arbitrary", "arbitrary")),
        name="l1_moe_experts",
    )(tile_expert, n_tiles, tile_rows, hs, wg, wu, wd)


def _residual_norm_kernel(x_ref, y0_ref, y1_ref, rt_ref, gpost_ref, gate_ref, o_ref):
    y = rt_ref[:, 0:1] * y0_ref[...].astype(F32) + rt_ref[:, 1:2] * y1_ref[...].astype(F32)
    o_ref[...] = x_ref[...] + gate_ref[0] * (_rms(y) * gpost_ref[...])


def _residual_norm_call(x, y0, y1, rt, g_post, gate, *, seq):
    n, d = x.shape
    tm = _tile(seq, 512)
    tpb = seq // tm
    row = lambda i: (i, 0)
    return pl.pallas_call(
        _residual_norm_kernel,
        out_shape=jax.ShapeDtypeStruct((n, d), F32),
        grid=(n // tm,),
        in_specs=[pl.BlockSpec((tm, d), row), pl.BlockSpec((tm, d), row), pl.BlockSpec((tm, d), row),
                  pl.BlockSpec((tm, LANES), row),
                  pl.BlockSpec((1, d), lambda i: (0, 0)),
                  pl.BlockSpec((1, 1, d), lambda i: (i // tpb, 0, 0))],
        out_specs=pl.BlockSpec((tm, d), row),
        compiler_params=_cparams(("parallel",)),
        name="l1_moe_residual",
    )(x, y0, y1, rt, g_post.reshape(1, d), gate)


CAST_BLOCK_BYTES = 8 * 1024 * 1024
CAST_BLOCK_COLS = 2048


def _cast_kernel(x_ref, o_ref):
    o_ref[...] = x_ref[...].astype(o_ref.dtype)


def _to_bf16(w):
    cols = w.shape[-1]
    rows = w.size // cols
    tc = _tile(cols, CAST_BLOCK_COLS)
    tr = rows
    while tr * tc * 4 > CAST_BLOCK_BYTES and tr % 2 == 0 and (tr // 2) % 16 == 0:
        tr //= 2
    out = pl.pallas_call(
        _cast_kernel,
        out_shape=jax.ShapeDtypeStruct((rows, cols), BF16),
        grid=(rows // tr, cols // tc),
        in_specs=[pl.BlockSpec((tr, tc), lambda i, j: (i, j))],
        out_specs=pl.BlockSpec((tr, tc), lambda i, j: (i, j)),
        compiler_params=_cparams(("parallel", "parallel")),
        name="weight_cast",
    )(w.reshape(rows, cols))
    return out.reshape(w.shape)


def _even_layer(x2d, c, ada_w, ada_b, mix_pre_g, mix_post_g, ffn_pre_g, ffn_post_g,
                w_in, v_ln_g, v_ln_b, w_spatial, b_spatial, conv_w, w_out,
                ffn_w_gate, ffn_w_up, ffn_w_down, *, seq):
    sh_m, sc_m, gt_m, sh_f, sc_f, gt_f = _ada_call(c, ada_w, ada_b)
    proj = _proj_call(x2d, mix_pre_g, sh_m, sc_m, _to_bf16(w_in), seq=seq, name="l0_in_proj")
    x2d = _mixer0_call(proj, x2d, v_ln_g, v_ln_b, w_spatial, b_spatial, conv_w, _to_bf16(w_out),
                       mix_post_g, gt_m, seq=seq)
    return _ffn_call(x2d, ffn_pre_g, sh_f, sc_f, _to_bf16(ffn_w_gate), _to_bf16(ffn_w_up),
                     _to_bf16(ffn_w_down), ffn_post_g, gt_f, seq=seq)


def _moe_routing(rt, *, tm):
    n = rt.shape[0]
    na = n * TOP_K
    mp = na + N_EXPERTS * tm
    e_flat = rt[:, TOP_K:2 * TOP_K].astype(jnp.int32).reshape(na)
    onehot = (e_flat[:, None] == jnp.arange(N_EXPERTS, dtype=jnp.int32)[None, :]).astype(jnp.int32)
    csum = jnp.cumsum(onehot, axis=0)
    counts = csum[-1]
    rank = jnp.sum(onehot * csum, axis=1) - 1
    padded = ((counts + tm - 1) // tm) * tm
    ends = jnp.cumsum(padded)
    starts = ends - padded
    dest = jnp.sum(onehot * starts[None, :], axis=1) + rank
    order = jnp.argsort(e_flat, stable=True).astype(jnp.int32)
    rows = jnp.arange(mp, dtype=jnp.int32)
    row_e = jnp.minimum(jnp.sum((rows[:, None] >= ends[None, :]).astype(jnp.int32), axis=1), N_EXPERTS - 1)
    slot = rows - starts[row_e]
    src = (jnp.cumsum(counts) - counts)[row_e] + slot
    row_token = jnp.where(slot < counts[row_e], order[jnp.clip(src, 0, na - 1)] // TOP_K, 0)
    n_tiles = (ends[-1] // tm).astype(jnp.int32)
    tile_ids = jnp.arange(mp // tm, dtype=jnp.int32)
    tile_expert = jnp.sum((tile_ids[:, None] * tm >= ends[None, :]).astype(jnp.int32), axis=1)
    last = jnp.sum(((n_tiles - 1) * tm >= ends).astype(jnp.int32))
    tile_expert = jnp.where(tile_ids < n_tiles, tile_expert, last).astype(jnp.int32)
    tile_rows = jnp.clip(starts[tile_expert] + counts[tile_expert] - tile_ids * tm, 0, tm)
    tile_rows = jnp.where(tile_ids < n_tiles, tile_rows, 0).astype(jnp.int32)
    return row_token, dest.reshape(n, TOP_K), tile_expert, n_tiles.reshape(1), tile_rows


def _alibi_slope_rows():
    h = np.arange(1, N_HEADS + 1, dtype=np.float64)
    s = (LOG2E * np.exp2(-8.0 * h / N_HEADS)).astype(np.float32).reshape(N_KV, 1, Q_PER_KV, 1)
    rows = np.broadcast_to(s, (N_KV, SUBLANES, Q_PER_KV, Q_BLOCK)).reshape(N_KV, SUBLANES, GQ_WIDTH)
    return jnp.asarray(rows)


def _overlap_t(seq):
    ncp = seq // CMP_STRIDE
    nslc = seq // SLC_BLOCK
    cs = np.arange(ncp)[None, :] * CMP_STRIDE
    ss = np.arange(nslc)[:, None] * SLC_BLOCK
    ov = (cs < ss + SLC_BLOCK) & (cs + CMP_BLOCK > ss) & (np.arange(ncp)[None, :] < ncp - 1)
    return jnp.asarray(ov.astype(np.float32)).astype(BF16)


def _odd_layer(x2d, c, ada_w, ada_b, mix_pre_g, mix_post_g, ffn_pre_g, ffn_post_g,
               w_in, cmp_k_pe, cmp_k_w1, cmp_k_w2, cmp_v_pe, cmp_v_w1, cmp_v_w2, w_out,
               router_w, exp_w_gate, exp_w_up, exp_w_down, *, bsz, seq):
    n, d = x2d.shape
    sh_m, sc_m, gt_m, sh_f, sc_f, gt_f = _ada_call(c, ada_w, ada_b)
    nmain = Q_WIDTH + 6 * KV_WIDTH
    w_main = w_in[:, :nmain].astype(BF16)
    w_gates = jnp.pad(w_in[:, nmain:], ((0, 0), (0, LANES - 3 * N_HEADS))).astype(BF16)
    proj, gates = _proj_call(x2d, mix_pre_g, sh_m, sc_m, w_main, w_gates, seq=seq,
                             q_cols=Q_WIDTH, q_scale=LOG2E * HEAD_DIM ** -0.5, name="l1_in_proj")

    nqb = seq // Q_BLOCK
    ncp = seq // CMP_STRIDE

    def kv_cols(k):
        o = Q_WIDTH + k * KV_WIDTH
        return proj[:, o:o + KV_WIDTH]

    def chunked(t):
        t = t.reshape(bsz, ncp, CMP_STRIDE, N_KV, HEAD_DIM).transpose(0, 3, 1, 2, 4)
        return t.reshape(bsz * N_KV, ncp, CMP_STRIDE * HEAD_DIM)

    kcmp, _ = _compress_call(chunked(kv_cols(0)), cmp_k_pe, cmp_k_w1, cmp_k_w2)
    _, vcmpt = _compress_call(chunked(kv_cols(1)), cmp_v_pe, cmp_v_w1, cmp_v_w2)
    slopes = _alibi_slope_rows()
    oc, selb, cnt = _attn_cmp_call(proj, kcmp, vcmpt, _overlap_t(seq), slopes, bsz=bsz, seq=seq)
    bits = _active_pair_bits(cnt, bsz=bsz, seq=seq)

    gt = gates[:, :3 * N_HEADS].reshape(bsz, nqb, Q_BLOCK, 3, N_KV, Q_PER_KV).transpose(0, 4, 1, 3, 5, 2)
    gt = gt.reshape(bsz, N_KV, nqb, 3, GQ_WIDTH)
    gt = jnp.pad(gt, ((0, 0), (0, 0), (0, 0), (0, SUBLANES - 3), (0, 0)))
    attn = _attn_sel_call(bits, proj, selb, oc, gt, slopes, _window_bias_tiles(), bsz=bsz, seq=seq)

    x2, h2, rt = _outproj_router_call(attn, x2d, _to_bf16(w_out), mix_post_g, gt_m,
                                      ffn_pre_g, sh_f, sc_f, router_w, seq=seq)

    tm = min(1024, n)
    row_token, dest, tile_expert, n_tiles, tile_rows = _moe_routing(rt, tm=tm)
    hs = jnp.take(h2, row_token, axis=0)
    rows = _moe_call(tile_expert, n_tiles, tile_rows, hs, _to_bf16(exp_w_gate), _to_bf16(exp_w_up),
                     _to_bf16(exp_w_down), tm=tm)
    y0 = jnp.take(rows, dest[:, 0], axis=0)
    y1 = jnp.take(rows, dest[:, 1], axis=0)
    return _residual_norm_call(x2, y0, y1, rt, ffn_post_g, gt_f, seq=seq)


def kernel(x, c, l0_ada_w, l0_ada_b, l0_mix_pre_g, l0_mix_post_g, l0_ffn_pre_g, l0_ffn_post_g, l0_w_in, l0_v_ln_g, l0_v_ln_b, l0_w_spatial, l0_b_spatial, l0_conv_w, l0_w_out, l0_ffn_w_gate, l0_ffn_w_up, l0_ffn_w_down, l1_ada_w, l1_ada_b, l1_mix_pre_g, l1_mix_post_g, l1_ffn_pre_g, l1_ffn_post_g, l1_w_in, l1_cmp_k_pe, l1_cmp_k_w1, l1_cmp_k_w2, l1_cmp_v_pe, l1_cmp_v_w1, l1_cmp_v_w2, l1_w_out, l1_router_w, l1_exp_w_gate, l1_exp_w_up, l1_exp_w_down):
    bsz, seq, d = x.shape
    x2d = x.reshape(bsz * seq, d)
    x2d = _even_layer(x2d, c, l0_ada_w, l0_ada_b, l0_mix_pre_g, l0_mix_post_g, l0_ffn_pre_g, l0_ffn_post_g,
                      l0_w_in, l0_v_ln_g, l0_v_ln_b, l0_w_spatial, l0_b_spatial, l0_conv_w, l0_w_out,
                      l0_ffn_w_gate, l0_ffn_w_up, l0_ffn_w_down, seq=seq)
    x2d = _odd_layer(x2d, c, l1_ada_w, l1_ada_b, l1_mix_pre_g, l1_mix_post_g, l1_ffn_pre_g, l1_ffn_post_g,
                     l1_w_in, l1_cmp_k_pe, l1_cmp_k_w1, l1_cmp_k_w2, l1_cmp_v_pe, l1_cmp_v_w1, l1_cmp_v_w2,
                     l1_w_out, l1_router_w, l1_exp_w_gate, l1_exp_w_up, l1_exp_w_down, bsz=bsz, seq=seq)
    return x2d.reshape(bsz, seq, d)
```

```python
import functools
import math

import numpy as np
import jax
import jax.numpy as jnp
from jax import lax
from jax.experimental import pallas as pl
from jax.experimental.pallas import tpu as pltpu

F32 = jnp.float32
BF16 = jnp.bfloat16
HIGHEST = lax.Precision.HIGHEST

EPS = 1e-6
LOG2E = math.log2(math.e)
NEG_INF = -1e30
FORCE_SCORE = 1e4

LANES = 128
SUBLANES = 8
VMEM_LIMIT_BYTES = 56 * 1024 * 1024
FFN_VMEM_LIMIT_BYTES = 60 * 1024 * 1024

A_GROUPS = 8
A_WIDTH = 1024
B_WIDTH = 1024
CHUNK = 128
CONV_W = 3
N_HEADS = 16
HEAD_DIM = 128
N_KV = 4
Q_PER_KV = N_HEADS // N_KV
Q_WIDTH = N_HEADS * HEAD_DIM
KV_WIDTH = N_KV * HEAD_DIM
GQ_WIDTH = Q_PER_KV * HEAD_DIM
CMP_BLOCK = 32
CMP_STRIDE = 16
CMP_HIDDEN = 256
SLC_BLOCK = 64
N_SELECT = 16
WINDOW = 512
Q_BLOCK = 128
N_EXPERTS = 8
TOP_K = 2


def _cparams(semantics, vmem_limit_bytes=VMEM_LIMIT_BYTES):
    return pltpu.CompilerParams(dimension_semantics=semantics, vmem_limit_bytes=vmem_limit_bytes)


def _sigmoid(x):
    return 1.0 / (1.0 + jnp.exp(-x))


def _gelu_tanh(x):
    return 0.5 * x * (1.0 + jnp.tanh(0.7978845608028654 * (x + 0.044715 * (x * x * x))))


def _rms(x):
    return x * lax.rsqrt(jnp.mean(x * x, axis=-1, keepdims=True) + EPS)


def _tile(n, pref):
    if n <= pref:
        return n
    t = (pref // LANES) * LANES
    while n % t:
        t -= LANES
    assert t > 0, (n, pref)
    return t


def _ada_kernel(c_ref, w_ref, b_ref, o_ref):
    c = c_ref[...]
    s = c * _sigmoid(c)
    o_ref[...] = jnp.dot(s, w_ref[...], preferred_element_type=F32, precision=HIGHEST) + b_ref[...]


def _ada_call(c, w, b):
    bsz, d = c.shape
    n = w.shape[1]
    tn = _tile(n, 1024)
    cp = jnp.pad(c, ((0, SUBLANES - bsz), (0, 0)))
    out = pl.pallas_call(
        _ada_kernel,
        out_shape=jax.ShapeDtypeStruct((SUBLANES, n), F32),
        grid=(n // tn,),
        in_specs=[pl.BlockSpec((SUBLANES, d), lambda j: (0, 0)),
                  pl.BlockSpec((d, tn), lambda j: (0, j)),
                  pl.BlockSpec((1, tn), lambda j: (0, j))],
        out_specs=pl.BlockSpec((SUBLANES, tn), lambda j: (0, j)),
        compiler_params=_cparams(("parallel",)),
        name="ada_modulation",
    )(cp, w, b.reshape(1, n))
    m = out[:bsz]
    return [t.reshape(bsz, 1, d) for t in jnp.split(m, 6, axis=-1)]


def _proj_kernel(*refs, q_tiles, q_scale, with_gates):
    if with_gates:
        x_ref, g_ref, sh_ref, sc_ref, w_ref, wg_ref, o_ref, og_ref, h_ref = refs
    else:
        x_ref, g_ref, sh_ref, sc_ref, w_ref, o_ref, h_ref = refs
    j = pl.program_id(1)

    @pl.when(j == 0)
    def _():
        h = _rms(x_ref[...]) * g_ref[...]
        h = h * (1.0 + sc_ref[0]) + sh_ref[0]
        h_ref[...] = h.astype(BF16)
        if with_gates:
            og_ref[...] = _sigmoid(jnp.dot(h_ref[...], wg_ref[...], preferred_element_type=F32))

    acc = jnp.dot(h_ref[...], w_ref[...], preferred_element_type=F32)
    if q_tiles:
        acc = acc * jnp.where(j < q_tiles, jnp.float32(q_scale), jnp.float32(1.0))
    o_ref[...] = acc.astype(o_ref.dtype)


def _proj_call(x, g, shift, scale, w, wg=None, *, seq, q_cols=0, q_scale=1.0, name):
    n, d = x.shape
    nout = w.shape[1]
    tm = _tile(seq, 1024)
    tn = _tile(nout, 1024)
    assert q_cols % tn == 0
    tpb = seq // tm
    with_gates = wg is not None
    in_specs = [pl.BlockSpec((tm, d), lambda i, j: (i, 0)),
                pl.BlockSpec((1, d), lambda i, j: (0, 0)),
                pl.BlockSpec((1, 1, d), lambda i, j: (i // tpb, 0, 0)),
                pl.BlockSpec((1, 1, d), lambda i, j: (i // tpb, 0, 0)),
                pl.BlockSpec((d, tn), lambda i, j: (0, j))]
    args = [x, g.reshape(1, d), shift, scale, w]
    out_shape = [jax.ShapeDtypeStruct((n, nout), BF16)]
    out_specs = [pl.BlockSpec((tm, tn), lambda i, j: (i, j))]
    if with_gates:
        in_specs.append(pl.BlockSpec((d, LANES), lambda i, j: (0, 0)))
        args.append(wg)
        out_shape.append(jax.ShapeDtypeStruct((n, LANES), F32))
        out_specs.append(pl.BlockSpec((tm, LANES), lambda i, j: (i, 0)))
    out = pl.pallas_call(
        functools.partial(_proj_kernel, q_tiles=q_cols // tn, q_scale=q_scale, with_gates=with_gates),
        out_shape=out_shape,
        grid=(n // tm, nout // tn),
        in_specs=in_specs,
        out_specs=out_specs,
        scratch_shapes=[pltpu.VMEM((tm, d), BF16)],
        compiler_params=_cparams(("parallel", "arbitrary")),
        name=name,
    )(*args)
    return out if with_gates else out[0]


def _mixer0_kernel(p_ref, pgc_ref, pxb_ref, x_ref, vg_ref, vb_ref, wsp_ref, bspt_ref, cw_ref,
                   wout_ref, gpost_ref, gate_ref, o_ref, cat_ref, *, tm, tiles_per_batch):
    i = pl.program_id(0)
    nchunk = tm // CHUNK
    v = _gelu_tanh(p_ref[:, A_WIDTH:2 * A_WIDTH].astype(F32))
    mu = jnp.mean(v, axis=-1, keepdims=True)
    vc = v - mu
    var = jnp.mean(vc * vc, axis=-1, keepdims=True)
    vn = (vc * lax.rsqrt(var + EPS) * vg_ref[...] + vb_ref[...]).astype(BF16)
    row = lax.broadcasted_iota(jnp.int32, (CHUNK, CHUNK), 0)
    col = lax.broadcasted_iota(jnp.int32, (CHUNK, CHUNK), 1)
    causal = col <= row
    for h in range(A_GROUPS):
        cs = slice(h * LANES, (h + 1) * LANES)
        w = jnp.where(causal, wsp_ref[h], 0.0).astype(BF16)
        rhs = jnp.concatenate([vn[c * CHUNK:(c + 1) * CHUNK, cs] for c in range(nchunk)], axis=1)
        z = jnp.dot(w, rhs, preferred_element_type=F32) + bspt_ref[:, h:h + 1]
        for c in range(nchunk):
            rs = slice(c * CHUNK, (c + 1) * CHUNK)
            u = _gelu_tanh(p_ref[rs, cs].astype(F32))
            cat_ref[rs, cs] = (u * z[:, c * CHUNK:(c + 1) * CHUNK]).astype(BF16)
    o_gb, o_gc, o_xb = 2 * A_WIDTH, 2 * A_WIDTH + B_WIDTH, 2 * A_WIDTH + 2 * B_WIDTH
    zc = p_ref[:, o_gc:o_gc + B_WIDTH].astype(F32) * p_ref[:, o_xb:o_xb + B_WIDTH].astype(F32)
    prev = pgc_ref[...].astype(F32) * pxb_ref[...].astype(F32)
    prev = jnp.where(i % tiles_per_batch == 0, 0.0, prev)
    nprev = prev.shape[0]
    p1 = prev[nprev - 1:nprev, :]
    p2 = prev[nprev - 2:nprev - 1, :]
    rowi = lax.broadcasted_iota(jnp.int32, (tm, B_WIDTH), 0)
    r1 = jnp.where(rowi == 0, p1, pltpu.roll(zc, 1, 0))
    r2 = jnp.where(rowi == 0, p2, jnp.where(rowi == 1, p1, pltpu.roll(zc, 2, 0)))
    y = cw_ref[0:1, :] * r2 + cw_ref[1:2, :] * r1 + cw_ref[2:3, :] * zc
    cat_ref[:, A_WIDTH:] = (p_ref[:, o_gb:o_gb + B_WIDTH].astype(F32) * y).astype(BF16)
    yo = jnp.dot(cat_ref[...], wout_ref[...], preferred_element_type=F32)
    o_ref[...] = x_ref[...] + gate_ref[0] * (_rms(yo) * gpost_ref[...])


def _mixer0_call(proj, x, v_ln_g, v_ln_b, w_spatial, b_spatial, conv_w, w_out, g_post, gate, *, seq):
    n, d = x.shape
    tm = _tile(seq, 512)
    tpb = seq // tm
    pr = 16
    rpb = tm // pr
    wcat = A_WIDTH + B_WIDTH
    cwp = jnp.pad(conv_w, ((0, SUBLANES - CONV_W), (0, 0)))
    return pl.pallas_call(
        functools.partial(_mixer0_kernel, tm=tm, tiles_per_batch=tpb),
        out_shape=jax.ShapeDtypeStruct((n, d), F32),
        grid=(n // tm,),
        in_specs=[pl.BlockSpec((tm, proj.shape[1]), lambda i: (i, 0)),
                  pl.BlockSpec((pr, B_WIDTH), lambda i: (jnp.maximum(i * rpb - 1, 0), 3)),
                  pl.BlockSpec((pr, B_WIDTH), lambda i: (jnp.maximum(i * rpb - 1, 0), 4)),
                  pl.BlockSpec((tm, d), lambda i: (i, 0)),
                  pl.BlockSpec((1, A_WIDTH), lambda i: (0, 0)),
                  pl.BlockSpec((1, A_WIDTH), lambda i: (0, 0)),
                  pl.BlockSpec((A_GROUPS, CHUNK, CHUNK), lambda i: (0, 0, 0)),
                  pl.BlockSpec((CHUNK, A_GROUPS), lambda i: (0, 0)),
                  pl.BlockSpec((SUBLANES, B_WIDTH), lambda i: (0, 0)),
                  pl.BlockSpec((wcat, d), lambda i: (0, 0)),
                  pl.BlockSpec((1, d), lambda i: (0, 0)),
                  pl.BlockSpec((1, 1, d), lambda i: (i // tpb, 0, 0))],
        out_specs=pl.BlockSpec((tm, d), lambda i: (i, 0)),
        scratch_shapes=[pltpu.VMEM((tm, wcat), BF16)],
        compiler_params=_cparams(("parallel",)),
        name="l0_mixer",
    )(proj, proj, proj, x, v_ln_g.reshape(1, -1), v_ln_b.reshape(1, -1), w_spatial, b_spatial.T, cwp,
      w_out, g_post.reshape(1, d), gate)


def _ffn_kernel(*refs, n_side):
    x_ref, gpre_ref, sh_ref, sc_ref, wg_ref, wu_ref, wd_ref, gpost_ref, gate_ref = refs[:9]
    side_in = refs[9:9 + n_side]
    o_ref = refs[9 + n_side]
    side_out = refs[10 + n_side:10 + 2 * n_side]
    h_ref = refs[10 + 2 * n_side]
    j = pl.program_id(1)
    for src, dst in zip(side_in, side_out):
        dst[...] = src[...].astype(dst.dtype)

    @pl.when(j == 0)
    def _():
        h = _rms(x_ref[...]) * gpre_ref[...]
        h_ref[...] = (h * (1.0 + sc_ref[0]) + sh_ref[0]).astype(BF16)
        o_ref[...] = jnp.zeros_like(o_ref)

    h = h_ref[...]
    g = jnp.dot(h, wg_ref[...], preferred_element_type=F32)
    u = jnp.dot(h, wu_ref[...], preferred_element_type=F32)
    a = (g * _sigmoid(g) * u).astype(BF16)
    o_ref[...] += jnp.dot(a, wd_ref[...], preferred_element_type=F32)

    @pl.when(j == pl.num_programs(1) - 1)
    def _():
        o_ref[...] = x_ref[...] + gate_ref[0] * (_rms(o_ref[...]) * gpost_ref[...])


def _side_cast_spec(w, gi, gj):
    cols = w.shape[-1]
    rows = w.size // cols
    if rows % gi == 0 and cols % gj == 0 and (rows // gi) % 16 == 0 and (cols // gj) % LANES == 0:
        return w.reshape(rows, cols), pl.BlockSpec((rows // gi, cols // gj), lambda i, j: (i, j))
    if rows % (gi * gj) == 0 and (rows // (gi * gj)) % 16 == 0:
        return w.reshape(rows, cols), pl.BlockSpec((rows // (gi * gj), cols), lambda i, j: (i * gj + j, 0))
    return None


def _ffn_call(x, g_pre, shift, scale, wg, wu, wd, g_post, gate, *, seq, side_casts=()):
    n, d = x.shape
    f = wg.shape[1]
    tm = _tile(seq, 512)
    tf = _tile(f, 1024)
    tpb = seq // tm
    gi, gj = n // tm, f // tf
    specs = [_side_cast_spec(w, gi, gj) for w in side_casts]
    fused = [s for s in specs if s is not None]
    out = pl.pallas_call(
        functools.partial(_ffn_kernel, n_side=len(fused)),
        out_shape=[jax.ShapeDtypeStruct((n, d), F32)] + [jax.ShapeDtypeStruct(w2.shape, BF16) for w2, _ in fused],
        grid=(gi, gj),
        in_specs=[pl.BlockSpec((tm, d), lambda i, j: (i, 0)),
                  pl.BlockSpec((1, d), lambda i, j: (0, 0)),
                  pl.BlockSpec((1, 1, d), lambda i, j: (i // tpb, 0, 0)),
                  pl.BlockSpec((1, 1, d), lambda i, j: (i // tpb, 0, 0)),
                  pl.BlockSpec((d, tf), lambda i, j: (0, j)),
                  pl.BlockSpec((d, tf), lambda i, j: (0, j)),
                  pl.BlockSpec((tf, d), lambda i, j: (j, 0)),
                  pl.BlockSpec((1, d), lambda i, j: (0, 0)),
                  pl.BlockSpec((1, 1, d), lambda i, j: (i // tpb, 0, 0))] + [spec for _, spec in fused],
        out_specs=[pl.BlockSpec((tm, d), lambda i, j: (i, 0))] + [spec for _, spec in fused],
        scratch_shapes=[pltpu.VMEM((tm, d), BF16)],
        compiler_params=_cparams(("arbitrary", "arbitrary"), FFN_VMEM_LIMIT_BYTES),
        name="l0_ffn",
    )(x, g_pre.reshape(1, d), shift, scale, wg, wu, wd, g_post.reshape(1, d), gate, *[w2 for w2, _ in fused])
    fused_out = iter(out[1:])
    casts = [next(fused_out).reshape(w.shape) if s is not None else _to_bf16(w) for w, s in zip(side_casts, specs)]
    return out[0], casts


def _compress_kernel(x_ref, pe_ref, w1_ref, w2_ref, o_ref, ot_ref):
    half = CMP_STRIDE * HEAD_DIM
    x = x_ref[0]
    nc = x.shape[0]
    a = jnp.dot(x, w1_ref[0:half, :], preferred_element_type=F32)
    b = jnp.dot(x, w1_ref[half:2 * half, :], preferred_element_type=F32)
    bias = jnp.dot(pe_ref[...], w1_ref[...], preferred_element_type=F32)[0:1, :]
    hid = _gelu_tanh(a + pltpu.roll(b, nc - 1, 0) + bias)
    out = jnp.dot(hid.astype(BF16), w2_ref[...], preferred_element_type=F32)
    rowi = lax.broadcasted_iota(jnp.int32, out.shape, 0)
    out = jnp.where(rowi < nc - 1, out, 0.0)
    o_ref[0] = out.astype(BF16)
    ot_ref[0] = out.T.astype(BF16)


def _compress_call(xc, pe, w1, w2):
    bg, nc, kdim = xc.shape
    pe_flat = jnp.pad(pe.reshape(1, -1), ((0, SUBLANES - 1), (0, 0))).astype(BF16)
    w1f = w1.reshape(CMP_BLOCK * HEAD_DIM, CMP_HIDDEN).astype(BF16)
    return pl.pallas_call(
        _compress_kernel,
        out_shape=[jax.ShapeDtypeStruct((bg, nc, HEAD_DIM), BF16),
                   jax.ShapeDtypeStruct((bg, HEAD_DIM, nc), BF16)],
        grid=(bg,),
        in_specs=[pl.BlockSpec((1, nc, kdim), lambda i: (i, 0, 0)),
                  pl.BlockSpec((SUBLANES, CMP_BLOCK * HEAD_DIM), lambda i: (0, 0)),
                  pl.BlockSpec((CMP_BLOCK * HEAD_DIM, CMP_HIDDEN), lambda i: (0, 0)),
                  pl.BlockSpec((CMP_HIDDEN, HEAD_DIM), lambda i: (0, 0))],
        out_specs=[pl.BlockSpec((1, nc, HEAD_DIM), lambda i: (i, 0, 0)),
                   pl.BlockSpec((1, HEAD_DIM, nc), lambda i: (i, 0, 0))],
        compiler_params=_cparams(("parallel",)),
        name="nsa_compress",
    )(xc, pe_flat, w1f, w2.astype(BF16))


def _q_transposed(q_ref):
    parts = []
    for r in range(Q_PER_KV):
        parts.append(q_ref[:, r * HEAD_DIM:(r + 1) * HEAD_DIM].astype(F32).T.astype(BF16))
    return jnp.concatenate(parts, axis=1)


def _attn_cmp_kernel(q_ref, kc_ref, vct_ref, ovt_ref, slope_ref, oc_ref, sel_ref, cnt_ref, imp_ref,
                     *, n_sel, n_class):
    qb = pl.program_id(2)
    q0 = qb * Q_BLOCK
    qt = _q_transposed(q_ref)
    ncp = kc_ref.shape[1]
    nslc = ovt_ref.shape[0]
    rows_per_class = ncp // n_class
    visible = (Q_BLOCK // CMP_STRIDE) * (qb + 1) - 1
    cls = jnp.minimum((visible - 1) // rows_per_class, n_class - 1)

    def compressed(n):
        s_all = jnp.dot(kc_ref[0, 0:n, :], qt, preferred_element_type=F32)
        isub = lax.broadcasted_iota(jnp.int32, (n, Q_BLOCK), 0)
        tl = lax.broadcasted_iota(jnp.int32, (n, Q_BLOCK), 1) + q0
        dist = tl - (isub * CMP_STRIDE + (CMP_BLOCK - 1))
        valid = dist >= 0
        distf = dist.astype(F32)
        psum = jnp.zeros((n, Q_BLOCK), F32)
        pparts = []
        for r in range(Q_PER_KV):
            ls = slice(r * Q_BLOCK, (r + 1) * Q_BLOCK)
            s = s_all[:, ls] - slope_ref[0, 0:1, ls] * distf
            s = jnp.where(valid, s, NEG_INF)
            m = jnp.max(s, axis=0, keepdims=True)
            e = jnp.exp2(s - m)
            inv = jnp.where(m > 0.5 * NEG_INF, 1.0 / jnp.sum(e, axis=0, keepdims=True), 0.0)
            p = e * inv
            psum = psum + p
            pparts.append(p.astype(BF16))
        pt = jnp.concatenate(pparts, axis=1)
        oc_ref[0, 0, 0] = jnp.dot(vct_ref[0, :, 0:n], pt,
                                  preferred_element_type=F32).astype(oc_ref.dtype)
        hi = psum.astype(BF16)
        lo = (psum - hi.astype(F32)).astype(BF16)
        ovt = ovt_ref[:, 0:n]
        imp_ref[...] = (jnp.dot(ovt, hi, preferred_element_type=F32)
                        + jnp.dot(ovt, lo, preferred_element_type=F32))

    for k in range(n_class):
        pl.when(cls == k)(functools.partial(compressed, (k + 1) * rows_per_class))

    jb = lax.broadcasted_iota(jnp.int32, (nslc, Q_BLOCK), 0)
    cur = (lax.broadcasted_iota(jnp.int32, (nslc, Q_BLOCK), 1) + q0) // SLC_BLOCK
    forced = (jb == 0) | (jb == cur) | (jb == cur - 1)
    score = jnp.where(forced, FORCE_SCORE, imp_ref[...])
    score = jnp.where(jb <= cur, score, -1.0)
    picked = -3e38
    three_forced = qb > 0
    score = jnp.where(forced & three_forced, picked, score)

    def pick_next(_, work):
        m = jnp.max(work, axis=0, keepdims=True)
        first = jnp.min(jnp.where(work == m, jb, nslc), axis=0, keepdims=True)
        return jnp.where(jb == first, picked, work)

    work = lax.fori_loop(0, jnp.where(three_forced, n_sel - 3, n_sel), pick_next, score)
    chosen = (work == picked) & (jb <= cur)
    sel_ref[0, 0, 0] = jnp.where(chosen, 0.0, NEG_INF)
    flags = jnp.where(chosen, 1.0, 0.0).astype(BF16)
    cnt_ref[0, 0, 0] = lax.dot_general(jnp.ones((SUBLANES, Q_BLOCK), BF16), flags, (((1,), (1,)), ((), ())),
                                       preferred_element_type=F32)


def _attn_cmp_call(proj, kcmp, vcmpt, ovt, slopes, *, bsz, seq):
    nqb = seq // Q_BLOCK
    ncp = kcmp.shape[1]
    nslc = seq // SLC_BLOCK
    n_sel = min(N_SELECT, nslc)
    return pl.pallas_call(
        functools.partial(_attn_cmp_kernel, n_sel=n_sel, n_class=8),
        out_shape=[jax.ShapeDtypeStruct((bsz, N_KV, nqb, HEAD_DIM, GQ_WIDTH), BF16),
                   jax.ShapeDtypeStruct((bsz, N_KV, nqb, nslc, Q_BLOCK), F32),
                   jax.ShapeDtypeStruct((bsz, N_KV, nqb, SUBLANES, nslc), F32)],
        grid=(bsz, N_KV, nqb),
        in_specs=[pl.BlockSpec((Q_BLOCK, GQ_WIDTH), lambda b, g, q: (b * nqb + q, g)),
                  pl.BlockSpec((1, ncp, HEAD_DIM), lambda b, g, q: (b * N_KV + g, 0, 0)),
                  pl.BlockSpec((1, HEAD_DIM, ncp), lambda b, g, q: (b * N_KV + g, 0, 0)),
                  pl.BlockSpec((nslc, ncp), lambda b, g, q: (0, 0)),
                  pl.BlockSpec((1, SUBLANES, GQ_WIDTH), lambda b, g, q: (g, 0, 0))],
        out_specs=[pl.BlockSpec((1, 1, 1, HEAD_DIM, GQ_WIDTH), lambda b, g, q: (b, g, q, 0, 0)),
                   pl.BlockSpec((1, 1, 1, nslc, Q_BLOCK), lambda b, g, q: (b, g, q, 0, 0)),
                   pl.BlockSpec((1, 1, 1, SUBLANES, nslc), lambda b, g, q: (b, g, q, 0, 0))],
        scratch_shapes=[pltpu.VMEM((nslc, Q_BLOCK), F32)],
        compiler_params=_cparams(("parallel", "parallel", "arbitrary")),
        name="nsa_compressed_select",
    )(proj, kcmp, vcmpt, ovt, slopes)


N_WTILES = WINDOW // Q_BLOCK + 1
HALF = GQ_WIDTH // 2
STEP_TILES = N_WTILES


def _attn_sel_kernel(bits_ref, q_ref, ks_ref, vs_ref, kw_ref, vw_ref, selb_ref, oc_ref, gates_ref, slope_ref,
                     wb_ref, o_ref, qt_ref, m_ref, l_ref, acc_ref, u0_ref, u1_ref, list_ref, *, words):
    b = pl.program_id(0)
    g = pl.program_id(1)
    qb = pl.program_id(2)
    nqb = pl.num_programs(2)
    qt_ref[...] = _q_transposed(q_ref)
    m_ref[...] = jnp.full(m_ref.shape, NEG_INF, F32)
    l_ref[...] = jnp.zeros(l_ref.shape, F32)
    acc_ref[...] = jnp.zeros(acc_ref.shape, F32)

    def key_tile(ref, t):
        return ref[0, pl.ds(pl.multiple_of(t * Q_BLOCK, Q_BLOCK), Q_BLOCK), :]

    def plan(t):
        is_win = t == 0
        out = []
        for j in range(STEP_TILES):
            kt = qb - (N_WTILES - 1) + j
            idx = list_ref[jnp.maximum(t - 1, 0) * STEP_TILES + j]
            w_bias = jnp.where(kt < 0, N_WTILES, j)
            s_bias = jnp.where(idx < 0, N_WTILES, jnp.where(idx == qb, N_WTILES - 1, 1))
            out.append((jnp.where(is_win, jnp.maximum(kt, 0), jnp.maximum(idx, 0)),
                        jnp.where(is_win, w_bias, s_bias)))
        return is_win, out

    def scores(t, u_ref):
        is_win, tiles = plan(t)
        for j, (tile, widx) in enumerate(tiles):
            k = jnp.where(is_win, key_tile(kw_ref, tile), key_tile(ks_ref, tile))
            two = jnp.where(is_win, 0.0, selb_ref[0, 0, 0, tile])
            lo = jnp.broadcast_to(jnp.concatenate([two[0:1, :]] * Q_PER_KV, axis=1), (SLC_BLOCK, GQ_WIDTH))
            hi = jnp.broadcast_to(jnp.concatenate([two[1:2, :]] * Q_PER_KV, axis=1), (SLC_BLOCK, GQ_WIDTH))
            s = jnp.dot(k, qt_ref[...], preferred_element_type=F32)
            u_ref[j * Q_BLOCK:(j + 1) * Q_BLOCK, :] = s + wb_ref[0, widx] + jnp.concatenate([lo, hi], axis=0)

    def update(t, u_ref):
        is_win, tiles = plan(t)
        st = jnp.where(is_win, 0, 1)
        v = jnp.concatenate([jnp.where(is_win, key_tile(vw_ref, tile), key_tile(vs_ref, tile))
                             for tile, _ in tiles], axis=0)
        for h in range(2):
            cs = slice(h * HALF, (h + 1) * HALF)
            offs = [slope_ref[0, 0:1, cs] * ((tile - qb) * Q_BLOCK).astype(F32) for tile, _ in tiles]
            mx = None
            for j in range(STEP_TILES):
                cm = jnp.max(u_ref[j * Q_BLOCK:(j + 1) * Q_BLOCK, cs], axis=0, keepdims=True) + offs[j]
                mx = cm if mx is None else jnp.maximum(mx, cm)
            m_old = m_ref[st, 0:1, cs]
            m_new = jnp.maximum(m_old, mx)
            es = [jnp.exp2(u_ref[j * Q_BLOCK:(j + 1) * Q_BLOCK, cs] + (offs[j] - m_new)) for j in range(STEP_TILES)]
            lsum = es[0].sum(axis=0, keepdims=True)
            for j in range(1, STEP_TILES):
                lsum = lsum + es[j].sum(axis=0, keepdims=True)
            e = jnp.concatenate([x.astype(BF16) for x in es], axis=0)
            alpha = jnp.exp2(m_old - m_new)
            l_ref[st, 0:1, cs] = alpha * l_ref[st, 0:1, cs] + lsum
            pv = lax.dot_general(v, e, (((0,), (0,)), ((), ())), preferred_element_type=F32)
            acc_ref[st, :, cs] = alpha * acc_ref[st, :, cs] + pv
            m_ref[st, 0:1, cs] = m_new

    base = ((b * N_KV + g) * nqb + qb) * (words + 1)

    def collect(p, n):
        list_ref[n] = p
        return n + ((bits_ref[base + 1 + (p >> 5)] >> (p & 31)) & 1)

    list_ref[0] = 0
    n_first = jnp.where(qb > 0, bits_ref[base + 1] & 1, 0)
    n_act = lax.fori_loop(jnp.maximum(bits_ref[base], 1), qb, collect, n_first)
    list_ref[n_act] = qb
    for j in range(1, 3 * STEP_TILES):
        list_ref[n_act + j] = -1

    n_updates = 1 + (n_act + STEP_TILES) // STEP_TILES
    scores(0, u0_ref)

    def pair(i, carry):
        scores(2 * i + 1, u1_ref)
        update(2 * i, u0_ref)
        scores(2 * i + 2, u0_ref)
        update(2 * i + 1, u1_ref)
        return carry

    lax.fori_loop(0, n_updates // 2, pair, 0)

    @pl.when(n_updates % 2 == 1)
    def _():
        update(n_updates - 1, u0_ref)

    gt = gates_ref[0, 0, 0]
    ot = (gt[0:1, :] * oc_ref[0, 0, 0].astype(F32) + gt[1:2, :] * (acc_ref[1] / l_ref[1, 0:1, :])
          + gt[2:3, :] * (acc_ref[0] / l_ref[0, 0:1, :]))
    for r in range(Q_PER_KV):
        cs = slice(r * HEAD_DIM, (r + 1) * HEAD_DIM)
        o_ref[:, cs] = ot[:, cs].T.astype(o_ref.dtype)


def _window_bias_tiles():
    h = np.arange(1, N_HEADS + 1, dtype=np.float64)
    slope = (LOG2E * np.exp2(-8.0 * h / N_HEADS)).astype(np.float32).reshape(N_KV, 1, 1, Q_PER_KV, 1)
    sub = np.arange(Q_BLOCK, dtype=np.float32).reshape(1, 1, Q_BLOCK, 1, 1)
    tq = np.arange(Q_BLOCK).reshape(1, 1, 1, 1, Q_BLOCK)
    ks = np.arange(Q_BLOCK).reshape(1, 1, Q_BLOCK, 1, 1)
    mask = np.zeros((1, N_WTILES + 1, Q_BLOCK, 1, Q_BLOCK), np.float32)
    mask[0, 0] = np.where(tq[0, 0] < ks[0, 0], 0.0, NEG_INF)
    mask[0, N_WTILES - 1] = np.where(tq[0, 0] >= ks[0, 0], 0.0, NEG_INF)
    mask[0, N_WTILES] = NEG_INF
    out = (slope * sub).astype(np.float32) + mask
    return jnp.asarray(out.reshape(N_KV, N_WTILES + 1, Q_BLOCK, GQ_WIDTH))


def _attn_sel_call(bits, proj, selb, oc, gates_t, slopes, wbias, *, bsz, seq):
    nqb = seq // Q_BLOCK
    n = bsz * seq
    proj3 = proj.reshape(bsz, seq, proj.shape[1])
    ks_col = (Q_WIDTH + 2 * KV_WIDTH) // HEAD_DIM
    kw_col = (Q_WIDTH + 4 * KV_WIDTH) // HEAD_DIM
    vs_col = ks_col + N_KV
    vw_col = kw_col + N_KV
    selb5 = selb.reshape(bsz, N_KV, nqb, nqb, 2, Q_BLOCK)
    words = bits.shape[0] // (bsz * N_KV * nqb) - 1
    grid_spec = pltpu.PrefetchScalarGridSpec(
        num_scalar_prefetch=1,
        grid=(bsz, N_KV, nqb),
        in_specs=[pl.BlockSpec((Q_BLOCK, GQ_WIDTH), lambda b, g, q, s: (b * nqb + q, g)),
                  pl.BlockSpec((1, seq, HEAD_DIM), lambda b, g, q, s: (b, 0, ks_col + g)),
                  pl.BlockSpec((1, seq, HEAD_DIM), lambda b, g, q, s: (b, 0, vs_col + g)),
                  pl.BlockSpec((1, seq, HEAD_DIM), lambda b, g, q, s: (b, 0, kw_col + g)),
                  pl.BlockSpec((1, seq, HEAD_DIM), lambda b, g, q, s: (b, 0, vw_col + g)),
                  pl.BlockSpec((1, 1, 1, nqb, 2, Q_BLOCK), lambda b, g, q, s: (b, g, q, 0, 0, 0)),
                  pl.BlockSpec((1, 1, 1, HEAD_DIM, GQ_WIDTH), lambda b, g, q, s: (b, g, q, 0, 0)),
                  pl.BlockSpec((1, 1, 1, SUBLANES, GQ_WIDTH), lambda b, g, q, s: (b, g, q, 0, 0)),
                  pl.BlockSpec((1, SUBLANES, GQ_WIDTH), lambda b, g, q, s: (g, 0, 0)),
                  pl.BlockSpec((1, N_WTILES + 1, Q_BLOCK, GQ_WIDTH), lambda b, g, q, s: (g, 0, 0, 0))],
        out_specs=pl.BlockSpec((Q_BLOCK, GQ_WIDTH), lambda b, g, q, s: (b * nqb + q, g)),
        scratch_shapes=[pltpu.VMEM((HEAD_DIM, GQ_WIDTH), BF16),
                        pltpu.VMEM((2, SUBLANES, GQ_WIDTH), F32),
                        pltpu.VMEM((2, SUBLANES, GQ_WIDTH), F32),
                        pltpu.VMEM((2, HEAD_DIM, GQ_WIDTH), F32),
                        pltpu.VMEM((STEP_TILES * Q_BLOCK, GQ_WIDTH), F32),
                        pltpu.VMEM((STEP_TILES * Q_BLOCK, GQ_WIDTH), F32),
                        pltpu.SMEM((nqb + 3 * STEP_TILES,), jnp.int32)],
    )
    return pl.pallas_call(
        functools.partial(_attn_sel_kernel, words=words),
        out_shape=jax.ShapeDtypeStruct((n, Q_WIDTH), BF16),
        grid_spec=grid_spec,
        compiler_params=_cparams(("parallel", "parallel", "arbitrary")),
        name="nsa_selected_window",
    )(bits, proj, proj3, proj3, proj3, proj3, selb5, oc, gates_t, slopes, wbias)


def _active_pair_bits(cnt, *, bsz, seq):
    nqb = seq // Q_BLOCK
    words = (nqb + 31) // 32
    act = (cnt[:, :, :, 0, :] > 0.5).reshape(bsz, N_KV, nqb, nqb, 2).any(axis=-1)
    tile_id = jnp.arange(nqb, dtype=jnp.int32)
    first = jnp.min(jnp.where(act & (tile_id >= 1), tile_id, nqb), axis=-1, keepdims=True)
    act = jnp.pad(act, ((0, 0), (0, 0), (0, 0), (0, words * 32 - nqb))).reshape(bsz, N_KV, nqb, words, 32)
    weights = jnp.left_shift(jnp.uint32(1), jnp.arange(32, dtype=jnp.uint32))
    packed = jnp.sum(act.astype(jnp.uint32) * weights, axis=-1, dtype=jnp.uint32)
    packed = lax.bitcast_convert_type(packed, jnp.int32)
    return jnp.concatenate([first, packed], axis=-1).reshape(-1)


def _outproj_router_kernel(a_ref, x_ref, w_ref, gpost_ref, gate_ref, gpre_ref, sh_ref, sc_ref, wr_ref, wrlo_ref,
                           x2_ref, h2_ref, rt_ref):
    y = jnp.dot(a_ref[...], w_ref[...], preferred_element_type=F32)
    x2 = x_ref[...] + gate_ref[0] * (_rms(y) * gpost_ref[...])
    x2_ref[...] = x2
    h = _rms(x2) * gpre_ref[...]
    h = h * (1.0 + sc_ref[0]) + sh_ref[0]
    hb = h.astype(BF16)
    h2_ref[...] = hb
    h_lo = (h - hb.astype(F32)).astype(BF16)
    logits = (jnp.dot(hb, wr_ref[...], preferred_element_type=F32)
              + jnp.dot(h_lo, wr_ref[...], preferred_element_type=F32)
              + jnp.dot(hb, wrlo_ref[...], preferred_element_type=F32))
    lane = lax.broadcasted_iota(jnp.int32, logits.shape, 1)
    logits = jnp.where(lane < N_EXPERTS, logits, -jnp.inf)
    v1 = jnp.max(logits, axis=-1, keepdims=True)
    i1 = jnp.min(jnp.where(logits == v1, lane, LANES), axis=-1, keepdims=True)
    rest = jnp.where(lane == i1, -jnp.inf, logits)
    v2 = jnp.max(rest, axis=-1, keepdims=True)
    i2 = jnp.min(jnp.where(rest == v2, lane, LANES), axis=-1, keepdims=True)
    e2 = jnp.exp(v2 - v1)
    w1 = 1.0 / (1.0 + e2)
    w2 = e2 / (1.0 + e2)
    out = jnp.where(lane == 0, w1, jnp.where(lane == 1, w2, 0.0))
    out = jnp.where(lane == 2, i1.astype(F32), jnp.where(lane == 3, i2.astype(F32), out))
    rt_ref[...] = out


def _outproj_router_call(a, x, w_out, g_post, gate, g_pre, shift, scale, w_router, *, seq):
    n, d = x.shape
    tm = _tile(seq, 512)
    tpb = seq // tm
    wr32 = jnp.pad(w_router, ((0, 0), (0, LANES - w_router.shape[1])))
    wr = wr32.astype(BF16)
    wr_lo = (wr32 - wr.astype(F32)).astype(BF16)
    row = lambda i: (i, 0)
    fixed = lambda i: (0, 0)
    per_b = lambda i: (i // tpb, 0, 0)
    return pl.pallas_call(
        _outproj_router_kernel,
        out_shape=[jax.ShapeDtypeStruct((n, d), F32),
                   jax.ShapeDtypeStruct((n, d), BF16),
                   jax.ShapeDtypeStruct((n, LANES), F32)],
        grid=(n // tm,),
        in_specs=[pl.BlockSpec((tm, a.shape[1]), row),
                  pl.BlockSpec((tm, d), row),
                  pl.BlockSpec(w_out.shape, fixed),
                  pl.BlockSpec((1, d), fixed),
                  pl.BlockSpec((1, 1, d), per_b),
                  pl.BlockSpec((1, d), fixed),
                  pl.BlockSpec((1, 1, d), per_b),
                  pl.BlockSpec((1, 1, d), per_b),
                  pl.BlockSpec((d, LANES), fixed),
                  pl.BlockSpec((d, LANES), fixed)],
        out_specs=[pl.BlockSpec((tm, d), row), pl.BlockSpec((tm, d), row), pl.BlockSpec((tm, LANES), row)],
        compiler_params=_cparams(("parallel",)),
        name="l1_outproj_router",
    )(a, x, w_out, g_post.reshape(1, d), gate, g_pre.reshape(1, d), shift, scale, wr, wr_lo)


MOE_SUB_ROWS = 256


def _moe_kernel(te_ref, nt_ref, rows_ref, h_ref, wg_ref, wu_ref, wd_ref, o_ref, acc_ref):
    i = pl.program_id(0)
    j = pl.program_id(1)
    tm = h_ref.shape[0]
    n_rows = rows_ref[i]

    @pl.when(j == 0)
    def _():
        acc_ref[...] = jnp.zeros_like(acc_ref)

    def swiglu_rows(rs):
        h = h_ref[rs, :]
        g = jnp.dot(h, wg_ref[0], preferred_element_type=F32)
        u = jnp.dot(h, wu_ref[0], preferred_element_type=F32)
        a = (g * _sigmoid(g) * u).astype(BF16)
        acc_ref[rs, :] += jnp.dot(a, wd_ref[0], preferred_element_type=F32)

    @pl.when(n_rows == tm)
    def _():
        swiglu_rows(slice(None))

    @pl.when((n_rows > 0) & (n_rows < tm))
    def _():
        for sb in range(tm // MOE_SUB_ROWS):
            pl.when(n_rows > sb * MOE_SUB_ROWS)(
                functools.partial(swiglu_rows, slice(sb * MOE_SUB_ROWS, (sb + 1) * MOE_SUB_ROWS)))

    @pl.when(j == pl.num_programs(1) - 1)
    def _():
        o_ref[...] = acc_ref[...].astype(o_ref.dtype)


def _moe_call(tile_expert, n_tiles, tile_rows, hs, wg, wu, wd, *, tm):
    mp, d = hs.shape
    f = wg.shape[2]
    tf = _tile(f, 512)
    nj = f // tf

    def wcol(i, j, nt):
        return jnp.where(i < nt[0], j, nj - 1)

    grid_spec = pltpu.PrefetchScalarGridSpec(
        num_scalar_prefetch=3,
        grid=(mp // tm, f // tf),
        in_specs=[pl.BlockSpec((tm, d), lambda i, j, te, nt, tr: (i, 0)),
                  pl.BlockSpec((1, d, tf), lambda i, j, te, nt, tr: (te[i], 0, wcol(i, j, nt))),
                  pl.BlockSpec((1, d, tf), lambda i, j, te, nt, tr: (te[i], 0, wcol(i, j, nt))),
                  pl.BlockSpec((1, tf, d), lambda i, j, te, nt, tr: (te[i], wcol(i, j, nt), 0))],
        out_specs=pl.BlockSpec((tm, d), lambda i, j, te, nt, tr: (i, 0)),
        scratch_shapes=[pltpu.VMEM((tm, d), F32)],
    )
    return pl.pallas_call(
        _moe_kernel,
        out_shape=jax.ShapeDtypeStruct((mp, d), BF16),
        grid_spec=grid_spec,
        compiler_params=_cparams(("arbitrary", "arbitrary")),
        name="l1_moe_experts",
    )(tile_expert, n_tiles, tile_rows, hs, wg, wu, wd)


def _residual_norm_kernel(x_ref, y0_ref, y1_ref, rt_ref, gpost_ref, gate_ref, o_ref):
    y = rt_ref[:, 0:1] * y0_ref[...].astype(F32) + rt_ref[:, 1:2] * y1_ref[...].astype(F32)
    o_ref[...] = x_ref[...] + gate_ref[0] * (_rms(y) * gpost_ref[...])


def _residual_norm_call(x, y0, y1, rt, g_post, gate, *, seq):
    n, d = x.shape
    tm = _tile(seq, 512)
    tpb = seq // tm
    row = lambda i: (i, 0)
    return pl.pallas_call(
        _residual_norm_kernel,
        out_shape=jax.ShapeDtypeStruct((n, d), F32),
        grid=(n // tm,),
        in_specs=[pl.BlockSpec((tm, d), row), pl.BlockSpec((tm, d), row), pl.BlockSpec((tm, d), row),
                  pl.BlockSpec((tm, LANES), row),
                  pl.BlockSpec((1, d), lambda i: (0, 0)),
                  pl.BlockSpec((1, 1, d), lambda i: (i // tpb, 0, 0))],
        out_specs=pl.BlockSpec((tm, d), row),
        compiler_params=_cparams(("parallel",)),
        name="l1_moe_residual",
    )(x, y0, y1, rt, g_post.reshape(1, d), gate)


CAST_BLOCK_BYTES = 8 * 1024 * 1024
CAST_BLOCK_COLS = 2048


def _cast_kernel(x_ref, o_ref):
    o_ref[...] = x_ref[...].astype(o_ref.dtype)


def _to_bf16(w):
    cols = w.shape[-1]
    rows = w.size // cols
    tc = _tile(cols, CAST_BLOCK_COLS)
    tr = rows
    while tr * tc * 4 > CAST_BLOCK_BYTES and tr % 2 == 0 and (tr // 2) % 16 == 0:
        tr //= 2
    out = pl.pallas_call(
        _cast_kernel,
        out_shape=jax.ShapeDtypeStruct((rows, cols), BF16),
        grid=(rows // tr, cols // tc),
        in_specs=[pl.BlockSpec((tr, tc), lambda i, j: (i, j))],
        out_specs=pl.BlockSpec((tr, tc), lambda i, j: (i, j)),
        compiler_params=_cparams(("parallel", "parallel")),
        name="weight_cast",
    )(w.reshape(rows, cols))
    return out.reshape(w.shape)


def _even_layer(x2d, c, ada_w, ada_b, mix_pre_g, mix_post_g, ffn_pre_g, ffn_post_g,
                w_in, v_ln_g, v_ln_b, w_spatial, b_spatial, conv_w, w_out,
                ffn_w_gate, ffn_w_up, ffn_w_down, *, seq, side_casts=()):
    sh_m, sc_m, gt_m, sh_f, sc_f, gt_f = _ada_call(c, ada_w, ada_b)
    proj = _proj_call(x2d, mix_pre_g, sh_m, sc_m, _to_bf16(w_in), seq=seq, name="l0_in_proj")
    x2d = _mixer0_call(proj, x2d, v_ln_g, v_ln_b, w_spatial, b_spatial, conv_w, _to_bf16(w_out),
                       mix_post_g, gt_m, seq=seq)
    return _ffn_call(x2d, ffn_pre_g, sh_f, sc_f, _to_bf16(ffn_w_gate), _to_bf16(ffn_w_up),
                     _to_bf16(ffn_w_down), ffn_post_g, gt_f, seq=seq, side_casts=side_casts)


def _moe_routing(rt, *, tm):
    n = rt.shape[0]
    na = n * TOP_K
    mp = na + N_EXPERTS * tm
    e_flat = rt[:, TOP_K:2 * TOP_K].astype(jnp.int32).reshape(na)
    onehot = (e_flat[:, None] == jnp.arange(N_EXPERTS, dtype=jnp.int32)[None, :]).astype(jnp.int32)
    csum = jnp.cumsum(onehot, axis=0)
    counts = csum[-1]
    rank = jnp.sum(onehot * csum, axis=1) - 1
    padded = ((counts + tm - 1) // tm) * tm
    ends = jnp.cumsum(padded)
    starts = ends - padded
    dest = jnp.sum(onehot * starts[None, :], axis=1) + rank
    order = jnp.argsort(e_flat, stable=True).astype(jnp.int32)
    rows = jnp.arange(mp, dtype=jnp.int32)
    row_e = jnp.minimum(jnp.sum((rows[:, None] >= ends[None, :]).astype(jnp.int32), axis=1), N_EXPERTS - 1)
    slot = rows - starts[row_e]
    src = (jnp.cumsum(counts) - counts)[row_e] + slot
    row_token = jnp.where(slot < counts[row_e], order[jnp.clip(src, 0, na - 1)] // TOP_K, 0)
    n_tiles = (ends[-1] // tm).astype(jnp.int32)
    tile_ids = jnp.arange(mp // tm, dtype=jnp.int32)
    tile_expert = jnp.sum((tile_ids[:, None] * tm >= ends[None, :]).astype(jnp.int32), axis=1)
    last = jnp.sum(((n_tiles - 1) * tm >= ends).astype(jnp.int32))
    tile_expert = jnp.where(tile_ids < n_tiles, tile_expert, last).astype(jnp.int32)
    tile_rows = jnp.clip(starts[tile_expert] + counts[tile_expert] - tile_ids * tm, 0, tm)
    tile_rows = jnp.where(tile_ids < n_tiles, tile_rows, 0).astype(jnp.int32)
    return row_token, dest.reshape(n, TOP_K), tile_expert, n_tiles.reshape(1), tile_rows


def _alibi_slope_rows():
    h = np.arange(1, N_HEADS + 1, dtype=np.float64)
    s = (LOG2E * np.exp2(-8.0 * h / N_HEADS)).astype(np.float32).reshape(N_KV, 1, Q_PER_KV, 1)
    rows = np.broadcast_to(s, (N_KV, SUBLANES, Q_PER_KV, Q_BLOCK)).reshape(N_KV, SUBLANES, GQ_WIDTH)
    return jnp.asarray(rows)


def _overlap_t(seq):
    ncp = seq // CMP_STRIDE
    nslc = seq // SLC_BLOCK
    cs = np.arange(ncp)[None, :] * CMP_STRIDE
    ss = np.arange(nslc)[:, None] * SLC_BLOCK
    ov = (cs < ss + SLC_BLOCK) & (cs + CMP_BLOCK > ss) & (np.arange(ncp)[None, :] < ncp - 1)
    return jnp.asarray(ov.astype(np.float32)).astype(BF16)


def _odd_layer(x2d, c, ada_w, ada_b, mix_pre_g, mix_post_g, ffn_pre_g, ffn_post_g,
               w_in, cmp_k_pe, cmp_k_w1, cmp_k_w2, cmp_v_pe, cmp_v_w1, cmp_v_w2, w_out,
               router_w, exp_w_gate, exp_w_up, exp_w_down, *, bsz, seq):
    n, d = x2d.shape
    sh_m, sc_m, gt_m, sh_f, sc_f, gt_f = _ada_call(c, ada_w, ada_b)
    nmain = Q_WIDTH + 6 * KV_WIDTH
    w_main = w_in[:, :nmain].astype(BF16)
    w_gates = jnp.pad(w_in[:, nmain:], ((0, 0), (0, LANES - 3 * N_HEADS))).astype(BF16)
    proj, gates = _proj_call(x2d, mix_pre_g, sh_m, sc_m, w_main, w_gates, seq=seq,
                             q_cols=Q_WIDTH, q_scale=LOG2E * HEAD_DIM ** -0.5, name="l1_in_proj")

    nqb = seq // Q_BLOCK
    ncp = seq // CMP_STRIDE

    def kv_cols(k):
        o = Q_WIDTH + k * KV_WIDTH
        return proj[:, o:o + KV_WIDTH]

    def chunked(t):
        t = t.reshape(bsz, ncp, CMP_STRIDE, N_KV, HEAD_DIM).transpose(0, 3, 1, 2, 4)
        return t.reshape(bsz * N_KV, ncp, CMP_STRIDE * HEAD_DIM)

    kcmp, _ = _compress_call(chunked(kv_cols(0)), cmp_k_pe, cmp_k_w1, cmp_k_w2)
    _, vcmpt = _compress_call(chunked(kv_cols(1)), cmp_v_pe, cmp_v_w1, cmp_v_w2)
    slopes = _alibi_slope_rows()
    oc, selb, cnt = _attn_cmp_call(proj, kcmp, vcmpt, _overlap_t(seq), slopes, bsz=bsz, seq=seq)
    bits = _active_pair_bits(cnt, bsz=bsz, seq=seq)

    gt = gates[:, :3 * N_HEADS].reshape(bsz, nqb, Q_BLOCK, 3, N_KV, Q_PER_KV).transpose(0, 4, 1, 3, 5, 2)
    gt = gt.reshape(bsz, N_KV, nqb, 3, GQ_WIDTH)
    gt = jnp.pad(gt, ((0, 0), (0, 0), (0, 0), (0, SUBLANES - 3), (0, 0)))
    attn = _attn_sel_call(bits, proj, selb, oc, gt, slopes, _window_bias_tiles(), bsz=bsz, seq=seq)

    x2, h2, rt = _outproj_router_call(attn, x2d, _to_bf16(w_out), mix_post_g, gt_m,
                                      ffn_pre_g, sh_f, sc_f, router_w, seq=seq)

    tm = min(1024, n)
    row_token, dest, tile_expert, n_tiles, tile_rows = _moe_routing(rt, tm=tm)
    hs = jnp.take(h2, row_token, axis=0)
    as_bf16 = lambda w: w if w.dtype == BF16 else _to_bf16(w)
    rows = _moe_call(tile_expert, n_tiles, tile_rows, hs, as_bf16(exp_w_gate), as_bf16(exp_w_up),
                     as_bf16(exp_w_down), tm=tm)
    y0 = jnp.take(rows, dest[:, 0], axis=0)
    y1 = jnp.take(rows, dest[:, 1], axis=0)
    return _residual_norm_call(x2, y0, y1, rt, ffn_post_g, gt_f, seq=seq)


def kernel(x, c, l0_ada_w, l0_ada_b, l0_mix_pre_g, l0_mix_post_g, l0_ffn_pre_g, l0_ffn_post_g, l0_w_in, l0_v_ln_g, l0_v_ln_b, l0_w_spatial, l0_b_spatial, l0_conv_w, l0_w_out, l0_ffn_w_gate, l0_ffn_w_up, l0_ffn_w_down, l1_ada_w, l1_ada_b, l1_mix_pre_g, l1_mix_post_g, l1_ffn_pre_g, l1_ffn_post_g, l1_w_in, l1_cmp_k_pe, l1_cmp_k_w1, l1_cmp_k_w2, l1_cmp_v_pe, l1_cmp_v_w1, l1_cmp_v_w2, l1_w_out, l1_router_w, l1_exp_w_gate, l1_exp_w_up, l1_exp_w_down):
    bsz, seq, d = x.shape
    x2d = x.reshape(bsz * seq, d)
    x2d, (exp_w_gate, exp_w_up) = _even_layer(
        x2d, c, l0_ada_w, l0_ada_b, l0_mix_pre_g, l0_mix_post_g, l0_ffn_pre_g, l0_ffn_post_g,
        l0_w_in, l0_v_ln_g, l0_v_ln_b, l0_w_spatial, l0_b_spatial, l0_conv_w, l0_w_out,
        l0_ffn_w_gate, l0_ffn_w_up, l0_ffn_w_down, seq=seq, side_casts=(l1_exp_w_gate, l1_exp_w_up))
    x2d = _odd_layer(x2d, c, l1_ada_w, l1_ada_b, l1_mix_pre_g, l1_mix_post_g, l1_ffn_pre_g, l1_ffn_post_g,
                     l1_w_in, l1_cmp_k_pe, l1_cmp_k_w1, l1_cmp_k_w2, l1_cmp_v_pe, l1_cmp_v_w1, l1_cmp_v_w2,
                     l1_w_out, l1_router_w, exp_w_gate, exp_w_up, l1_exp_w_down, bsz=bsz, seq=seq)
    return x2d.reshape(bsz, seq, d)
```

```python
import functools
import math

import numpy as np
import jax
import jax.numpy as jnp
from jax import lax
from jax.experimental import pallas as pl
from jax.experimental.pallas import tpu as pltpu

F32 = jnp.float32
BF16 = jnp.bfloat16
HIGHEST = lax.Precision.HIGHEST

EPS = 1e-6
LOG2E = math.log2(math.e)
NEG_INF = -1e30
FORCE_SCORE = 1e4

LANES = 128
SUBLANES = 8
VMEM_LIMIT_BYTES = 56 * 1024 * 1024
FFN_VMEM_LIMIT_BYTES = 60 * 1024 * 1024

A_GROUPS = 8
A_WIDTH = 1024
B_WIDTH = 1024
CHUNK = 128
CONV_W = 3
N_HEADS = 16
HEAD_DIM = 128
N_KV = 4
Q_PER_KV = N_HEADS // N_KV
Q_WIDTH = N_HEADS * HEAD_DIM
KV_WIDTH = N_KV * HEAD_DIM
GQ_WIDTH = Q_PER_KV * HEAD_DIM
CMP_BLOCK = 32
CMP_STRIDE = 16
CMP_HIDDEN = 256
SLC_BLOCK = 64
N_SELECT = 16
WINDOW = 512
Q_BLOCK = 128
N_EXPERTS = 8
TOP_K = 2


def _cparams(semantics, vmem_limit_bytes=VMEM_LIMIT_BYTES):
    return pltpu.CompilerParams(dimension_semantics=semantics, vmem_limit_bytes=vmem_limit_bytes)


def _sigmoid(x):
    return 1.0 / (1.0 + jnp.exp(-x))


def _gelu_tanh(x):
    return 0.5 * x * (1.0 + jnp.tanh(0.7978845608028654 * (x + 0.044715 * (x * x * x))))


def _rms(x):
    return x * lax.rsqrt(jnp.mean(x * x, axis=-1, keepdims=True) + EPS)


def _tile(n, pref):
    if n <= pref:
        return n
    t = (pref // LANES) * LANES
    while n % t:
        t -= LANES
    assert t > 0, (n, pref)
    return t


def _ada_kernel(c_ref, w_ref, b_ref, o_ref):
    c = c_ref[...]
    s = c * _sigmoid(c)
    o_ref[...] = jnp.dot(s, w_ref[...], preferred_element_type=F32, precision=HIGHEST) + b_ref[...]


def _ada_call(c, w, b):
    bsz, d = c.shape
    n = w.shape[1]
    tn = _tile(n, 1024)
    cp = jnp.pad(c, ((0, SUBLANES - bsz), (0, 0)))
    out = pl.pallas_call(
        _ada_kernel,
        out_shape=jax.ShapeDtypeStruct((SUBLANES, n), F32),
        grid=(n // tn,),
        in_specs=[pl.BlockSpec((SUBLANES, d), lambda j: (0, 0)),
                  pl.BlockSpec((d, tn), lambda j: (0, j)),
                  pl.BlockSpec((1, tn), lambda j: (0, j))],
        out_specs=pl.BlockSpec((SUBLANES, tn), lambda j: (0, j)),
        compiler_params=_cparams(("parallel",)),
        name="ada_modulation",
    )(cp, w, b.reshape(1, n))
    m = out[:bsz]
    return [t.reshape(bsz, 1, d) for t in jnp.split(m, 6, axis=-1)]


def _proj_kernel(*refs, q_tiles, q_scale, with_gates):
    if with_gates:
        x_ref, g_ref, sh_ref, sc_ref, w_ref, wg_ref, o_ref, og_ref, h_ref = refs
    else:
        x_ref, g_ref, sh_ref, sc_ref, w_ref, o_ref, h_ref = refs
    j = pl.program_id(1)

    @pl.when(j == 0)
    def _():
        h = _rms(x_ref[...]) * g_ref[...]
        h = h * (1.0 + sc_ref[0]) + sh_ref[0]
        h_ref[...] = h.astype(BF16)
        if with_gates:
            og_ref[...] = _sigmoid(jnp.dot(h_ref[...], wg_ref[...], preferred_element_type=F32))

    acc = jnp.dot(h_ref[...], w_ref[...], preferred_element_type=F32)
    if q_tiles:
        acc = acc * jnp.where(j < q_tiles, jnp.float32(q_scale), jnp.float32(1.0))
    o_ref[...] = acc.astype(o_ref.dtype)


def _proj_call(x, g, shift, scale, w, wg=None, *, seq, q_cols=0, q_scale=1.0, name):
    n, d = x.shape
    nout = w.shape[1]
    tm = _tile(seq, 1024)
    tn = _tile(nout, 1024)
    assert q_cols % tn == 0
    tpb = seq // tm
    with_gates = wg is not None
    in_specs = [pl.BlockSpec((tm, d), lambda i, j: (i, 0)),
                pl.BlockSpec((1, d), lambda i, j: (0, 0)),
                pl.BlockSpec((1, 1, d), lambda i, j: (i // tpb, 0, 0)),
                pl.BlockSpec((1, 1, d), lambda i, j: (i // tpb, 0, 0)),
                pl.BlockSpec((d, tn), lambda i, j: (0, j))]
    args = [x, g.reshape(1, d), shift, scale, w]
    out_shape = [jax.ShapeDtypeStruct((n, nout), BF16)]
    out_specs = [pl.BlockSpec((tm, tn), lambda i, j: (i, j))]
    if with_gates:
        in_specs.append(pl.BlockSpec((d, LANES), lambda i, j: (0, 0)))
        args.append(wg)
        out_shape.append(jax.ShapeDtypeStruct((n, LANES), F32))
        out_specs.append(pl.BlockSpec((tm, LANES), lambda i, j: (i, 0)))
    out = pl.pallas_call(
        functools.partial(_proj_kernel, q_tiles=q_cols // tn, q_scale=q_scale, with_gates=with_gates),
        out_shape=out_shape,
        grid=(n // tm, nout // tn),
        in_specs=in_specs,
        out_specs=out_specs,
        scratch_shapes=[pltpu.VMEM((tm, d), BF16)],
        compiler_params=_cparams(("parallel", "arbitrary")),
        name=name,
    )(*args)
    return out if with_gates else out[0]


def _mixer0_kernel(p_ref, pgc_ref, pxb_ref, x_ref, vg_ref, vb_ref, wsp_ref, bspt_ref, cw_ref,
                   wout_ref, gpost_ref, gate_ref, o_ref, cat_ref, *, tm, tiles_per_batch):
    i = pl.program_id(0)
    nchunk = tm // CHUNK
    v = _gelu_tanh(p_ref[:, A_WIDTH:2 * A_WIDTH].astype(F32))
    mu = jnp.mean(v, axis=-1, keepdims=True)
    vc = v - mu
    var = jnp.mean(vc * vc, axis=-1, keepdims=True)
    vn = (vc * lax.rsqrt(var + EPS) * vg_ref[...] + vb_ref[...]).astype(BF16)
    row = lax.broadcasted_iota(jnp.int32, (CHUNK, CHUNK), 0)
    col = lax.broadcasted_iota(jnp.int32, (CHUNK, CHUNK), 1)
    causal = col <= row
    for h in range(A_GROUPS):
        cs = slice(h * LANES, (h + 1) * LANES)
        w = jnp.where(causal, wsp_ref[h], 0.0).astype(BF16)
        rhs = jnp.concatenate([vn[c * CHUNK:(c + 1) * CHUNK, cs] for c in range(nchunk)], axis=1)
        z = jnp.dot(w, rhs, preferred_element_type=F32) + bspt_ref[:, h:h + 1]
        for c in range(nchunk):
            rs = slice(c * CHUNK, (c + 1) * CHUNK)
            u = _gelu_tanh(p_ref[rs, cs].astype(F32))
            cat_ref[rs, cs] = (u * z[:, c * CHUNK:(c + 1) * CHUNK]).astype(BF16)
    o_gb, o_gc, o_xb = 2 * A_WIDTH, 2 * A_WIDTH + B_WIDTH, 2 * A_WIDTH + 2 * B_WIDTH
    zc = p_ref[:, o_gc:o_gc + B_WIDTH].astype(F32) * p_ref[:, o_xb:o_xb + B_WIDTH].astype(F32)
    prev = pgc_ref[...].astype(F32) * pxb_ref[...].astype(F32)
    prev = jnp.where(i % tiles_per_batch == 0, 0.0, prev)
    nprev = prev.shape[0]
    p1 = prev[nprev - 1:nprev, :]
    p2 = prev[nprev - 2:nprev - 1, :]
    rowi = lax.broadcasted_iota(jnp.int32, (tm, B_WIDTH), 0)
    r1 = jnp.where(rowi == 0, p1, pltpu.roll(zc, 1, 0))
    r2 = jnp.where(rowi == 0, p2, jnp.where(rowi == 1, p1, pltpu.roll(zc, 2, 0)))
    y = cw_ref[0:1, :] * r2 + cw_ref[1:2, :] * r1 + cw_ref[2:3, :] * zc
    cat_ref[:, A_WIDTH:] = (p_ref[:, o_gb:o_gb + B_WIDTH].astype(F32) * y).astype(BF16)
    yo = jnp.dot(cat_ref[...], wout_ref[...], preferred_element_type=F32)
    o_ref[...] = x_ref[...] + gate_ref[0] * (_rms(yo) * gpost_ref[...])


def _mixer0_call(proj, x, v_ln_g, v_ln_b, w_spatial, b_spatial, conv_w, w_out, g_post, gate, *, seq):
    n, d = x.shape
    tm = _tile(seq, 512)
    tpb = seq // tm
    pr = 16
    rpb = tm // pr
    wcat = A_WIDTH + B_WIDTH
    cwp = jnp.pad(conv_w, ((0, SUBLANES - CONV_W), (0, 0)))
    return pl.pallas_call(
        functools.partial(_mixer0_kernel, tm=tm, tiles_per_batch=tpb),
        out_shape=jax.ShapeDtypeStruct((n, d), F32),
        grid=(n // tm,),
        in_specs=[pl.BlockSpec((tm, proj.shape[1]), lambda i: (i, 0)),
                  pl.BlockSpec((pr, B_WIDTH), lambda i: (jnp.maximum(i * rpb - 1, 0), 3)),
                  pl.BlockSpec((pr, B_WIDTH), lambda i: (jnp.maximum(i * rpb - 1, 0), 4)),
                  pl.BlockSpec((tm, d), lambda i: (i, 0)),
                  pl.BlockSpec((1, A_WIDTH), lambda i: (0, 0)),
                  pl.BlockSpec((1, A_WIDTH), lambda i: (0, 0)),
                  pl.BlockSpec((A_GROUPS, CHUNK, CHUNK), lambda i: (0, 0, 0)),
                  pl.BlockSpec((CHUNK, A_GROUPS), lambda i: (0, 0)),
                  pl.BlockSpec((SUBLANES, B_WIDTH), lambda i: (0, 0)),
                  pl.BlockSpec((wcat, d), lambda i: (0, 0)),
                  pl.BlockSpec((1, d), lambda i: (0, 0)),
                  pl.BlockSpec((1, 1, d), lambda i: (i // tpb, 0, 0))],
        out_specs=pl.BlockSpec((tm, d), lambda i: (i, 0)),
        scratch_shapes=[pltpu.VMEM((tm, wcat), BF16)],
        compiler_params=_cparams(("parallel",)),
        name="l0_mixer",
    )(proj, proj, proj, x, v_ln_g.reshape(1, -1), v_ln_b.reshape(1, -1), w_spatial, b_spatial.T, cwp,
      w_out, g_post.reshape(1, d), gate)


def _ffn_kernel(*refs, n_side):
    x_ref, gpre_ref, sh_ref, sc_ref, wg_ref, wu_ref, wd_ref, gpost_ref, gate_ref = refs[:9]
    side_in = refs[9:9 + n_side]
    o_ref = refs[9 + n_side]
    side_out = refs[10 + n_side:10 + 2 * n_side]
    h_ref = refs[10 + 2 * n_side]
    j = pl.program_id(1)
    for src, dst in zip(side_in, side_out):
        dst[...] = src[...].astype(dst.dtype)

    @pl.when(j == 0)
    def _():
        h = _rms(x_ref[...]) * gpre_ref[...]
        h_ref[...] = (h * (1.0 + sc_ref[0]) + sh_ref[0]).astype(BF16)
        o_ref[...] = jnp.zeros_like(o_ref)

    h = h_ref[...]
    g = jnp.dot(h, wg_ref[...], preferred_element_type=F32)
    u = jnp.dot(h, wu_ref[...], preferred_element_type=F32)
    a = (g * _sigmoid(g) * u).astype(BF16)
    o_ref[...] += jnp.dot(a, wd_ref[...], preferred_element_type=F32)

    @pl.when(j == pl.num_programs(1) - 1)
    def _():
        o_ref[...] = x_ref[...] + gate_ref[0] * (_rms(o_ref[...]) * gpost_ref[...])


def _side_cast_spec(w, gi, gj):
    cols = w.shape[-1]
    rows = w.size // cols
    if rows % gi == 0 and cols % gj == 0 and (rows // gi) % 16 == 0 and (cols // gj) % LANES == 0:
        return w.reshape(rows, cols), pl.BlockSpec((rows // gi, cols // gj), lambda i, j: (i, j))
    if rows % (gi * gj) == 0 and (rows // (gi * gj)) % 16 == 0:
        return w.reshape(rows, cols), pl.BlockSpec((rows // (gi * gj), cols), lambda i, j: (i * gj + j, 0))
    return None


def _ffn_call(x, g_pre, shift, scale, wg, wu, wd, g_post, gate, *, seq, side_casts=()):
    n, d = x.shape
    f = wg.shape[1]
    tm = _tile(seq, 512)
    tf = _tile(f, 1024)
    tpb = seq // tm
    gi, gj = n // tm, f // tf
    specs = [_side_cast_spec(w, gi, gj) for w in side_casts]
    fused = [s for s in specs if s is not None]
    out = pl.pallas_call(
        functools.partial(_ffn_kernel, n_side=len(fused)),
        out_shape=[jax.ShapeDtypeStruct((n, d), F32)] + [jax.ShapeDtypeStruct(w2.shape, BF16) for w2, _ in fused],
        grid=(gi, gj),
        in_specs=[pl.BlockSpec((tm, d), lambda i, j: (i, 0)),
                  pl.BlockSpec((1, d), lambda i, j: (0, 0)),
                  pl.BlockSpec((1, 1, d), lambda i, j: (i // tpb, 0, 0)),
                  pl.BlockSpec((1, 1, d), lambda i, j: (i // tpb, 0, 0)),
                  pl.BlockSpec((d, tf), lambda i, j: (0, j)),
                  pl.BlockSpec((d, tf), lambda i, j: (0, j)),
                  pl.BlockSpec((tf, d), lambda i, j: (j, 0)),
                  pl.BlockSpec((1, d), lambda i, j: (0, 0)),
                  pl.BlockSpec((1, 1, d), lambda i, j: (i // tpb, 0, 0))] + [spec for _, spec in fused],
        out_specs=[pl.BlockSpec((tm, d), lambda i, j: (i, 0))] + [spec for _, spec in fused],
        scratch_shapes=[pltpu.VMEM((tm, d), BF16)],
        compiler_params=_cparams(("arbitrary", "arbitrary"), FFN_VMEM_LIMIT_BYTES),
        name="l0_ffn",
    )(x, g_pre.reshape(1, d), shift, scale, wg, wu, wd, g_post.reshape(1, d), gate, *[w2 for w2, _ in fused])
    fused_out = iter(out[1:])
    casts = [next(fused_out).reshape(w.shape) if s is not None else _to_bf16(w) for w, s in zip(side_casts, specs)]
    return out[0], casts


def _compress_kernel(x_ref, pe_ref, w1_ref, w2_ref, o_ref, ot_ref):
    half = CMP_STRIDE * HEAD_DIM
    x = x_ref[0]
    nc = x.shape[0]
    a = jnp.dot(x, w1_ref[0:half, :], preferred_element_type=F32)
    b = jnp.dot(x, w1_ref[half:2 * half, :], preferred_element_type=F32)
    bias = jnp.dot(pe_ref[...], w1_ref[...], preferred_element_type=F32)[0:1, :]
    hid = _gelu_tanh(a + pltpu.roll(b, nc - 1, 0) + bias)
    out = jnp.dot(hid.astype(BF16), w2_ref[...], preferred_element_type=F32)
    rowi = lax.broadcasted_iota(jnp.int32, out.shape, 0)
    out = jnp.where(rowi < nc - 1, out, 0.0)
    o_ref[0] = out.astype(BF16)
    ot_ref[0] = out.T.astype(BF16)


def _compress_call(xc, pe, w1, w2):
    bg, nc, kdim = xc.shape
    pe_flat = jnp.pad(pe.reshape(1, -1), ((0, SUBLANES - 1), (0, 0))).astype(BF16)
    w1f = w1.reshape(CMP_BLOCK * HEAD_DIM, CMP_HIDDEN).astype(BF16)
    return pl.pallas_call(
        _compress_kernel,
        out_shape=[jax.ShapeDtypeStruct((bg, nc, HEAD_DIM), BF16),
                   jax.ShapeDtypeStruct((bg, HEAD_DIM, nc), BF16)],
        grid=(bg,),
        in_specs=[pl.BlockSpec((1, nc, kdim), lambda i: (i, 0, 0)),
                  pl.BlockSpec((SUBLANES, CMP_BLOCK * HEAD_DIM), lambda i: (0, 0)),
                  pl.BlockSpec((CMP_BLOCK * HEAD_DIM, CMP_HIDDEN), lambda i: (0, 0)),
                  pl.BlockSpec((CMP_HIDDEN, HEAD_DIM), lambda i: (0, 0))],
        out_specs=[pl.BlockSpec((1, nc, HEAD_DIM), lambda i: (i, 0, 0)),
                   pl.BlockSpec((1, HEAD_DIM, nc), lambda i: (i, 0, 0))],
        compiler_params=_cparams(("parallel",)),
        name="nsa_compress",
    )(xc, pe_flat, w1f, w2.astype(BF16))


def _q_transposed(q_ref):
    parts = []
    for r in range(Q_PER_KV):
        parts.append(q_ref[:, r * HEAD_DIM:(r + 1) * HEAD_DIM].astype(F32).T.astype(BF16))
    return jnp.concatenate(parts, axis=1)


def _attn_cmp_kernel(q_ref, kc_ref, vct_ref, ovt_ref, slope_ref, oc_ref, sel_ref, cnt_ref, imp_ref,
                     *, n_sel, n_class):
    qb = pl.program_id(2)
    q0 = qb * Q_BLOCK
    qt = _q_transposed(q_ref)
    ncp = kc_ref.shape[1]
    nslc = ovt_ref.shape[0]
    rows_per_class = ncp // n_class
    visible = (Q_BLOCK // CMP_STRIDE) * (qb + 1) - 1
    cls = jnp.minimum((visible - 1) // rows_per_class, n_class - 1)

    def compressed(n):
        s_all = jnp.dot(kc_ref[0, 0:n, :], qt, preferred_element_type=F32)
        isub = lax.broadcasted_iota(jnp.int32, (n, Q_BLOCK), 0)
        tl = lax.broadcasted_iota(jnp.int32, (n, Q_BLOCK), 1) + q0
        dist = tl - (isub * CMP_STRIDE + (CMP_BLOCK - 1))
        valid = dist >= 0
        distf = dist.astype(F32)
        psum = jnp.zeros((n, Q_BLOCK), F32)
        pparts = []
        for r in range(Q_PER_KV):
            ls = slice(r * Q_BLOCK, (r + 1) * Q_BLOCK)
            s = s_all[:, ls] - slope_ref[0, 0:1, ls] * distf
            s = jnp.where(valid, s, NEG_INF)
            m = jnp.max(s, axis=0, keepdims=True)
            e = jnp.exp2(s - m)
            inv = jnp.where(m > 0.5 * NEG_INF, 1.0 / jnp.sum(e, axis=0, keepdims=True), 0.0)
            p = e * inv
            psum = psum + p
            pparts.append(p.astype(BF16))
        pt = jnp.concatenate(pparts, axis=1)
        oc_ref[0, 0, 0] = jnp.dot(vct_ref[0, :, 0:n], pt,
                                  preferred_element_type=F32).astype(oc_ref.dtype)
        hi = psum.astype(BF16)
        lo = (psum - hi.astype(F32)).astype(BF16)
        ovt = ovt_ref[:, 0:n]
        imp_ref[...] = (jnp.dot(ovt, hi, preferred_element_type=F32)
                        + jnp.dot(ovt, lo, preferred_element_type=F32))

    for k in range(n_class):
        pl.when(cls == k)(functools.partial(compressed, (k + 1) * rows_per_class))

    jb = lax.broadcasted_iota(jnp.int32, (nslc, Q_BLOCK), 0)
    cur = (lax.broadcasted_iota(jnp.int32, (nslc, Q_BLOCK), 1) + q0) // SLC_BLOCK
    forced = (jb == 0) | (jb == cur) | (jb == cur - 1)
    score = jnp.where(forced, FORCE_SCORE, imp_ref[...])
    score = jnp.where(jb <= cur, score, -1.0)
    picked = -3e38
    three_forced = qb > 0
    score = jnp.where(forced & three_forced, picked, score)

    def pick_next(_, work):
        m = jnp.max(work, axis=0, keepdims=True)
        first = jnp.min(jnp.where(work == m, jb, nslc), axis=0, keepdims=True)
        return jnp.where(jb == first, picked, work)

    work = lax.fori_loop(0, jnp.where(three_forced, n_sel - 3, n_sel), pick_next, score)
    chosen = (work == picked) & (jb <= cur)
    sel_ref[0, 0, 0] = jnp.where(chosen, 0.0, NEG_INF)
    flags = jnp.where(chosen, 1.0, 0.0).astype(BF16)
    cnt_ref[0, 0, 0] = lax.dot_general(jnp.ones((SUBLANES, Q_BLOCK), BF16), flags, (((1,), (1,)), ((), ())),
                                       preferred_element_type=F32)


def _attn_cmp_call(proj, kcmp, vcmpt, ovt, slopes, *, bsz, seq):
    nqb = seq // Q_BLOCK
    ncp = kcmp.shape[1]
    nslc = seq // SLC_BLOCK
    n_sel = min(N_SELECT, nslc)
    return pl.pallas_call(
        functools.partial(_attn_cmp_kernel, n_sel=n_sel, n_class=8),
        out_shape=[jax.ShapeDtypeStruct((bsz, N_KV, nqb, HEAD_DIM, GQ_WIDTH), BF16),
                   jax.ShapeDtypeStruct((bsz, N_KV, nqb, nslc, Q_BLOCK), F32),
                   jax.ShapeDtypeStruct((bsz, N_KV, nqb, SUBLANES, nslc), F32)],
        grid=(bsz, N_KV, nqb),
        in_specs=[pl.BlockSpec((Q_BLOCK, GQ_WIDTH), lambda b, g, q: (b * nqb + q, g)),
                  pl.BlockSpec((1, ncp, HEAD_DIM), lambda b, g, q: (b * N_KV + g, 0, 0)),
                  pl.BlockSpec((1, HEAD_DIM, ncp), lambda b, g, q: (b * N_KV + g, 0, 0)),
                  pl.BlockSpec((nslc, ncp), lambda b, g, q: (0, 0)),
                  pl.BlockSpec((1, SUBLANES, GQ_WIDTH), lambda b, g, q: (g, 0, 0))],
        out_specs=[pl.BlockSpec((1, 1, 1, HEAD_DIM, GQ_WIDTH), lambda b, g, q: (b, g, q, 0, 0)),
                   pl.BlockSpec((1, 1, 1, nslc, Q_BLOCK), lambda b, g, q: (b, g, q, 0, 0)),
                   pl.BlockSpec((1, 1, 1, SUBLANES, nslc), lambda b, g, q: (b, g, q, 0, 0))],
        scratch_shapes=[pltpu.VMEM((nslc, Q_BLOCK), F32)],
        compiler_params=_cparams(("parallel", "parallel", "arbitrary")),
        name="nsa_compressed_select",
    )(proj, kcmp, vcmpt, ovt, slopes)


N_WTILES = WINDOW // Q_BLOCK + 1
HALF = GQ_WIDTH // 2
STEP_TILES = N_WTILES


def _attn_sel_kernel(bits_ref, q_ref, ks_ref, vs_ref, kw_ref, vw_ref, selb_ref, oc_ref, gates_ref, slope_ref,
                     wb_ref, o_ref, qt_ref, m_ref, l_ref, acc_ref, u0_ref, u1_ref, list_ref, *, words):
    b = pl.program_id(0)
    g = pl.program_id(1)
    qb = pl.program_id(2)
    nqb = pl.num_programs(2)
    qt_ref[...] = _q_transposed(q_ref)
    m_ref[...] = jnp.full(m_ref.shape, NEG_INF, F32)
    l_ref[...] = jnp.zeros(l_ref.shape, F32)
    acc_ref[...] = jnp.zeros(acc_ref.shape, F32)

    def key_tile(ref, t):
        return ref[0, pl.ds(pl.multiple_of(t * Q_BLOCK, Q_BLOCK), Q_BLOCK), :]

    def plan(t):
        is_win = t == 0
        out = []
        for j in range(STEP_TILES):
            kt = qb - (N_WTILES - 1) + j
            idx = list_ref[jnp.maximum(t - 1, 0) * STEP_TILES + j]
            w_bias = jnp.where(kt < 0, N_WTILES, j)
            s_bias = jnp.where(idx < 0, N_WTILES, jnp.where(idx == qb, N_WTILES - 1, 1))
            out.append((jnp.where(is_win, jnp.maximum(kt, 0), jnp.maximum(idx, 0)),
                        jnp.where(is_win, w_bias, s_bias)))
        return is_win, out

    def scores(t, u_ref):
        is_win, tiles = plan(t)
        for j, (tile, widx) in enumerate(tiles):
            k = jnp.where(is_win, key_tile(kw_ref, tile), key_tile(ks_ref, tile))
            two = jnp.where(is_win, 0.0, selb_ref[0, 0, 0, tile])
            lo = jnp.broadcast_to(jnp.concatenate([two[0:1, :]] * Q_PER_KV, axis=1), (SLC_BLOCK, GQ_WIDTH))
            hi = jnp.broadcast_to(jnp.concatenate([two[1:2, :]] * Q_PER_KV, axis=1), (SLC_BLOCK, GQ_WIDTH))
            s = jnp.dot(k, qt_ref[...], preferred_element_type=F32)
            u_ref[j * Q_BLOCK:(j + 1) * Q_BLOCK, :] = s + wb_ref[0, widx] + jnp.concatenate([lo, hi], axis=0)

    def update(t, u_ref):
        is_win, tiles = plan(t)
        st = jnp.where(is_win, 0, 1)
        v = jnp.concatenate([jnp.where(is_win, key_tile(vw_ref, tile), key_tile(vs_ref, tile))
                             for tile, _ in tiles], axis=0)
        for h in range(2):
            cs = slice(h * HALF, (h + 1) * HALF)
            offs = [slope_ref[0, 0:1, cs] * ((tile - qb) * Q_BLOCK).astype(F32) for tile, _ in tiles]
            mx = None
            for j in range(STEP_TILES):
                cm = jnp.max(u_ref[j * Q_BLOCK:(j + 1) * Q_BLOCK, cs], axis=0, keepdims=True) + offs[j]
                mx = cm if mx is None else jnp.maximum(mx, cm)
            m_old = m_ref[st, 0:1, cs]
            m_new = jnp.maximum(m_old, mx)
            es = [jnp.exp2(u_ref[j * Q_BLOCK:(j + 1) * Q_BLOCK, cs] + (offs[j] - m_new)) for j in range(STEP_TILES)]
            lsum = es[0].sum(axis=0, keepdims=True)
            for j in range(1, STEP_TILES):
                lsum = lsum + es[j].sum(axis=0, keepdims=True)
            e = jnp.concatenate([x.astype(BF16) for x in es], axis=0)
            alpha = jnp.exp2(m_old - m_new)
            l_ref[st, 0:1, cs] = alpha * l_ref[st, 0:1, cs] + lsum
            pv = lax.dot_general(v, e, (((0,), (0,)), ((), ())), preferred_element_type=F32)
            acc_ref[st, :, cs] = alpha * acc_ref[st, :, cs] + pv
            m_ref[st, 0:1, cs] = m_new

    base = ((b * N_KV + g) * nqb + qb) * (words + 1)

    def collect(p, n):
        list_ref[n] = p
        return n + ((bits_ref[base + 1 + (p >> 5)] >> (p & 31)) & 1)

    list_ref[0] = 0
    n_first = jnp.where(qb > 0, bits_ref[base + 1] & 1, 0)
    n_act = lax.fori_loop(jnp.maximum(bits_ref[base], 1), qb, collect, n_first)
    list_ref[n_act] = qb
    for j in range(1, 3 * STEP_TILES):
        list_ref[n_act + j] = -1

    n_updates = 1 + (n_act + STEP_TILES) // STEP_TILES
    scores(0, u0_ref)

    def pair(i, carry):
        scores(2 * i + 1, u1_ref)
        update(2 * i, u0_ref)
        scores(2 * i + 2, u0_ref)
        update(2 * i + 1, u1_ref)
        return carry

    lax.fori_loop(0, n_updates // 2, pair, 0)

    @pl.when(n_updates % 2 == 1)
    def _():
        update(n_updates - 1, u0_ref)

    gt = gates_ref[0, 0, 0]
    ot = (gt[0:1, :] * oc_ref[0, 0, 0].astype(F32) + gt[1:2, :] * (acc_ref[1] / l_ref[1, 0:1, :])
          + gt[2:3, :] * (acc_ref[0] / l_ref[0, 0:1, :]))
    for r in range(Q_PER_KV):
        cs = slice(r * HEAD_DIM, (r + 1) * HEAD_DIM)
        o_ref[:, cs] = ot[:, cs].T.astype(o_ref.dtype)


def _window_bias_tiles():
    h = np.arange(1, N_HEADS + 1, dtype=np.float64)
    slope = (LOG2E * np.exp2(-8.0 * h / N_HEADS)).astype(np.float32).reshape(N_KV, 1, 1, Q_PER_KV, 1)
    sub = np.arange(Q_BLOCK, dtype=np.float32).reshape(1, 1, Q_BLOCK, 1, 1)
    tq = np.arange(Q_BLOCK).reshape(1, 1, 1, 1, Q_BLOCK)
    ks = np.arange(Q_BLOCK).reshape(1, 1, Q_BLOCK, 1, 1)
    mask = np.zeros((1, N_WTILES + 1, Q_BLOCK, 1, Q_BLOCK), np.float32)
    mask[0, 0] = np.where(tq[0, 0] < ks[0, 0], 0.0, NEG_INF)
    mask[0, N_WTILES - 1] = np.where(tq[0, 0] >= ks[0, 0], 0.0, NEG_INF)
    mask[0, N_WTILES] = NEG_INF
    out = (slope * sub).astype(np.float32) + mask
    return jnp.asarray(out.reshape(N_KV, N_WTILES + 1, Q_BLOCK, GQ_WIDTH))


def _attn_sel_call(bits, proj, selb, oc, gates_t, slopes, wbias, *, bsz, seq):
    nqb = seq // Q_BLOCK
    n = bsz * seq
    proj3 = proj.reshape(bsz, seq, proj.shape[1])
    ks_col = (Q_WIDTH + 2 * KV_WIDTH) // HEAD_DIM
    kw_col = (Q_WIDTH + 4 * KV_WIDTH) // HEAD_DIM
    vs_col = ks_col + N_KV
    vw_col = kw_col + N_KV
    selb5 = selb.reshape(bsz, N_KV, nqb, nqb, 2, Q_BLOCK)
    words = bits.shape[0] // (bsz * N_KV * nqb) - 1
    grid_spec = pltpu.PrefetchScalarGridSpec(
        num_scalar_prefetch=1,
        grid=(bsz, N_KV, nqb),
        in_specs=[pl.BlockSpec((Q_BLOCK, GQ_WIDTH), lambda b, g, q, s: (b * nqb + q, g)),
                  pl.BlockSpec((1, seq, HEAD_DIM), lambda b, g, q, s: (b, 0, ks_col + g)),
                  pl.BlockSpec((1, seq, HEAD_DIM), lambda b, g, q, s: (b, 0, vs_col + g)),
                  pl.BlockSpec((1, seq, HEAD_DIM), lambda b, g, q, s: (b, 0, kw_col + g)),
                  pl.BlockSpec((1, seq, HEAD_DIM), lambda b, g, q, s: (b, 0, vw_col + g)),
                  pl.BlockSpec((1, 1, 1, nqb, 2, Q_BLOCK), lambda b, g, q, s: (b, g, q, 0, 0, 0)),
                  pl.BlockSpec((1, 1, 1, HEAD_DIM, GQ_WIDTH), lambda b, g, q, s: (b, g, q, 0, 0)),
                  pl.BlockSpec((1, 1, 1, SUBLANES, GQ_WIDTH), lambda b, g, q, s: (b, g, q, 0, 0)),
                  pl.BlockSpec((1, SUBLANES, GQ_WIDTH), lambda b, g, q, s: (g, 0, 0)),
                  pl.BlockSpec((1, N_WTILES + 1, Q_BLOCK, GQ_WIDTH), lambda b, g, q, s: (g, 0, 0, 0))],
        out_specs=pl.BlockSpec((Q_BLOCK, GQ_WIDTH), lambda b, g, q, s: (b * nqb + q, g)),
        scratch_shapes=[pltpu.VMEM((HEAD_DIM, GQ_WIDTH), BF16),
                        pltpu.VMEM((2, SUBLANES, GQ_WIDTH), F32),
                        pltpu.VMEM((2, SUBLANES, GQ_WIDTH), F32),
                        pltpu.VMEM((2, HEAD_DIM, GQ_WIDTH), F32),
                        pltpu.VMEM((STEP_TILES * Q_BLOCK, GQ_WIDTH), F32),
                        pltpu.VMEM((STEP_TILES * Q_BLOCK, GQ_WIDTH), F32),
                        pltpu.SMEM((nqb + 3 * STEP_TILES,), jnp.int32)],
    )
    return pl.pallas_call(
        functools.partial(_attn_sel_kernel, words=words),
        out_shape=jax.ShapeDtypeStruct((n, Q_WIDTH), BF16),
        grid_spec=grid_spec,
        compiler_params=_cparams(("parallel", "parallel", "arbitrary")),
        name="nsa_selected_window",
    )(bits, proj, proj3, proj3, proj3, proj3, selb5, oc, gates_t, slopes, wbias)


def _active_pair_bits(cnt, *, bsz, seq):
    nqb = seq // Q_BLOCK
    words = (nqb + 31) // 32
    act = (cnt[:, :, :, 0, :] > 0.5).reshape(bsz, N_KV, nqb, nqb, 2).any(axis=-1)
    tile_id = jnp.arange(nqb, dtype=jnp.int32)
    first = jnp.min(jnp.where(act & (tile_id >= 1), tile_id, nqb), axis=-1, keepdims=True)
    act = jnp.pad(act, ((0, 0), (0, 0), (0, 0), (0, words * 32 - nqb))).reshape(bsz, N_KV, nqb, words, 32)
    weights = jnp.left_shift(jnp.uint32(1), jnp.arange(32, dtype=jnp.uint32))
    packed = jnp.sum(act.astype(jnp.uint32) * weights, axis=-1, dtype=jnp.uint32)
    packed = lax.bitcast_convert_type(packed, jnp.int32)
    return jnp.concatenate([first, packed], axis=-1).reshape(-1)


def _outproj_router_kernel(a_ref, x_ref, w_ref, gpost_ref, gate_ref, gpre_ref, sh_ref, sc_ref, wr_ref, wrlo_ref,
                           x2_ref, h2_ref, rt_ref):
    y = jnp.dot(a_ref[...], w_ref[...], preferred_element_type=F32)
    x2 = x_ref[...] + gate_ref[0] * (_rms(y) * gpost_ref[...])
    x2_ref[...] = x2
    h = _rms(x2) * gpre_ref[...]
    h = h * (1.0 + sc_ref[0]) + sh_ref[0]
    hb = h.astype(BF16)
    h2_ref[...] = hb
    h_lo = (h - hb.astype(F32)).astype(BF16)
    logits = (jnp.dot(hb, wr_ref[...], preferred_element_type=F32)
              + jnp.dot(h_lo, wr_ref[...], preferred_element_type=F32)
              + jnp.dot(hb, wrlo_ref[...], preferred_element_type=F32))
    lane = lax.broadcasted_iota(jnp.int32, logits.shape, 1)
    logits = jnp.where(lane < N_EXPERTS, logits, -jnp.inf)
    v1 = jnp.max(logits, axis=-1, keepdims=True)
    i1 = jnp.min(jnp.where(logits == v1, lane, LANES), axis=-1, keepdims=True)
    rest = jnp.where(lane == i1, -jnp.inf, logits)
    v2 = jnp.max(rest, axis=-1, keepdims=True)
    i2 = jnp.min(jnp.where(rest == v2, lane, LANES), axis=-1, keepdims=True)
    e2 = jnp.exp(v2 - v1)
    w1 = 1.0 / (1.0 + e2)
    w2 = e2 / (1.0 + e2)
    out = jnp.where(lane == 0, w1, jnp.where(lane == 1, w2, 0.0))
    out = jnp.where(lane == 2, i1.astype(F32), jnp.where(lane == 3, i2.astype(F32), out))
    rt_ref[...] = out


def _outproj_router_call(a, x, w_out, g_post, gate, g_pre, shift, scale, w_router, *, seq, row_start, n_rows):
    d = x.shape[1]
    n = n_rows
    tm = _tile(seq, 512)
    tpb = seq // tm
    off = row_start // tm
    wr32 = jnp.pad(w_router, ((0, 0), (0, LANES - w_router.shape[1])))
    wr = wr32.astype(BF16)
    wr_lo = (wr32 - wr.astype(F32)).astype(BF16)
    row = lambda i: (i, 0)
    src_row = lambda i: (i + off, 0)
    fixed = lambda i: (0, 0)
    per_b = lambda i: ((i + off) // tpb, 0, 0)
    return pl.pallas_call(
        _outproj_router_kernel,
        out_shape=[jax.ShapeDtypeStruct((n, d), F32),
                   jax.ShapeDtypeStruct((n, d), BF16),
                   jax.ShapeDtypeStruct((n, LANES), F32)],
        grid=(n // tm,),
        in_specs=[pl.BlockSpec((tm, a.shape[1]), src_row),
                  pl.BlockSpec((tm, d), src_row),
                  pl.BlockSpec(w_out.shape, fixed),
                  pl.BlockSpec((1, d), fixed),
                  pl.BlockSpec((1, 1, d), per_b),
                  pl.BlockSpec((1, d), fixed),
                  pl.BlockSpec((1, 1, d), per_b),
                  pl.BlockSpec((1, 1, d), per_b),
                  pl.BlockSpec((d, LANES), fixed),
                  pl.BlockSpec((d, LANES), fixed)],
        out_specs=[pl.BlockSpec((tm, d), row), pl.BlockSpec((tm, d), row), pl.BlockSpec((tm, LANES), row)],
        compiler_params=_cparams(("parallel",)),
        name="l1_outproj_router",
    )(a, x, w_out, g_post.reshape(1, d), gate, g_pre.reshape(1, d), shift, scale, wr, wr_lo)


MOE_SUB_ROWS = 256


def _moe_kernel(te_ref, nt_ref, rows_ref, h_ref, wg_ref, wu_ref, wd_ref, o_ref, acc_ref):
    i = pl.program_id(0)
    j = pl.program_id(1)
    tm = h_ref.shape[0]
    n_rows = rows_ref[i]

    @pl.when(j == 0)
    def _():
        acc_ref[...] = jnp.zeros_like(acc_ref)

    def swiglu_rows(rs):
        h = h_ref[rs, :]
        g = jnp.dot(h, wg_ref[0], preferred_element_type=F32)
        u = jnp.dot(h, wu_ref[0], preferred_element_type=F32)
        a = (g * _sigmoid(g) * u).astype(BF16)
        acc_ref[rs, :] += jnp.dot(a, wd_ref[0], preferred_element_type=F32)

    @pl.when(n_rows == tm)
    def _():
        swiglu_rows(slice(None))

    @pl.when((n_rows > 0) & (n_rows < tm))
    def _():
        for sb in range(tm // MOE_SUB_ROWS):
            pl.when(n_rows > sb * MOE_SUB_ROWS)(
                functools.partial(swiglu_rows, slice(sb * MOE_SUB_ROWS, (sb + 1) * MOE_SUB_ROWS)))

    @pl.when(j == pl.num_programs(1) - 1)
    def _():
        o_ref[...] = acc_ref[...].astype(o_ref.dtype)


def _moe_call(tile_expert, n_tiles, tile_rows, hs, wg, wu, wd, *, tm):
    mp, d = hs.shape
    f = wg.shape[2]
    tf = _tile(f, 512)
    nj = f // tf

    def wcol(i, j, nt):
        return jnp.where(i < nt[0], j, nj - 1)

    grid_spec = pltpu.PrefetchScalarGridSpec(
        num_scalar_prefetch=3,
        grid=(mp // tm, f // tf),
        in_specs=[pl.BlockSpec((tm, d), lambda i, j, te, nt, tr: (i, 0)),
                  pl.BlockSpec((1, d, tf), lambda i, j, te, nt, tr: (te[i], 0, wcol(i, j, nt))),
                  pl.BlockSpec((1, d, tf), lambda i, j, te, nt, tr: (te[i], 0, wcol(i, j, nt))),
                  pl.BlockSpec((1, tf, d), lambda i, j, te, nt, tr: (te[i], wcol(i, j, nt), 0))],
        out_specs=pl.BlockSpec((tm, d), lambda i, j, te, nt, tr: (i, 0)),
        scratch_shapes=[pltpu.VMEM((tm, d), F32)],
    )
    return pl.pallas_call(
        _moe_kernel,
        out_shape=jax.ShapeDtypeStruct((mp, d), BF16),
        grid_spec=grid_spec,
        compiler_params=_cparams(("arbitrary", "arbitrary")),
        name="l1_moe_experts",
    )(tile_expert, n_tiles, tile_rows, hs, wg, wu, wd)


def _residual_norm_kernel(*refs):
    x_ref, y0_ref, y1_ref, rt_ref, gpost_ref, gate_ref, o_ref = refs[-7:]
    y = rt_ref[:, 0:1] * y0_ref[...].astype(F32) + rt_ref[:, 1:2] * y1_ref[...].astype(F32)
    o_ref[...] = x_ref[...] + gate_ref[0] * (_rms(y) * gpost_ref[...])


def _residual_norm_call(x, y0, y1, rt, g_post, gate, *, seq, row_start, total_rows, prev=None):
    n, d = x.shape
    tm = _tile(seq, 512)
    tpb = seq // tm
    off = row_start // tm
    row = lambda i: (i, 0)
    in_specs = [pl.BlockSpec((tm, d), row), pl.BlockSpec((tm, d), row), pl.BlockSpec((tm, d), row),
                pl.BlockSpec((tm, LANES), row),
                pl.BlockSpec((1, d), lambda i: (0, 0)),
                pl.BlockSpec((1, 1, d), lambda i: ((i + off) // tpb, 0, 0))]
    args = [x, y0, y1, rt, g_post.reshape(1, d), gate]
    aliases = {}
    if prev is not None:
        in_specs = [pl.BlockSpec(memory_space=pl.ANY)] + in_specs
        args = [prev] + args
        aliases = {0: 0}
    return pl.pallas_call(
        _residual_norm_kernel,
        out_shape=jax.ShapeDtypeStruct((total_rows, d), F32),
        grid=(n // tm,),
        in_specs=in_specs,
        out_specs=pl.BlockSpec((tm, d), lambda i: (i + off, 0)),
        input_output_aliases=aliases,
        compiler_params=_cparams(("parallel",)),
        name="l1_moe_residual",
    )(*args)


CAST_BLOCK_BYTES = 8 * 1024 * 1024
CAST_BLOCK_COLS = 2048


def _cast_kernel(x_ref, o_ref):
    o_ref[...] = x_ref[...].astype(o_ref.dtype)


def _to_bf16(w):
    cols = w.shape[-1]
    rows = w.size // cols
    tc = _tile(cols, CAST_BLOCK_COLS)
    tr = rows
    while tr * tc * 4 > CAST_BLOCK_BYTES and tr % 2 == 0 and (tr // 2) % 16 == 0:
        tr //= 2
    out = pl.pallas_call(
        _cast_kernel,
        out_shape=jax.ShapeDtypeStruct((rows, cols), BF16),
        grid=(rows // tr, cols // tc),
        in_specs=[pl.BlockSpec((tr, tc), lambda i, j: (i, j))],
        out_specs=pl.BlockSpec((tr, tc), lambda i, j: (i, j)),
        compiler_params=_cparams(("parallel", "parallel")),
        name="weight_cast",
    )(w.reshape(rows, cols))
    return out.reshape(w.shape)


def _even_layer(x2d, c, ada_w, ada_b, mix_pre_g, mix_post_g, ffn_pre_g, ffn_post_g,
                w_in, v_ln_g, v_ln_b, w_spatial, b_spatial, conv_w, w_out,
                ffn_w_gate, ffn_w_up, ffn_w_down, *, seq, side_casts=()):
    sh_m, sc_m, gt_m, sh_f, sc_f, gt_f = _ada_call(c, ada_w, ada_b)
    proj = _proj_call(x2d, mix_pre_g, sh_m, sc_m, _to_bf16(w_in), seq=seq, name="l0_in_proj")
    x2d = _mixer0_call(proj, x2d, v_ln_g, v_ln_b, w_spatial, b_spatial, conv_w, _to_bf16(w_out),
                       mix_post_g, gt_m, seq=seq)
    return _ffn_call(x2d, ffn_pre_g, sh_f, sc_f, _to_bf16(ffn_w_gate), _to_bf16(ffn_w_up),
                     _to_bf16(ffn_w_down), ffn_post_g, gt_f, seq=seq, side_casts=side_casts)


def _moe_routing(rt, *, tm):
    n = rt.shape[0]
    na = n * TOP_K
    mp = na + N_EXPERTS * tm
    e_flat = rt[:, TOP_K:2 * TOP_K].astype(jnp.int32).reshape(na)
    onehot = (e_flat[:, None] == jnp.arange(N_EXPERTS, dtype=jnp.int32)[None, :]).astype(jnp.int32)
    csum = jnp.cumsum(onehot, axis=0)
    counts = csum[-1]
    rank = jnp.sum(onehot * csum, axis=1) - 1
    padded = ((counts + tm - 1) // tm) * tm
    ends = jnp.cumsum(padded)
    starts = ends - padded
    dest = jnp.sum(onehot * starts[None, :], axis=1) + rank
    order = jnp.argsort(e_flat, stable=True).astype(jnp.int32)
    rows = jnp.arange(mp, dtype=jnp.int32)
    row_e = jnp.minimum(jnp.sum((rows[:, None] >= ends[None, :]).astype(jnp.int32), axis=1), N_EXPERTS - 1)
    slot = rows - starts[row_e]
    src = (jnp.cumsum(counts) - counts)[row_e] + slot
    row_token = jnp.where(slot < counts[row_e], order[jnp.clip(src, 0, na - 1)] // TOP_K, 0)
    n_tiles = (ends[-1] // tm).astype(jnp.int32)
    tile_ids = jnp.arange(mp // tm, dtype=jnp.int32)
    tile_expert = jnp.sum((tile_ids[:, None] * tm >= ends[None, :]).astype(jnp.int32), axis=1)
    last = jnp.sum(((n_tiles - 1) * tm >= ends).astype(jnp.int32))
    tile_expert = jnp.where(tile_ids < n_tiles, tile_expert, last).astype(jnp.int32)
    tile_rows = jnp.clip(starts[tile_expert] + counts[tile_expert] - tile_ids * tm, 0, tm)
    tile_rows = jnp.where(tile_ids < n_tiles, tile_rows, 0).astype(jnp.int32)
    return row_token, dest.reshape(n, TOP_K), tile_expert, n_tiles.reshape(1), tile_rows


def _alibi_slope_rows():
    h = np.arange(1, N_HEADS + 1, dtype=np.float64)
    s = (LOG2E * np.exp2(-8.0 * h / N_HEADS)).astype(np.float32).reshape(N_KV, 1, Q_PER_KV, 1)
    rows = np.broadcast_to(s, (N_KV, SUBLANES, Q_PER_KV, Q_BLOCK)).reshape(N_KV, SUBLANES, GQ_WIDTH)
    return jnp.asarray(rows)


def _overlap_t(seq):
    ncp = seq // CMP_STRIDE
    nslc = seq // SLC_BLOCK
    cs = np.arange(ncp)[None, :] * CMP_STRIDE
    ss = np.arange(nslc)[:, None] * SLC_BLOCK
    ov = (cs < ss + SLC_BLOCK) & (cs + CMP_BLOCK > ss) & (np.arange(ncp)[None, :] < ncp - 1)
    return jnp.asarray(ov.astype(np.float32)).astype(BF16)


def _odd_layer(x2d, c, ada_w, ada_b, mix_pre_g, mix_post_g, ffn_pre_g, ffn_post_g,
               w_in, cmp_k_pe, cmp_k_w1, cmp_k_w2, cmp_v_pe, cmp_v_w1, cmp_v_w2, w_out,
               router_w, exp_w_gate, exp_w_up, exp_w_down, *, bsz, seq):
    n, d = x2d.shape
    sh_m, sc_m, gt_m, sh_f, sc_f, gt_f = _ada_call(c, ada_w, ada_b)
    nmain = Q_WIDTH + 6 * KV_WIDTH
    w_main = w_in[:, :nmain].astype(BF16)
    w_gates = jnp.pad(w_in[:, nmain:], ((0, 0), (0, LANES - 3 * N_HEADS))).astype(BF16)
    proj, gates = _proj_call(x2d, mix_pre_g, sh_m, sc_m, w_main, w_gates, seq=seq,
                             q_cols=Q_WIDTH, q_scale=LOG2E * HEAD_DIM ** -0.5, name="l1_in_proj")

    nqb = seq // Q_BLOCK
    ncp = seq // CMP_STRIDE

    def kv_cols(k):
        o = Q_WIDTH + k * KV_WIDTH
        return proj[:, o:o + KV_WIDTH]

    def chunked(t):
        t = t.reshape(bsz, ncp, CMP_STRIDE, N_KV, HEAD_DIM).transpose(0, 3, 1, 2, 4)
        return t.reshape(bsz * N_KV, ncp, CMP_STRIDE * HEAD_DIM)

    kcmp, _ = _compress_call(chunked(kv_cols(0)), cmp_k_pe, cmp_k_w1, cmp_k_w2)
    _, vcmpt = _compress_call(chunked(kv_cols(1)), cmp_v_pe, cmp_v_w1, cmp_v_w2)
    slopes = _alibi_slope_rows()
    oc, selb, cnt = _attn_cmp_call(proj, kcmp, vcmpt, _overlap_t(seq), slopes, bsz=bsz, seq=seq)
    bits = _active_pair_bits(cnt, bsz=bsz, seq=seq)

    gt = gates[:, :3 * N_HEADS].reshape(bsz, nqb, Q_BLOCK, 3, N_KV, Q_PER_KV).transpose(0, 4, 1, 3, 5, 2)
    gt = gt.reshape(bsz, N_KV, nqb, 3, GQ_WIDTH)
    gt = jnp.pad(gt, ((0, 0), (0, 0), (0, 0), (0, SUBLANES - 3), (0, 0)))
    attn = _attn_sel_call(bits, proj, selb, oc, gt, slopes, _window_bias_tiles(), bsz=bsz, seq=seq)

    as_bf16 = lambda w: w if w.dtype == BF16 else _to_bf16(w)
    w_out_b, wg_b, wu_b, wd_b = _to_bf16(w_out), as_bf16(exp_w_gate), as_bf16(exp_w_up), as_bf16(exp_w_down)
    tm = min(1024, seq)
    out = None
    for part in range(bsz):
        x2, h2, rt = _outproj_router_call(attn, x2d, w_out_b, mix_post_g, gt_m, ffn_pre_g, sh_f, sc_f, router_w,
                                          seq=seq, row_start=part * seq, n_rows=seq)
        row_token, dest, tile_expert, n_tiles, tile_rows = _moe_routing(rt, tm=tm)
        hs = jnp.take(h2, row_token, axis=0)
        rows = _moe_call(tile_expert, n_tiles, tile_rows, hs, wg_b, wu_b, wd_b, tm=tm)
        y0 = jnp.take(rows, dest[:, 0], axis=0)
        y1 = jnp.take(rows, dest[:, 1], axis=0)
        out = _residual_norm_call(x2, y0, y1, rt, ffn_post_g, gt_f, seq=seq, row_start=part * seq,
                                  total_rows=n, prev=out)
    return out


def kernel(x, c, l0_ada_w, l0_ada_b, l0_mix_pre_g, l0_mix_post_g, l0_ffn_pre_g, l0_ffn_post_g, l0_w_in, l0_v_ln_g, l0_v_ln_b, l0_w_spatial, l0_b_spatial, l0_conv_w, l0_w_out, l0_ffn_w_gate, l0_ffn_w_up, l0_ffn_w_down, l1_ada_w, l1_ada_b, l1_mix_pre_g, l1_mix_post_g, l1_ffn_pre_g, l1_ffn_post_g, l1_w_in, l1_cmp_k_pe, l1_cmp_k_w1, l1_cmp_k_w2, l1_cmp_v_pe, l1_cmp_v_w1, l1_cmp_v_w2, l1_w_out, l1_router_w, l1_exp_w_gate, l1_exp_w_up, l1_exp_w_down):
    bsz, seq, d = x.shape
    x2d = x.reshape(bsz * seq, d)
    x2d, (exp_w_gate, exp_w_up) = _even_layer(
        x2d, c, l0_ada_w, l0_ada_b, l0_mix_pre_g, l0_mix_post_g, l0_ffn_pre_g, l0_ffn_post_g,
        l0_w_in, l0_v_ln_g, l0_v_ln_b, l0_w_spatial, l0_b_spatial, l0_conv_w, l0_w_out,
        l0_ffn_w_gate, l0_ffn_w_up, l0_ffn_w_down, seq=seq, side_casts=(l1_exp_w_gate, l1_exp_w_up))
    x2d = _odd_layer(x2d, c, l1_ada_w, l1_ada_b, l1_mix_pre_g, l1_mix_post_g, l1_ffn_pre_g, l1_ffn_post_g,
                     l1_w_in, l1_cmp_k_pe, l1_cmp_k_w1, l1_cmp_k_w2, l1_cmp_v_pe, l1_cmp_v_w1, l1_cmp_v_w2,
                     l1_w_out, l1_router_w, exp_w_gate, exp_w_up, l1_exp_w_down, bsz=bsz, seq=seq)
    return x2d.reshape(bsz, seq, d)
```

```python
import functools
import math

import numpy as np
import jax
import jax.numpy as jnp
from jax import lax
from jax.experimental import pallas as pl
from jax.experimental.pallas import tpu as pltpu

F32 = jnp.float32
BF16 = jnp.bfloat16
HIGHEST = lax.Precision.HIGHEST

EPS = 1e-6
LOG2E = math.log2(math.e)
NEG_INF = -1e30
FORCE_SCORE = 1e4

LANES = 128
SUBLANES = 8
VMEM_LIMIT_BYTES = 56 * 1024 * 1024
FFN_VMEM_LIMIT_BYTES = 60 * 1024 * 1024

A_GROUPS = 8
A_WIDTH = 1024
B_WIDTH = 1024
CHUNK = 128
CONV_W = 3
N_HEADS = 16
HEAD_DIM = 128
N_KV = 4
Q_PER_KV = N_HEADS // N_KV
Q_WIDTH = N_HEADS * HEAD_DIM
KV_WIDTH = N_KV * HEAD_DIM
GQ_WIDTH = Q_PER_KV * HEAD_DIM
CMP_BLOCK = 32
CMP_STRIDE = 16
CMP_HIDDEN = 256
SLC_BLOCK = 64
N_SELECT = 16
WINDOW = 512
Q_BLOCK = 128
N_EXPERTS = 8
TOP_K = 2


def _cparams(semantics, vmem_limit_bytes=VMEM_LIMIT_BYTES):
    return pltpu.CompilerParams(dimension_semantics=semantics, vmem_limit_bytes=vmem_limit_bytes)


def _sigmoid(x):
    return 1.0 / (1.0 + jnp.exp(-x))


def _gelu_tanh(x):
    return 0.5 * x * (1.0 + jnp.tanh(0.7978845608028654 * (x + 0.044715 * (x * x * x))))


def _rms(x):
    return x * lax.rsqrt(jnp.mean(x * x, axis=-1, keepdims=True) + EPS)


def _tile(n, pref):
    if n <= pref:
        return n
    t = (pref // LANES) * LANES
    while n % t:
        t -= LANES
    assert t > 0, (n, pref)
    return t


def _ada_kernel(c_ref, w_ref, b_ref, o_ref):
    c = c_ref[...]
    s = c * _sigmoid(c)
    o_ref[...] = jnp.dot(s, w_ref[...], preferred_element_type=F32, precision=HIGHEST) + b_ref[...]


def _ada_call(c, w, b):
    bsz, d = c.shape
    n = w.shape[1]
    tn = _tile(n, 1024)
    cp = jnp.pad(c, ((0, SUBLANES - bsz), (0, 0)))
    out = pl.pallas_call(
        _ada_kernel,
        out_shape=jax.ShapeDtypeStruct((SUBLANES, n), F32),
        grid=(n // tn,),
        in_specs=[pl.BlockSpec((SUBLANES, d), lambda j: (0, 0)),
                  pl.BlockSpec((d, tn), lambda j: (0, j)),
                  pl.BlockSpec((1, tn), lambda j: (0, j))],
        out_specs=pl.BlockSpec((SUBLANES, tn), lambda j: (0, j)),
        compiler_params=_cparams(("parallel",)),
        name="ada_modulation",
    )(cp, w, b.reshape(1, n))
    m = out[:bsz]
    return [t.reshape(bsz, 1, d) for t in jnp.split(m, 6, axis=-1)]


def _proj_kernel(*refs, q_tiles, q_scale, with_gates):
    if with_gates:
        x_ref, g_ref, sh_ref, sc_ref, w_ref, wg_ref, o_ref, og_ref, h_ref = refs
    else:
        x_ref, g_ref, sh_ref, sc_ref, w_ref, o_ref, h_ref = refs
    j = pl.program_id(1)

    @pl.when(j == 0)
    def _():
        h = _rms(x_ref[...]) * g_ref[...]
        h = h * (1.0 + sc_ref[0]) + sh_ref[0]
        h_ref[...] = h.astype(BF16)
        if with_gates:
            og_ref[...] = _sigmoid(jnp.dot(h_ref[...], wg_ref[...], preferred_element_type=F32))

    acc = jnp.dot(h_ref[...], w_ref[...], preferred_element_type=F32)
    if q_tiles:
        acc = acc * jnp.where(j < q_tiles, jnp.float32(q_scale), jnp.float32(1.0))
    o_ref[...] = acc.astype(o_ref.dtype)


def _proj_call(x, g, shift, scale, w, wg=None, *, seq, q_cols=0, q_scale=1.0, name):
    n, d = x.shape
    nout = w.shape[1]
    tm = _tile(seq, 1024)
    tn = _tile(nout, 1024)
    assert q_cols % tn == 0
    tpb = seq // tm
    with_gates = wg is not None
    in_specs = [pl.BlockSpec((tm, d), lambda i, j: (i, 0)),
                pl.BlockSpec((1, d), lambda i, j: (0, 0)),
                pl.BlockSpec((1, 1, d), lambda i, j: (i // tpb, 0, 0)),
                pl.BlockSpec((1, 1, d), lambda i, j: (i // tpb, 0, 0)),
                pl.BlockSpec((d, tn), lambda i, j: (0, j))]
    args = [x, g.reshape(1, d), shift, scale, w]
    out_shape = [jax.ShapeDtypeStruct((n, nout), BF16)]
    out_specs = [pl.BlockSpec((tm, tn), lambda i, j: (i, j))]
    if with_gates:
        in_specs.append(pl.BlockSpec((d, LANES), lambda i, j: (0, 0)))
        args.append(wg)
        out_shape.append(jax.ShapeDtypeStruct((n, LANES), F32))
        out_specs.append(pl.BlockSpec((tm, LANES), lambda i, j: (i, 0)))
    out = pl.pallas_call(
        functools.partial(_proj_kernel, q_tiles=q_cols // tn, q_scale=q_scale, with_gates=with_gates),
        out_shape=out_shape,
        grid=(n // tm, nout // tn),
        in_specs=in_specs,
        out_specs=out_specs,
        scratch_shapes=[pltpu.VMEM((tm, d), BF16)],
        compiler_params=_cparams(("parallel", "arbitrary")),
        name=name,
    )(*args)
    return out if with_gates else out[0]


def _mixer0_kernel(p_ref, pgc_ref, pxb_ref, x_ref, vg_ref, vb_ref, wsp_ref, bspt_ref, cw_ref,
                   wout_ref, gpost_ref, gate_ref, o_ref, cat_ref, *, tm, tiles_per_batch):
    i = pl.program_id(0)
    nchunk = tm // CHUNK
    v = _gelu_tanh(p_ref[:, A_WIDTH:2 * A_WIDTH].astype(F32))
    mu = jnp.mean(v, axis=-1, keepdims=True)
    vc = v - mu
    var = jnp.mean(vc * vc, axis=-1, keepdims=True)
    vn = (vc * lax.rsqrt(var + EPS) * vg_ref[...] + vb_ref[...]).astype(BF16)
    row = lax.broadcasted_iota(jnp.int32, (CHUNK, CHUNK), 0)
    col = lax.broadcasted_iota(jnp.int32, (CHUNK, CHUNK), 1)
    causal = col <= row
    for h in range(A_GROUPS):
        cs = slice(h * LANES, (h + 1) * LANES)
        w = jnp.where(causal, wsp_ref[h], 0.0).astype(BF16)
        rhs = jnp.concatenate([vn[c * CHUNK:(c + 1) * CHUNK, cs] for c in range(nchunk)], axis=1)
        z = jnp.dot(w, rhs, preferred_element_type=F32) + bspt_ref[:, h:h + 1]
        for c in range(nchunk):
            rs = slice(c * CHUNK, (c + 1) * CHUNK)
            u = _gelu_tanh(p_ref[rs, cs].astype(F32))
            cat_ref[rs, cs] = (u * z[:, c * CHUNK:(c + 1) * CHUNK]).astype(BF16)
    o_gb, o_gc, o_xb = 2 * A_WIDTH, 2 * A_WIDTH + B_WIDTH, 2 * A_WIDTH + 2 * B_WIDTH
    zc = p_ref[:, o_gc:o_gc + B_WIDTH].astype(F32) * p_ref[:, o_xb:o_xb + B_WIDTH].astype(F32)
    prev = pgc_ref[...].astype(F32) * pxb_ref[...].astype(F32)
    prev = jnp.where(i % tiles_per_batch == 0, 0.0, prev)
    nprev = prev.shape[0]
    p1 = prev[nprev - 1:nprev, :]
    p2 = prev[nprev - 2:nprev - 1, :]
    rowi = lax.broadcasted_iota(jnp.int32, (tm, B_WIDTH), 0)
    r1 = jnp.where(rowi == 0, p1, pltpu.roll(zc, 1, 0))
    r2 = jnp.where(rowi == 0, p2, jnp.where(rowi == 1, p1, pltpu.roll(zc, 2, 0)))
    y = cw_ref[0:1, :] * r2 + cw_ref[1:2, :] * r1 + cw_ref[2:3, :] * zc
    cat_ref[:, A_WIDTH:] = (p_ref[:, o_gb:o_gb + B_WIDTH].astype(F32) * y).astype(BF16)
    yo = jnp.dot(cat_ref[...], wout_ref[...], preferred_element_type=F32)
    o_ref[...] = x_ref[...] + gate_ref[0] * (_rms(yo) * gpost_ref[...])


def _mixer0_call(proj, x, v_ln_g, v_ln_b, w_spatial, b_spatial, conv_w, w_out, g_post, gate, *, seq):
    n, d = x.shape
    tm = _tile(seq, 512)
    tpb = seq // tm
    pr = 16
    rpb = tm // pr
    wcat = A_WIDTH + B_WIDTH
    cwp = jnp.pad(conv_w, ((0, SUBLANES - CONV_W), (0, 0)))
    return pl.pallas_call(
        functools.partial(_mixer0_kernel, tm=tm, tiles_per_batch=tpb),
        out_shape=jax.ShapeDtypeStruct((n, d), F32),
        grid=(n // tm,),
        in_specs=[pl.BlockSpec((tm, proj.shape[1]), lambda i: (i, 0)),
                  pl.BlockSpec((pr, B_WIDTH), lambda i: (jnp.maximum(i * rpb - 1, 0), 3)),
                  pl.BlockSpec((pr, B_WIDTH), lambda i: (jnp.maximum(i * rpb - 1, 0), 4)),
                  pl.BlockSpec((tm, d), lambda i: (i, 0)),
                  pl.BlockSpec((1, A_WIDTH), lambda i: (0, 0)),
                  pl.BlockSpec((1, A_WIDTH), lambda i: (0, 0)),
                  pl.BlockSpec((A_GROUPS, CHUNK, CHUNK), lambda i: (0, 0, 0)),
                  pl.BlockSpec((CHUNK, A_GROUPS), lambda i: (0, 0)),
                  pl.BlockSpec((SUBLANES, B_WIDTH), lambda i: (0, 0)),
                  pl.BlockSpec((wcat, d), lambda i: (0, 0)),
                  pl.BlockSpec((1, d), lambda i: (0, 0)),
                  pl.BlockSpec((1, 1, d), lambda i: (i // tpb, 0, 0))],
        out_specs=pl.BlockSpec((tm, d), lambda i: (i, 0)),
        scratch_shapes=[pltpu.VMEM((tm, wcat), BF16)],
        compiler_params=_cparams(("parallel",)),
        name="l0_mixer",
    )(proj, proj, proj, x, v_ln_g.reshape(1, -1), v_ln_b.reshape(1, -1), w_spatial, b_spatial.T, cwp,
      w_out, g_post.reshape(1, d), gate)


def _ffn_kernel(*refs, n_side):
    x_ref, gpre_ref, sh_ref, sc_ref, wg_ref, wu_ref, wd_ref, gpost_ref, gate_ref = refs[:9]
    side_in = refs[9:9 + n_side]
    o_ref = refs[9 + n_side]
    side_out = refs[10 + n_side:10 + 2 * n_side]
    h_ref = refs[10 + 2 * n_side]
    j = pl.program_id(1)
    for src, dst in zip(side_in, side_out):
        dst[...] = src[...].astype(dst.dtype)

    @pl.when(j == 0)
    def _():
        h = _rms(x_ref[...]) * gpre_ref[...]
        h_ref[...] = (h * (1.0 + sc_ref[0]) + sh_ref[0]).astype(BF16)
        o_ref[...] = jnp.zeros_like(o_ref)

    h = h_ref[...]
    g = jnp.dot(h, wg_ref[...], preferred_element_type=F32)
    u = jnp.dot(h, wu_ref[...], preferred_element_type=F32)
    a = (g * _sigmoid(g) * u).astype(BF16)
    o_ref[...] += jnp.dot(a, wd_ref[...], preferred_element_type=F32)

    @pl.when(j == pl.num_programs(1) - 1)
    def _():
        o_ref[...] = x_ref[...] + gate_ref[0] * (_rms(o_ref[...]) * gpost_ref[...])


def _side_cast_spec(w, gi, gj):
    cols = w.shape[-1]
    rows = w.size // cols
    if rows % gi == 0 and cols % gj == 0 and (rows // gi) % 16 == 0 and (cols // gj) % LANES == 0:
        return w.reshape(rows, cols), pl.BlockSpec((rows // gi, cols // gj), lambda i, j: (i, j))
    if rows % (gi * gj) == 0 and (rows // (gi * gj)) % 16 == 0:
        return w.reshape(rows, cols), pl.BlockSpec((rows // (gi * gj), cols), lambda i, j: (i * gj + j, 0))
    return None


def _ffn_call(x, g_pre, shift, scale, wg, wu, wd, g_post, gate, *, seq, side_casts=()):
    n, d = x.shape
    f = wg.shape[1]
    tm = _tile(seq, 512)
    tf = _tile(f, 1024)
    tpb = seq // tm
    gi, gj = n // tm, f // tf
    specs = [_side_cast_spec(w, gi, gj) for w in side_casts]
    fused = [s for s in specs if s is not None]
    out = pl.pallas_call(
        functools.partial(_ffn_kernel, n_side=len(fused)),
        out_shape=[jax.ShapeDtypeStruct((n, d), F32)] + [jax.ShapeDtypeStruct(w2.shape, BF16) for w2, _ in fused],
        grid=(gi, gj),
        in_specs=[pl.BlockSpec((tm, d), lambda i, j: (i, 0)),
                  pl.BlockSpec((1, d), lambda i, j: (0, 0)),
                  pl.BlockSpec((1, 1, d), lambda i, j: (i // tpb, 0, 0)),
                  pl.BlockSpec((1, 1, d), lambda i, j: (i // tpb, 0, 0)),
                  pl.BlockSpec((d, tf), lambda i, j: (0, j)),
                  pl.BlockSpec((d, tf), lambda i, j: (0, j)),
                  pl.BlockSpec((tf, d), lambda i, j: (j, 0)),
                  pl.BlockSpec((1, d), lambda i, j: (0, 0)),
                  pl.BlockSpec((1, 1, d), lambda i, j: (i // tpb, 0, 0))] + [spec for _, spec in fused],
        out_specs=[pl.BlockSpec((tm, d), lambda i, j: (i, 0))] + [spec for _, spec in fused],
        scratch_shapes=[pltpu.VMEM((tm, d), BF16)],
        compiler_params=_cparams(("arbitrary", "arbitrary"), FFN_VMEM_LIMIT_BYTES),
        name="l0_ffn",
    )(x, g_pre.reshape(1, d), shift, scale, wg, wu, wd, g_post.reshape(1, d), gate, *[w2 for w2, _ in fused])
    fused_out = iter(out[1:])
    casts = [next(fused_out).reshape(w.shape) if s is not None else _to_bf16(w) for w, s in zip(side_casts, specs)]
    return out[0], casts


def _compress_kernel(x_ref, pe_ref, w1_ref, w2_ref, o_ref, ot_ref):
    half = CMP_STRIDE * HEAD_DIM
    x = x_ref[0]
    nc = x.shape[0]
    a = jnp.dot(x, w1_ref[0:half, :], preferred_element_type=F32)
    b = jnp.dot(x, w1_ref[half:2 * half, :], preferred_element_type=F32)
    bias = jnp.dot(pe_ref[...], w1_ref[...], preferred_element_type=F32)[0:1, :]
    hid = _gelu_tanh(a + pltpu.roll(b, nc - 1, 0) + bias)
    out = jnp.dot(hid.astype(BF16), w2_ref[...], preferred_element_type=F32)
    rowi = lax.broadcasted_iota(jnp.int32, out.shape, 0)
    out = jnp.where(rowi < nc - 1, out, 0.0)
    o_ref[0] = out.astype(BF16)
    ot_ref[0] = out.T.astype(BF16)


def _compress_call(xc, pe, w1, w2):
    bg, nc, kdim = xc.shape
    pe_flat = jnp.pad(pe.reshape(1, -1), ((0, SUBLANES - 1), (0, 0))).astype(BF16)
    w1f = w1.reshape(CMP_BLOCK * HEAD_DIM, CMP_HIDDEN).astype(BF16)
    return pl.pallas_call(
        _compress_kernel,
        out_shape=[jax.ShapeDtypeStruct((bg, nc, HEAD_DIM), BF16),
                   jax.ShapeDtypeStruct((bg, HEAD_DIM, nc), BF16)],
        grid=(bg,),
        in_specs=[pl.BlockSpec((1, nc, kdim), lambda i: (i, 0, 0)),
                  pl.BlockSpec((SUBLANES, CMP_BLOCK * HEAD_DIM), lambda i: (0, 0)),
                  pl.BlockSpec((CMP_BLOCK * HEAD_DIM, CMP_HIDDEN), lambda i: (0, 0)),
                  pl.BlockSpec((CMP_HIDDEN, HEAD_DIM), lambda i: (0, 0))],
        out_specs=[pl.BlockSpec((1, nc, HEAD_DIM), lambda i: (i, 0, 0)),
                   pl.BlockSpec((1, HEAD_DIM, nc), lambda i: (i, 0, 0))],
        compiler_params=_cparams(("parallel",)),
        name="nsa_compress",
    )(xc, pe_flat, w1f, w2.astype(BF16))


def _q_transposed(q_ref):
    parts = []
    for r in range(Q_PER_KV):
        parts.append(q_ref[:, r * HEAD_DIM:(r + 1) * HEAD_DIM].astype(F32).T.astype(BF16))
    return jnp.concatenate(parts, axis=1)


def _attn_cmp_kernel(q_ref, kc_ref, vct_ref, ovt_ref, slope_ref, oc_ref, sel_ref, cnt_ref, imp_ref,
                     *, n_sel, n_class):
    qb = pl.program_id(2)
    q0 = qb * Q_BLOCK
    qt = _q_transposed(q_ref)
    ncp = kc_ref.shape[1]
    nslc = ovt_ref.shape[0]
    rows_per_class = ncp // n_class
    visible = (Q_BLOCK // CMP_STRIDE) * (qb + 1) - 1
    cls = jnp.minimum((visible - 1) // rows_per_class, n_class - 1)

    def compressed(n):
        s_all = jnp.dot(kc_ref[0, 0:n, :], qt, preferred_element_type=F32)
        isub = lax.broadcasted_iota(jnp.int32, (n, Q_BLOCK), 0)
        tl = lax.broadcasted_iota(jnp.int32, (n, Q_BLOCK), 1) + q0
        dist = tl - (isub * CMP_STRIDE + (CMP_BLOCK - 1))
        valid = dist >= 0
        distf = dist.astype(F32)
        psum = jnp.zeros((n, Q_BLOCK), F32)
        pparts = []
        for r in range(Q_PER_KV):
            ls = slice(r * Q_BLOCK, (r + 1) * Q_BLOCK)
            s = s_all[:, ls] - slope_ref[0, 0:1, ls] * distf
            s = jnp.where(valid, s, NEG_INF)
            m = jnp.max(s, axis=0, keepdims=True)
            e = jnp.exp2(s - m)
            inv = jnp.where(m > 0.5 * NEG_INF, 1.0 / jnp.sum(e, axis=0, keepdims=True), 0.0)
            p = e * inv
            psum = psum + p
            pparts.append(p.astype(BF16))
        pt = jnp.concatenate(pparts, axis=1)
        oc_ref[0, 0, 0] = jnp.dot(vct_ref[0, :, 0:n], pt,
                                  preferred_element_type=F32).astype(oc_ref.dtype)
        hi = psum.astype(BF16)
        lo = (psum - hi.astype(F32)).astype(BF16)
        ovt = ovt_ref[:, 0:n]
        imp_ref[...] = (jnp.dot(ovt, hi, preferred_element_type=F32)
                        + jnp.dot(ovt, lo, preferred_element_type=F32))

    for k in range(n_class):
        pl.when(cls == k)(functools.partial(compressed, (k + 1) * rows_per_class))

    jb = lax.broadcasted_iota(jnp.int32, (nslc, Q_BLOCK), 0)
    cur = (lax.broadcasted_iota(jnp.int32, (nslc, Q_BLOCK), 1) + q0) // SLC_BLOCK
    forced = (jb == 0) | (jb == cur) | (jb == cur - 1)
    score = jnp.where(forced, FORCE_SCORE, imp_ref[...])
    score = jnp.where(jb <= cur, score, -1.0)
    picked = -3e38
    three_forced = qb > 0
    score = jnp.where(forced & three_forced, picked, score)

    def pick_next(_, work):
        m = jnp.max(work, axis=0, keepdims=True)
        first = jnp.min(jnp.where(work == m, jb, nslc), axis=0, keepdims=True)
        return jnp.where(jb == first, picked, work)

    work = lax.fori_loop(0, jnp.where(three_forced, n_sel - 3, n_sel), pick_next, score)
    chosen = (work == picked) & (jb <= cur)
    sel_ref[0, 0, 0] = jnp.where(chosen, 0.0, NEG_INF)
    flags = jnp.where(chosen, 1.0, 0.0).astype(BF16)
    cnt_ref[0, 0, 0] = lax.dot_general(jnp.ones((SUBLANES, Q_BLOCK), BF16), flags, (((1,), (1,)), ((), ())),
                                       preferred_element_type=F32)


def _attn_cmp_call(proj, kcmp, vcmpt, ovt, slopes, *, bsz, seq):
    nqb = seq // Q_BLOCK
    ncp = kcmp.shape[1]
    nslc = seq // SLC_BLOCK
    n_sel = min(N_SELECT, nslc)
    return pl.pallas_call(
        functools.partial(_attn_cmp_kernel, n_sel=n_sel, n_class=8),
        out_shape=[jax.ShapeDtypeStruct((bsz, N_KV, nqb, HEAD_DIM, GQ_WIDTH), BF16),
                   jax.ShapeDtypeStruct((bsz, N_KV, nqb, nslc, Q_BLOCK), F32),
                   jax.ShapeDtypeStruct((bsz, N_KV, nqb, SUBLANES, nslc), F32)],
        grid=(bsz, N_KV, nqb),
        in_specs=[pl.BlockSpec((Q_BLOCK, GQ_WIDTH), lambda b, g, q: (b * nqb + q, g)),
                  pl.BlockSpec((1, ncp, HEAD_DIM), lambda b, g, q: (b * N_KV + g, 0, 0)),
                  pl.BlockSpec((1, HEAD_DIM, ncp), lambda b, g, q: (b * N_KV + g, 0, 0)),
                  pl.BlockSpec((nslc, ncp), lambda b, g, q: (0, 0)),
                  pl.BlockSpec((1, SUBLANES, GQ_WIDTH), lambda b, g, q: (g, 0, 0))],
        out_specs=[pl.BlockSpec((1, 1, 1, HEAD_DIM, GQ_WIDTH), lambda b, g, q: (b, g, q, 0, 0)),
                   pl.BlockSpec((1, 1, 1, nslc, Q_BLOCK), lambda b, g, q: (b, g, q, 0, 0)),
                   pl.BlockSpec((1, 1, 1, SUBLANES, nslc), lambda b, g, q: (b, g, q, 0, 0))],
        scratch_shapes=[pltpu.VMEM((nslc, Q_BLOCK), F32)],
        compiler_params=_cparams(("parallel", "parallel", "arbitrary")),
        name="nsa_compressed_select",
    )(proj, kcmp, vcmpt, ovt, slopes)


N_WTILES = WINDOW // Q_BLOCK + 1
HALF = GQ_WIDTH // 2
STEP_TILES = N_WTILES


def _attn_sel_kernel(bits_ref, q_ref, ks_ref, vs_ref, kw_ref, vw_ref, selb_ref, oc_ref, gates_ref, slope_ref,
                     wb_ref, o_ref, qt_ref, m_ref, l_ref, acc_ref, u0_ref, u1_ref, list_ref, *, words):
    b = pl.program_id(0)
    g = pl.program_id(1)
    qb = pl.program_id(2)
    nqb = pl.num_programs(2)
    qt_ref[...] = _q_transposed(q_ref)
    m_ref[...] = jnp.full(m_ref.shape, NEG_INF, F32)
    l_ref[...] = jnp.zeros(l_ref.shape, F32)
    acc_ref[...] = jnp.zeros(acc_ref.shape, F32)

    def key_tile(ref, t):
        return ref[0, pl.ds(pl.multiple_of(t * Q_BLOCK, Q_BLOCK), Q_BLOCK), :]

    def plan(t):
        is_win = t == 0
        out = []
        for j in range(STEP_TILES):
            kt = qb - (N_WTILES - 1) + j
            idx = list_ref[jnp.maximum(t - 1, 0) * STEP_TILES + j]
            w_bias = jnp.where(kt < 0, N_WTILES, j)
            s_bias = jnp.where(idx < 0, N_WTILES, jnp.where(idx == qb, N_WTILES - 1, 1))
            out.append((jnp.where(is_win, jnp.maximum(kt, 0), jnp.maximum(idx, 0)),
                        jnp.where(is_win, w_bias, s_bias)))
        return is_win, out

    def window_scores(u_ref):
        for j in range(STEP_TILES):
            kt = qb - (N_WTILES - 1) + j
            s = jnp.dot(key_tile(kw_ref, jnp.maximum(kt, 0)), qt_ref[...], preferred_element_type=F32)
            u_ref[j * Q_BLOCK:(j + 1) * Q_BLOCK, :] = s + wb_ref[0, jnp.where(kt < 0, N_WTILES, j)]

    def scores(t, u_ref):
        _, tiles = plan(t)
        for j, (tile, widx) in enumerate(tiles):
            k = key_tile(ks_ref, tile)
            two = selb_ref[0, 0, 0, tile]
            lo = jnp.broadcast_to(jnp.concatenate([two[0:1, :]] * Q_PER_KV, axis=1), (SLC_BLOCK, GQ_WIDTH))
            hi = jnp.broadcast_to(jnp.concatenate([two[1:2, :]] * Q_PER_KV, axis=1), (SLC_BLOCK, GQ_WIDTH))
            s = jnp.dot(k, qt_ref[...], preferred_element_type=F32)
            u_ref[j * Q_BLOCK:(j + 1) * Q_BLOCK, :] = s + wb_ref[0, widx] + jnp.concatenate([lo, hi], axis=0)

    def update(t, u_ref):
        is_win, tiles = plan(t)
        st = jnp.where(is_win, 0, 1)
        v = jnp.concatenate([jnp.where(is_win, key_tile(vw_ref, tile), key_tile(vs_ref, tile))
                             for tile, _ in tiles], axis=0)
        for h in range(2):
            cs = slice(h * HALF, (h + 1) * HALF)
            offs = [slope_ref[0, 0:1, cs] * ((tile - qb) * Q_BLOCK).astype(F32) for tile, _ in tiles]
            mx = None
            for j in range(STEP_TILES):
                cm = jnp.max(u_ref[j * Q_BLOCK:(j + 1) * Q_BLOCK, cs], axis=0, keepdims=True) + offs[j]
                mx = cm if mx is None else jnp.maximum(mx, cm)
            m_old = m_ref[st, 0:1, cs]
            m_new = jnp.maximum(m_old, mx)
            es = [jnp.exp2(u_ref[j * Q_BLOCK:(j + 1) * Q_BLOCK, cs] + (offs[j] - m_new)) for j in range(STEP_TILES)]
            lsum = es[0].sum(axis=0, keepdims=True)
            for j in range(1, STEP_TILES):
                lsum = lsum + es[j].sum(axis=0, keepdims=True)
            e = jnp.concatenate([x.astype(BF16) for x in es], axis=0)
            alpha = jnp.exp2(m_old - m_new)
            l_ref[st, 0:1, cs] = alpha * l_ref[st, 0:1, cs] + lsum
            pv = lax.dot_general(v, e, (((0,), (0,)), ((), ())), preferred_element_type=F32)
            acc_ref[st, :, cs] = alpha * acc_ref[st, :, cs] + pv
            m_ref[st, 0:1, cs] = m_new

    window_scores(u0_ref)

    base = ((b * N_KV + g) * nqb + qb) * (words + 1)

    def collect(p, n):
        list_ref[n] = p
        return n + ((bits_ref[base + 1 + (p >> 5)] >> (p & 31)) & 1)

    list_ref[0] = 0
    n_first = jnp.where(qb > 0, bits_ref[base + 1] & 1, 0)
    n_act = lax.fori_loop(jnp.maximum(bits_ref[base], 1), qb, collect, n_first)
    list_ref[n_act] = qb
    for j in range(1, 3 * STEP_TILES):
        list_ref[n_act + j] = -1

    n_updates = 1 + (n_act + STEP_TILES) // STEP_TILES

    def pair(i, carry):
        scores(2 * i + 1, u1_ref)
        update(2 * i, u0_ref)
        scores(2 * i + 2, u0_ref)
        update(2 * i + 1, u1_ref)
        return carry

    lax.fori_loop(0, n_updates // 2, pair, 0)

    @pl.when(n_updates % 2 == 1)
    def _():
        update(n_updates - 1, u0_ref)

    gt = gates_ref[0, 0, 0]
    ot = (gt[0:1, :] * oc_ref[0, 0, 0].astype(F32) + gt[1:2, :] * (acc_ref[1] / l_ref[1, 0:1, :])
          + gt[2:3, :] * (acc_ref[0] / l_ref[0, 0:1, :]))
    for r in range(Q_PER_KV):
        cs = slice(r * HEAD_DIM, (r + 1) * HEAD_DIM)
        o_ref[:, cs] = ot[:, cs].T.astype(o_ref.dtype)


def _window_bias_tiles():
    h = np.arange(1, N_HEADS + 1, dtype=np.float64)
    slope = (LOG2E * np.exp2(-8.0 * h / N_HEADS)).astype(np.float32).reshape(N_KV, 1, 1, Q_PER_KV, 1)
    sub = np.arange(Q_BLOCK, dtype=np.float32).reshape(1, 1, Q_BLOCK, 1, 1)
    tq = np.arange(Q_BLOCK).reshape(1, 1, 1, 1, Q_BLOCK)
    ks = np.arange(Q_BLOCK).reshape(1, 1, Q_BLOCK, 1, 1)
    mask = np.zeros((1, N_WTILES + 1, Q_BLOCK, 1, Q_BLOCK), np.float32)
    mask[0, 0] = np.where(tq[0, 0] < ks[0, 0], 0.0, NEG_INF)
    mask[0, N_WTILES - 1] = np.where(tq[0, 0] >= ks[0, 0], 0.0, NEG_INF)
    mask[0, N_WTILES] = NEG_INF
    out = (slope * sub).astype(np.float32) + mask
    return jnp.asarray(out.reshape(N_KV, N_WTILES + 1, Q_BLOCK, GQ_WIDTH))


def _attn_sel_call(bits, proj, selb, oc, gates_t, slopes, wbias, *, bsz, seq):
    nqb = seq // Q_BLOCK
    n = bsz * seq
    proj3 = proj.reshape(bsz, seq, proj.shape[1])
    ks_col = (Q_WIDTH + 2 * KV_WIDTH) // HEAD_DIM
    kw_col = (Q_WIDTH + 4 * KV_WIDTH) // HEAD_DIM
    vs_col = ks_col + N_KV
    vw_col = kw_col + N_KV
    selb5 = selb.reshape(bsz, N_KV, nqb, nqb, 2, Q_BLOCK)
    words = bits.shape[0] // (bsz * N_KV * nqb) - 1
    grid_spec = pltpu.PrefetchScalarGridSpec(
        num_scalar_prefetch=1,
        grid=(bsz, N_KV, nqb),
        in_specs=[pl.BlockSpec((Q_BLOCK, GQ_WIDTH), lambda b, g, q, s: (b * nqb + q, g)),
                  pl.BlockSpec((1, seq, HEAD_DIM), lambda b, g, q, s: (b, 0, ks_col + g)),
                  pl.BlockSpec((1, seq, HEAD_DIM), lambda b, g, q, s: (b, 0, vs_col + g)),
                  pl.BlockSpec((1, seq, HEAD_DIM), lambda b, g, q, s: (b, 0, kw_col + g)),
                  pl.BlockSpec((1, seq, HEAD_DIM), lambda b, g, q, s: (b, 0, vw_col + g)),
                  pl.BlockSpec((1, 1, 1, nqb, 2, Q_BLOCK), lambda b, g, q, s: (b, g, q, 0, 0, 0)),
                  pl.BlockSpec((1, 1, 1, HEAD_DIM, GQ_WIDTH), lambda b, g, q, s: (b, g, q, 0, 0)),
                  pl.BlockSpec((1, 1, 1, SUBLANES, GQ_WIDTH), lambda b, g, q, s: (b, g, q, 0, 0)),
                  pl.BlockSpec((1, SUBLANES, GQ_WIDTH), lambda b, g, q, s: (g, 0, 0)),
                  pl.BlockSpec((1, N_WTILES + 1, Q_BLOCK, GQ_WIDTH), lambda b, g, q, s: (g, 0, 0, 0))],
        out_specs=pl.BlockSpec((Q_BLOCK, GQ_WIDTH), lambda b, g, q, s: (b * nqb + q, g)),
        scratch_shapes=[pltpu.VMEM((HEAD_DIM, GQ_WIDTH), BF16),
                        pltpu.VMEM((2, SUBLANES, GQ_WIDTH), F32),
                        pltpu.VMEM((2, SUBLANES, GQ_WIDTH), F32),
                        pltpu.VMEM((2, HEAD_DIM, GQ_WIDTH), F32),
                        pltpu.VMEM((STEP_TILES * Q_BLOCK, GQ_WIDTH), F32),
                        pltpu.VMEM((STEP_TILES * Q_BLOCK, GQ_WIDTH), F32),
                        pltpu.SMEM((nqb + 3 * STEP_TILES,), jnp.int32)],
    )
    return pl.pallas_call(
        functools.partial(_attn_sel_kernel, words=words),
        out_shape=jax.ShapeDtypeStruct((n, Q_WIDTH), BF16),
        grid_spec=grid_spec,
        compiler_params=_cparams(("parallel", "parallel", "arbitrary")),
        name="nsa_selected_window",
    )(bits, proj, proj3, proj3, proj3, proj3, selb5, oc, gates_t, slopes, wbias)


def _active_pair_bits(cnt, *, bsz, seq):
    nqb = seq // Q_BLOCK
    words = (nqb + 31) // 32
    act = (cnt[:, :, :, 0, :] > 0.5).reshape(bsz, N_KV, nqb, nqb, 2).any(axis=-1)
    tile_id = jnp.arange(nqb, dtype=jnp.int32)
    first = jnp.min(jnp.where(act & (tile_id >= 1), tile_id, nqb), axis=-1, keepdims=True)
    act = jnp.pad(act, ((0, 0), (0, 0), (0, 0), (0, words * 32 - nqb))).reshape(bsz, N_KV, nqb, words, 32)
    weights = jnp.left_shift(jnp.uint32(1), jnp.arange(32, dtype=jnp.uint32))
    packed = jnp.sum(act.astype(jnp.uint32) * weights, axis=-1, dtype=jnp.uint32)
    packed = lax.bitcast_convert_type(packed, jnp.int32)
    return jnp.concatenate([first, packed], axis=-1).reshape(-1)


def _outproj_router_kernel(a_ref, x_ref, w_ref, gpost_ref, gate_ref, gpre_ref, sh_ref, sc_ref, wr_ref, wrlo_ref,
                           x2_ref, h2_ref, rt_ref):
    y = jnp.dot(a_ref[...], w_ref[...], preferred_element_type=F32)
    x2 = x_ref[...] + gate_ref[0] * (_rms(y) * gpost_ref[...])
    x2_ref[...] = x2
    h = _rms(x2) * gpre_ref[...]
    h = h * (1.0 + sc_ref[0]) + sh_ref[0]
    hb = h.astype(BF16)
    h2_ref[...] = hb
    h_lo = (h - hb.astype(F32)).astype(BF16)
    logits = (jnp.dot(hb, wr_ref[...], preferred_element_type=F32)
              + jnp.dot(h_lo, wr_ref[...], preferred_element_type=F32)
              + jnp.dot(hb, wrlo_ref[...], preferred_element_type=F32))
    lane = lax.broadcasted_iota(jnp.int32, logits.shape, 1)
    logits = jnp.where(lane < N_EXPERTS, logits, -jnp.inf)
    v1 = jnp.max(logits, axis=-1, keepdims=True)
    i1 = jnp.min(jnp.where(logits == v1, lane, LANES), axis=-1, keepdims=True)
    rest = jnp.where(lane == i1, -jnp.inf, logits)
    v2 = jnp.max(rest, axis=-1, keepdims=True)
    i2 = jnp.min(jnp.where(rest == v2, lane, LANES), axis=-1, keepdims=True)
    e2 = jnp.exp(v2 - v1)
    w1 = 1.0 / (1.0 + e2)
    w2 = e2 / (1.0 + e2)
    out = jnp.where(lane == 0, w1, jnp.where(lane == 1, w2, 0.0))
    out = jnp.where(lane == 2, i1.astype(F32), jnp.where(lane == 3, i2.astype(F32), out))
    rt_ref[...] = out


def _outproj_router_call(a, x, w_out, g_post, gate, g_pre, shift, scale, w_router, *, seq):
    n, d = x.shape
    tm = _tile(seq, 512)
    tpb = seq // tm
    wr32 = jnp.pad(w_router, ((0, 0), (0, LANES - w_router.shape[1])))
    wr = wr32.astype(BF16)
    wr_lo = (wr32 - wr.astype(F32)).astype(BF16)
    row = lambda i: (i, 0)
    fixed = lambda i: (0, 0)
    per_b = lambda i: (i // tpb, 0, 0)
    return pl.pallas_call(
        _outproj_router_kernel,
        out_shape=[jax.ShapeDtypeStruct((n, d), F32),
                   jax.ShapeDtypeStruct((n, d), BF16),
                   jax.ShapeDtypeStruct((n, LANES), F32)],
        grid=(n // tm,),
        in_specs=[pl.BlockSpec((tm, a.shape[1]), row),
                  pl.BlockSpec((tm, d), row),
                  pl.BlockSpec(w_out.shape, fixed),
                  pl.BlockSpec((1, d), fixed),
                  pl.BlockSpec((1, 1, d), per_b),
                  pl.BlockSpec((1, d), fixed),
                  pl.BlockSpec((1, 1, d), per_b),
                  pl.BlockSpec((1, 1, d), per_b),
                  pl.BlockSpec((d, LANES), fixed),
                  pl.BlockSpec((d, LANES), fixed)],
        out_specs=[pl.BlockSpec((tm, d), row), pl.BlockSpec((tm, d), row), pl.BlockSpec((tm, LANES), row)],
        compiler_params=_cparams(("parallel",)),
        name="l1_outproj_router",
    )(a, x, w_out, g_post.reshape(1, d), gate, g_pre.reshape(1, d), shift, scale, wr, wr_lo)


MOE_SUB_ROWS = 256


def _moe_kernel(te_ref, nt_ref, rows_ref, h_ref, wg_ref, wu_ref, wd_ref, o_ref, acc_ref):
    i = pl.program_id(0)
    j = pl.program_id(1)
    tm = h_ref.shape[0]
    n_rows = rows_ref[i]

    @pl.when(j == 0)
    def _():
        acc_ref[...] = jnp.zeros_like(acc_ref)

    def swiglu_rows(rs):
        h = h_ref[rs, :]
        g = jnp.dot(h, wg_ref[0], preferred_element_type=F32)
        u = jnp.dot(h, wu_ref[0], preferred_element_type=F32)
        a = (g * _sigmoid(g) * u).astype(BF16)
        acc_ref[rs, :] += jnp.dot(a, wd_ref[0], preferred_element_type=F32)

    @pl.when(n_rows == tm)
    def _():
        swiglu_rows(slice(None))

    @pl.when((n_rows > 0) & (n_rows < tm))
    def _():
        for sb in range(tm // MOE_SUB_ROWS):
            pl.when(n_rows > sb * MOE_SUB_ROWS)(
                functools.partial(swiglu_rows, slice(sb * MOE_SUB_ROWS, (sb + 1) * MOE_SUB_ROWS)))

    @pl.when(j == pl.num_programs(1) - 1)
    def _():
        o_ref[...] = acc_ref[...].astype(o_ref.dtype)


def _moe_call(tile_expert, n_tiles, tile_rows, hs, wg, wu, wd, *, tm):
    mp, d = hs.shape
    f = wg.shape[2]
    tf = _tile(f, 512)
    nj = f // tf

    def wcol(i, j, nt):
        return jnp.where(i < nt[0], j, nj - 1)

    grid_spec = pltpu.PrefetchScalarGridSpec(
        num_scalar_prefetch=3,
        grid=(mp // tm, f // tf),
        in_specs=[pl.BlockSpec((tm, d), lambda i, j, te, nt, tr: (i, 0)),
                  pl.BlockSpec((1, d, tf), lambda i, j, te, nt, tr: (te[i], 0, wcol(i, j, nt))),
                  pl.BlockSpec((1, d, tf), lambda i, j, te, nt, tr: (te[i], 0, wcol(i, j, nt))),
                  pl.BlockSpec((1, tf, d), lambda i, j, te, nt, tr: (te[i], wcol(i, j, nt), 0))],
        out_specs=pl.BlockSpec((tm, d), lambda i, j, te, nt, tr: (i, 0)),
        scratch_shapes=[pltpu.VMEM((tm, d), F32)],
    )
    return pl.pallas_call(
        _moe_kernel,
        out_shape=jax.ShapeDtypeStruct((mp, d), BF16),
        grid_spec=grid_spec,
        compiler_params=_cparams(("arbitrary", "arbitrary")),
        name="l1_moe_experts",
    )(tile_expert, n_tiles, tile_rows, hs, wg, wu, wd)


def _residual_norm_kernel(x_ref, y0_ref, y1_ref, rt_ref, gpost_ref, gate_ref, o_ref):
    y = rt_ref[:, 0:1] * y0_ref[...].astype(F32) + rt_ref[:, 1:2] * y1_ref[...].astype(F32)
    o_ref[...] = x_ref[...] + gate_ref[0] * (_rms(y) * gpost_ref[...])


def _residual_norm_call(x, y0, y1, rt, g_post, gate, *, seq):
    n, d = x.shape
    tm = _tile(seq, 512)
    tpb = seq // tm
    row = lambda i: (i, 0)
    return pl.pallas_call(
        _residual_norm_kernel,
        out_shape=jax.ShapeDtypeStruct((n, d), F32),
        grid=(n // tm,),
        in_specs=[pl.BlockSpec((tm, d), row), pl.BlockSpec((tm, d), row), pl.BlockSpec((tm, d), row),
                  pl.BlockSpec((tm, LANES), row),
                  pl.BlockSpec((1, d), lambda i: (0, 0)),
                  pl.BlockSpec((1, 1, d), lambda i: (i // tpb, 0, 0))],
        out_specs=pl.BlockSpec((tm, d), row),
        compiler_params=_cparams(("parallel",)),
        name="l1_moe_residual",
    )(x, y0, y1, rt, g_post.reshape(1, d), gate)


CAST_BLOCK_BYTES = 8 * 1024 * 1024
CAST_BLOCK_COLS = 2048


def _cast_kernel(x_ref, o_ref):
    o_ref[...] = x_ref[...].astype(o_ref.dtype)


def _to_bf16(w):
    cols = w.shape[-1]
    rows = w.size // cols
    tc = _tile(cols, CAST_BLOCK_COLS)
    tr = rows
    while tr * tc * 4 > CAST_BLOCK_BYTES and tr % 2 == 0 and (tr // 2) % 16 == 0:
        tr //= 2
    out = pl.pallas_call(
        _cast_kernel,
        out_shape=jax.ShapeDtypeStruct((rows, cols), BF16),
        grid=(rows // tr, cols // tc),
        in_specs=[pl.BlockSpec((tr, tc), lambda i, j: (i, j))],
        out_specs=pl.BlockSpec((tr, tc), lambda i, j: (i, j)),
        compiler_params=_cparams(("parallel", "parallel")),
        name="weight_cast",
    )(w.reshape(rows, cols))
    return out.reshape(w.shape)


def _even_layer(x2d, c, ada_w, ada_b, mix_pre_g, mix_post_g, ffn_pre_g, ffn_post_g,
                w_in, v_ln_g, v_ln_b, w_spatial, b_spatial, conv_w, w_out,
                ffn_w_gate, ffn_w_up, ffn_w_down, *, seq, side_casts=()):
    sh_m, sc_m, gt_m, sh_f, sc_f, gt_f = _ada_call(c, ada_w, ada_b)
    proj = _proj_call(x2d, mix_pre_g, sh_m, sc_m, _to_bf16(w_in), seq=seq, name="l0_in_proj")
    x2d = _mixer0_call(proj, x2d, v_ln_g, v_ln_b, w_spatial, b_spatial, conv_w, _to_bf16(w_out),
                       mix_post_g, gt_m, seq=seq)
    return _ffn_call(x2d, ffn_pre_g, sh_f, sc_f, _to_bf16(ffn_w_gate), _to_bf16(ffn_w_up),
                     _to_bf16(ffn_w_down), ffn_post_g, gt_f, seq=seq, side_casts=side_casts)


def _moe_routing(rt, *, tm):
    n = rt.shape[0]
    na = n * TOP_K
    mp = na + N_EXPERTS * tm
    e_flat = rt[:, TOP_K:2 * TOP_K].astype(jnp.int32).reshape(na)
    onehot = (e_flat[:, None] == jnp.arange(N_EXPERTS, dtype=jnp.int32)[None, :]).astype(jnp.int32)
    csum = jnp.cumsum(onehot, axis=0)
    counts = csum[-1]
    rank = jnp.sum(onehot * csum, axis=1) - 1
    padded = ((counts + tm - 1) // tm) * tm
    ends = jnp.cumsum(padded)
    starts = ends - padded
    dest = jnp.sum(onehot * starts[None, :], axis=1) + rank
    order = jnp.argsort(e_flat, stable=True).astype(jnp.int32)
    rows = jnp.arange(mp, dtype=jnp.int32)
    row_e = jnp.minimum(jnp.sum((rows[:, None] >= ends[None, :]).astype(jnp.int32), axis=1), N_EXPERTS - 1)
    slot = rows - starts[row_e]
    src = (jnp.cumsum(counts) - counts)[row_e] + slot
    row_token = jnp.where(slot < counts[row_e], order[jnp.clip(src, 0, na - 1)] // TOP_K, 0)
    n_tiles = (ends[-1] // tm).astype(jnp.int32)
    tile_ids = jnp.arange(mp // tm, dtype=jnp.int32)
    tile_expert = jnp.sum((tile_ids[:, None] * tm >= ends[None, :]).astype(jnp.int32), axis=1)
    last = jnp.sum(((n_tiles - 1) * tm >= ends).astype(jnp.int32))
    tile_expert = jnp.where(tile_ids < n_tiles, tile_expert, last).astype(jnp.int32)
    tile_rows = jnp.clip(starts[tile_expert] + counts[tile_expert] - tile_ids * tm, 0, tm)
    tile_rows = jnp.where(tile_ids < n_tiles, tile_rows, 0).astype(jnp.int32)
    return row_token, dest.reshape(n, TOP_K), tile_expert, n_tiles.reshape(1), tile_rows


def _alibi_slope_rows():
    h = np.arange(1, N_HEADS + 1, dtype=np.float64)
    s = (LOG2E * np.exp2(-8.0 * h / N_HEADS)).astype(np.float32).reshape(N_KV, 1, Q_PER_KV, 1)
    rows = np.broadcast_to(s, (N_KV, SUBLANES, Q_PER_KV, Q_BLOCK)).reshape(N_KV, SUBLANES, GQ_WIDTH)
    return jnp.asarray(rows)


def _overlap_t(seq):
    ncp = seq // CMP_STRIDE
    nslc = seq // SLC_BLOCK
    cs = np.arange(ncp)[None, :] * CMP_STRIDE
    ss = np.arange(nslc)[:, None] * SLC_BLOCK
    ov = (cs < ss + SLC_BLOCK) & (cs + CMP_BLOCK > ss) & (np.arange(ncp)[None, :] < ncp - 1)
    return jnp.asarray(ov.astype(np.float32)).astype(BF16)


def _odd_layer(x2d, c, ada_w, ada_b, mix_pre_g, mix_post_g, ffn_pre_g, ffn_post_g,
               w_in, cmp_k_pe, cmp_k_w1, cmp_k_w2, cmp_v_pe, cmp_v_w1, cmp_v_w2, w_out,
               router_w, exp_w_gate, exp_w_up, exp_w_down, *, bsz, seq):
    n, d = x2d.shape
    sh_m, sc_m, gt_m, sh_f, sc_f, gt_f = _ada_call(c, ada_w, ada_b)
    nmain = Q_WIDTH + 6 * KV_WIDTH
    w_main = w_in[:, :nmain].astype(BF16)
    w_gates = jnp.pad(w_in[:, nmain:], ((0, 0), (0, LANES - 3 * N_HEADS))).astype(BF16)
    proj, gates = _proj_call(x2d, mix_pre_g, sh_m, sc_m, w_main, w_gates, seq=seq,
                             q_cols=Q_WIDTH, q_scale=LOG2E * HEAD_DIM ** -0.5, name="l1_in_proj")

    nqb = seq // Q_BLOCK
    ncp = seq // CMP_STRIDE

    def kv_cols(k):
        o = Q_WIDTH + k * KV_WIDTH
        return proj[:, o:o + KV_WIDTH]

    def chunked(t):
        t = t.reshape(bsz, ncp, CMP_STRIDE, N_KV, HEAD_DIM).transpose(0, 3, 1, 2, 4)
        return t.reshape(bsz * N_KV, ncp, CMP_STRIDE * HEAD_DIM)

    kcmp, _ = _compress_call(chunked(kv_cols(0)), cmp_k_pe, cmp_k_w1, cmp_k_w2)
    _, vcmpt = _compress_call(chunked(kv_cols(1)), cmp_v_pe, cmp_v_w1, cmp_v_w2)
    slopes = _alibi_slope_rows()
    oc, selb, cnt = _attn_cmp_call(proj, kcmp, vcmpt, _overlap_t(seq), slopes, bsz=bsz, seq=seq)
    bits = _active_pair_bits(cnt, bsz=bsz, seq=seq)

    gt = gates[:, :3 * N_HEADS].reshape(bsz, nqb, Q_BLOCK, 3, N_KV, Q_PER_KV).transpose(0, 4, 1, 3, 5, 2)
    gt = gt.reshape(bsz, N_KV, nqb, 3, GQ_WIDTH)
    gt = jnp.pad(gt, ((0, 0), (0, 0), (0, 0), (0, SUBLANES - 3), (0, 0)))
    attn = _attn_sel_call(bits, proj, selb, oc, gt, slopes, _window_bias_tiles(), bsz=bsz, seq=seq)

    as_bf16 = lambda w: w if w.dtype == BF16 else _to_bf16(w)
    x2, h2, rt = _outproj_router_call(attn, x2d, _to_bf16(w_out), mix_post_g, gt_m, ffn_pre_g, sh_f, sc_f,
                                      router_w, seq=seq)
    tm = min(1024, n)
    row_token, dest, tile_expert, n_tiles, tile_rows = _moe_routing(rt, tm=tm)
    hs = h2.at[row_token].get(mode="promise_in_bounds")
    rows = _moe_call(tile_expert, n_tiles, tile_rows, hs, as_bf16(exp_w_gate), as_bf16(exp_w_up),
                     as_bf16(exp_w_down), tm=tm)
    y0 = rows.at[dest[:, 0]].get(mode="promise_in_bounds")
    y1 = rows.at[dest[:, 1]].get(mode="promise_in_bounds")
    return _residual_norm_call(x2, y0, y1, rt, ffn_post_g, gt_f, seq=seq)


def kernel(x, c, l0_ada_w, l0_ada_b, l0_mix_pre_g, l0_mix_post_g, l0_ffn_pre_g, l0_ffn_post_g, l0_w_in, l0_v_ln_g, l0_v_ln_b, l0_w_spatial, l0_b_spatial, l0_conv_w, l0_w_out, l0_ffn_w_gate, l0_ffn_w_up, l0_ffn_w_down, l1_ada_w, l1_ada_b, l1_mix_pre_g, l1_mix_post_g, l1_ffn_pre_g, l1_ffn_post_g, l1_w_in, l1_cmp_k_pe, l1_cmp_k_w1, l1_cmp_k_w2, l1_cmp_v_pe, l1_cmp_v_w1, l1_cmp_v_w2, l1_w_out, l1_router_w, l1_exp_w_gate, l1_exp_w_up, l1_exp_w_down):
    bsz, seq, d = x.shape
    x2d = x.reshape(bsz * seq, d)
    x2d, (exp_w_gate, exp_w_up) = _even_layer(
        x2d, c, l0_ada_w, l0_ada_b, l0_mix_pre_g, l0_mix_post_g, l0_ffn_pre_g, l0_ffn_post_g,
        l0_w_in, l0_v_ln_g, l0_v_ln_b, l0_w_spatial, l0_b_spatial, l0_conv_w, l0_w_out,
        l0_ffn_w_gate, l0_ffn_w_up, l0_ffn_w_down, seq=seq, side_casts=(l1_exp_w_gate, l1_exp_w_up))
    x2d = _odd_layer(x2d, c, l1_ada_w, l1_ada_b, l1_mix_pre_g, l1_mix_post_g, l1_ffn_pre_g, l1_ffn_post_g,
                     l1_w_in, l1_cmp_k_pe, l1_cmp_k_w1, l1_cmp_k_w2, l1_cmp_v_pe, l1_cmp_v_w1, l1_cmp_v_w2,
                     l1_w_out, l1_router_w, exp_w_gate, exp_w_up, l1_exp_w_down, bsz=bsz, seq=seq)
    return x2d.reshape(bsz, seq, d)
```

```python
import functools
import math

import numpy as np
import jax
import jax.numpy as jnp
from jax import lax
from jax.experimental import pallas as pl
from jax.experimental.pallas import tpu as pltpu

F32 = jnp.float32
BF16 = jnp.bfloat16
HIGHEST = lax.Precision.HIGHEST

EPS = 1e-6
LOG2E = math.log2(math.e)
NEG_INF = -1e30
FORCE_SCORE = 1e4

LANES = 128
SUBLANES = 8
VMEM_LIMIT_BYTES = 56 * 1024 * 1024
FFN_VMEM_LIMIT_BYTES = 60 * 1024 * 1024

A_GROUPS = 8
A_WIDTH = 1024
B_WIDTH = 1024
CHUNK = 128
CONV_W = 3
N_HEADS = 16
HEAD_DIM = 128
N_KV = 4
Q_PER_KV = N_HEADS // N_KV
Q_WIDTH = N_HEADS * HEAD_DIM
KV_WIDTH = N_KV * HEAD_DIM
GQ_WIDTH = Q_PER_KV * HEAD_DIM
CMP_BLOCK = 32
CMP_STRIDE = 16
CMP_HIDDEN = 256
SLC_BLOCK = 64
N_SELECT = 16
WINDOW = 512
Q_BLOCK = 128
N_EXPERTS = 8
TOP_K = 2


def _cparams(semantics, vmem_limit_bytes=VMEM_LIMIT_BYTES):
    return pltpu.CompilerParams(dimension_semantics=semantics, vmem_limit_bytes=vmem_limit_bytes)


def _sigmoid(x):
    return 1.0 / (1.0 + jnp.exp(-x))


def _gelu_tanh(x):
    return 0.5 * x * (1.0 + jnp.tanh(0.7978845608028654 * (x + 0.044715 * (x * x * x))))


def _rms(x):
    return x * lax.rsqrt(jnp.mean(x * x, axis=-1, keepdims=True) + EPS)


def _tile(n, pref):
    if n <= pref:
        return n
    t = (pref // LANES) * LANES
    while n % t:
        t -= LANES
    assert t > 0, (n, pref)
    return t


def _ada_kernel(c_ref, w_ref, b_ref, o_ref):
    c = c_ref[...]
    s = c * _sigmoid(c)
    o_ref[...] = jnp.dot(s, w_ref[...], preferred_element_type=F32, precision=HIGHEST) + b_ref[...]


def _ada_call(c, w, b):
    bsz, d = c.shape
    n = w.shape[1]
    tn = _tile(n, 1024)
    cp = jnp.pad(c, ((0, SUBLANES - bsz), (0, 0)))
    out = pl.pallas_call(
        _ada_kernel,
        out_shape=jax.ShapeDtypeStruct((SUBLANES, n), F32),
        grid=(n // tn,),
        in_specs=[pl.BlockSpec((SUBLANES, d), lambda j: (0, 0)),
                  pl.BlockSpec((d, tn), lambda j: (0, j)),
                  pl.BlockSpec((1, tn), lambda j: (0, j))],
        out_specs=pl.BlockSpec((SUBLANES, tn), lambda j: (0, j)),
        compiler_params=_cparams(("parallel",)),
        name="ada_modulation",
    )(cp, w, b.reshape(1, n))
    m = out[:bsz]
    return [t.reshape(bsz, 1, d) for t in jnp.split(m, 6, axis=-1)]


def _proj_kernel(*refs, q_tiles, q_scale, with_gates):
    if with_gates:
        x_ref, g_ref, sh_ref, sc_ref, w_ref, wg_ref, o_ref, og_ref, h_ref = refs
    else:
        x_ref, g_ref, sh_ref, sc_ref, w_ref, o_ref, h_ref = refs
    j = pl.program_id(1)

    @pl.when(j == 0)
    def _():
        h = _rms(x_ref[...]) * g_ref[...]
        h = h * (1.0 + sc_ref[0]) + sh_ref[0]
        h_ref[...] = h.astype(BF16)
        if with_gates:
            og_ref[...] = _sigmoid(jnp.dot(h_ref[...], wg_ref[...], preferred_element_type=F32))

    acc = jnp.dot(h_ref[...], w_ref[...], preferred_element_type=F32)
    if q_tiles:
        acc = acc * jnp.where(j < q_tiles, jnp.float32(q_scale), jnp.float32(1.0))
    o_ref[...] = acc.astype(o_ref.dtype)


def _proj_call(x, g, shift, scale, w, wg=None, *, seq, q_cols=0, q_scale=1.0, name):
    n, d = x.shape
    nout = w.shape[1]
    tm = _tile(seq, 1024)
    tn = _tile(nout, 1024)
    assert q_cols % tn == 0
    tpb = seq // tm
    with_gates = wg is not None
    in_specs = [pl.BlockSpec((tm, d), lambda i, j: (i, 0)),
                pl.BlockSpec((1, d), lambda i, j: (0, 0)),
                pl.BlockSpec((1, 1, d), lambda i, j: (i // tpb, 0, 0)),
                pl.BlockSpec((1, 1, d), lambda i, j: (i // tpb, 0, 0)),
                pl.BlockSpec((d, tn), lambda i, j: (0, j))]
    args = [x, g.reshape(1, d), shift, scale, w]
    out_shape = [jax.ShapeDtypeStruct((n, nout), BF16)]
    out_specs = [pl.BlockSpec((tm, tn), lambda i, j: (i, j))]
    if with_gates:
        in_specs.append(pl.BlockSpec((d, LANES), lambda i, j: (0, 0)))
        args.append(wg)
        out_shape.append(jax.ShapeDtypeStruct((n, LANES), F32))
        out_specs.append(pl.BlockSpec((tm, LANES), lambda i, j: (i, 0)))
    out = pl.pallas_call(
        functools.partial(_proj_kernel, q_tiles=q_cols // tn, q_scale=q_scale, with_gates=with_gates),
        out_shape=out_shape,
        grid=(n // tm, nout // tn),
        in_specs=in_specs,
        out_specs=out_specs,
        scratch_shapes=[pltpu.VMEM((tm, d), BF16)],
        compiler_params=_cparams(("parallel", "arbitrary")),
        name=name,
    )(*args)
    return out if with_gates else out[0]


def _mixer0_kernel(p_ref, pgc_ref, pxb_ref, x_ref, vg_ref, vb_ref, wsp_ref, bspt_ref, cw_ref,
                   wout_ref, gpost_ref, gate_ref, o_ref, cat_ref, *, tm, tiles_per_batch):
    i = pl.program_id(0)
    nchunk = tm // CHUNK
    v = _gelu_tanh(p_ref[:, A_WIDTH:2 * A_WIDTH].astype(F32))
    mu = jnp.mean(v, axis=-1, keepdims=True)
    vc = v - mu
    var = jnp.mean(vc * vc, axis=-1, keepdims=True)
    vn = (vc * lax.rsqrt(var + EPS) * vg_ref[...] + vb_ref[...]).astype(BF16)
    row = lax.broadcasted_iota(jnp.int32, (CHUNK, CHUNK), 0)
    col = lax.broadcasted_iota(jnp.int32, (CHUNK, CHUNK), 1)
    causal = col <= row
    for h in range(A_GROUPS):
        cs = slice(h * LANES, (h + 1) * LANES)
        w = jnp.where(causal, wsp_ref[h], 0.0).astype(BF16)
        rhs = jnp.concatenate([vn[c * CHUNK:(c + 1) * CHUNK, cs] for c in range(nchunk)], axis=1)
        z = jnp.dot(w, rhs, preferred_element_type=F32) + bspt_ref[:, h:h + 1]
        for c in range(nchunk):
            rs = slice(c * CHUNK, (c + 1) * CHUNK)
            u = _gelu_tanh(p_ref[rs, cs].astype(F32))
            cat_ref[rs, cs] = (u * z[:, c * CHUNK:(c + 1) * CHUNK]).astype(BF16)
    o_gb, o_gc, o_xb = 2 * A_WIDTH, 2 * A_WIDTH + B_WIDTH, 2 * A_WIDTH + 2 * B_WIDTH
    zc = p_ref[:, o_gc:o_gc + B_WIDTH].astype(F32) * p_ref[:, o_xb:o_xb + B_WIDTH].astype(F32)
    prev = pgc_ref[...].astype(F32) * pxb_ref[...].astype(F32)
    prev = jnp.where(i % tiles_per_batch == 0, 0.0, prev)
    nprev = prev.shape[0]
    p1 = prev[nprev - 1:nprev, :]
    p2 = prev[nprev - 2:nprev - 1, :]
    rowi = lax.broadcasted_iota(jnp.int32, (tm, B_WIDTH), 0)
    r1 = jnp.where(rowi == 0, p1, pltpu.roll(zc, 1, 0))
    r2 = jnp.where(rowi == 0, p2, jnp.where(rowi == 1, p1, pltpu.roll(zc, 2, 0)))
    y = cw_ref[0:1, :] * r2 + cw_ref[1:2, :] * r1 + cw_ref[2:3, :] * zc
    cat_ref[:, A_WIDTH:] = (p_ref[:, o_gb:o_gb + B_WIDTH].astype(F32) * y).astype(BF16)
    yo = jnp.dot(cat_ref[...], wout_ref[...], preferred_element_type=F32)
    o_ref[...] = x_ref[...] + gate_ref[0] * (_rms(yo) * gpost_ref[...])


def _mixer0_call(proj, x, v_ln_g, v_ln_b, w_spatial, b_spatial, conv_w, w_out, g_post, gate, *, seq):
    n, d = x.shape
    tm = _tile(seq, 512)
    tpb = seq // tm
    pr = 16
    rpb = tm // pr
    wcat = A_WIDTH + B_WIDTH
    cwp = jnp.pad(conv_w, ((0, SUBLANES - CONV_W), (0, 0)))
    return pl.pallas_call(
        functools.partial(_mixer0_kernel, tm=tm, tiles_per_batch=tpb),
        out_shape=jax.ShapeDtypeStruct((n, d), F32),
        grid=(n // tm,),
        in_specs=[pl.BlockSpec((tm, proj.shape[1]), lambda i: (i, 0)),
                  pl.BlockSpec((pr, B_WIDTH), lambda i: (jnp.maximum(i * rpb - 1, 0), 3)),
                  pl.BlockSpec((pr, B_WIDTH), lambda i: (jnp.maximum(i * rpb - 1, 0), 4)),
                  pl.BlockSpec((tm, d), lambda i: (i, 0)),
                  pl.BlockSpec((1, A_WIDTH), lambda i: (0, 0)),
                  pl.BlockSpec((1, A_WIDTH), lambda i: (0, 0)),
                  pl.BlockSpec((A_GROUPS, CHUNK, CHUNK), lambda i: (0, 0, 0)),
                  pl.BlockSpec((CHUNK, A_GROUPS), lambda i: (0, 0)),
                  pl.BlockSpec((SUBLANES, B_WIDTH), lambda i: (0, 0)),
                  pl.BlockSpec((wcat, d), lambda i: (0, 0)),
                  pl.BlockSpec((1, d), lambda i: (0, 0)),
                  pl.BlockSpec((1, 1, d), lambda i: (i // tpb, 0, 0))],
        out_specs=pl.BlockSpec((tm, d), lambda i: (i, 0)),
        scratch_shapes=[pltpu.VMEM((tm, wcat), BF16)],
        compiler_params=_cparams(("parallel",)),
        name="l0_mixer",
    )(proj, proj, proj, x, v_ln_g.reshape(1, -1), v_ln_b.reshape(1, -1), w_spatial, b_spatial.T, cwp,
      w_out, g_post.reshape(1, d), gate)


def _ffn_kernel(*refs, n_side):
    x_ref, gpre_ref, sh_ref, sc_ref, wg_ref, wu_ref, wd_ref, gpost_ref, gate_ref = refs[:9]
    side_in = refs[9:9 + n_side]
    o_ref = refs[9 + n_side]
    side_out = refs[10 + n_side:10 + 2 * n_side]
    h_ref = refs[10 + 2 * n_side]
    j = pl.program_id(1)
    for src, dst in zip(side_in, side_out):
        dst[...] = src[...].astype(dst.dtype)

    @pl.when(j == 0)
    def _():
        h = _rms(x_ref[...]) * gpre_ref[...]
        h_ref[...] = (h * (1.0 + sc_ref[0]) + sh_ref[0]).astype(BF16)
        o_ref[...] = jnp.zeros_like(o_ref)

    h = h_ref[...]
    g = jnp.dot(h, wg_ref[...], preferred_element_type=F32)
    u = jnp.dot(h, wu_ref[...], preferred_element_type=F32)
    a = (g * _sigmoid(g) * u).astype(BF16)
    o_ref[...] += jnp.dot(a, wd_ref[...], preferred_element_type=F32)

    @pl.when(j == pl.num_programs(1) - 1)
    def _():
        o_ref[...] = x_ref[...] + gate_ref[0] * (_rms(o_ref[...]) * gpost_ref[...])


def _side_cast_spec(w, gi, gj):
    cols = w.shape[-1]
    rows = w.size // cols
    if rows % gi == 0 and cols % gj == 0 and (rows // gi) % 16 == 0 and (cols // gj) % LANES == 0:
        return w.reshape(rows, cols), pl.BlockSpec((rows // gi, cols // gj), lambda i, j: (i, j))
    if rows % (gi * gj) == 0 and (rows // (gi * gj)) % 16 == 0:
        return w.reshape(rows, cols), pl.BlockSpec((rows // (gi * gj), cols), lambda i, j: (i * gj + j, 0))
    return None


def _ffn_call(x, g_pre, shift, scale, wg, wu, wd, g_post, gate, *, seq, side_casts=()):
    n, d = x.shape
    f = wg.shape[1]
    tm = _tile(seq, 512)
    tf = _tile(f, 1024)
    tpb = seq // tm
    gi, gj = n // tm, f // tf
    specs = [_side_cast_spec(w, gi, gj) for w in side_casts]
    fused = [s for s in specs if s is not None]
    out = pl.pallas_call(
        functools.partial(_ffn_kernel, n_side=len(fused)),
        out_shape=[jax.ShapeDtypeStruct((n, d), F32)] + [jax.ShapeDtypeStruct(w2.shape, BF16) for w2, _ in fused],
        grid=(gi, gj),
        in_specs=[pl.BlockSpec((tm, d), lambda i, j: (i, 0)),
                  pl.BlockSpec((1, d), lambda i, j: (0, 0)),
                  pl.BlockSpec((1, 1, d), lambda i, j: (i // tpb, 0, 0)),
                  pl.BlockSpec((1, 1, d), lambda i, j: (i // tpb, 0, 0)),
                  pl.BlockSpec((d, tf), lambda i, j: (0, j)),
                  pl.BlockSpec((d, tf), lambda i, j: (0, j)),
                  pl.BlockSpec((tf, d), lambda i, j: (j, 0)),
                  pl.BlockSpec((1, d), lambda i, j: (0, 0)),
                  pl.BlockSpec((1, 1, d), lambda i, j: (i // tpb, 0, 0))] + [spec for _, spec in fused],
        out_specs=[pl.BlockSpec((tm, d), lambda i, j: (i, 0))] + [spec for _, spec in fused],
        scratch_shapes=[pltpu.VMEM((tm, d), BF16)],
        compiler_params=_cparams(("arbitrary", "arbitrary"), FFN_VMEM_LIMIT_BYTES),
        name="l0_ffn",
    )(x, g_pre.reshape(1, d), shift, scale, wg, wu, wd, g_post.reshape(1, d), gate, *[w2 for w2, _ in fused])
    fused_out = iter(out[1:])
    casts = [next(fused_out).reshape(w.shape) if s is not None else _to_bf16(w) for w, s in zip(side_casts, specs)]
    return out[0], casts


def _compress_kernel(x_ref, pe_ref, w1_ref, w2_ref, o_ref, ot_ref):
    half = CMP_STRIDE * HEAD_DIM
    x = x_ref[0]
    nc = x.shape[0]
    a = jnp.dot(x, w1_ref[0:half, :], preferred_element_type=F32)
    b = jnp.dot(x, w1_ref[half:2 * half, :], preferred_element_type=F32)
    bias = jnp.dot(pe_ref[...], w1_ref[...], preferred_element_type=F32)[0:1, :]
    hid = _gelu_tanh(a + pltpu.roll(b, nc - 1, 0) + bias)
    out = jnp.dot(hid.astype(BF16), w2_ref[...], preferred_element_type=F32)
    rowi = lax.broadcasted_iota(jnp.int32, out.shape, 0)
    out = jnp.where(rowi < nc - 1, out, 0.0)
    o_ref[0] = out.astype(BF16)
    ot_ref[0] = out.T.astype(BF16)


def _compress_call(xc, pe, w1, w2):
    bg, nc, kdim = xc.shape
    pe_flat = jnp.pad(pe.reshape(1, -1), ((0, SUBLANES - 1), (0, 0))).astype(BF16)
    w1f = w1.reshape(CMP_BLOCK * HEAD_DIM, CMP_HIDDEN).astype(BF16)
    return pl.pallas_call(
        _compress_kernel,
        out_shape=[jax.ShapeDtypeStruct((bg, nc, HEAD_DIM), BF16),
                   jax.ShapeDtypeStruct((bg, HEAD_DIM, nc), BF16)],
        grid=(bg,),
        in_specs=[pl.BlockSpec((1, nc, kdim), lambda i: (i, 0, 0)),
                  pl.BlockSpec((SUBLANES, CMP_BLOCK * HEAD_DIM), lambda i: (0, 0)),
                  pl.BlockSpec((CMP_BLOCK * HEAD_DIM, CMP_HIDDEN), lambda i: (0, 0)),
                  pl.BlockSpec((CMP_HIDDEN, HEAD_DIM), lambda i: (0, 0))],
        out_specs=[pl.BlockSpec((1, nc, HEAD_DIM), lambda i: (i, 0, 0)),
                   pl.BlockSpec((1, HEAD_DIM, nc), lambda i: (i, 0, 0))],
        compiler_params=_cparams(("parallel",)),
        name="nsa_compress",
    )(xc, pe_flat, w1f, w2.astype(BF16))


def _q_transposed(q_ref):
    parts = []
    for r in range(Q_PER_KV):
        parts.append(q_ref[:, r * HEAD_DIM:(r + 1) * HEAD_DIM].astype(F32).T.astype(BF16))
    return jnp.concatenate(parts, axis=1)


def _attn_cmp_kernel(q_ref, kc_ref, vct_ref, ovt_ref, slope_ref, oc_ref, sel_ref, cnt_ref, imp_ref,
                     *, n_sel, n_class):
    qb = pl.program_id(2)
    q0 = qb * Q_BLOCK
    qt = _q_transposed(q_ref)
    ncp = kc_ref.shape[1]
    nslc = ovt_ref.shape[0]
    rows_per_class = ncp // n_class
    visible = (Q_BLOCK // CMP_STRIDE) * (qb + 1) - 1
    cls = jnp.minimum((visible - 1) // rows_per_class, n_class - 1)

    def compressed(n):
        s_all = jnp.dot(kc_ref[0, 0:n, :], qt, preferred_element_type=F32)
        isub = lax.broadcasted_iota(jnp.int32, (n, Q_BLOCK), 0)
        tl = lax.broadcasted_iota(jnp.int32, (n, Q_BLOCK), 1) + q0
        dist = tl - (isub * CMP_STRIDE + (CMP_BLOCK - 1))
        valid = dist >= 0
        distf = dist.astype(F32)
        psum = jnp.zeros((n, Q_BLOCK), F32)
        pparts = []
        for r in range(Q_PER_KV):
            ls = slice(r * Q_BLOCK, (r + 1) * Q_BLOCK)
            s = s_all[:, ls] - slope_ref[0, 0:1, ls] * distf
            s = jnp.where(valid, s, NEG_INF)
            m = jnp.max(s, axis=0, keepdims=True)
            e = jnp.exp2(s - m)
            inv = jnp.where(m > 0.5 * NEG_INF, 1.0 / jnp.sum(e, axis=0, keepdims=True), 0.0)
            p = e * inv
            psum = psum + p
            pparts.append(p.astype(BF16))
        pt = jnp.concatenate(pparts, axis=1)
        oc_ref[0, 0, 0] = jnp.dot(vct_ref[0, :, 0:n], pt,
                                  preferred_element_type=F32).astype(oc_ref.dtype)
        hi = psum.astype(BF16)
        lo = (psum - hi.astype(F32)).astype(BF16)
        ovt = ovt_ref[:, 0:n]
        imp_ref[...] = (jnp.dot(ovt, hi, preferred_element_type=F32)
                        + jnp.dot(ovt, lo, preferred_element_type=F32))

    for k in range(n_class):
        pl.when(cls == k)(functools.partial(compressed, (k + 1) * rows_per_class))

    jb = lax.broadcasted_iota(jnp.int32, (nslc, Q_BLOCK), 0)
    cur = (lax.broadcasted_iota(jnp.int32, (nslc, Q_BLOCK), 1) + q0) // SLC_BLOCK
    forced = (jb == 0) | (jb == cur) | (jb == cur - 1)
    score = jnp.where(forced, FORCE_SCORE, imp_ref[...])
    score = jnp.where(jb <= cur, score, -1.0)
    picked = -3e38
    three_forced = qb > 0
    score = jnp.where(forced & three_forced, picked, score)

    def pick_next(_, work):
        m = jnp.max(work, axis=0, keepdims=True)
        first = jnp.min(jnp.where(work == m, jb, nslc), axis=0, keepdims=True)
        return jnp.where(jb == first, picked, work)

    work = lax.fori_loop(0, jnp.where(three_forced, n_sel - 3, n_sel), pick_next, score)
    chosen = (work == picked) & (jb <= cur)
    sel_ref[0, 0, 0] = jnp.where(chosen, 0.0, NEG_INF)
    flags = jnp.where(chosen, 1.0, 0.0).astype(BF16)
    cnt_ref[0, 0, 0] = lax.dot_general(jnp.ones((SUBLANES, Q_BLOCK), BF16), flags, (((1,), (1,)), ((), ())),
                                       preferred_element_type=F32)


def _attn_cmp_call(proj, kcmp, vcmpt, ovt, slopes, *, bsz, seq):
    nqb = seq // Q_BLOCK
    ncp = kcmp.shape[1]
    nslc = seq // SLC_BLOCK
    n_sel = min(N_SELECT, nslc)
    return pl.pallas_call(
        functools.partial(_attn_cmp_kernel, n_sel=n_sel, n_class=8),
        out_shape=[jax.ShapeDtypeStruct((bsz, N_KV, nqb, HEAD_DIM, GQ_WIDTH), BF16),
                   jax.ShapeDtypeStruct((bsz, N_KV, nqb, nslc, Q_BLOCK), F32),
                   jax.ShapeDtypeStruct((bsz, N_KV, nqb, SUBLANES, nslc), F32)],
        grid=(bsz, N_KV, nqb),
        in_specs=[pl.BlockSpec((Q_BLOCK, GQ_WIDTH), lambda b, g, q: (b * nqb + q, g)),
                  pl.BlockSpec((1, ncp, HEAD_DIM), lambda b, g, q: (b * N_KV + g, 0, 0)),
                  pl.BlockSpec((1, HEAD_DIM, ncp), lambda b, g, q: (b * N_KV + g, 0, 0)),
                  pl.BlockSpec((nslc, ncp), lambda b, g, q: (0, 0)),
                  pl.BlockSpec((1, SUBLANES, GQ_WIDTH), lambda b, g, q: (g, 0, 0))],
        out_specs=[pl.BlockSpec((1, 1, 1, HEAD_DIM, GQ_WIDTH), lambda b, g, q: (b, g, q, 0, 0)),
                   pl.BlockSpec((1, 1, 1, nslc, Q_BLOCK), lambda b, g, q: (b, g, q, 0, 0)),
                   pl.BlockSpec((1, 1, 1, SUBLANES, nslc), lambda b, g, q: (b, g, q, 0, 0))],
        scratch_shapes=[pltpu.VMEM((nslc, Q_BLOCK), F32)],
        compiler_params=_cparams(("parallel", "parallel", "arbitrary")),
        name="nsa_compressed_select",
    )(proj, kcmp, vcmpt, ovt, slopes)


N_WTILES = WINDOW // Q_BLOCK + 1
HALF = GQ_WIDTH // 2
STEP_TILES = N_WTILES


def _attn_sel_kernel(bits_ref, q_ref, ks_ref, vs_ref, kw_ref, vw_ref, selb_ref, oc_ref, gates_ref, slope_ref,
                     wb_ref, o_ref, qt_ref, m_ref, l_ref, acc_ref, u0_ref, u1_ref, list_ref, *, words):
    b = pl.program_id(0)
    g = pl.program_id(1)
    qb = pl.program_id(2)
    nqb = pl.num_programs(2)
    qt_ref[...] = _q_transposed(q_ref)
    m_ref[...] = jnp.full(m_ref.shape, NEG_INF, F32)
    l_ref[...] = jnp.zeros(l_ref.shape, F32)
    acc_ref[...] = jnp.zeros(acc_ref.shape, F32)

    def key_tile(ref, t):
        return ref[0, pl.ds(pl.multiple_of(t * Q_BLOCK, Q_BLOCK), Q_BLOCK), :]

    def plan(t):
        is_win = t == 0
        out = []
        for j in range(STEP_TILES):
            kt = qb - (N_WTILES - 1) + j
            idx = list_ref[jnp.maximum(t - 1, 0) * STEP_TILES + j]
            w_bias = jnp.where(kt < 0, N_WTILES, j)
            s_bias = jnp.where(idx < 0, N_WTILES, jnp.where(idx == qb, N_WTILES - 1, 1))
            out.append((jnp.where(is_win, jnp.maximum(kt, 0), jnp.maximum(idx, 0)),
                        jnp.where(is_win, w_bias, s_bias)))
        return is_win, out

    def window_scores(u_ref):
        for j in range(STEP_TILES):
            kt = qb - (N_WTILES - 1) + j
            s = jnp.dot(key_tile(kw_ref, jnp.maximum(kt, 0)), qt_ref[...], preferred_element_type=F32)
            u_ref[j * Q_BLOCK:(j + 1) * Q_BLOCK, :] = s + wb_ref[0, jnp.where(kt < 0, N_WTILES, j)]

    def scores(t, u_ref):
        _, tiles = plan(t)
        for j, (tile, widx) in enumerate(tiles):
            k = key_tile(ks_ref, tile)
            two = selb_ref[0, 0, 0, tile]
            lo = jnp.broadcast_to(jnp.concatenate([two[0:1, :]] * Q_PER_KV, axis=1), (SLC_BLOCK, GQ_WIDTH))
            hi = jnp.broadcast_to(jnp.concatenate([two[1:2, :]] * Q_PER_KV, axis=1), (SLC_BLOCK, GQ_WIDTH))
            s = jnp.dot(k, qt_ref[...], preferred_element_type=F32)
            u_ref[j * Q_BLOCK:(j + 1) * Q_BLOCK, :] = s + wb_ref[0, widx] + jnp.concatenate([lo, hi], axis=0)

    def update(t, u_ref):
        is_win, tiles = plan(t)
        st = jnp.where(is_win, 0, 1)
        v = jnp.concatenate([jnp.where(is_win, key_tile(vw_ref, tile), key_tile(vs_ref, tile))
                             for tile, _ in tiles], axis=0)
        for h in range(2):
            cs = slice(h * HALF, (h + 1) * HALF)
            offs = [slope_ref[0, 0:1, cs] * ((tile - qb) * Q_BLOCK).astype(F32) for tile, _ in tiles]
            mx = None
            for j in range(STEP_TILES):
                cm = jnp.max(u_ref[j * Q_BLOCK:(j + 1) * Q_BLOCK, cs], axis=0, keepdims=True) + offs[j]
                mx = cm if mx is None else jnp.maximum(mx, cm)
            m_old = m_ref[st, 0:1, cs]
            m_new = jnp.maximum(m_old, mx)
            es = [jnp.exp2(u_ref[j * Q_BLOCK:(j + 1) * Q_BLOCK, cs] + (offs[j] - m_new)) for j in range(STEP_TILES)]
            lsum = es[0].sum(axis=0, keepdims=True)
            for j in range(1, STEP_TILES):
                lsum = lsum + es[j].sum(axis=0, keepdims=True)
            e = jnp.concatenate([x.astype(BF16) for x in es], axis=0)
            alpha = jnp.exp2(m_old - m_new)
            l_ref[st, 0:1, cs] = alpha * l_ref[st, 0:1, cs] + lsum
            pv = lax.dot_general(v, e, (((0,), (0,)), ((), ())), preferred_element_type=F32)
            acc_ref[st, :, cs] = alpha * acc_ref[st, :, cs] + pv
            m_ref[st, 0:1, cs] = m_new

    window_scores(u0_ref)

    base = ((b * N_KV + g) * nqb + qb) * (words + 1)

    def collect(p, n):
        list_ref[n] = p
        return n + ((bits_ref[base + 1 + (p >> 5)] >> (p & 31)) & 1)

    list_ref[0] = 0
    n_first = jnp.where(qb > 0, bits_ref[base + 1] & 1, 0)
    n_act = lax.fori_loop(jnp.maximum(bits_ref[base], 1), qb, collect, n_first)
    list_ref[n_act] = qb
    for j in range(1, 3 * STEP_TILES):
        list_ref[n_act + j] = -1

    n_updates = 1 + (n_act + STEP_TILES) // STEP_TILES

    def pair(i, carry):
        scores(2 * i + 1, u1_ref)
        update(2 * i, u0_ref)
        scores(2 * i + 2, u0_ref)
        update(2 * i + 1, u1_ref)
        return carry

    lax.fori_loop(0, n_updates // 2, pair, 0)

    @pl.when(n_updates % 2 == 1)
    def _():
        update(n_updates - 1, u0_ref)

    gt = gates_ref[0, 0, 0]
    ot = (gt[0:1, :] * oc_ref[0, 0, 0].astype(F32) + gt[1:2, :] * (acc_ref[1] / l_ref[1, 0:1, :])
          + gt[2:3, :] * (acc_ref[0] / l_ref[0, 0:1, :]))
    for r in range(Q_PER_KV):
        cs = slice(r * HEAD_DIM, (r + 1) * HEAD_DIM)
        o_ref[:, cs] = ot[:, cs].T.astype(o_ref.dtype)


def _window_bias_tiles():
    h = np.arange(1, N_HEADS + 1, dtype=np.float64)
    slope = (LOG2E * np.exp2(-8.0 * h / N_HEADS)).astype(np.float32).reshape(N_KV, 1, 1, Q_PER_KV, 1)
    sub = np.arange(Q_BLOCK, dtype=np.float32).reshape(1, 1, Q_BLOCK, 1, 1)
    tq = np.arange(Q_BLOCK).reshape(1, 1, 1, 1, Q_BLOCK)
    ks = np.arange(Q_BLOCK).reshape(1, 1, Q_BLOCK, 1, 1)
    mask = np.zeros((1, N_WTILES + 1, Q_BLOCK, 1, Q_BLOCK), np.float32)
    mask[0, 0] = np.where(tq[0, 0] < ks[0, 0], 0.0, NEG_INF)
    mask[0, N_WTILES - 1] = np.where(tq[0, 0] >= ks[0, 0], 0.0, NEG_INF)
    mask[0, N_WTILES] = NEG_INF
    out = (slope * sub).astype(np.float32) + mask
    return jnp.asarray(out.reshape(N_KV, N_WTILES + 1, Q_BLOCK, GQ_WIDTH))


def _attn_sel_call(bits, proj, selb, oc, gates_t, slopes, wbias, *, bsz, seq):
    nqb = seq // Q_BLOCK
    n = bsz * seq
    proj3 = proj.reshape(bsz, seq, proj.shape[1])
    ks_col = (Q_WIDTH + 2 * KV_WIDTH) // HEAD_DIM
    kw_col = (Q_WIDTH + 4 * KV_WIDTH) // HEAD_DIM
    vs_col = ks_col + N_KV
    vw_col = kw_col + N_KV
    selb5 = selb.reshape(bsz, N_KV, nqb, nqb, 2, Q_BLOCK)
    words = bits.shape[0] // (bsz * N_KV * nqb) - 1
    grid_spec = pltpu.PrefetchScalarGridSpec(
        num_scalar_prefetch=1,
        grid=(bsz, N_KV, nqb),
        in_specs=[pl.BlockSpec((Q_BLOCK, GQ_WIDTH), lambda b, g, q, s: (b * nqb + q, g)),
                  pl.BlockSpec((1, seq, HEAD_DIM), lambda b, g, q, s: (b, 0, ks_col + g)),
                  pl.BlockSpec((1, seq, HEAD_DIM), lambda b, g, q, s: (b, 0, vs_col + g)),
                  pl.BlockSpec((1, seq, HEAD_DIM), lambda b, g, q, s: (b, 0, kw_col + g)),
                  pl.BlockSpec((1, seq, HEAD_DIM), lambda b, g, q, s: (b, 0, vw_col + g)),
                  pl.BlockSpec((1, 1, 1, nqb, 2, Q_BLOCK), lambda b, g, q, s: (b, g, q, 0, 0, 0)),
                  pl.BlockSpec((1, 1, 1, HEAD_DIM, GQ_WIDTH), lambda b, g, q, s: (b, g, q, 0, 0)),
                  pl.BlockSpec((1, 1, 1, SUBLANES, GQ_WIDTH), lambda b, g, q, s: (b, g, q, 0, 0)),
                  pl.BlockSpec((1, SUBLANES, GQ_WIDTH), lambda b, g, q, s: (g, 0, 0)),
                  pl.BlockSpec((1, N_WTILES + 1, Q_BLOCK, GQ_WIDTH), lambda b, g, q, s: (g, 0, 0, 0))],
        out_specs=pl.BlockSpec((Q_BLOCK, GQ_WIDTH), lambda b, g, q, s: (b * nqb + q, g)),
        scratch_shapes=[pltpu.VMEM((HEAD_DIM, GQ_WIDTH), BF16),
                        pltpu.VMEM((2, SUBLANES, GQ_WIDTH), F32),
                        pltpu.VMEM((2, SUBLANES, GQ_WIDTH), F32),
                        pltpu.VMEM((2, HEAD_DIM, GQ_WIDTH), F32),
                        pltpu.VMEM((STEP_TILES * Q_BLOCK, GQ_WIDTH), F32),
                        pltpu.VMEM((STEP_TILES * Q_BLOCK, GQ_WIDTH), F32),
                        pltpu.SMEM((nqb + 3 * STEP_TILES,), jnp.int32)],
    )
    return pl.pallas_call(
        functools.partial(_attn_sel_kernel, words=words),
        out_shape=jax.ShapeDtypeStruct((n, Q_WIDTH), BF16),
        grid_spec=grid_spec,
        compiler_params=_cparams(("parallel", "parallel", "arbitrary")),
        name="nsa_selected_window",
    )(bits, proj, proj3, proj3, proj3, proj3, selb5, oc, gates_t, slopes, wbias)


def _active_pair_bits(cnt, *, bsz, seq):
    nqb = seq // Q_BLOCK
    words = (nqb + 31) // 32
    act = (cnt[:, :, :, 0, :] > 0.5).reshape(bsz, N_KV, nqb, nqb, 2).any(axis=-1)
    tile_id = jnp.arange(nqb, dtype=jnp.int32)
    first = jnp.min(jnp.where(act & (tile_id >= 1), tile_id, nqb), axis=-1, keepdims=True)
    act = jnp.pad(act, ((0, 0), (0, 0), (0, 0), (0, words * 32 - nqb))).reshape(bsz, N_KV, nqb, words, 32)
    weights = jnp.left_shift(jnp.uint32(1), jnp.arange(32, dtype=jnp.uint32))
    packed = jnp.sum(act.astype(jnp.uint32) * weights, axis=-1, dtype=jnp.uint32)
    packed = lax.bitcast_convert_type(packed, jnp.int32)
    return jnp.concatenate([first, packed], axis=-1).reshape(-1)


def _outproj_router_kernel(a_ref, x_ref, w_ref, gpost_ref, gate_ref, gpre_ref, sh_ref, sc_ref, wr_ref, wrlo_ref,
                           x2_ref, h2_ref, rt_ref):
    y = jnp.dot(a_ref[...], w_ref[...], preferred_element_type=F32)
    x2 = x_ref[...] + gate_ref[0] * (_rms(y) * gpost_ref[...])
    x2_ref[...] = x2
    h = _rms(x2) * gpre_ref[...]
    h = h * (1.0 + sc_ref[0]) + sh_ref[0]
    hb = h.astype(BF16)
    h2_ref[...] = hb
    h_lo = (h - hb.astype(F32)).astype(BF16)
    logits = (jnp.dot(hb, wr_ref[...], preferred_element_type=F32)
              + jnp.dot(h_lo, wr_ref[...], preferred_element_type=F32)
              + jnp.dot(hb, wrlo_ref[...], preferred_element_type=F32))
    lane = lax.broadcasted_iota(jnp.int32, logits.shape, 1)
    logits = jnp.where(lane < N_EXPERTS, logits, -jnp.inf)
    v1 = jnp.max(logits, axis=-1, keepdims=True)
    i1 = jnp.min(jnp.where(logits == v1, lane, LANES), axis=-1, keepdims=True)
    rest = jnp.where(lane == i1, -jnp.inf, logits)
    v2 = jnp.max(rest, axis=-1, keepdims=True)
    i2 = jnp.min(jnp.where(rest == v2, lane, LANES), axis=-1, keepdims=True)
    e2 = jnp.exp(v2 - v1)
    w1 = 1.0 / (1.0 + e2)
    w2 = e2 / (1.0 + e2)
    out = jnp.where(lane == 0, w1, jnp.where(lane == 1, w2, 0.0))
    out = jnp.where(lane == 2, i1.astype(F32), jnp.where(lane == 3, i2.astype(F32), out))
    rt_ref[...] = out


def _outproj_router_call(a, x, w_out, g_post, gate, g_pre, shift, scale, w_router, *, seq):
    n, d = x.shape
    tm = _tile(seq, 512)
    tpb = seq // tm
    wr32 = jnp.pad(w_router, ((0, 0), (0, LANES - w_router.shape[1])))
    wr = wr32.astype(BF16)
    wr_lo = (wr32 - wr.astype(F32)).astype(BF16)
    row = lambda i: (i, 0)
    fixed = lambda i: (0, 0)
    per_b = lambda i: (i // tpb, 0, 0)
    return pl.pallas_call(
        _outproj_router_kernel,
        out_shape=[jax.ShapeDtypeStruct((n, d), F32),
                   jax.ShapeDtypeStruct((n, d), BF16),
                   jax.ShapeDtypeStruct((n, LANES), F32)],
        grid=(n // tm,),
        in_specs=[pl.BlockSpec((tm, a.shape[1]), row),
                  pl.BlockSpec((tm, d), row),
                  pl.BlockSpec(w_out.shape, fixed),
                  pl.BlockSpec((1, d), fixed),
                  pl.BlockSpec((1, 1, d), per_b),
                  pl.BlockSpec((1, d), fixed),
                  pl.BlockSpec((1, 1, d), per_b),
                  pl.BlockSpec((1, 1, d), per_b),
                  pl.BlockSpec((d, LANES), fixed),
                  pl.BlockSpec((d, LANES), fixed)],
        out_specs=[pl.BlockSpec((tm, d), row), pl.BlockSpec((tm, d), row), pl.BlockSpec((tm, LANES), row)],
        compiler_params=_cparams(("parallel",)),
        name="l1_outproj_router",
    )(a, x, w_out, g_post.reshape(1, d), gate, g_pre.reshape(1, d), shift, scale, wr, wr_lo)


MOE_SUB_ROWS = 256
MOE_ROW_PARTS = 2


def _moe_kernel(*refs):
    rows_ref = refs[2]
    h_ref, wg_ref, wu_ref, wd_ref, o_ref, acc_ref = refs[-6:]
    i = pl.program_id(0)
    j = pl.program_id(1)
    tm = h_ref.shape[0]
    n_rows = rows_ref[i]

    @pl.when(j == 0)
    def _():
        acc_ref[...] = jnp.zeros_like(acc_ref)

    def swiglu_rows(rs):
        h = h_ref[rs, :]
        g = jnp.dot(h, wg_ref[0], preferred_element_type=F32)
        u = jnp.dot(h, wu_ref[0], preferred_element_type=F32)
        a = (g * _sigmoid(g) * u).astype(BF16)
        acc_ref[rs, :] += jnp.dot(a, wd_ref[0], preferred_element_type=F32)

    @pl.when(n_rows == tm)
    def _():
        swiglu_rows(slice(None))

    @pl.when((n_rows > 0) & (n_rows < tm))
    def _():
        for sb in range(tm // MOE_SUB_ROWS):
            pl.when(n_rows > sb * MOE_SUB_ROWS)(
                functools.partial(swiglu_rows, slice(sb * MOE_SUB_ROWS, (sb + 1) * MOE_SUB_ROWS)))

    @pl.when(j == pl.num_programs(1) - 1)
    def _():
        o_ref[...] = acc_ref[...].astype(o_ref.dtype)


def _moe_call(tile_expert, n_tiles, tile_rows, hs, wg, wu, wd, *, tm, tile_start, total_rows, prev=None):
    mp, d = hs.shape
    f = wg.shape[2]
    tf = _tile(f, 512)
    nj = f // tf

    def wcol(i, j, nt):
        return jnp.where(i < nt[0], j, nj - 1)

    in_specs = [pl.BlockSpec((tm, d), lambda i, j, te, nt, tr: (i, 0)),
                pl.BlockSpec((1, d, tf), lambda i, j, te, nt, tr: (te[i], 0, wcol(i, j, nt))),
                pl.BlockSpec((1, d, tf), lambda i, j, te, nt, tr: (te[i], 0, wcol(i, j, nt))),
                pl.BlockSpec((1, tf, d), lambda i, j, te, nt, tr: (te[i], wcol(i, j, nt), 0))]
    args = [hs, wg, wu, wd]
    aliases = {}
    if prev is not None:
        in_specs = [pl.BlockSpec(memory_space=pl.ANY)] + in_specs
        args = [prev] + args
        aliases = {3: 0}
    grid_spec = pltpu.PrefetchScalarGridSpec(
        num_scalar_prefetch=3,
        grid=(mp // tm, f // tf),
        in_specs=in_specs,
        out_specs=pl.BlockSpec((tm, d), lambda i, j, te, nt, tr: (i + tile_start, 0)),
        scratch_shapes=[pltpu.VMEM((tm, d), F32)],
    )
    return pl.pallas_call(
        _moe_kernel,
        out_shape=jax.ShapeDtypeStruct((total_rows, d), BF16),
        grid_spec=grid_spec,
        input_output_aliases=aliases,
        compiler_params=_cparams(("arbitrary", "arbitrary")),
        name="l1_moe_experts",
    )(tile_expert, n_tiles, tile_rows, *args)


def _residual_norm_kernel(x_ref, y0_ref, y1_ref, rt_ref, gpost_ref, gate_ref, o_ref):
    y = rt_ref[:, 0:1] * y0_ref[...].astype(F32) + rt_ref[:, 1:2] * y1_ref[...].astype(F32)
    o_ref[...] = x_ref[...] + gate_ref[0] * (_rms(y) * gpost_ref[...])


def _residual_norm_call(x, y0, y1, rt, g_post, gate, *, seq):
    n, d = x.shape
    tm = _tile(seq, 512)
    tpb = seq // tm
    row = lambda i: (i, 0)
    return pl.pallas_call(
        _residual_norm_kernel,
        out_shape=jax.ShapeDtypeStruct((n, d), F32),
        grid=(n // tm,),
        in_specs=[pl.BlockSpec((tm, d), row), pl.BlockSpec((tm, d), row), pl.BlockSpec((tm, d), row),
                  pl.BlockSpec((tm, LANES), row),
                  pl.BlockSpec((1, d), lambda i: (0, 0)),
                  pl.BlockSpec((1, 1, d), lambda i: (i // tpb, 0, 0))],
        out_specs=pl.BlockSpec((tm, d), row),
        compiler_params=_cparams(("parallel",)),
        name="l1_moe_residual",
    )(x, y0, y1, rt, g_post.reshape(1, d), gate)


CAST_BLOCK_BYTES = 8 * 1024 * 1024
CAST_BLOCK_COLS = 2048


def _cast_kernel(x_ref, o_ref):
    o_ref[...] = x_ref[...].astype(o_ref.dtype)


def _to_bf16(w):
    cols = w.shape[-1]
    rows = w.size // cols
    tc = _tile(cols, CAST_BLOCK_COLS)
    tr = rows
    while tr * tc * 4 > CAST_BLOCK_BYTES and tr % 2 == 0 and (tr // 2) % 16 == 0:
        tr //= 2
    out = pl.pallas_call(
        _cast_kernel,
        out_shape=jax.ShapeDtypeStruct((rows, cols), BF16),
        grid=(rows // tr, cols // tc),
        in_specs=[pl.BlockSpec((tr, tc), lambda i, j: (i, j))],
        out_specs=pl.BlockSpec((tr, tc), lambda i, j: (i, j)),
        compiler_params=_cparams(("parallel", "parallel")),
        name="weight_cast",
    )(w.reshape(rows, cols))
    return out.reshape(w.shape)


def _even_layer(x2d, c, ada_w, ada_b, mix_pre_g, mix_post_g, ffn_pre_g, ffn_post_g,
                w_in, v_ln_g, v_ln_b, w_spatial, b_spatial, conv_w, w_out,
                ffn_w_gate, ffn_w_up, ffn_w_down, *, seq, side_casts=()):
    sh_m, sc_m, gt_m, sh_f, sc_f, gt_f = _ada_call(c, ada_w, ada_b)
    proj = _proj_call(x2d, mix_pre_g, sh_m, sc_m, _to_bf16(w_in), seq=seq, name="l0_in_proj")
    x2d = _mixer0_call(proj, x2d, v_ln_g, v_ln_b, w_spatial, b_spatial, conv_w, _to_bf16(w_out),
                       mix_post_g, gt_m, seq=seq)
    return _ffn_call(x2d, ffn_pre_g, sh_f, sc_f, _to_bf16(ffn_w_gate), _to_bf16(ffn_w_up),
                     _to_bf16(ffn_w_down), ffn_post_g, gt_f, seq=seq, side_casts=side_casts)


def _moe_routing(rt, *, tm):
    n = rt.shape[0]
    na = n * TOP_K
    mp = na + N_EXPERTS * tm
    e_flat = rt[:, TOP_K:2 * TOP_K].astype(jnp.int32).reshape(na)
    onehot = (e_flat[:, None] == jnp.arange(N_EXPERTS, dtype=jnp.int32)[None, :]).astype(jnp.int32)
    csum = jnp.cumsum(onehot, axis=0)
    counts = csum[-1]
    rank = jnp.sum(onehot * csum, axis=1) - 1
    padded = ((counts + tm - 1) // tm) * tm
    ends = jnp.cumsum(padded)
    starts = ends - padded
    dest = jnp.sum(onehot * starts[None, :], axis=1) + rank
    order = jnp.argsort(e_flat, stable=True).astype(jnp.int32)
    rows = jnp.arange(mp, dtype=jnp.int32)
    row_e = jnp.minimum(jnp.sum((rows[:, None] >= ends[None, :]).astype(jnp.int32), axis=1), N_EXPERTS - 1)
    slot = rows - starts[row_e]
    src = (jnp.cumsum(counts) - counts)[row_e] + slot
    row_token = jnp.where(slot < counts[row_e], order[jnp.clip(src, 0, na - 1)] // TOP_K, 0)
    n_tiles = (ends[-1] // tm).astype(jnp.int32)
    tile_ids = jnp.arange(mp // tm, dtype=jnp.int32)
    tile_expert = jnp.sum((tile_ids[:, None] * tm >= ends[None, :]).astype(jnp.int32), axis=1)
    last = jnp.sum(((n_tiles - 1) * tm >= ends).astype(jnp.int32))
    tile_expert = jnp.where(tile_ids < n_tiles, tile_expert, last).astype(jnp.int32)
    tile_rows = jnp.clip(starts[tile_expert] + counts[tile_expert] - tile_ids * tm, 0, tm)
    tile_rows = jnp.where(tile_ids < n_tiles, tile_rows, 0).astype(jnp.int32)
    return row_token, dest.reshape(n, TOP_K), tile_expert, n_tiles.reshape(1), tile_rows


def _alibi_slope_rows():
    h = np.arange(1, N_HEADS + 1, dtype=np.float64)
    s = (LOG2E * np.exp2(-8.0 * h / N_HEADS)).astype(np.float32).reshape(N_KV, 1, Q_PER_KV, 1)
    rows = np.broadcast_to(s, (N_KV, SUBLANES, Q_PER_KV, Q_BLOCK)).reshape(N_KV, SUBLANES, GQ_WIDTH)
    return jnp.asarray(rows)


def _overlap_t(seq):
    ncp = seq // CMP_STRIDE
    nslc = seq // SLC_BLOCK
    cs = np.arange(ncp)[None, :] * CMP_STRIDE
    ss = np.arange(nslc)[:, None] * SLC_BLOCK
    ov = (cs < ss + SLC_BLOCK) & (cs + CMP_BLOCK > ss) & (np.arange(ncp)[None, :] < ncp - 1)
    return jnp.asarray(ov.astype(np.float32)).astype(BF16)


def _odd_layer(x2d, c, ada_w, ada_b, mix_pre_g, mix_post_g, ffn_pre_g, ffn_post_g,
               w_in, cmp_k_pe, cmp_k_w1, cmp_k_w2, cmp_v_pe, cmp_v_w1, cmp_v_w2, w_out,
               router_w, exp_w_gate, exp_w_up, exp_w_down, *, bsz, seq):
    n, d = x2d.shape
    sh_m, sc_m, gt_m, sh_f, sc_f, gt_f = _ada_call(c, ada_w, ada_b)
    nmain = Q_WIDTH + 6 * KV_WIDTH
    w_main = w_in[:, :nmain].astype(BF16)
    w_gates = jnp.pad(w_in[:, nmain:], ((0, 0), (0, LANES - 3 * N_HEADS))).astype(BF16)
    proj, gates = _proj_call(x2d, mix_pre_g, sh_m, sc_m, w_main, w_gates, seq=seq,
                             q_cols=Q_WIDTH, q_scale=LOG2E * HEAD_DIM ** -0.5, name="l1_in_proj")

    nqb = seq // Q_BLOCK
    ncp = seq // CMP_STRIDE

    def kv_cols(k):
        o = Q_WIDTH + k * KV_WIDTH
        return proj[:, o:o + KV_WIDTH]

    def chunked(t):
        t = t.reshape(bsz, ncp, CMP_STRIDE, N_KV, HEAD_DIM).transpose(0, 3, 1, 2, 4)
        return t.reshape(bsz * N_KV, ncp, CMP_STRIDE * HEAD_DIM)

    kcmp, _ = _compress_call(chunked(kv_cols(0)), cmp_k_pe, cmp_k_w1, cmp_k_w2)
    _, vcmpt = _compress_call(chunked(kv_cols(1)), cmp_v_pe, cmp_v_w1, cmp_v_w2)
    slopes = _alibi_slope_rows()
    oc, selb, cnt = _attn_cmp_call(proj, kcmp, vcmpt, _overlap_t(seq), slopes, bsz=bsz, seq=seq)
    bits = _active_pair_bits(cnt, bsz=bsz, seq=seq)

    gt = gates[:, :3 * N_HEADS].reshape(bsz, nqb, Q_BLOCK, 3, N_KV, Q_PER_KV).transpose(0, 4, 1, 3, 5, 2)
    gt = gt.reshape(bsz, N_KV, nqb, 3, GQ_WIDTH)
    gt = jnp.pad(gt, ((0, 0), (0, 0), (0, 0), (0, SUBLANES - 3), (0, 0)))
    attn = _attn_sel_call(bits, proj, selb, oc, gt, slopes, _window_bias_tiles(), bsz=bsz, seq=seq)

    as_bf16 = lambda w: w if w.dtype == BF16 else _to_bf16(w)
    x2, h2, rt = _outproj_router_call(attn, x2d, _to_bf16(w_out), mix_post_g, gt_m, ffn_pre_g, sh_f, sc_f,
                                      router_w, seq=seq)
    tm = min(1024, n)
    row_token, dest, tile_expert, n_tiles, tile_rows = _moe_routing(rt, tm=tm)
    wg_b, wu_b, wd_b = as_bf16(exp_w_gate), as_bf16(exp_w_up), as_bf16(exp_w_down)
    n_row_tiles = row_token.shape[0] // tm
    bounds = [n_row_tiles * p // MOE_ROW_PARTS for p in range(MOE_ROW_PARTS + 1)]
    rows = jnp.zeros((n_row_tiles * tm, d), BF16)
    for lo, hi in zip(bounds[:-1], bounds[1:]):
        if lo == hi:
            continue
        hs = h2.at[row_token[lo * tm:hi * tm]].get(mode="promise_in_bounds")
        rows = _moe_call(tile_expert[lo:hi], n_tiles - lo, tile_rows[lo:hi], hs, wg_b, wu_b, wd_b, tm=tm,
                         tile_start=lo, total_rows=n_row_tiles * tm, prev=rows)
    y0 = rows.at[dest[:, 0]].get(mode="promise_in_bounds")
    y1 = rows.at[dest[:, 1]].get(mode="promise_in_bounds")
    return _residual_norm_call(x2, y0, y1, rt, ffn_post_g, gt_f, seq=seq)


def kernel(x, c, l0_ada_w, l0_ada_b, l0_mix_pre_g, l0_mix_post_g, l0_ffn_pre_g, l0_ffn_post_g, l0_w_in, l0_v_ln_g, l0_v_ln_b, l0_w_spatial, l0_b_spatial, l0_conv_w, l0_w_out, l0_ffn_w_gate, l0_ffn_w_up, l0_ffn_w_down, l1_ada_w, l1_ada_b, l1_mix_pre_g, l1_mix_post_g, l1_ffn_pre_g, l1_ffn_post_g, l1_w_in, l1_cmp_k_pe, l1_cmp_k_w1, l1_cmp_k_w2, l1_cmp_v_pe, l1_cmp_v_w1, l1_cmp_v_w2, l1_w_out, l1_router_w, l1_exp_w_gate, l1_exp_w_up, l1_exp_w_down):
    bsz, seq, d = x.shape
    x2d = x.reshape(bsz * seq, d)
    x2d, (exp_w_gate, exp_w_up) = _even_layer(
        x2d, c, l0_ada_w, l0_ada_b, l0_mix_pre_g, l0_mix_post_g, l0_ffn_pre_g, l0_ffn_post_g,
        l0_w_in, l0_v_ln_g, l0_v_ln_b, l0_w_spatial, l0_b_spatial, l0_conv_w, l0_w_out,
        l0_ffn_w_gate, l0_ffn_w_up, l0_ffn_w_down, seq=seq, side_casts=(l1_exp_w_gate, l1_exp_w_up))
    x2d = _odd_layer(x2d, c, l1_ada_w, l1_ada_b, l1_mix_pre_g, l1_mix_post_g, l1_ffn_pre_g, l1_ffn_post_g,
                     l1_w_in, l1_cmp_k_pe, l1_cmp_k_w1, l1_cmp_k_w2, l1_cmp_v_pe, l1_cmp_v_w1, l1_cmp_v_w2,
                     l1_w_out, l1_router_w, exp_w_gate, exp_w_up, l1_exp_w_down, bsz=bsz, seq=seq)
    return x2d.reshape(bsz, seq, d)
```
